```python
import math
import jax, jax.numpy as jnp
from jax import lax
import numpy as np

D_MODEL = 1024
BATCH = 8
SEQ = 2048
DEPTH = 4
DEC_BATCH = 128
DEC_SEQ = 1
PAST_LEN = 8192
PAGE_SIZE = 128

H_RET = 4
HD_RET = 128
RET_CHUNK = 128
ROPE_BASE = 10000.0
H_SWA = 8
KV_SWA = 2
G_SWA = H_SWA // KV_SWA
HD_SWA = 64
WINDOW = 128
NUM_BUCKETS = 32
REL_MAX_DIST = 128
RET_W = H_RET * HD_RET
SWA_Q_W = H_SWA * HD_SWA
SWA_KV_W = KV_SWA * HD_SWA
MIX_W = RET_W + SWA_Q_W
D_IN = 4 * RET_W + SWA_Q_W + 2 * SWA_KV_W
SPLITS = (RET_W, 2 * RET_W, 3 * RET_W, 4 * RET_W, 4 * RET_W + SWA_Q_W, 4 * RET_W + SWA_Q_W + SWA_KV_W)
N_EXPERTS = 32
TOP_K = 4
D_FF = D_MODEL
SWIGLU_LIMIT = 7.0
SWIGLU_ALPHA = 1.702
MOE_BLOCK = 128
DEEPNORM_ALPHA = (2 * DEPTH) ** 0.25
DEEPNORM_BETA = (8 * DEPTH) ** -0.25
LN_EPS = 1e-5
GN_EPS = 1e-5

kernel_name = "hybrid_retention_swa_moe_deepnorm_step"

F32 = jnp.float32


def layer_norm(x, g, b):
    xf = x.astype(F32)
    mu = jnp.mean(xf, -1, keepdims=True)
    var = jnp.mean(jnp.square(xf - mu), -1, keepdims=True)
    return ((xf - mu) * lax.rsqrt(var + LN_EPS) * g.astype(F32) + b.astype(F32)).astype(x.dtype)


def rotary(x, pos):
    half = x.shape[-1] // 2
    inv = ROPE_BASE ** (-jnp.arange(half, dtype=F32) / half)
    ang = pos.astype(F32)[:, None] * inv[None, :]
    cos = jnp.cos(ang)[None, :, None, :]
    sin = jnp.sin(ang)[None, :, None, :]
    xf = x.astype(F32)
    x1, x2 = xf[..., :half], xf[..., half:]
    return jnp.concatenate([x1 * cos - x2 * sin, x1 * sin + x2 * cos], -1)


def ret_log_decay():
    return jnp.log1p(-jnp.exp2(-5.0 - jnp.arange(H_RET, dtype=F32)))


def retention(q, k, v, s0):
    B, L, H, _ = q.shape
    dv = v.shape[-1]
    C = RET_CHUNK if L % RET_CHUNK == 0 else L
    n = L // C
    log_g = ret_log_decay()
    idx = jnp.arange(C, dtype=F32)
    diff = idx[:, None] - idx[None, :]
    intra = jnp.where(diff[None] >= 0, jnp.exp(jnp.maximum(diff, 0.0)[None] * log_g[:, None, None]), 0.0)
    q_dec = jnp.exp((idx + 1.0)[None] * log_g[:, None])[..., None]
    k_dec = jnp.exp((C - 1.0 - idx)[None] * log_g[:, None])[..., None]
    c_dec = jnp.exp(C * log_g)[:, None, None]

    def to_chunks(t):
        return t.astype(F32).reshape(B, n, C, H, t.shape[-1]).transpose(1, 0, 3, 2, 4)

    def step(S, inp):
        qc, kc, vc = inp
        att = jnp.einsum('bhcd,bhsd->bhcs', qc, kc) * intra
        o = jnp.einsum('bhcs,bhsv->bhcv', att, vc) + jnp.einsum('bhcd,bhdv->bhcv', qc * q_dec, S)
        S = S * c_dec + jnp.einsum('bhsd,bhsv->bhdv', kc * k_dec, vc)
        return S, o

    S, o = lax.scan(step, s0.astype(F32), (to_chunks(q), to_chunks(k), to_chunks(v)))
    o = o.transpose(1, 0, 3, 2, 4).reshape(B, L, H, dv)
    return o, S


def head_group_norm(o):
    mu = jnp.mean(o, -1, keepdims=True)
    var = jnp.mean(jnp.square(o - mu), -1, keepdims=True)
    return (o - mu) * lax.rsqrt(var + GN_EPS)


def t5_bucket(dist):
    n = jnp.maximum(dist, 0)
    max_exact = NUM_BUCKETS // 2
    nf = jnp.maximum(n, 1).astype(F32)
    large = max_exact + (jnp.log(nf / max_exact) / math.log(REL_MAX_DIST / max_exact)
                         * (NUM_BUCKETS - max_exact)).astype(jnp.int32)
    large = jnp.minimum(large, NUM_BUCKETS - 1)
    return jnp.where(n < max_exact, n, large)


def sink_attention(q, k, v, dist, valid, sinks, rel_bias):
    Lq, Lk = dist.shape
    bias = jnp.moveaxis(rel_bias[t5_bucket(dist)], -1, 0).astype(F32).reshape(KV_SWA, G_SWA, Lq, Lk)
    logits = jnp.einsum('...qkgd,...skd->...kgqs', q, k, preferred_element_type=F32) * (HD_SWA ** -0.5) + bias
    logits = jnp.where(valid, logits, -jnp.inf)
    sink = sinks.astype(F32).reshape(KV_SWA, G_SWA, 1, 1)
    m = jnp.maximum(jnp.max(logits, -1, keepdims=True), sink)
    p = jnp.exp(logits - m)
    w = p / (jnp.sum(p, -1, keepdims=True) + jnp.exp(sink - m))
    return jnp.einsum('...kgqs,...skd->...qkgd', w.astype(v.dtype), v)


def swa_prompt(q, k, v, sinks, rel_bias):
    B, L = q.shape[:2]
    nb = L // WINDOW
    qb = q.reshape(B, nb, WINDOW, KV_SWA, G_SWA, HD_SWA)

    def banded(t):
        tb = t.reshape(B, nb, WINDOW, KV_SWA, HD_SWA)
        prev = jnp.concatenate([jnp.zeros_like(tb[:, :1]), tb[:, :-1]], axis=1)
        return jnp.concatenate([prev, tb], axis=2)

    i = jnp.arange(WINDOW)[:, None]
    j = jnp.arange(2 * WINDOW)[None, :]
    dist = WINDOW + i - j
    blk = jnp.arange(nb)[:, None, None]
    valid = (dist >= 0) & (dist < WINDOW) & (blk * WINDOW - WINDOW + j >= 0)
    o = sink_attention(qb, banded(k), banded(v), dist, valid[:, None, None], sinks, rel_bias)
    return o.reshape(B, L, SWA_Q_W)


def swa_sample(q, k, v, k_buf, v_buf, sinks, rel_bias):
    B, L = q.shape[:2]
    kk = jnp.concatenate([k_buf.astype(k.dtype), k], axis=1)
    vv = jnp.concatenate([v_buf.astype(v.dtype), v], axis=1)
    i = jnp.arange(L)[:, None]
    j = jnp.arange(WINDOW + L)[None, :]
    dist = WINDOW + i - j
    valid = (dist >= 0) & (dist < WINDOW)
    o = sink_attention(q, kk, vv, dist, valid, sinks, rel_bias)
    return o.reshape(B, L, SWA_Q_W), kk[:, -WINDOW:], vv[:, -WINDOW:]


def moe(x2d, router_w, router_b, w_gate_up, b_gate_up, w_down, b_down):
    T, D = x2d.shape
    M = T * TOP_K
    n_blocks = -(-M // MOE_BLOCK) + N_EXPERTS
    n_slots = n_blocks * MOE_BLOCK
    logits = jnp.dot(x2d, router_w, preferred_element_type=F32) + router_b.astype(F32)
    top_val, top_idx = lax.top_k(logits, TOP_K)
    gate = jax.nn.softmax(top_val, axis=-1).reshape(M)
    expert = top_idx.reshape(M)
    token = jnp.repeat(jnp.arange(T, dtype=jnp.int32), TOP_K)
    order = jnp.argsort(expert)
    e_sorted = expert[order]
    counts = jnp.bincount(expert, length=N_EXPERTS)
    padded = (counts + MOE_BLOCK - 1) // MOE_BLOCK * MOE_BLOCK
    pad_end = jnp.cumsum(padded)
    pad_start = pad_end - padded
    start = jnp.cumsum(counts) - counts
    dest = pad_start[e_sorted] + jnp.arange(M) - start[e_sorted]
    slot_token = jnp.zeros((n_slots,), jnp.int32).at[dest].set(token[order])
    slot_gate = jnp.zeros((n_slots,), F32).at[dest].set(gate[order])
    block_expert = jnp.minimum(jnp.searchsorted(pad_end, jnp.arange(n_blocks) * MOE_BLOCK, side='right'),
                               N_EXPERTS - 1).astype(jnp.int32)
    xs = x2d[slot_token].reshape(n_blocks, MOE_BLOCK, D)

    def expert_block(args):
        xb, e = args
        gu = jnp.dot(xb, w_gate_up[e], preferred_element_type=F32) + b_gate_up[e].astype(F32)
        glu = jnp.minimum(gu[:, :D_FF], SWIGLU_LIMIT)
        lin = jnp.clip(gu[:, D_FF:], -SWIGLU_LIMIT, SWIGLU_LIMIT)
        act = glu * jax.nn.sigmoid(SWIGLU_ALPHA * glu) * (lin + 1.0)
        return jnp.dot(act.astype(xb.dtype), w_down[e], preferred_element_type=F32) + b_down[e].astype(F32)

    ys = lax.map(expert_block, (xs, block_expert)).reshape(n_slots, D) * slot_gate[:, None]
    return jnp.zeros((T, D), F32).at[slot_token].add(ys).astype(x2d.dtype)


def decoder_layer(x, pos, s_ret, k_buf, v_buf, w_in, w_out, gn_gain, sinks, rel_bias,
                  ln1_g, ln1_b, ln2_g, ln2_b, router_w, router_b, w_gate_up, b_gate_up, w_down, b_down):
    B, L, _ = x.shape
    h = jnp.einsum('bld,de->ble', x, w_in)
    q_r, k_r, v_r, g_r, q_s, k_s, v_s = jnp.split(h, SPLITS, axis=-1)
    q_r = rotary(q_r.reshape(B, L, H_RET, HD_RET), pos)
    k_r = rotary(k_r.reshape(B, L, H_RET, HD_RET), pos) * (HD_RET ** -0.5)
    o_r, s_new = retention(q_r, k_r, v_r.reshape(B, L, H_RET, HD_RET), s_ret)
    o_r = head_group_norm(o_r).reshape(B, L, RET_W) * gn_gain.astype(F32)
    o_r = (o_r * jax.nn.silu(g_r.astype(F32))).astype(x.dtype)
    q_s = q_s.reshape(B, L, KV_SWA, G_SWA, HD_SWA)
    k_s = k_s.reshape(B, L, KV_SWA, HD_SWA)
    v_s = v_s.reshape(B, L, KV_SWA, HD_SWA)
    if k_buf is None:
        o_s = swa_prompt(q_s, k_s, v_s, sinks, rel_bias)
        kb_new, vb_new = k_s[:, -WINDOW:], v_s[:, -WINDOW:]
    else:
        o_s, kb_new, vb_new = swa_sample(q_s, k_s, v_s, k_buf, v_buf, sinks, rel_bias)
    mix = jnp.einsum('ble,ed->bld', jnp.concatenate([o_r, o_s.astype(x.dtype)], -1), w_out)
    x = layer_norm(DEEPNORM_ALPHA * x + mix, ln1_g, ln1_b)
    ffn = moe(x.reshape(B * L, D_MODEL), router_w, router_b, w_gate_up, b_gate_up, w_down, b_down)
    x = layer_norm(DEEPNORM_ALPHA * x + ffn.reshape(B, L, D_MODEL), ln2_g, ln2_b)
    return x, s_new, kb_new, vb_new


def setup_inputs(seed: int = 0) -> dict:
    key = jax.random.key(seed)
    ks = jax.random.split(key, 24)

    def nrm(k, shape, scale):
        return jax.random.normal(k, shape, F32) * scale

    return {
        "x_prompt": nrm(ks[0], (BATCH, SEQ, D_MODEL), 1.0),
        "x_sample": nrm(ks[1], (DEC_BATCH, DEC_SEQ, D_MODEL), 1.0),
        "state_ret": nrm(ks[2], (DEPTH, DEC_BATCH, H_RET, HD_RET, HD_RET), 0.5),
        "cache_swa_k": nrm(ks[3], (DEPTH, DEC_BATCH, WINDOW, KV_SWA, HD_SWA), 1.0),
        "cache_swa_v": nrm(ks[4], (DEPTH, DEC_BATCH, WINDOW, KV_SWA, HD_SWA), 1.0),
        "w_in": nrm(ks[5], (DEPTH, D_MODEL, D_IN), D_MODEL ** -0.5),
        "w_out": nrm(ks[6], (DEPTH, MIX_W, D_MODEL), MIX_W ** -0.5 * DEEPNORM_BETA),
        "ret_gn_gain": 1.0 + nrm(ks[7], (DEPTH, RET_W), 0.02),
        "swa_sinks": nrm(ks[8], (DEPTH, H_SWA), 0.5),
        "rel_bias": nrm(ks[9], (NUM_BUCKETS, H_SWA), 0.1),
        "ln1_g": 1.0 + nrm(ks[10], (DEPTH, D_MODEL), 0.02),
        "ln1_b": nrm(ks[11], (DEPTH, D_MODEL), 0.02),
        "ln2_g": 1.0 + nrm(ks[12], (DEPTH, D_MODEL), 0.02),
        "ln2_b": nrm(ks[13], (DEPTH, D_MODEL), 0.02),
        "router_w": nrm(ks[14], (DEPTH, D_MODEL, N_EXPERTS), D_MODEL ** -0.5),
        "router_b": nrm(ks[15], (DEPTH, N_EXPERTS), 0.01),
        "w_gate_up": nrm(ks[16], (DEPTH, N_EXPERTS, D_MODEL, 2 * D_FF), D_MODEL ** -0.5),
        "b_gate_up": nrm(ks[17], (DEPTH, N_EXPERTS, 2 * D_FF), 0.01),
        "w_down": nrm(ks[18], (DEPTH, N_EXPERTS, D_FF, D_MODEL), D_FF ** -0.5 * DEEPNORM_BETA),
        "b_down": nrm(ks[19], (DEPTH, N_EXPERTS, D_MODEL), 0.01),
    }


def reference(x_prompt, x_sample, state_ret, cache_swa_k, cache_swa_v, w_in, w_out, ret_gn_gain,
              swa_sinks, rel_bias, ln1_g, ln1_b, ln2_g, ln2_b, router_w, router_b,
              w_gate_up, b_gate_up, w_down, b_down):
    Bp, Lp = x_prompt.shape[:2]
    Ls = x_sample.shape[1]
    pos_p = jnp.arange(Lp, dtype=jnp.int32)
    pos_s = PAST_LEN + jnp.arange(Ls, dtype=jnp.int32)
    s0_prompt = jnp.zeros((Bp, H_RET, HD_RET, HD_RET), F32)
    yp, ys = x_prompt, x_sample
    sr_p, kc_p, vc_p, sr_s, kc_s, vc_s = [], [], [], [], [], []
    for l in range(DEPTH):
        lw = (w_in[l], w_out[l], ret_gn_gain[l], swa_sinks[l], rel_bias,
              ln1_g[l], ln1_b[l], ln2_g[l], ln2_b[l], router_w[l], router_b[l],
              w_gate_up[l], b_gate_up[l], w_down[l], b_down[l])
        yp, s_p, k_p, v_p = decoder_layer(yp, pos_p, s0_prompt, None, None, *lw)
        ys, s_s, k_s, v_s = decoder_layer(ys, pos_s, state_ret[l], cache_swa_k[l], cache_swa_v[l], *lw)
        sr_p.append(s_p); kc_p.append(k_p); vc_p.append(v_p)
        sr_s.append(s_s); kc_s.append(k_s); vc_s.append(v_s)
    return (yp, ys, jnp.stack(sr_p), jnp.stack(kc_p), jnp.stack(vc_p),
            jnp.stack(sr_s), jnp.stack(kc_s), jnp.stack(vc_s))
```

```python
import functools
import math

import jax
import jax.numpy as jnp
from jax import lax
from jax.experimental import pallas as pl
from jax.experimental.pallas import tpu as pltpu

F32 = jnp.float32
BF16 = jnp.bfloat16

D_MODEL = 1024
BATCH = 8
SEQ = 2048
DEPTH = 4
DEC_BATCH = 128
PAST_LEN = 8192
H_RET = 4
HD_RET = 128
RET_CHUNK = 128
ROPE_BASE = 10000.0
H_SWA = 8
KV_SWA = 2
G_SWA = H_SWA // KV_SWA
HD_SWA = 64
WINDOW = 128
NUM_BUCKETS = 32
REL_MAX_DIST = 128
RET_W = H_RET * HD_RET
SWA_Q_W = H_SWA * HD_SWA
SWA_KV_W = KV_SWA * HD_SWA
D_IN = 4 * RET_W + SWA_Q_W + 2 * SWA_KV_W
N_EXPERTS = 32
TOP_K = 4
D_FF = D_MODEL
SWIGLU_LIMIT = 7.0
SWIGLU_ALPHA = 1.702
DEEPNORM_ALPHA = (2 * DEPTH) ** 0.25
LN_EPS = 1e-5
GN_EPS = 1e-5

T_PROMPT = BATCH * SEQ
T_ALL = T_PROMPT + DEC_BATCH
ROW_BLOCK = 384
SAMPLE_BLOCK = 8
MOE_ROWS = 256
N_ASSIGN = T_ALL * TOP_K
MOE_BLOCKS = N_ASSIGN // MOE_ROWS + N_EXPERTS
MOE_SLOTS = MOE_BLOCKS * MOE_ROWS
VMEM_LIMIT = 56 * 1024 * 1024


def _cparams(*sem):
    return pltpu.CompilerParams(dimension_semantics=sem, vmem_limit_bytes=VMEM_LIMIT)


def _in_proj_kernel(x_ref, w_ref, h_ref):
    h_ref[...] = jnp.dot(x_ref[...].astype(BF16), w_ref[0], preferred_element_type=F32)


def _in_proj(x, w_in_bf, layer):
    return pl.pallas_call(
        _in_proj_kernel,
        grid=(T_ALL // ROW_BLOCK,),
        in_specs=[pl.BlockSpec((ROW_BLOCK, D_MODEL), lambda i: (i, 0)),
                  pl.BlockSpec((1, D_MODEL, D_IN), lambda i: (layer, 0, 0))],
        out_specs=pl.BlockSpec((ROW_BLOCK, D_IN), lambda i: (i, 0)),
        out_shape=jax.ShapeDtypeStruct((T_ALL, D_IN), F32),
        compiler_params=_cparams("parallel"),
        name="in_proj",
    )(x, w_in_bf)


def _rotate(x, cos, sin_signed):
    return x * cos + pltpu.roll(x, HD_RET // 2, 1) * sin_signed


def _group_norm_gate(o, gain, g):
    mu = jnp.mean(o, -1, keepdims=True)
    var = jnp.mean(jnp.square(o - mu), -1, keepdims=True)
    return (o - mu) * lax.rsqrt(var + GN_EPS) * gain * (g * jax.nn.sigmoid(g))


def _layer_norm(y, g, b):
    mu = jnp.mean(y, -1, keepdims=True)
    var = jnp.mean(jnp.square(y - mu), -1, keepdims=True)
    return (y - mu) * lax.rsqrt(var + LN_EPS) * g + b


def _ret_prompt_kernel(q_ref, k_ref, v_ref, g_ref, cos_ref, sin_ref, intra_ref, qdec_ref, kdec_ref,
                       cdec_ref, gain_ref, o_ref, s_ref):
    @pl.when(pl.program_id(1) == 0)
    def _():
        s_ref[...] = jnp.zeros_like(s_ref)

    cos = cos_ref[...]
    sin = sin_ref[...]
    nt = (((1,), (1,)), ((), ()))
    tn = (((0,), (0,)), ((), ()))
    for h in range(H_RET):
        sl = slice(h * HD_RET, (h + 1) * HD_RET)
        q = _rotate(q_ref[:, sl], cos, sin)
        k = _rotate(k_ref[:, sl], cos, sin) * (HD_RET ** -0.5)
        v = v_ref[:, sl].astype(BF16)
        s = s_ref[0, 0, h]
        att = lax.dot_general(q.astype(BF16), k.astype(BF16), nt, preferred_element_type=F32) * intra_ref[h]
        o = (jnp.dot(att.astype(BF16), v, preferred_element_type=F32)
             + jnp.dot((q * qdec_ref[h]).astype(BF16), s.astype(BF16), preferred_element_type=F32))
        s_ref[0, 0, h] = s * cdec_ref[h] + lax.dot_general((k * kdec_ref[h]).astype(BF16), v, tn,
                                                          preferred_element_type=F32)
        o_ref[:, sl] = _group_norm_gate(o, gain_ref[0, :, sl], g_ref[:, sl])


def _ret_prompt(h, tables, gain, s_prev, layer):
    cos, sin, intra, qdec, kdec, cdec = tables
    nc = SEQ // RET_CHUNK

    def col(j):
        return pl.BlockSpec((RET_CHUNK, RET_W), lambda b, c: (b * nc + c, j))

    def whole(a):
        return pl.BlockSpec(a.shape, lambda b, c: (0,) * a.ndim)

    in_specs = [col(0), col(1), col(2), col(3),
                pl.BlockSpec((RET_CHUNK, HD_RET), lambda b, c: (c, 0)),
                pl.BlockSpec((RET_CHUNK, HD_RET), lambda b, c: (c, 0)),
                whole(intra), whole(qdec), whole(kdec), whole(cdec),
                pl.BlockSpec((1, 1, RET_W), lambda b, c: (layer, 0, 0))]
    args = [h, h, h, h, cos, sin, intra, qdec, kdec, cdec, gain]
    aliases = {}
    if s_prev is not None:
        in_specs.append(pl.BlockSpec(memory_space=pl.ANY))
        args.append(s_prev)
        aliases = {len(args) - 1: 1}

    def body(*refs):
        _ret_prompt_kernel(*refs[:11], *refs[-2:])

    return pl.pallas_call(
        body,
        grid=(BATCH, nc),
        in_specs=in_specs,
        out_specs=[pl.BlockSpec((RET_CHUNK, RET_W), lambda b, c: (b * nc + c, 0)),
                   pl.BlockSpec((1, 1, H_RET, HD_RET, HD_RET), lambda b, c: (layer, b, 0, 0, 0))],
        out_shape=[jax.ShapeDtypeStruct((T_ALL, RET_W), F32),
                   jax.ShapeDtypeStruct((DEPTH, BATCH, H_RET, HD_RET, HD_RET), F32)],
        input_output_aliases=aliases,
        compiler_params=_cparams("parallel", "arbitrary"),
        name="ret_prompt",
    )(*args)


def _swa_prompt_kernel(sink_ref, q_ref, kvc_ref, kvp_ref, bias_ref, o_ref):
    c = pl.program_id(1)
    kvc = kvc_ref[...].astype(BF16)
    kvp = kvp_ref[...].astype(BF16)
    row = lax.broadcasted_iota(jnp.int32, (WINDOW, 2 * WINDOW), 0)
    colj = lax.broadcasted_iota(jnp.int32, (WINDOW, 2 * WINDOW), 1)
    valid = (colj > row) & (colj <= row + WINDOW) & jnp.logical_or(c > 0, colj >= WINDOW)
    nt = (((1,), (1,)), ((), ()))
    for kv in range(KV_SWA):
        ks = slice(kv * HD_SWA, (kv + 1) * HD_SWA)
        vs = slice(SWA_KV_W + kv * HD_SWA, SWA_KV_W + (kv + 1) * HD_SWA)
        kcat = jnp.concatenate([kvp[:, ks], kvc[:, ks]], axis=0)
        vcat = jnp.concatenate([kvp[:, vs], kvc[:, vs]], axis=0)
        for g in range(G_SWA):
            hd = kv * G_SWA + g
            hs = slice(hd * HD_SWA, (hd + 1) * HD_SWA)
            q = q_ref[:, hs].astype(BF16)
            logits = lax.dot_general(q, kcat, nt, preferred_element_type=F32) * (HD_SWA ** -0.5) + bias_ref[hd]
            logits = jnp.where(valid, logits, -jnp.inf)
            sink = sink_ref[hd]
            m = jnp.maximum(jnp.max(logits, -1, keepdims=True), sink)
            p = jnp.exp(logits - m)
            w = p / (jnp.sum(p, -1, keepdims=True) + jnp.exp(sink - m))
            o_ref[:, hs] = jnp.dot(w.astype(BF16), vcat, preferred_element_type=F32)


def _swa_prompt(h, sinks_l, bias):
    nb = SEQ // WINDOW
    qcol = 4 * RET_W // SWA_Q_W
    kvcol = (4 * RET_W + SWA_Q_W) // (2 * SWA_KV_W)
    return pl.pallas_call(
        _swa_prompt_kernel,
        grid=(BATCH, nb),
        in_specs=[pl.BlockSpec(memory_space=pltpu.SMEM),
                  pl.BlockSpec((WINDOW, SWA_Q_W), lambda b, c: (b * nb + c, qcol)),
                  pl.BlockSpec((WINDOW, 2 * SWA_KV_W), lambda b, c: (b * nb + c, kvcol)),
                  pl.BlockSpec((WINDOW, 2 * SWA_KV_W), lambda b, c: (b * nb + jnp.maximum(c - 1, 0), kvcol)),
                  pl.BlockSpec((H_SWA, WINDOW, 2 * WINDOW), lambda b, c: (0, 0, 0))],
        out_specs=pl.BlockSpec((WINDOW, SWA_Q_W), lambda b, c: (b * nb + c, 0)),
        out_shape=jax.ShapeDtypeStruct((T_ALL, SWA_Q_W), F32),
        compiler_params=_cparams("parallel", "arbitrary"),
        name="swa_prompt",
    )(sinks_l, h, h, h, bias)


def _ret_sample_kernel(dec_ref, q_ref, k_ref, v_ref, g_ref, cos_ref, sin_ref, gain_ref, st_ref, o_ref, so_ref):
    cos = cos_ref[...]
    sin = sin_ref[...]
    for h in range(H_RET):
        sl = slice(h * HD_RET, (h + 1) * HD_RET)
        intra, qdec, kdec, cdec = dec_ref[h, 0], dec_ref[h, 1], dec_ref[h, 2], dec_ref[h, 3]
        q = _rotate(q_ref[:, sl], cos, sin)
        k = _rotate(k_ref[:, sl], cos, sin) * (HD_RET ** -0.5)
        v = v_ref[:, sl]
        att = jnp.sum(q * k, -1, keepdims=True) * intra
        q_col = (q * qdec).T
        k_col = (k * kdec).T
        rows = []
        for b in range(SAMPLE_BLOCK):
            s = st_ref[0, b, h]
            v_row = v[b:b + 1, :]
            rows.append(att[b:b + 1, :] * v_row + jnp.sum(q_col[:, b:b + 1] * s, axis=0, keepdims=True))
            so_ref[0, b, h] = s * cdec + k_col[:, b:b + 1] * v_row
        o = jnp.concatenate(rows, axis=0)
        o_ref[:, sl] = _group_norm_gate(o, gain_ref[0, :, sl], g_ref[:, sl])


def _ret_sample(h, dec_s, cos_s, sin_s, gain, state_ret, o_r, st_prev, layer):
    base = T_PROMPT // SAMPLE_BLOCK

    def col(j):
        return pl.BlockSpec((SAMPLE_BLOCK, RET_W), lambda i: (base + i, j))

    in_specs = [pl.BlockSpec(memory_space=pltpu.SMEM),
                col(0), col(1), col(2), col(3),
                pl.BlockSpec((1, HD_RET), lambda i: (0, 0)),
                pl.BlockSpec((1, HD_RET), lambda i: (0, 0)),
                pl.BlockSpec((1, 1, RET_W), lambda i: (layer, 0, 0)),
                pl.BlockSpec((1, SAMPLE_BLOCK, H_RET, HD_RET, HD_RET), lambda i: (layer, i, 0, 0, 0)),
                pl.BlockSpec(memory_space=pl.ANY)]
    args = [dec_s, h, h, h, h, cos_s, sin_s, gain, state_ret, o_r]
    aliases = {9: 0}
    if st_prev is not None:
        in_specs.append(pl.BlockSpec(memory_space=pl.ANY))
        args.append(st_prev)
        aliases[10] = 1

    def body(*refs):
        _ret_sample_kernel(*refs[:9], *refs[-2:])

    return pl.pallas_call(
        body,
        grid=(DEC_BATCH // SAMPLE_BLOCK,),
        in_specs=in_specs,
        out_specs=[pl.BlockSpec((SAMPLE_BLOCK, RET_W), lambda i: (base + i, 0)),
                   pl.BlockSpec((1, SAMPLE_BLOCK, H_RET, HD_RET, HD_RET), lambda i: (layer, i, 0, 0, 0))],
        out_shape=[jax.ShapeDtypeStruct((T_ALL, RET_W), F32),
                   jax.ShapeDtypeStruct((DEPTH, DEC_BATCH, H_RET, HD_RET, HD_RET), F32)],
        input_output_aliases=aliases,
        compiler_params=_cparams("parallel"),
        name="ret_sample",
    )(*args)


def _swa_sample_kernel(qm_ref, kn_ref, vn_ref, kc_ref, vc_ref, bias_ref, biasn_ref, sink_ref,
                       o_ref, ko_ref, vo_ref):
    qm = qm_ref[...]
    kn = kn_ref[...]
    vn = vn_ref[...]
    kc = kc_ref[0]
    vc = vc_ref[0]
    scale = HD_SWA ** -0.5
    logits = jnp.einsum('bhd,bjd->bhj', qm.astype(BF16), kc.astype(BF16),
                        preferred_element_type=F32) * scale + bias_ref[...][None]
    j = lax.broadcasted_iota(jnp.int32, logits.shape, 2)
    logits = jnp.where(j >= 1, logits, -jnp.inf)
    ln = jnp.sum(qm * kn, -1, keepdims=True) * scale + biasn_ref[...][None]
    sink = sink_ref[...][None]
    m = jnp.maximum(jnp.maximum(jnp.max(logits, -1, keepdims=True), ln), sink)
    p = jnp.exp(logits - m)
    pn = jnp.exp(ln - m)
    denom = jnp.sum(p, -1, keepdims=True) + pn + jnp.exp(sink - m)
    w = p / denom
    o_ref[...] = (jnp.einsum('bhj,bjd->bhd', w.astype(BF16), vc.astype(BF16), preferred_element_type=F32)
                  + (pn / denom) * vn)
    ko_ref[0, :, 0:WINDOW - 1, :] = kc_ref[0, :, 1:WINDOW, :]
    ko_ref[0, :, WINDOW - 1:WINDOW, :] = kn
    vo_ref[0, :, 0:WINDOW - 1, :] = vc_ref[0, :, 1:WINDOW, :]
    vo_ref[0, :, WINDOW - 1:WINDOW, :] = vn


def _swa_sample(qm, kn, vn, cache_k, cache_v, bias_c, bias_n, sink_col, ko_prev, vo_prev, layer):
    sb = SAMPLE_BLOCK
    cache_spec = pl.BlockSpec((1, sb, WINDOW, 2 * HD_SWA), lambda i: (layer, i, 0, 0))
    in_specs = [pl.BlockSpec((sb, H_SWA, 2 * HD_SWA), lambda i: (i, 0, 0)),
                pl.BlockSpec((sb, 1, 2 * HD_SWA), lambda i: (i, 0, 0)),
                pl.BlockSpec((sb, 1, 2 * HD_SWA), lambda i: (i, 0, 0)),
                cache_spec, cache_spec,
                pl.BlockSpec((H_SWA, WINDOW), lambda i: (0, 0)),
                pl.BlockSpec((H_SWA, 1), lambda i: (0, 0)),
                pl.BlockSpec((H_SWA, 1), lambda i: (0, 0))]
    args = [qm, kn, vn, cache_k, cache_v, bias_c, bias_n, sink_col]
    aliases = {}
    if ko_prev is not None:
        in_specs += [pl.BlockSpec(memory_space=pl.ANY), pl.BlockSpec(memory_space=pl.ANY)]
        args += [ko_prev, vo_prev]
        aliases = {8: 1, 9: 2}

    def body(*refs):
        _swa_sample_kernel(*refs[:8], *refs[-3:])

    cshape = jax.ShapeDtypeStruct((DEPTH, DEC_BATCH, WINDOW, 2 * HD_SWA), F32)
    return pl.pallas_call(
        body,
        grid=(DEC_BATCH // sb,),
        in_specs=in_specs,
        out_specs=[pl.BlockSpec((sb, H_SWA, 2 * HD_SWA), lambda i: (i, 0, 0)), cache_spec, cache_spec],
        out_shape=[jax.ShapeDtypeStruct((DEC_BATCH, H_SWA, 2 * HD_SWA), F32), cshape, cshape],
        input_output_aliases=aliases,
        compiler_params=_cparams("parallel"),
        name="swa_sample",
    )(*args)


def _out_proj_kernel(or_ref, os_ref, x_ref, wr_ref, ws_ref, g_ref, b_ref, rw_ref, rb_ref,
                     x1_ref, idx_ref, gate_ref):
    mix = (jnp.dot(or_ref[...].astype(BF16), wr_ref[0], preferred_element_type=F32)
           + jnp.dot(os_ref[...].astype(BF16), ws_ref[0], preferred_element_type=F32))
    x1 = _layer_norm(DEEPNORM_ALPHA * x_ref[...] + mix, g_ref[0], b_ref[0])
    x1_ref[...] = x1
    logits = jnp.dot(x1.astype(BF16), rw_ref[0], preferred_element_type=F32) + rb_ref[0]
    lane = lax.broadcasted_iota(jnp.int32, logits.shape, 1)
    vals = []
    for kk in range(TOP_K):
        m = jnp.max(logits, -1, keepdims=True)
        idx = jnp.min(jnp.where(logits == m, lane, N_EXPERTS), -1, keepdims=True)
        vals.append(m)
        idx_ref[:, kk:kk + 1] = idx
        logits = jnp.where(lane == idx, -jnp.inf, logits)
    es = [jnp.exp(v - vals[0]) for v in vals]
    tot = es[0] + es[1] + es[2] + es[3]
    for kk in range(TOP_K):
        gate_ref[:, kk:kk + 1] = es[kk] / tot


def _out_proj(o_r, o_s, x, w_out_bf, ln_g, ln_b, router_w_bf, router_b, layer):
    def rows(w):
        return pl.BlockSpec((ROW_BLOCK, w), lambda i: (i, 0))

    def vec(w):
        return pl.BlockSpec((1, 1, w), lambda i: (layer, 0, 0))

    return pl.pallas_call(
        _out_proj_kernel,
        grid=(T_ALL // ROW_BLOCK,),
        in_specs=[rows(RET_W), rows(SWA_Q_W), rows(D_MODEL),
                  pl.BlockSpec((1, RET_W, D_MODEL), lambda i: (layer, 0, 0)),
                  pl.BlockSpec((1, SWA_Q_W, D_MODEL), lambda i: (layer, 1, 0)),
                  vec(D_MODEL), vec(D_MODEL),
                  pl.BlockSpec((1, D_MODEL, N_EXPERTS), lambda i: (layer, 0, 0)),
                  vec(N_EXPERTS)],
        out_specs=[rows(D_MODEL), rows(TOP_K), rows(TOP_K)],
        out_shape=[jax.ShapeDtypeStruct((T_ALL, D_MODEL), F32),
                   jax.ShapeDtypeStruct((T_ALL, TOP_K), jnp.int32),
                   jax.ShapeDtypeStruct((T_ALL, TOP_K), F32)],
        compiler_params=_cparams("parallel"),
        name="out_proj",
    )(o_r, o_s, x, w_out_bf, w_out_bf, ln_g, ln_b, router_w_bf, router_b)


def _moe_kernel(be_ref, first_ref, nused_ref, xs_ref, gate_ref, wgu_ref, bgu_ref, wd_ref, bd_ref,
                ys_ref, wgu_bf, wd_bf):
    i = pl.program_id(0)

    @pl.when(first_ref[i] == 1)
    def _():
        wgu_bf[...] = wgu_ref[0, 0].astype(BF16)
        wd_bf[...] = wd_ref[0, 0].astype(BF16)

    @pl.when(i < nused_ref[0])
    def _():
        gu = jnp.dot(xs_ref[...], wgu_bf[...], preferred_element_type=F32) + bgu_ref[0, 0]
        glu = jnp.minimum(gu[:, :D_FF], SWIGLU_LIMIT)
        lin = jnp.clip(gu[:, D_FF:], -SWIGLU_LIMIT, SWIGLU_LIMIT)
        act = glu * jax.nn.sigmoid(SWIGLU_ALPHA * glu) * (lin + 1.0)
        y = jnp.dot(act.astype(BF16), wd_bf[...], preferred_element_type=F32) + bd_ref[0, 0]
        ys_ref[...] = y * gate_ref[...]


def _moe(block_expert, first, n_used, xs, slot_gate, w_gate_up, b_gate_up, w_down, b_down, layer):
    def blk(i, be, fr, nu):
        return (jnp.minimum(i, nu[0] - 1), 0)

    def wsel(i, be, fr, nu):
        return (layer, be[i], 0, 0)

    return pl.pallas_call(
        _moe_kernel,
        grid_spec=pltpu.PrefetchScalarGridSpec(
            num_scalar_prefetch=3,
            grid=(MOE_BLOCKS,),
            in_specs=[pl.BlockSpec((MOE_ROWS, D_MODEL), blk),
                      pl.BlockSpec((MOE_ROWS, 1), blk),
                      pl.BlockSpec((1, 1, D_MODEL, 2 * D_FF), wsel),
                      pl.BlockSpec((1, 1, 1, 2 * D_FF), wsel),
                      pl.BlockSpec((1, 1, D_FF, D_MODEL), wsel),
                      pl.BlockSpec((1, 1, 1, D_MODEL), wsel)],
            out_specs=pl.BlockSpec((MOE_ROWS, D_MODEL), blk),
            scratch_shapes=[pltpu.VMEM((D_MODEL, 2 * D_FF), BF16), pltpu.VMEM((D_FF, D_MODEL), BF16)]),
        out_shape=jax.ShapeDtypeStruct((MOE_SLOTS, D_MODEL), F32),
        compiler_params=_cparams("arbitrary"),
        name="moe_experts",
    )(block_expert, first, n_used, xs, slot_gate, w_gate_up, b_gate_up, w_down, b_down)


def _route(top_idx, gate):
    expert = top_idx.reshape(N_ASSIGN)
    onehot = (expert[:, None] == jnp.arange(N_EXPERTS, dtype=jnp.int32)[None, :]).astype(jnp.int32)
    csum = jnp.cumsum(onehot, axis=0)
    rank = jnp.sum((csum - onehot) * onehot, axis=1)
    counts = csum[-1]
    padded = (counts + MOE_ROWS - 1) // MOE_ROWS * MOE_ROWS
    pad_end = jnp.cumsum(padded)
    pad_start = pad_end - padded
    pos = pad_start[expert] + rank
    token = jnp.repeat(jnp.arange(T_ALL, dtype=jnp.int32), TOP_K)
    slot_token = jnp.zeros((MOE_SLOTS,), jnp.int32).at[pos].set(token)
    slot_gate = jnp.zeros((MOE_SLOTS,), F32).at[pos].set(gate.reshape(N_ASSIGN))
    n_used = (pad_end[-1] // MOE_ROWS).astype(jnp.int32)
    blk = jnp.minimum(jnp.arange(MOE_BLOCKS, dtype=jnp.int32), n_used - 1)
    block_expert = jnp.minimum(jnp.searchsorted(pad_end, blk * MOE_ROWS, side='right'),
                               N_EXPERTS - 1).astype(jnp.int32)
    first = jnp.concatenate([jnp.ones((1,), jnp.int32),
                             (block_expert[1:] != block_expert[:-1]).astype(jnp.int32)])
    return pos.reshape(T_ALL, TOP_K), slot_token, slot_gate.reshape(MOE_SLOTS, 1), block_expert, first, n_used.reshape(1)


def _ffn_norm_kernel(x_ref, f_ref, g_ref, b_ref, o_ref):
    o_ref[...] = _layer_norm(DEEPNORM_ALPHA * x_ref[...] + f_ref[...], g_ref[0], b_ref[0])


def _ffn_norm(x1, ffn, ln_g, ln_b, layer):
    rows = pl.BlockSpec((ROW_BLOCK, D_MODEL), lambda i: (i, 0))
    vec = pl.BlockSpec((1, 1, D_MODEL), lambda i: (layer, 0, 0))
    return pl.pallas_call(
        _ffn_norm_kernel,
        grid=(T_ALL // ROW_BLOCK,),
        in_specs=[rows, rows, vec, vec],
        out_specs=rows,
        out_shape=jax.ShapeDtypeStruct((T_ALL, D_MODEL), F32),
        compiler_params=_cparams("parallel"),
        name="ffn_norm",
    )(x1, ffn, ln_g, ln_b)


def _rope_tables(pos):
    half = HD_RET // 2
    inv = ROPE_BASE ** (-jnp.arange(half, dtype=F32) / half)
    ang = pos.astype(F32)[:, None] * inv[None, :]
    cos, sin = jnp.cos(ang), jnp.sin(ang)
    return jnp.concatenate([cos, cos], -1), jnp.concatenate([-sin, sin], -1)


def _decay_tables(c):
    log_g = jnp.log1p(-jnp.exp2(-5.0 - jnp.arange(H_RET, dtype=F32)))
    idx = jnp.arange(c, dtype=F32)
    diff = idx[:, None] - idx[None, :]
    intra = jnp.where(diff[None] >= 0, jnp.exp(jnp.maximum(diff, 0.0)[None] * log_g[:, None, None]), 0.0)
    q_dec = jnp.exp((idx + 1.0)[None] * log_g[:, None])
    k_dec = jnp.exp((c - 1.0 - idx)[None] * log_g[:, None])
    c_dec = jnp.exp(c * log_g)
    return intra, q_dec, k_dec, c_dec


def _t5_bucket(dist):
    n = jnp.maximum(dist, 0)
    max_exact = NUM_BUCKETS // 2
    nf = jnp.maximum(n, 1).astype(F32)
    large = max_exact + (jnp.log(nf / max_exact) / math.log(REL_MAX_DIST / max_exact)
                         * (NUM_BUCKETS - max_exact)).astype(jnp.int32)
    large = jnp.minimum(large, NUM_BUCKETS - 1)
    return jnp.where(n < max_exact, n, large)


def kernel(x_prompt, x_sample, state_ret, cache_swa_k, cache_swa_v, w_in, w_out, ret_gn_gain, swa_sinks,
           rel_bias, ln1_g, ln1_b, ln2_g, ln2_b, router_w, router_b, w_gate_up, b_gate_up, w_down, b_down):
    x = jnp.concatenate([x_prompt.reshape(T_PROMPT, D_MODEL), x_sample.reshape(DEC_BATCH, D_MODEL)], axis=0)
    w_in_bf = w_in.astype(BF16)
    w_out_bf = w_out.astype(BF16)
    router_w_bf = router_w.astype(BF16)
    ret_gn_gain, ln1_g, ln1_b, ln2_g, ln2_b, router_b = (
        a.reshape(DEPTH, 1, a.shape[-1]) for a in (ret_gn_gain, ln1_g, ln1_b, ln2_g, ln2_b, router_b))
    b_gu4 = b_gate_up.reshape(DEPTH, N_EXPERTS, 1, 2 * D_FF)
    b_d4 = b_down.reshape(DEPTH, N_EXPERTS, 1, D_MODEL)
    cache_k = cache_swa_k.reshape(DEPTH, DEC_BATCH, WINDOW, 2 * HD_SWA)
    cache_v = cache_swa_v.reshape(DEPTH, DEC_BATCH, WINDOW, 2 * HD_SWA)

    cos_p, sin_p = _rope_tables(jnp.arange(SEQ, dtype=jnp.int32))
    cos_s, sin_s = _rope_tables(PAST_LEN + jnp.arange(1, dtype=jnp.int32))
    intra, q_dec, k_dec, c_dec = _decay_tables(RET_CHUNK)
    ones = jnp.ones((1, 1, HD_RET), F32)
    tables_p = (cos_p, sin_p, intra, q_dec[:, :, None] * ones, k_dec[:, :, None] * ones, c_dec[:, None, None] * ones)
    intra1, q_dec1, k_dec1, c_dec1 = _decay_tables(1)
    dec_s = jnp.stack([intra1[:, 0, 0], q_dec1[:, 0], k_dec1[:, 0], c_dec1], axis=1)
    i = jnp.arange(WINDOW)[:, None]
    j = jnp.arange(2 * WINDOW)[None, :]
    bias_p = jnp.moveaxis(rel_bias[_t5_bucket(WINDOW + i - j)], -1, 0).astype(F32)
    js = jnp.arange(WINDOW + 1)
    bias_s = jnp.moveaxis(rel_bias[_t5_bucket(WINDOW - js)], -1, 0).astype(F32)
    bias_c, bias_n = bias_s[:, :WINDOW], bias_s[:, WINDOW:]
    kv_of_head = jnp.arange(H_SWA) // G_SWA
    head_mask = (kv_of_head[:, None] == jnp.arange(KV_SWA)[None, :]).astype(F32)

    sp = ss = ko = vo = None
    kcp, vcp = [], []
    for l in range(DEPTH):
        h = _in_proj(x, w_in_bf, l)
        o_r, sp = _ret_prompt(h, tables_p, ret_gn_gain, sp, l)
        o_s = _swa_prompt(h, swa_sinks[l], bias_p)
        o_r, ss = _ret_sample(h, dec_s, cos_s, sin_s, ret_gn_gain, state_ret, o_r, ss, l)
        hs = h[T_PROMPT:, 4 * RET_W:]
        q8 = hs[:, :SWA_Q_W].reshape(DEC_BATCH, H_SWA, HD_SWA)
        qm = (q8[:, :, None, :] * head_mask[None, :, :, None]).reshape(DEC_BATCH, H_SWA, 2 * HD_SWA)
        kn = hs[:, SWA_Q_W:SWA_Q_W + SWA_KV_W].reshape(DEC_BATCH, 1, SWA_KV_W)
        vn = hs[:, SWA_Q_W + SWA_KV_W:].reshape(DEC_BATCH, 1, SWA_KV_W)
        osm, ko, vo = _swa_sample(qm, kn, vn, cache_k, cache_v, bias_c, bias_n, swa_sinks[l][:, None], ko, vo, l)
        os_s = jnp.sum(osm.reshape(DEC_BATCH, H_SWA, KV_SWA, HD_SWA) * head_mask[None, :, :, None], axis=2)
        o_s = lax.dynamic_update_slice(o_s, os_s.reshape(DEC_BATCH, SWA_Q_W), (T_PROMPT, 0))
        hp = h[:T_PROMPT].reshape(BATCH, SEQ, D_IN)
        kcp.append(hp[:, SEQ - WINDOW:, 4 * RET_W + SWA_Q_W:4 * RET_W + SWA_Q_W + SWA_KV_W])
        vcp.append(hp[:, SEQ - WINDOW:, 4 * RET_W + SWA_Q_W + SWA_KV_W:])

        x1, top_idx, gate = _out_proj(o_r, o_s, x, w_out_bf, ln1_g, ln1_b, router_w_bf, router_b, l)
        pos, slot_token, slot_gate, block_expert, first, n_used = _route(top_idx, gate)
        xs = x1[slot_token].astype(BF16)
        ys = _moe(block_expert, first, n_used, xs, slot_gate, w_gate_up, b_gu4, w_down, b_d4, l)
        ffn = jnp.sum(ys[pos], axis=1)
        x = _ffn_norm(x1, ffn, ln2_g, ln2_b, l)

    yp = x[:T_PROMPT].reshape(BATCH, SEQ, D_MODEL)
    ys_out = x[T_PROMPT:].reshape(DEC_BATCH, 1, D_MODEL)
    kcp = jnp.stack(kcp).reshape(DEPTH, BATCH, WINDOW, KV_SWA, HD_SWA)
    vcp = jnp.stack(vcp).reshape(DEPTH, BATCH, WINDOW, KV_SWA, HD_SWA)
    return (yp, ys_out, sp, kcp, vcp, ss,
            ko.reshape(DEPTH, DEC_BATCH, WINDOW, KV_SWA, HD_SWA),
            vo.reshape(DEPTH, DEC_BATCH, WINDOW, KV_SWA, HD_SWA))
```

```python
import functools
import math

import jax
import jax.numpy as jnp
from jax import lax
from jax.experimental import pallas as pl
from jax.experimental.pallas import tpu as pltpu
from jax.experimental.pallas import tpu_sc as plsc

F32 = jnp.float32
BF16 = jnp.bfloat16

D_MODEL = 1024
BATCH = 8
SEQ = 2048
DEPTH = 4
DEC_BATCH = 128
PAST_LEN = 8192
H_RET = 4
HD_RET = 128
RET_CHUNK = 128
ROPE_BASE = 10000.0
H_SWA = 8
KV_SWA = 2
G_SWA = H_SWA // KV_SWA
HD_SWA = 64
WINDOW = 128
NUM_BUCKETS = 32
REL_MAX_DIST = 128
RET_W = H_RET * HD_RET
SWA_Q_W = H_SWA * HD_SWA
SWA_KV_W = KV_SWA * HD_SWA
D_IN = 4 * RET_W + SWA_Q_W + 2 * SWA_KV_W
N_EXPERTS = 32
TOP_K = 4
D_FF = D_MODEL
SWIGLU_LIMIT = 7.0
SWIGLU_ALPHA = 1.702
DEEPNORM_ALPHA = (2 * DEPTH) ** 0.25
LN_EPS = 1e-5
GN_EPS = 1e-5

T_PROMPT = BATCH * SEQ
T_ALL = T_PROMPT + DEC_BATCH
ROW_BLOCK = 384
SAMPLE_BLOCK = 8
MOE_ROWS = 256
N_ASSIGN = T_ALL * TOP_K
MOE_BLOCKS = N_ASSIGN // MOE_ROWS + N_EXPERTS
MOE_SLOTS = MOE_BLOCKS * MOE_ROWS
N_UNUSED = MOE_SLOTS - N_ASSIGN
SC_CORES = 2
SC_WORKERS = SC_CORES * 16
SC_DISPATCH_ROWS = 64
SC_COLLECT_ROWS = 48
FINAL_ROW_BLOCK = 512
VMEM_LIMIT = 56 * 1024 * 1024


def _cparams(*sem):
    return pltpu.CompilerParams(dimension_semantics=sem, vmem_limit_bytes=VMEM_LIMIT)


def _in_proj_kernel(x_ref, w_ref, h_ref):
    h_ref[...] = jnp.dot(x_ref[...].astype(BF16), w_ref[0], preferred_element_type=F32)


def _in_proj(x, w_in_bf, layer):
    return pl.pallas_call(
        _in_proj_kernel,
        grid=(T_ALL // ROW_BLOCK,),
        in_specs=[pl.BlockSpec((ROW_BLOCK, D_MODEL), lambda i: (i, 0)),
                  pl.BlockSpec((1, D_MODEL, D_IN), lambda i: (layer, 0, 0))],
        out_specs=pl.BlockSpec((ROW_BLOCK, D_IN), lambda i: (i, 0)),
        out_shape=jax.ShapeDtypeStruct((T_ALL, D_IN), F32),
        compiler_params=_cparams("parallel"),
        name="in_proj",
    )(x, w_in_bf)


def _rotate(x, cos, sin_signed):
    return x * cos + pltpu.roll(x, HD_RET // 2, 1) * sin_signed


def _group_norm_gate(o, gain, g):
    mu = jnp.mean(o, -1, keepdims=True)
    var = jnp.mean(jnp.square(o - mu), -1, keepdims=True)
    return (o - mu) * lax.rsqrt(var + GN_EPS) * gain * (g * jax.nn.sigmoid(g))


def _layer_norm(y, g, b):
    mu = jnp.mean(y, -1, keepdims=True)
    var = jnp.mean(jnp.square(y - mu), -1, keepdims=True)
    return (y - mu) * lax.rsqrt(var + LN_EPS) * g + b


def _ret_prompt_kernel(q_ref, k_ref, v_ref, g_ref, cos_ref, sin_ref, intra_ref, qdec_ref, kdec_ref,
                       cdec_ref, gain_ref, o_ref, s_ref):
    @pl.when(pl.program_id(1) == 0)
    def _():
        s_ref[...] = jnp.zeros_like(s_ref)

    cos = cos_ref[...]
    sin = sin_ref[...]
    nt = (((1,), (1,)), ((), ()))
    tn = (((0,), (0,)), ((), ()))
    for h in range(H_RET):
        sl = slice(h * HD_RET, (h + 1) * HD_RET)
        q = _rotate(q_ref[:, sl], cos, sin)
        k = _rotate(k_ref[:, sl], cos, sin) * (HD_RET ** -0.5)
        v = v_ref[:, sl].astype(BF16)
        s = s_ref[0, 0, h]
        att = lax.dot_general(q.astype(BF16), k.astype(BF16), nt, preferred_element_type=F32) * intra_ref[h]
        o = (jnp.dot(att.astype(BF16), v, preferred_element_type=F32)
             + jnp.dot((q * qdec_ref[h]).astype(BF16), s.astype(BF16), preferred_element_type=F32))
        s_ref[0, 0, h] = s * cdec_ref[h] + lax.dot_general((k * kdec_ref[h]).astype(BF16), v, tn,
                                                          preferred_element_type=F32)
        o_ref[:, sl] = _group_norm_gate(o, gain_ref[0, :, sl], g_ref[:, sl])


def _ret_prompt(h, tables, gain, s_prev, layer):
    cos, sin, intra, qdec, kdec, cdec = tables
    nc = SEQ // RET_CHUNK

    def col(j):
        return pl.BlockSpec((RET_CHUNK, RET_W), lambda b, c: (b * nc + c, j))

    def whole(a):
        return pl.BlockSpec(a.shape, lambda b, c: (0,) * a.ndim)

    in_specs = [col(0), col(1), col(2), col(3),
                pl.BlockSpec((RET_CHUNK, HD_RET), lambda b, c: (c, 0)),
                pl.BlockSpec((RET_CHUNK, HD_RET), lambda b, c: (c, 0)),
                whole(intra), whole(qdec), whole(kdec), whole(cdec),
                pl.BlockSpec((1, 1, RET_W), lambda b, c: (layer, 0, 0))]
    args = [h, h, h, h, cos, sin, intra, qdec, kdec, cdec, gain]
    aliases = {}
    if s_prev is not None:
        in_specs.append(pl.BlockSpec(memory_space=pl.ANY))
        args.append(s_prev)
        aliases = {len(args) - 1: 1}

    def body(*refs):
        _ret_prompt_kernel(*refs[:11], *refs[-2:])

    return pl.pallas_call(
        body,
        grid=(BATCH, nc),
        in_specs=in_specs,
        out_specs=[pl.BlockSpec((RET_CHUNK, RET_W), lambda b, c: (b * nc + c, 0)),
                   pl.BlockSpec((1, 1, H_RET, HD_RET, HD_RET), lambda b, c: (layer, b, 0, 0, 0))],
        out_shape=[jax.ShapeDtypeStruct((T_ALL, RET_W), F32),
                   jax.ShapeDtypeStruct((DEPTH, BATCH, H_RET, HD_RET, HD_RET), F32)],
        input_output_aliases=aliases,
        compiler_params=_cparams("parallel", "arbitrary"),
        name="ret_prompt",
    )(*args)


def _swa_prompt_kernel(sink_ref, q_ref, kvc_ref, kvp_ref, bias_ref, o_ref, tail_ref):
    c = pl.program_id(1)

    @pl.when(c == pl.num_programs(1) - 1)
    def _():
        tail_ref[0, 0] = kvc_ref[...]

    kvc = kvc_ref[...].astype(BF16)
    kvp = kvp_ref[...].astype(BF16)
    row = lax.broadcasted_iota(jnp.int32, (WINDOW, 2 * WINDOW), 0)
    colj = lax.broadcasted_iota(jnp.int32, (WINDOW, 2 * WINDOW), 1)
    valid = (colj > row) & (colj <= row + WINDOW) & jnp.logical_or(c > 0, colj >= WINDOW)
    nt = (((1,), (1,)), ((), ()))
    for kv in range(KV_SWA):
        ks = slice(kv * HD_SWA, (kv + 1) * HD_SWA)
        vs = slice(SWA_KV_W + kv * HD_SWA, SWA_KV_W + (kv + 1) * HD_SWA)
        kcat = jnp.concatenate([kvp[:, ks], kvc[:, ks]], axis=0)
        vcat = jnp.concatenate([kvp[:, vs], kvc[:, vs]], axis=0)
        for g in range(G_SWA):
            hd = kv * G_SWA + g
            hs = slice(hd * HD_SWA, (hd + 1) * HD_SWA)
            q = q_ref[:, hs].astype(BF16)
            logits = lax.dot_general(q, kcat, nt, preferred_element_type=F32) * (HD_SWA ** -0.5) + bias_ref[hd]
            logits = jnp.where(valid, logits, -jnp.inf)
            sink = sink_ref[hd]
            m = jnp.maximum(jnp.max(logits, -1, keepdims=True), sink)
            p = jnp.exp(logits - m)
            w = p / (jnp.sum(p, -1, keepdims=True) + jnp.exp(sink - m))
            o_ref[:, hs] = jnp.dot(w.astype(BF16), vcat, preferred_element_type=F32)


def _swa_prompt(h, sinks_l, bias, tail_prev, layer):
    nb = SEQ // WINDOW
    qcol = 4 * RET_W // SWA_Q_W
    kvcol = (4 * RET_W + SWA_Q_W) // (2 * SWA_KV_W)
    in_specs = [pl.BlockSpec(memory_space=pltpu.SMEM),
                pl.BlockSpec((WINDOW, SWA_Q_W), lambda b, c: (b * nb + c, qcol)),
                pl.BlockSpec((WINDOW, 2 * SWA_KV_W), lambda b, c: (b * nb + c, kvcol)),
                pl.BlockSpec((WINDOW, 2 * SWA_KV_W), lambda b, c: (b * nb + jnp.maximum(c - 1, 0), kvcol)),
                pl.BlockSpec((H_SWA, WINDOW, 2 * WINDOW), lambda b, c: (0, 0, 0))]
    args = [sinks_l, h, h, h, bias]
    aliases = {}
    if tail_prev is not None:
        in_specs.append(pl.BlockSpec(memory_space=pl.ANY))
        args.append(tail_prev)
        aliases = {5: 1}

    def body(*refs):
        _swa_prompt_kernel(*refs[:5], *refs[-2:])

    return pl.pallas_call(
        body,
        grid=(BATCH, nb),
        in_specs=in_specs,
        out_specs=[pl.BlockSpec((WINDOW, SWA_Q_W), lambda b, c: (b * nb + c, 0)),
                   pl.BlockSpec((1, 1, WINDOW, 2 * SWA_KV_W), lambda b, c: (layer, b, 0, 0))],
        out_shape=[jax.ShapeDtypeStruct((T_ALL, SWA_Q_W), F32),
                   jax.ShapeDtypeStruct((DEPTH, BATCH, WINDOW, 2 * SWA_KV_W), F32)],
        input_output_aliases=aliases,
        compiler_params=_cparams("parallel", "arbitrary"),
        name="swa_prompt",
    )(*args)


def _ret_sample_kernel(dec_ref, q_ref, k_ref, v_ref, g_ref, cos_ref, sin_ref, gain_ref, st_ref, o_ref, so_ref):
    cos = cos_ref[...]
    sin = sin_ref[...]
    for h in range(H_RET):
        sl = slice(h * HD_RET, (h + 1) * HD_RET)
        intra, qdec, kdec, cdec = dec_ref[h, 0], dec_ref[h, 1], dec_ref[h, 2], dec_ref[h, 3]
        q = _rotate(q_ref[:, sl], cos, sin)
        k = _rotate(k_ref[:, sl], cos, sin) * (HD_RET ** -0.5)
        v = v_ref[:, sl]
        att = jnp.sum(q * k, -1, keepdims=True) * intra
        q_col = (q * qdec).T
        k_col = (k * kdec).T
        rows = []
        for b in range(SAMPLE_BLOCK):
            s = st_ref[0, b, h]
            v_row = v[b:b + 1, :]
            rows.append(att[b:b + 1, :] * v_row + jnp.sum(q_col[:, b:b + 1] * s, axis=0, keepdims=True))
            so_ref[0, b, h] = s * cdec + k_col[:, b:b + 1] * v_row
        o = jnp.concatenate(rows, axis=0)
        o_ref[:, sl] = _group_norm_gate(o, gain_ref[0, :, sl], g_ref[:, sl])


def _ret_sample(h, dec_s, cos_s, sin_s, gain, state_ret, o_r, st_prev, layer):
    base = T_PROMPT // SAMPLE_BLOCK

    def col(j):
        return pl.BlockSpec((SAMPLE_BLOCK, RET_W), lambda i: (base + i, j))

    in_specs = [pl.BlockSpec(memory_space=pltpu.SMEM),
                col(0), col(1), col(2), col(3),
                pl.BlockSpec((1, HD_RET), lambda i: (0, 0)),
                pl.BlockSpec((1, HD_RET), lambda i: (0, 0)),
                pl.BlockSpec((1, 1, RET_W), lambda i: (layer, 0, 0)),
                pl.BlockSpec((1, SAMPLE_BLOCK, H_RET, HD_RET, HD_RET), lambda i: (layer, i, 0, 0, 0)),
                pl.BlockSpec(memory_space=pl.ANY)]
    args = [dec_s, h, h, h, h, cos_s, sin_s, gain, state_ret, o_r]
    aliases = {9: 0}
    if st_prev is not None:
        in_specs.append(pl.BlockSpec(memory_space=pl.ANY))
        args.append(st_prev)
        aliases[10] = 1

    def body(*refs):
        _ret_sample_kernel(*refs[:9], *refs[-2:])

    return pl.pallas_call(
        body,
        grid=(DEC_BATCH // SAMPLE_BLOCK,),
        in_specs=in_specs,
        out_specs=[pl.BlockSpec((SAMPLE_BLOCK, RET_W), lambda i: (base + i, 0)),
                   pl.BlockSpec((1, SAMPLE_BLOCK, H_RET, HD_RET, HD_RET), lambda i: (layer, i, 0, 0, 0))],
        out_shape=[jax.ShapeDtypeStruct((T_ALL, RET_W), F32),
                   jax.ShapeDtypeStruct((DEPTH, DEC_BATCH, H_RET, HD_RET, HD_RET), F32)],
        input_output_aliases=aliases,
        compiler_params=_cparams("parallel"),
        name="ret_sample",
    )(*args)


def _swa_sample_kernel(qm_ref, kn_ref, vn_ref, kc_ref, vc_ref, bias_ref, biasn_ref, sink_ref,
                       o_ref, ko_ref, vo_ref):
    qm = qm_ref[...]
    kn = kn_ref[...]
    vn = vn_ref[...]
    kc = kc_ref[0]
    vc = vc_ref[0]
    scale = HD_SWA ** -0.5
    logits = jnp.einsum('bhd,bjd->bhj', qm.astype(BF16), kc.astype(BF16),
                        preferred_element_type=F32) * scale + bias_ref[...][None]
    j = lax.broadcasted_iota(jnp.int32, logits.shape, 2)
    logits = jnp.where(j >= 1, logits, -jnp.inf)
    ln = jnp.sum(qm * kn, -1, keepdims=True) * scale + biasn_ref[...][None]
    sink = sink_ref[...][None]
    m = jnp.maximum(jnp.maximum(jnp.max(logits, -1, keepdims=True), ln), sink)
    p = jnp.exp(logits - m)
    pn = jnp.exp(ln - m)
    denom = jnp.sum(p, -1, keepdims=True) + pn + jnp.exp(sink - m)
    w = p / denom
    o_ref[...] = (jnp.einsum('bhj,bjd->bhd', w.astype(BF16), vc.astype(BF16), preferred_element_type=F32)
                  + (pn / denom) * vn)
    ko_ref[0, :, 0:WINDOW - 1, :] = kc_ref[0, :, 1:WINDOW, :]
    ko_ref[0, :, WINDOW - 1:WINDOW, :] = kn
    vo_ref[0, :, 0:WINDOW - 1, :] = vc_ref[0, :, 1:WINDOW, :]
    vo_ref[0, :, WINDOW - 1:WINDOW, :] = vn


def _swa_sample(qm, kn, vn, cache_k, cache_v, bias_c, bias_n, sink_col, ko_prev, vo_prev, layer):
    sb = SAMPLE_BLOCK
    cache_spec = pl.BlockSpec((1, sb, WINDOW, 2 * HD_SWA), lambda i: (layer, i, 0, 0))
    in_specs = [pl.BlockSpec((sb, H_SWA, 2 * HD_SWA), lambda i: (i, 0, 0)),
                pl.BlockSpec((sb, 1, 2 * HD_SWA), lambda i: (i, 0, 0)),
                pl.BlockSpec((sb, 1, 2 * HD_SWA), lambda i: (i, 0, 0)),
                cache_spec, cache_spec,
                pl.BlockSpec((H_SWA, WINDOW), lambda i: (0, 0)),
                pl.BlockSpec((H_SWA, 1), lambda i: (0, 0)),
                pl.BlockSpec((H_SWA, 1), lambda i: (0, 0))]
    args = [qm, kn, vn, cache_k, cache_v, bias_c, bias_n, sink_col]
    aliases = {}
    if ko_prev is not None:
        in_specs += [pl.BlockSpec(memory_space=pl.ANY), pl.BlockSpec(memory_space=pl.ANY)]
        args += [ko_prev, vo_prev]
        aliases = {8: 1, 9: 2}

    def body(*refs):
        _swa_sample_kernel(*refs[:8], *refs[-3:])

    cshape = jax.ShapeDtypeStruct((DEPTH, DEC_BATCH, WINDOW, 2 * HD_SWA), F32)
    return pl.pallas_call(
        body,
        grid=(DEC_BATCH // sb,),
        in_specs=in_specs,
        out_specs=[pl.BlockSpec((sb, H_SWA, 2 * HD_SWA), lambda i: (i, 0, 0)), cache_spec, cache_spec],
        out_shape=[jax.ShapeDtypeStruct((DEC_BATCH, H_SWA, 2 * HD_SWA), F32), cshape, cshape],
        input_output_aliases=aliases,
        compiler_params=_cparams("parallel"),
        name="swa_sample",
    )(*args)


def _pack_bf16_pairs(x):
    w = x.shape[1] // 2
    lo = lax.bitcast_convert_type(x[:, :w].astype(BF16).astype(F32), jnp.uint32) >> 16
    hi = lax.bitcast_convert_type(x[:, w:].astype(BF16).astype(F32), jnp.uint32)
    return lo | hi


def _unpack_bf16_pairs(p):
    lo = lax.bitcast_convert_type(p << 16, F32).astype(BF16)
    hi = lax.bitcast_convert_type(p & jnp.uint32(0xFFFF0000), F32).astype(BF16)
    return lo, hi


def _out_proj_kernel(or_ref, os_ref, x_ref, wr_ref, ws_ref, g_ref, b_ref, rw_ref, rb_ref,
                     x1_ref, xpk_ref, idx_ref, gate_ref, rank_ref, cnt_ref):
    mix = (jnp.dot(or_ref[...].astype(BF16), wr_ref[0], preferred_element_type=F32)
           + jnp.dot(os_ref[...].astype(BF16), ws_ref[0], preferred_element_type=F32))
    x1 = _layer_norm(DEEPNORM_ALPHA * x_ref[...] + mix, g_ref[0], b_ref[0])
    x1_ref[...] = x1
    xpk_ref[...] = _pack_bf16_pairs(x1)
    logits = jnp.dot(x1.astype(BF16), rw_ref[0], preferred_element_type=F32) + rb_ref[0]
    lane = lax.broadcasted_iota(jnp.int32, logits.shape, 1)
    vals, hits = [], []
    for kk in range(TOP_K):
        m = jnp.max(logits, -1, keepdims=True)
        idx = jnp.min(jnp.where(logits == m, lane, N_EXPERTS), -1, keepdims=True)
        vals.append(m)
        hits.append(lane == idx)
        idx_ref[:, kk:kk + 1] = idx
        logits = jnp.where(lane == idx, -jnp.inf, logits)
    es = [jnp.exp(v - vals[0]) for v in vals]
    tot = es[0] + es[1] + es[2] + es[3]
    for kk in range(TOP_K):
        gate_ref[:, kk:kk + 1] = es[kk] / tot
    chosen = sum(h.astype(F32) for h in hits)
    rows = logits.shape[0]
    earlier = (lax.broadcasted_iota(jnp.int32, (rows, rows), 0)
               > lax.broadcasted_iota(jnp.int32, (rows, rows), 1)).astype(BF16)
    before = jnp.dot(earlier, chosen.astype(BF16), preferred_element_type=F32)
    for kk in range(TOP_K):
        rank_ref[:, kk:kk + 1] = jnp.sum(jnp.where(hits[kk], before, 0.0), -1, keepdims=True).astype(jnp.int32)
    cnt_ref[0] = jnp.sum(chosen, axis=0, keepdims=True).astype(jnp.int32)


def _out_proj(o_r, o_s, x, w_out_bf, ln_g, ln_b, router_w_bf, router_b, layer):
    def rows(w):
        return pl.BlockSpec((ROW_BLOCK, w), lambda i: (i, 0))

    def vec(w):
        return pl.BlockSpec((1, 1, w), lambda i: (layer, 0, 0))

    return pl.pallas_call(
        _out_proj_kernel,
        grid=(T_ALL // ROW_BLOCK,),
        in_specs=[rows(RET_W), rows(SWA_Q_W), rows(D_MODEL),
                  pl.BlockSpec((1, RET_W, D_MODEL), lambda i: (layer, 0, 0)),
                  pl.BlockSpec((1, SWA_Q_W, D_MODEL), lambda i: (layer, 1, 0)),
                  vec(D_MODEL), vec(D_MODEL),
                  pl.BlockSpec((1, D_MODEL, N_EXPERTS), lambda i: (layer, 0, 0)),
                  vec(N_EXPERTS)],
        out_specs=[rows(D_MODEL), rows(D_MODEL // 2), rows(TOP_K), rows(TOP_K), rows(TOP_K),
                   pl.BlockSpec((1, 1, N_EXPERTS), lambda i: (i, 0, 0))],
        out_shape=[jax.ShapeDtypeStruct((T_ALL, D_MODEL), F32),
                   jax.ShapeDtypeStruct((T_ALL, D_MODEL // 2), jnp.uint32),
                   jax.ShapeDtypeStruct((T_ALL, TOP_K), jnp.int32),
                   jax.ShapeDtypeStruct((T_ALL, TOP_K), F32),
                   jax.ShapeDtypeStruct((T_ALL, TOP_K), jnp.int32),
                   jax.ShapeDtypeStruct((T_ALL // ROW_BLOCK, 1, N_EXPERTS), jnp.int32)],
        compiler_params=_cparams("parallel"),
        name="out_proj",
    )(o_r, o_s, x, w_out_bf, w_out_bf, ln_g, ln_b, router_w_bf, router_b)


def _moe_kernel(be_ref, first_ref, nused_ref, xs_ref, wgu_ref, bgu_ref, wd_ref, bd_ref,
                ys_ref, wgu_bf, wd_bf):
    i = pl.program_id(0)

    @pl.when(first_ref[i] == 1)
    def _():
        wgu_bf[...] = wgu_ref[0, 0].astype(BF16)
        wd_bf[...] = wd_ref[0, 0].astype(BF16)

    @pl.when(i < nused_ref[0])
    def _():
        lo, hi = _unpack_bf16_pairs(xs_ref[...])
        half = D_MODEL // 2
        gu = (jnp.dot(lo, wgu_bf[:half, :], preferred_element_type=F32)
              + jnp.dot(hi, wgu_bf[half:, :], preferred_element_type=F32) + bgu_ref[0, 0])
        glu = jnp.minimum(gu[:, :D_FF], SWIGLU_LIMIT)
        lin = jnp.clip(gu[:, D_FF:], -SWIGLU_LIMIT, SWIGLU_LIMIT)
        act = glu * jax.nn.sigmoid(SWIGLU_ALPHA * glu) * (lin + 1.0)
        ys_ref[...] = jnp.dot(act.astype(BF16), wd_bf[...], preferred_element_type=F32) + bd_ref[0, 0]


def _moe(block_expert, first, n_used, xs, w_gate_up, b_gate_up, w_down, b_down, layer):
    def blk(i, be, fr, nu):
        return (jnp.minimum(i, nu[0] - 1), 0)

    def wsel(i, be, fr, nu):
        return (layer, be[i], 0, 0)

    return pl.pallas_call(
        _moe_kernel,
        grid_spec=pltpu.PrefetchScalarGridSpec(
            num_scalar_prefetch=3,
            grid=(MOE_BLOCKS,),
            in_specs=[pl.BlockSpec((MOE_ROWS, D_MODEL // 2), blk),
                      pl.BlockSpec((1, 1, D_MODEL, 2 * D_FF), wsel),
                      pl.BlockSpec((1, 1, 1, 2 * D_FF), wsel),
                      pl.BlockSpec((1, 1, D_FF, D_MODEL), wsel),
                      pl.BlockSpec((1, 1, 1, D_MODEL), wsel)],
            out_specs=pl.BlockSpec((MOE_ROWS, D_MODEL), blk),
            scratch_shapes=[pltpu.VMEM((D_MODEL, 2 * D_FF), BF16), pltpu.VMEM((D_FF, D_MODEL), BF16)]),
        out_shape=jax.ShapeDtypeStruct((MOE_SLOTS, D_MODEL), F32),
        compiler_params=_cparams("arbitrary"),
        name="moe_experts",
    )(block_expert, first, n_used, xs, w_gate_up, b_gate_up, w_down, b_down)


def _route(idx, rank, counts):
    experts = jnp.arange(N_EXPERTS, dtype=jnp.int32)
    counts = counts.reshape(T_ALL // ROW_BLOCK, N_EXPERTS)
    before_block = jnp.cumsum(counts, axis=0) - counts
    total = jnp.sum(counts, axis=0)
    padded = (total + MOE_ROWS - 1) // MOE_ROWS * MOE_ROWS
    pad_end = jnp.cumsum(padded)
    pad_start = pad_end - padded
    base = jnp.repeat(pad_start[None, :] + before_block, ROW_BLOCK, axis=0)
    pos = jnp.sum(jnp.where(idx[:, :, None] == experts[None, None, :], base[:, None, :], 0), axis=-1) + rank
    gap = padded - total
    gap_end = jnp.concatenate([jnp.cumsum(gap), jnp.full((1,), N_UNUSED, jnp.int32)])
    gap_start = jnp.concatenate([jnp.zeros((1,), jnp.int32), gap_end[:-1]])
    first_free = jnp.concatenate([pad_start + total, pad_end[-1:]])
    j = jnp.arange(N_UNUSED, dtype=jnp.int32)
    region = (j[:, None] >= gap_start[None, :]) & (j[:, None] < gap_end[None, :])
    unused = jnp.sum(jnp.where(region, first_free[None, :] + j[:, None] - gap_start[None, :], 0), axis=-1)
    n_used = (pad_end[-1] // MOE_ROWS).astype(jnp.int32)
    blk = jnp.minimum(jnp.arange(MOE_BLOCKS, dtype=jnp.int32), n_used - 1)
    block_expert = jnp.minimum(jnp.sum((blk[:, None] * MOE_ROWS >= pad_end[None, :]).astype(jnp.int32), axis=-1),
                               N_EXPERTS - 1)
    first = jnp.concatenate([jnp.ones((1,), jnp.int32),
                             (block_expert[1:] != block_expert[:-1]).astype(jnp.int32)])
    return pos.T.astype(jnp.int32), unused.astype(jnp.int32), block_expert, first, n_used.reshape(1)


def _sc_worker():
    return lax.axis_index("s") * SC_CORES + lax.axis_index("c")


def _sc_dispatch(xpk, pos_km, unused, zero_rows):
    ch = SC_DISPATCH_ROWS
    n_chunks = T_ALL // ch
    n_zero_chunks = N_UNUSED // ch
    width = xpk.shape[1]
    mesh = plsc.VectorSubcoreMesh(core_axis_name="c", subcore_axis_name="s")

    @functools.partial(
        pl.kernel, mesh=mesh,
        out_type=jax.ShapeDtypeStruct((MOE_SLOTS, width), xpk.dtype),
        scratch_types=[pltpu.VMEM((ch, width), xpk.dtype), pltpu.VMEM((TOP_K, ch), jnp.int32),
                       pltpu.SemaphoreType.DMA((TOP_K,))],
        name="sc_dispatch")
    def run(x_hbm, pos_hbm, unused_hbm, zero_hbm, xs_hbm, rows_v, idx_v, sems):
        wid = _sc_worker()
        pltpu.sync_copy(zero_hbm, rows_v)
        for r in range(n_zero_chunks // SC_WORKERS):
            start = pl.multiple_of((wid + r * SC_WORKERS) * ch, ch)
            pltpu.sync_copy(unused_hbm.at[pl.ds(start, ch)], idx_v.at[0])
            pltpu.sync_copy(rows_v, xs_hbm.at[idx_v.at[0]])

        def body(r, carry):
            chunk = wid + r * SC_WORKERS

            @pl.when(chunk < n_chunks)
            def _():
                start = pl.multiple_of(chunk * ch, ch)
                pltpu.sync_copy(x_hbm.at[pl.ds(start, ch)], rows_v)
                for k in range(TOP_K):
                    pltpu.sync_copy(pos_hbm.at[k, pl.ds(start, ch)], idx_v.at[k])
                copies = [pltpu.async_copy(rows_v, xs_hbm.at[idx_v.at[k]], sems.at[k]) for k in range(TOP_K)]
                for cp in copies:
                    cp.wait()
            return carry

        lax.fori_loop(0, pl.cdiv(n_chunks, SC_WORKERS), body, 0)

    return run(xpk, pos_km, unused, zero_rows)


def _sc_collect(ys, pos_flat):
    ch = SC_COLLECT_ROWS
    per_worker = N_ASSIGN // SC_WORKERS
    mesh = plsc.VectorSubcoreMesh(core_axis_name="c", subcore_axis_name="s")

    @functools.partial(
        pl.kernel, mesh=mesh,
        out_type=jax.ShapeDtypeStruct((N_ASSIGN, D_MODEL), ys.dtype),
        scratch_types=[pltpu.VMEM((ch, D_MODEL), ys.dtype), pltpu.VMEM((ch,), jnp.int32), pltpu.SemaphoreType.DMA],
        name="sc_collect")
    def run(ys_hbm, pos_hbm, y4_hbm, rows_v, idx_v, sem):
        wid = _sc_worker()

        def body(r, carry):
            start = pl.multiple_of(wid * per_worker + r * ch, 8)
            pltpu.sync_copy(pos_hbm.at[pl.ds(start, ch)], idx_v)
            pltpu.async_copy(ys_hbm.at[idx_v], rows_v, sem).wait()
            pltpu.sync_copy(rows_v, y4_hbm.at[pl.ds(start, ch)])
            return carry

        lax.fori_loop(0, per_worker // ch, body, 0)

    return run(ys, pos_flat)


def _ffn_norm_kernel(x_ref, y0_ref, y1_ref, y2_ref, y3_ref, gate_ref, g_ref, b_ref, o_ref):
    gate = gate_ref[...]
    ffn = (gate[:, 0:1] * y0_ref[0] + gate[:, 1:2] * y1_ref[0]
           + gate[:, 2:3] * y2_ref[0] + gate[:, 3:4] * y3_ref[0])
    o_ref[...] = _layer_norm(DEEPNORM_ALPHA * x_ref[...] + ffn, g_ref[0], b_ref[0])


def _ffn_norm(x1, y4, gate, ln_g, ln_b, layer, row_block, first_block, n_blocks):
    def rows(w):
        return pl.BlockSpec((row_block, w), lambda i: (first_block + i, 0))

    def plane(k):
        return pl.BlockSpec((1, row_block, D_MODEL), lambda i: (k, first_block + i, 0))

    vec = pl.BlockSpec((1, 1, D_MODEL), lambda i: (layer, 0, 0))
    return pl.pallas_call(
        _ffn_norm_kernel,
        grid=(n_blocks,),
        in_specs=[rows(D_MODEL), plane(0), plane(1), plane(2), plane(3), rows(TOP_K), vec, vec],
        out_specs=pl.BlockSpec((row_block, D_MODEL), lambda i: (i, 0)),
        out_shape=jax.ShapeDtypeStruct((n_blocks * row_block, D_MODEL), F32),
        compiler_params=_cparams("parallel"),
        name="ffn_norm",
    )(x1, y4, y4, y4, y4, gate, ln_g, ln_b)


def _rope_tables(pos):
    half = HD_RET // 2
    inv = ROPE_BASE ** (-jnp.arange(half, dtype=F32) / half)
    ang = pos.astype(F32)[:, None] * inv[None, :]
    cos, sin = jnp.cos(ang), jnp.sin(ang)
    return jnp.concatenate([cos, cos], -1), jnp.concatenate([-sin, sin], -1)


def _decay_tables(c):
    log_g = jnp.log1p(-jnp.exp2(-5.0 - jnp.arange(H_RET, dtype=F32)))
    idx = jnp.arange(c, dtype=F32)
    diff = idx[:, None] - idx[None, :]
    intra = jnp.where(diff[None] >= 0, jnp.exp(jnp.maximum(diff, 0.0)[None] * log_g[:, None, None]), 0.0)
    q_dec = jnp.exp((idx + 1.0)[None] * log_g[:, None])
    k_dec = jnp.exp((c - 1.0 - idx)[None] * log_g[:, None])
    c_dec = jnp.exp(c * log_g)
    return intra, q_dec, k_dec, c_dec


def _t5_bucket(dist):
    n = jnp.maximum(dist, 0)
    max_exact = NUM_BUCKETS // 2
    nf = jnp.maximum(n, 1).astype(F32)
    large = max_exact + (jnp.log(nf / max_exact) / math.log(REL_MAX_DIST / max_exact)
                         * (NUM_BUCKETS - max_exact)).astype(jnp.int32)
    large = jnp.minimum(large, NUM_BUCKETS - 1)
    return jnp.where(n < max_exact, n, large)


def kernel(x_prompt, x_sample, state_ret, cache_swa_k, cache_swa_v, w_in, w_out, ret_gn_gain, swa_sinks,
           rel_bias, ln1_g, ln1_b, ln2_g, ln2_b, router_w, router_b, w_gate_up, b_gate_up, w_down, b_down):
    x = jnp.concatenate([x_prompt.reshape(T_PROMPT, D_MODEL), x_sample.reshape(DEC_BATCH, D_MODEL)], axis=0)
    w_in_bf = w_in.astype(BF16)
    w_out_bf = w_out.astype(BF16)
    router_w_bf = router_w.astype(BF16)
    ret_gn_gain, ln1_g, ln1_b, ln2_g, ln2_b, router_b = (
        a.reshape(DEPTH, 1, a.shape[-1]) for a in (ret_gn_gain, ln1_g, ln1_b, ln2_g, ln2_b, router_b))
    b_gu4 = b_gate_up.reshape(DEPTH, N_EXPERTS, 1, 2 * D_FF)
    b_d4 = b_down.reshape(DEPTH, N_EXPERTS, 1, D_MODEL)
    cache_k = cache_swa_k.reshape(DEPTH, DEC_BATCH, WINDOW, 2 * HD_SWA)
    cache_v = cache_swa_v.reshape(DEPTH, DEC_BATCH, WINDOW, 2 * HD_SWA)

    cos_p, sin_p = _rope_tables(jnp.arange(SEQ, dtype=jnp.int32))
    cos_s, sin_s = _rope_tables(PAST_LEN + jnp.arange(1, dtype=jnp.int32))
    intra, q_dec, k_dec, c_dec = _decay_tables(RET_CHUNK)
    ones = jnp.ones((1, 1, HD_RET), F32)
    tables_p = (cos_p, sin_p, intra, q_dec[:, :, None] * ones, k_dec[:, :, None] * ones, c_dec[:, None, None] * ones)
    intra1, q_dec1, k_dec1, c_dec1 = _decay_tables(1)
    dec_s = jnp.stack([intra1[:, 0, 0], q_dec1[:, 0], k_dec1[:, 0], c_dec1], axis=1)
    i = jnp.arange(WINDOW)[:, None]
    j = jnp.arange(2 * WINDOW)[None, :]
    bias_p = jnp.moveaxis(rel_bias[_t5_bucket(WINDOW + i - j)], -1, 0).astype(F32)
    js = jnp.arange(WINDOW + 1)
    bias_s = jnp.moveaxis(rel_bias[_t5_bucket(WINDOW - js)], -1, 0).astype(F32)
    bias_c, bias_n = bias_s[:, :WINDOW], bias_s[:, WINDOW:]
    kv_of_head = jnp.arange(H_SWA) // G_SWA
    head_mask = (kv_of_head[:, None] == jnp.arange(KV_SWA)[None, :]).astype(F32)

    zero_rows = jnp.zeros((SC_DISPATCH_ROWS, D_MODEL // 2), jnp.uint32)

    sp = ss = ko = vo = tail = None
    for l in range(DEPTH):
        h = _in_proj(x, w_in_bf, l)
        o_r, sp = _ret_prompt(h, tables_p, ret_gn_gain, sp, l)
        o_s, tail = _swa_prompt(h, swa_sinks[l], bias_p, tail, l)
        o_r, ss = _ret_sample(h, dec_s, cos_s, sin_s, ret_gn_gain, state_ret, o_r, ss, l)
        hs = h[T_PROMPT:, 4 * RET_W:]
        q8 = hs[:, :SWA_Q_W].reshape(DEC_BATCH, H_SWA, HD_SWA)
        qm = (q8[:, :, None, :] * head_mask[None, :, :, None]).reshape(DEC_BATCH, H_SWA, 2 * HD_SWA)
        kn = hs[:, SWA_Q_W:SWA_Q_W + SWA_KV_W].reshape(DEC_BATCH, 1, SWA_KV_W)
        vn = hs[:, SWA_Q_W + SWA_KV_W:].reshape(DEC_BATCH, 1, SWA_KV_W)
        osm, ko, vo = _swa_sample(qm, kn, vn, cache_k, cache_v, bias_c, bias_n, swa_sinks[l][:, None], ko, vo, l)
        os_s = jnp.sum(osm.reshape(DEC_BATCH, H_SWA, KV_SWA, HD_SWA) * head_mask[None, :, :, None], axis=2)
        o_s = lax.dynamic_update_slice(o_s, os_s.reshape(DEC_BATCH, SWA_Q_W), (T_PROMPT, 0))

        x1, xpk, top_idx, gate, rank, counts = _out_proj(o_r, o_s, x, w_out_bf, ln1_g, ln1_b, router_w_bf,
                                                         router_b, l)
        pos_km, unused, block_expert, first, n_used = _route(top_idx, rank, counts)
        xs = _sc_dispatch(xpk, pos_km, unused, zero_rows)
        ys = _moe(block_expert, first, n_used, xs, w_gate_up, b_gu4, w_down, b_d4, l)
        y4 = _sc_collect(ys, pos_km.reshape(N_ASSIGN)).reshape(TOP_K, T_ALL, D_MODEL)
        if l < DEPTH - 1:
            x = _ffn_norm(x1, y4, gate, ln2_g, ln2_b, l, ROW_BLOCK, 0, T_ALL // ROW_BLOCK)
        else:
            yp = _ffn_norm(x1, y4, gate, ln2_g, ln2_b, l, FINAL_ROW_BLOCK, 0, T_PROMPT // FINAL_ROW_BLOCK)
            ys_out = _ffn_norm(x1, y4, gate, ln2_g, ln2_b, l, DEC_BATCH, T_PROMPT // DEC_BATCH, 1)

    yp = yp.reshape(BATCH, SEQ, D_MODEL)
    ys_out = ys_out.reshape(DEC_BATCH, 1, D_MODEL)
    kcp = tail[..., :SWA_KV_W].reshape(DEPTH, BATCH, WINDOW, KV_SWA, HD_SWA)
    vcp = tail[..., SWA_KV_W:].reshape(DEPTH, BATCH, WINDOW, KV_SWA, HD_SWA)
    return (yp, ys_out, sp, kcp, vcp, ss,
            ko.reshape(DEPTH, DEC_BATCH, WINDOW, KV_SWA, HD_SWA),
            vo.reshape(DEPTH, DEC_BATCH, WINDOW, KV_SWA, HD_SWA))
```

```python
import functools
import math

import jax
import jax.numpy as jnp
from jax import lax
from jax.experimental import pallas as pl
from jax.experimental.pallas import tpu as pltpu
from jax.experimental.pallas import tpu_sc as plsc

F32 = jnp.float32
BF16 = jnp.bfloat16

D_MODEL = 1024
BATCH = 8
SEQ = 2048
DEPTH = 4
DEC_BATCH = 128
PAST_LEN = 8192
H_RET = 4
HD_RET = 128
RET_CHUNK = 128
ROPE_BASE = 10000.0
H_SWA = 8
KV_SWA = 2
G_SWA = H_SWA // KV_SWA
HD_SWA = 64
WINDOW = 128
NUM_BUCKETS = 32
REL_MAX_DIST = 128
RET_W = H_RET * HD_RET
SWA_Q_W = H_SWA * HD_SWA
SWA_KV_W = KV_SWA * HD_SWA
D_IN = 4 * RET_W + SWA_Q_W + 2 * SWA_KV_W
N_EXPERTS = 32
TOP_K = 4
D_FF = D_MODEL
SWIGLU_LIMIT = 7.0
SWIGLU_ALPHA = 1.702
DEEPNORM_ALPHA = (2 * DEPTH) ** 0.25
LN_EPS = 1e-5
GN_EPS = 1e-5

T_PROMPT = BATCH * SEQ
T_ALL = T_PROMPT + DEC_BATCH
ROW_BLOCK = 384
SAMPLE_BLOCK = 8
MOE_ROWS = 256
N_ASSIGN = T_ALL * TOP_K
MOE_BLOCKS = N_ASSIGN // MOE_ROWS + N_EXPERTS
MOE_SLOTS = MOE_BLOCKS * MOE_ROWS
N_UNUSED = MOE_SLOTS - N_ASSIGN
SC_CORES = 2
SC_WORKERS = SC_CORES * 16
SC_DISPATCH_ROWS = 64
SC_COLLECT_ROWS = 48
FINAL_ROW_BLOCK = 512
VMEM_LIMIT = 56 * 1024 * 1024


def _cparams(*sem):
    return pltpu.CompilerParams(dimension_semantics=sem, vmem_limit_bytes=VMEM_LIMIT)


def _in_proj_kernel(x_ref, w_ref, h_ref):
    h_ref[...] = jnp.dot(x_ref[...].astype(BF16), w_ref[0], preferred_element_type=F32)


def _in_proj(x, w_in_bf, layer):
    return pl.pallas_call(
        _in_proj_kernel,
        grid=(T_ALL // ROW_BLOCK,),
        in_specs=[pl.BlockSpec((ROW_BLOCK, D_MODEL), lambda i: (i, 0)),
                  pl.BlockSpec((1, D_MODEL, D_IN), lambda i: (layer, 0, 0))],
        out_specs=pl.BlockSpec((ROW_BLOCK, D_IN), lambda i: (i, 0)),
        out_shape=jax.ShapeDtypeStruct((T_ALL, D_IN), F32),
        compiler_params=_cparams("parallel"),
        name="in_proj",
    )(x, w_in_bf)


def _rotate(x, cos, sin_signed):
    return x * cos + pltpu.roll(x, HD_RET // 2, 1) * sin_signed


def _group_norm_gate(o, gain, g):
    mu = jnp.mean(o, -1, keepdims=True)
    var = jnp.mean(jnp.square(o - mu), -1, keepdims=True)
    return (o - mu) * lax.rsqrt(var + GN_EPS) * gain * (g * jax.nn.sigmoid(g))


def _layer_norm(y, g, b):
    mu = jnp.mean(y, -1, keepdims=True)
    var = jnp.mean(jnp.square(y - mu), -1, keepdims=True)
    return (y - mu) * lax.rsqrt(var + LN_EPS) * g + b


def _ret_prompt_kernel(q_ref, k_ref, v_ref, g_ref, cos_ref, sin_ref, intra_ref, qdec_ref, kdec_ref,
                       cdec_ref, gain_ref, o_ref, s_ref):
    @pl.when(pl.program_id(1) == 0)
    def _():
        s_ref[...] = jnp.zeros_like(s_ref)

    cos = cos_ref[...]
    sin = sin_ref[...]
    nt = (((1,), (1,)), ((), ()))
    tn = (((0,), (0,)), ((), ()))
    heads = [slice(h * HD_RET, (h + 1) * HD_RET) for h in range(H_RET)]
    vs, atts, crosses = [], [], []
    for h, sl in enumerate(heads):
        q = _rotate(q_ref[:, sl], cos, sin)
        k = _rotate(k_ref[:, sl], cos, sin) * (HD_RET ** -0.5)
        v = v_ref[:, sl].astype(BF16)
        s = s_ref[0, 0, h]
        atts.append(lax.dot_general(q.astype(BF16), k.astype(BF16), nt, preferred_element_type=F32))
        crosses.append(jnp.dot((q * qdec_ref[h]).astype(BF16), s.astype(BF16), preferred_element_type=F32))
        s_ref[0, 0, h] = s * cdec_ref[h] + lax.dot_general((k * kdec_ref[h]).astype(BF16), v, tn,
                                                          preferred_element_type=F32)
        vs.append(v)
    atts = [(att * intra_ref[h]).astype(BF16) for h, att in enumerate(atts)]
    outs = [jnp.dot(atts[h], vs[h], preferred_element_type=F32) + crosses[h] for h in range(H_RET)]
    for h, sl in enumerate(heads):
        o_ref[:, sl] = _group_norm_gate(outs[h], gain_ref[0, :, sl], g_ref[:, sl])


def _ret_prompt(h, tables, gain, s_prev, layer):
    cos, sin, intra, qdec, kdec, cdec = tables
    nc = SEQ // RET_CHUNK

    def col(j):
        return pl.BlockSpec((RET_CHUNK, RET_W), lambda b, c: (b * nc + c, j))

    def whole(a):
        return pl.BlockSpec(a.shape, lambda b, c: (0,) * a.ndim)

    in_specs = [col(0), col(1), col(2), col(3),
                pl.BlockSpec((RET_CHUNK, HD_RET), lambda b, c: (c, 0)),
                pl.BlockSpec((RET_CHUNK, HD_RET), lambda b, c: (c, 0)),
                whole(intra), whole(qdec), whole(kdec), whole(cdec),
                pl.BlockSpec((1, 1, RET_W), lambda b, c: (layer, 0, 0))]
    args = [h, h, h, h, cos, sin, intra, qdec, kdec, cdec, gain]
    aliases = {}
    if s_prev is not None:
        in_specs.append(pl.BlockSpec(memory_space=pl.ANY))
        args.append(s_prev)
        aliases = {len(args) - 1: 1}

    def body(*refs):
        _ret_prompt_kernel(*refs[:11], *refs[-2:])

    return pl.pallas_call(
        body,
        grid=(BATCH, nc),
        in_specs=in_specs,
        out_specs=[pl.BlockSpec((RET_CHUNK, RET_W), lambda b, c: (b * nc + c, 0)),
                   pl.BlockSpec((1, 1, H_RET, HD_RET, HD_RET), lambda b, c: (layer, b, 0, 0, 0))],
        out_shape=[jax.ShapeDtypeStruct((T_ALL, RET_W), F32),
                   jax.ShapeDtypeStruct((DEPTH, BATCH, H_RET, HD_RET, HD_RET), F32)],
        input_output_aliases=aliases,
        compiler_params=_cparams("parallel", "arbitrary"),
        name="ret_prompt",
    )(*args)


def _swa_prompt_kernel(sink_ref, q_ref, kvc_ref, kvp_ref, bias_ref, o_ref, tail_ref, lg_scr, p_scr):
    c = pl.program_id(1)

    @pl.when(c == pl.num_programs(1) - 1)
    def _():
        tail_ref[0, 0] = kvc_ref[...]

    row = lax.broadcasted_iota(jnp.int32, (WINDOW, 2 * WINDOW), 0)
    colj = lax.broadcasted_iota(jnp.int32, (WINDOW, 2 * WINDOW), 1)
    valid = (colj > row) & (colj <= row + WINDOW) & jnp.logical_or(c > 0, colj >= WINDOW)

    kv = jnp.concatenate([kvp_ref[...], kvc_ref[...]], axis=0)
    k_t = kv[:, :SWA_KV_W].T
    k_t_swapped = pltpu.roll(k_t, HD_SWA, 0)
    v = kv[:, SWA_KV_W:]
    v_swapped = pltpu.roll(v, HD_SWA, 1)
    k_row = lax.broadcasted_iota(jnp.int32, k_t.shape, 0)
    v_lane = lax.broadcasted_iota(jnp.int32, v.shape, 1)
    ones = jnp.ones(v.shape, BF16)
    k_pad, v_pad = {}, {}
    for kvh in range(KV_SWA):
        for par in range(2):
            k_src = k_t if kvh == par else k_t_swapped
            v_src = v if kvh == par else v_swapped
            k_pad[kvh, par] = jnp.where((k_row >= par * HD_SWA) & (k_row < (par + 1) * HD_SWA), k_src, 0.0).astype(BF16)
            v_pad[kvh, par] = jnp.where((v_lane >= par * HD_SWA) & (v_lane < (par + 1) * HD_SWA), v_src, 0.0).astype(BF16)
    for hd in range(H_SWA):
        pair, par, kvh = hd // 2, hd % 2, hd // G_SWA
        q2 = (q_ref[:, pair * 2 * HD_SWA:(pair + 1) * 2 * HD_SWA] * (HD_SWA ** -0.5)).astype(BF16)
        logits = jnp.dot(q2, k_pad[kvh, par], preferred_element_type=F32) + bias_ref[hd]
        lg_scr[hd] = jnp.where(valid, logits, -jnp.inf)
    sink_terms = []
    for hd in range(H_SWA):
        logits = lg_scr[hd]
        sink = sink_ref[hd]
        m = jnp.maximum(jnp.max(logits, -1, keepdims=True), sink)
        p_scr[hd] = jnp.exp(logits - m).astype(BF16)
        sink_terms.append(jnp.exp(sink - m))
    for pair in range(H_SWA // 2):
        acc = None
        for par in range(2):
            hd = 2 * pair + par
            p = p_scr[hd]
            den = jnp.dot(p, ones, preferred_element_type=F32) + sink_terms[hd]
            part = jnp.dot(p, v_pad[hd // G_SWA, par], preferred_element_type=F32) / den
            acc = part if acc is None else acc + part
        o_ref[:, pair * 2 * HD_SWA:(pair + 1) * 2 * HD_SWA] = acc


def _swa_prompt(h, sinks_l, bias, tail_prev, layer):
    nb = SEQ // WINDOW
    qcol = 4 * RET_W // SWA_Q_W
    kvcol = (4 * RET_W + SWA_Q_W) // (2 * SWA_KV_W)
    in_specs = [pl.BlockSpec(memory_space=pltpu.SMEM),
                pl.BlockSpec((WINDOW, SWA_Q_W), lambda b, c: (b * nb + c, qcol)),
                pl.BlockSpec((WINDOW, 2 * SWA_KV_W), lambda b, c: (b * nb + c, kvcol)),
                pl.BlockSpec((WINDOW, 2 * SWA_KV_W), lambda b, c: (b * nb + jnp.maximum(c - 1, 0), kvcol)),
                pl.BlockSpec((H_SWA, WINDOW, 2 * WINDOW), lambda b, c: (0, 0, 0))]
    args = [sinks_l, h, h, h, bias]
    aliases = {}
    if tail_prev is not None:
        in_specs.append(pl.BlockSpec(memory_space=pl.ANY))
        args.append(tail_prev)
        aliases = {5: 1}

    def body(*refs):
        _swa_prompt_kernel(*refs[:5], *refs[-4:])

    return pl.pallas_call(
        body,
        grid=(BATCH, nb),
        in_specs=in_specs,
        scratch_shapes=[pltpu.VMEM((H_SWA, WINDOW, 2 * WINDOW), F32), pltpu.VMEM((H_SWA, WINDOW, 2 * WINDOW), BF16)],
        out_specs=[pl.BlockSpec((WINDOW, SWA_Q_W), lambda b, c: (b * nb + c, 0)),
                   pl.BlockSpec((1, 1, WINDOW, 2 * SWA_KV_W), lambda b, c: (layer, b, 0, 0))],
        out_shape=[jax.ShapeDtypeStruct((T_ALL, SWA_Q_W), F32),
                   jax.ShapeDtypeStruct((DEPTH, BATCH, WINDOW, 2 * SWA_KV_W), F32)],
        input_output_aliases=aliases,
        compiler_params=_cparams("parallel", "arbitrary"),
        name="swa_prompt",
    )(*args)


def _ret_sample_kernel(dec_ref, q_ref, k_ref, v_ref, g_ref, cos_ref, sin_ref, gain_ref, st_ref, o_ref, so_ref):
    cos = cos_ref[...]
    sin = sin_ref[...]
    for h in range(H_RET):
        sl = slice(h * HD_RET, (h + 1) * HD_RET)
        intra, qdec, kdec, cdec = dec_ref[h, 0], dec_ref[h, 1], dec_ref[h, 2], dec_ref[h, 3]
        q = _rotate(q_ref[:, sl], cos, sin)
        k = _rotate(k_ref[:, sl], cos, sin) * (HD_RET ** -0.5)
        v = v_ref[:, sl]
        att = jnp.sum(q * k, -1, keepdims=True) * intra
        q_col = (q * qdec).T
        k_col = (k * kdec).T
        rows = []
        for b in range(SAMPLE_BLOCK):
            s = st_ref[0, b, h]
            v_row = v[b:b + 1, :]
            rows.append(att[b:b + 1, :] * v_row + jnp.sum(q_col[:, b:b + 1] * s, axis=0, keepdims=True))
            so_ref[0, b, h] = s * cdec + k_col[:, b:b + 1] * v_row
        o = jnp.concatenate(rows, axis=0)
        o_ref[:, sl] = _group_norm_gate(o, gain_ref[0, :, sl], g_ref[:, sl])


def _ret_sample(h, dec_s, cos_s, sin_s, gain, state_ret, o_r, st_prev, layer):
    base = T_PROMPT // SAMPLE_BLOCK

    def col(j):
        return pl.BlockSpec((SAMPLE_BLOCK, RET_W), lambda i: (base + i, j))

    in_specs = [pl.BlockSpec(memory_space=pltpu.SMEM),
                col(0), col(1), col(2), col(3),
                pl.BlockSpec((1, HD_RET), lambda i: (0, 0)),
                pl.BlockSpec((1, HD_RET), lambda i: (0, 0)),
                pl.BlockSpec((1, 1, RET_W), lambda i: (layer, 0, 0)),
                pl.BlockSpec((1, SAMPLE_BLOCK, H_RET, HD_RET, HD_RET), lambda i: (layer, i, 0, 0, 0)),
                pl.BlockSpec(memory_space=pl.ANY)]
    args = [dec_s, h, h, h, h, cos_s, sin_s, gain, state_ret, o_r]
    aliases = {9: 0}
    if st_prev is not None:
        in_specs.append(pl.BlockSpec(memory_space=pl.ANY))
        args.append(st_prev)
        aliases[10] = 1

    def body(*refs):
        _ret_sample_kernel(*refs[:9], *refs[-2:])

    return pl.pallas_call(
        body,
        grid=(DEC_BATCH // SAMPLE_BLOCK,),
        in_specs=in_specs,
        out_specs=[pl.BlockSpec((SAMPLE_BLOCK, RET_W), lambda i: (base + i, 0)),
                   pl.BlockSpec((1, SAMPLE_BLOCK, H_RET, HD_RET, HD_RET), lambda i: (layer, i, 0, 0, 0))],
        out_shape=[jax.ShapeDtypeStruct((T_ALL, RET_W), F32),
                   jax.ShapeDtypeStruct((DEPTH, DEC_BATCH, H_RET, HD_RET, HD_RET), F32)],
        input_output_aliases=aliases,
        compiler_params=_cparams("parallel"),
        name="ret_sample",
    )(*args)


def _swa_sample_kernel(qm_ref, kn_ref, vn_ref, kc_ref, vc_ref, bias_ref, biasn_ref, sink_ref,
                       o_ref, ko_ref, vo_ref):
    qm = qm_ref[...]
    kn = kn_ref[...]
    vn = vn_ref[...]
    kc = kc_ref[0]
    vc = vc_ref[0]
    scale = HD_SWA ** -0.5
    logits = jnp.einsum('bhd,bjd->bhj', qm.astype(BF16), kc.astype(BF16),
                        preferred_element_type=F32) * scale + bias_ref[...][None]
    j = lax.broadcasted_iota(jnp.int32, logits.shape, 2)
    logits = jnp.where(j >= 1, logits, -jnp.inf)
    ln = jnp.sum(qm * kn, -1, keepdims=True) * scale + biasn_ref[...][None]
    sink = sink_ref[...][None]
    m = jnp.maximum(jnp.maximum(jnp.max(logits, -1, keepdims=True), ln), sink)
    p = jnp.exp(logits - m)
    pn = jnp.exp(ln - m)
    denom = jnp.sum(p, -1, keepdims=True) + pn + jnp.exp(sink - m)
    w = p / denom
    o_ref[...] = (jnp.einsum('bhj,bjd->bhd', w.astype(BF16), vc.astype(BF16), preferred_element_type=F32)
                  + (pn / denom) * vn)
    ko_ref[0, :, 0:WINDOW - 1, :] = kc_ref[0, :, 1:WINDOW, :]
    ko_ref[0, :, WINDOW - 1:WINDOW, :] = kn
    vo_ref[0, :, 0:WINDOW - 1, :] = vc_ref[0, :, 1:WINDOW, :]
    vo_ref[0, :, WINDOW - 1:WINDOW, :] = vn


def _swa_sample(qm, kn, vn, cache_k, cache_v, bias_c, bias_n, sink_col, ko_prev, vo_prev, layer):
    sb = SAMPLE_BLOCK
    cache_spec = pl.BlockSpec((1, sb, WINDOW, 2 * HD_SWA), lambda i: (layer, i, 0, 0))
    in_specs = [pl.BlockSpec((sb, H_SWA, 2 * HD_SWA), lambda i: (i, 0, 0)),
                pl.BlockSpec((sb, 1, 2 * HD_SWA), lambda i: (i, 0, 0)),
                pl.BlockSpec((sb, 1, 2 * HD_SWA), lambda i: (i, 0, 0)),
                cache_spec, cache_spec,
                pl.BlockSpec((H_SWA, WINDOW), lambda i: (0, 0)),
                pl.BlockSpec((H_SWA, 1), lambda i: (0, 0)),
                pl.BlockSpec((H_SWA, 1), lambda i: (0, 0))]
    args = [qm, kn, vn, cache_k, cache_v, bias_c, bias_n, sink_col]
    aliases = {}
    if ko_prev is not None:
        in_specs += [pl.BlockSpec(memory_space=pl.ANY), pl.BlockSpec(memory_space=pl.ANY)]
        args += [ko_prev, vo_prev]
        aliases = {8: 1, 9: 2}

    def body(*refs):
        _swa_sample_kernel(*refs[:8], *refs[-3:])

    cshape = jax.ShapeDtypeStruct((DEPTH, DEC_BATCH, WINDOW, 2 * HD_SWA), F32)
    return pl.pallas_call(
        body,
        grid=(DEC_BATCH // sb,),
        in_specs=in_specs,
        out_specs=[pl.BlockSpec((sb, H_SWA, 2 * HD_SWA), lambda i: (i, 0, 0)), cache_spec, cache_spec],
        out_shape=[jax.ShapeDtypeStruct((DEC_BATCH, H_SWA, 2 * HD_SWA), F32), cshape, cshape],
        input_output_aliases=aliases,
        compiler_params=_cparams("parallel"),
        name="swa_sample",
    )(*args)


def _pack_bf16_pairs(x):
    w = x.shape[1] // 2
    lo = lax.bitcast_convert_type(x[:, :w].astype(BF16).astype(F32), jnp.uint32) >> 16
    hi = lax.bitcast_convert_type(x[:, w:].astype(BF16).astype(F32), jnp.uint32)
    return lo | hi


def _unpack_bf16_pairs(p):
    lo = lax.bitcast_convert_type(p << 16, F32).astype(BF16)
    hi = lax.bitcast_convert_type(p & jnp.uint32(0xFFFF0000), F32).astype(BF16)
    return lo, hi


def _out_proj_kernel(or_ref, os_ref, x_ref, wr_ref, ws_ref, g_ref, b_ref, rw_ref, rb_ref,
                     x1_ref, xpk_ref, idx_ref, gate_ref, rank_ref, cnt_ref):
    mix = (jnp.dot(or_ref[...].astype(BF16), wr_ref[0], preferred_element_type=F32)
           + jnp.dot(os_ref[...].astype(BF16), ws_ref[0], preferred_element_type=F32))
    x1 = _layer_norm(DEEPNORM_ALPHA * x_ref[...] + mix, g_ref[0], b_ref[0])
    x1_ref[...] = x1
    xpk_ref[...] = _pack_bf16_pairs(x1)
    logits = jnp.dot(x1.astype(BF16), rw_ref[0], preferred_element_type=F32) + rb_ref[0]
    lane = lax.broadcasted_iota(jnp.int32, logits.shape, 1)
    vals, hits = [], []
    for kk in range(TOP_K):
        m = jnp.max(logits, -1, keepdims=True)
        idx = jnp.min(jnp.where(logits == m, lane, N_EXPERTS), -1, keepdims=True)
        vals.append(m)
        hits.append(lane == idx)
        idx_ref[:, kk:kk + 1] = idx
        logits = jnp.where(lane == idx, -jnp.inf, logits)
    es = [jnp.exp(v - vals[0]) for v in vals]
    tot = es[0] + es[1] + es[2] + es[3]
    for kk in range(TOP_K):
        gate_ref[:, kk:kk + 1] = es[kk] / tot
    chosen = sum(h.astype(F32) for h in hits)
    rows = logits.shape[0]
    earlier = (lax.broadcasted_iota(jnp.int32, (rows, rows), 0)
               > lax.broadcasted_iota(jnp.int32, (rows, rows), 1)).astype(BF16)
    before = jnp.dot(earlier, chosen.astype(BF16), preferred_element_type=F32)
    for kk in range(TOP_K):
        rank_ref[:, kk:kk + 1] = jnp.sum(jnp.where(hits[kk], before, 0.0), -1, keepdims=True).astype(jnp.int32)
    cnt_ref[0] = jnp.sum(chosen, axis=0, keepdims=True).astype(jnp.int32)


def _out_proj(o_r, o_s, x, w_out_bf, ln_g, ln_b, router_w_bf, router_b, layer):
    def rows(w):
        return pl.BlockSpec((ROW_BLOCK, w), lambda i: (i, 0))

    def vec(w):
        return pl.BlockSpec((1, 1, w), lambda i: (layer, 0, 0))

    return pl.pallas_call(
        _out_proj_kernel,
        grid=(T_ALL // ROW_BLOCK,),
        in_specs=[rows(RET_W), rows(SWA_Q_W), rows(D_MODEL),
                  pl.BlockSpec((1, RET_W, D_MODEL), lambda i: (layer, 0, 0)),
                  pl.BlockSpec((1, SWA_Q_W, D_MODEL), lambda i: (layer, 1, 0)),
                  vec(D_MODEL), vec(D_MODEL),
                  pl.BlockSpec((1, D_MODEL, N_EXPERTS), lambda i: (layer, 0, 0)),
                  vec(N_EXPERTS)],
        out_specs=[rows(D_MODEL), rows(D_MODEL // 2), rows(TOP_K), rows(TOP_K), rows(TOP_K),
                   pl.BlockSpec((1, 1, N_EXPERTS), lambda i: (i, 0, 0))],
        out_shape=[jax.ShapeDtypeStruct((T_ALL, D_MODEL), F32),
                   jax.ShapeDtypeStruct((T_ALL, D_MODEL // 2), jnp.uint32),
                   jax.ShapeDtypeStruct((T_ALL, TOP_K), jnp.int32),
                   jax.ShapeDtypeStruct((T_ALL, TOP_K), F32),
                   jax.ShapeDtypeStruct((T_ALL, TOP_K), jnp.int32),
                   jax.ShapeDtypeStruct((T_ALL // ROW_BLOCK, 1, N_EXPERTS), jnp.int32)],
        compiler_params=_cparams("parallel"),
        name="out_proj",
    )(o_r, o_s, x, w_out_bf, w_out_bf, ln_g, ln_b, router_w_bf, router_b)


def _moe_kernel(be_ref, first_ref, nused_ref, xs_ref, wgu_ref, bgu_ref, wd_ref, bd_ref,
                ys_ref, wgu_bf, wd_bf):
    i = pl.program_id(0)

    @pl.when(first_ref[i] == 1)
    def _():
        wgu_bf[...] = wgu_ref[0, 0].astype(BF16)
        wd_bf[...] = wd_ref[0, 0].astype(BF16)

    @pl.when(i < nused_ref[0])
    def _():
        lo, hi = _unpack_bf16_pairs(xs_ref[...])
        half = D_MODEL // 2
        gu = (jnp.dot(lo, wgu_bf[:half, :], preferred_element_type=F32)
              + jnp.dot(hi, wgu_bf[half:, :], preferred_element_type=F32) + bgu_ref[0, 0])
        glu = jnp.minimum(gu[:, :D_FF], SWIGLU_LIMIT)
        lin = jnp.clip(gu[:, D_FF:], -SWIGLU_LIMIT, SWIGLU_LIMIT)
        act = glu * jax.nn.sigmoid(SWIGLU_ALPHA * glu) * (lin + 1.0)
        ys_ref[...] = jnp.dot(act.astype(BF16), wd_bf[...], preferred_element_type=F32) + bd_ref[0, 0]


def _moe(block_expert, first, n_used, xs, w_gate_up, b_gate_up, w_down, b_down, layer):
    def blk(i, be, fr, nu):
        return (jnp.minimum(i, nu[0] - 1), 0)

    def wsel(i, be, fr, nu):
        return (layer, be[i], 0, 0)

    return pl.pallas_call(
        _moe_kernel,
        grid_spec=pltpu.PrefetchScalarGridSpec(
            num_scalar_prefetch=3,
            grid=(MOE_BLOCKS,),
            in_specs=[pl.BlockSpec((MOE_ROWS, D_MODEL // 2), blk),
                      pl.BlockSpec((1, 1, D_MODEL, 2 * D_FF), wsel),
                      pl.BlockSpec((1, 1, 1, 2 * D_FF), wsel),
                      pl.BlockSpec((1, 1, D_FF, D_MODEL), wsel),
                      pl.BlockSpec((1, 1, 1, D_MODEL), wsel)],
            out_specs=pl.BlockSpec((MOE_ROWS, D_MODEL), blk),
            scratch_shapes=[pltpu.VMEM((D_MODEL, 2 * D_FF), BF16), pltpu.VMEM((D_FF, D_MODEL), BF16)]),
        out_shape=jax.ShapeDtypeStruct((MOE_SLOTS, D_MODEL), F32),
        compiler_params=_cparams("arbitrary"),
        name="moe_experts",
    )(block_expert, first, n_used, xs, w_gate_up, b_gate_up, w_down, b_down)


def _route(idx, rank, counts):
    experts = jnp.arange(N_EXPERTS, dtype=jnp.int32)
    counts = counts.reshape(T_ALL // ROW_BLOCK, N_EXPERTS)
    before_block = jnp.cumsum(counts, axis=0) - counts
    total = jnp.sum(counts, axis=0)
    padded = (total + MOE_ROWS - 1) // MOE_ROWS * MOE_ROWS
    pad_end = jnp.cumsum(padded)
    pad_start = pad_end - padded
    base = jnp.repeat(pad_start[None, :] + before_block, ROW_BLOCK, axis=0)
    pos = jnp.sum(jnp.where(idx[:, :, None] == experts[None, None, :], base[:, None, :], 0), axis=-1) + rank
    gap = padded - total
    gap_end = jnp.concatenate([jnp.cumsum(gap), jnp.full((1,), N_UNUSED, jnp.int32)])
    gap_start = jnp.concatenate([jnp.zeros((1,), jnp.int32), gap_end[:-1]])
    first_free = jnp.concatenate([pad_start + total, pad_end[-1:]])
    j = jnp.arange(N_UNUSED, dtype=jnp.int32)
    region = (j[:, None] >= gap_start[None, :]) & (j[:, None] < gap_end[None, :])
    unused = jnp.sum(jnp.where(region, first_free[None, :] + j[:, None] - gap_start[None, :], 0), axis=-1)
    n_used = (pad_end[-1] // MOE_ROWS).astype(jnp.int32)
    blk = jnp.minimum(jnp.arange(MOE_BLOCKS, dtype=jnp.int32), n_used - 1)
    block_expert = jnp.minimum(jnp.sum((blk[:, None] * MOE_ROWS >= pad_end[None, :]).astype(jnp.int32), axis=-1),
                               N_EXPERTS - 1)
    first = jnp.concatenate([jnp.ones((1,), jnp.int32),
                             (block_expert[1:] != block_expert[:-1]).astype(jnp.int32)])
    return pos.T.astype(jnp.int32), unused.astype(jnp.int32), block_expert, first, n_used.reshape(1)


def _sc_worker():
    return lax.axis_index("s") * SC_CORES + lax.axis_index("c")


def _sc_dispatch(xpk, pos_km, unused, zero_rows):
    ch = SC_DISPATCH_ROWS
    n_chunks = T_ALL // ch
    n_zero_chunks = N_UNUSED // ch
    width = xpk.shape[1]
    mesh = plsc.VectorSubcoreMesh(core_axis_name="c", subcore_axis_name="s")

    @functools.partial(
        pl.kernel, mesh=mesh,
        out_type=jax.ShapeDtypeStruct((MOE_SLOTS, width), xpk.dtype),
        scratch_types=[pltpu.VMEM((2, ch, width), xpk.dtype), pltpu.VMEM((2, TOP_K, ch), jnp.int32),
                       pltpu.SemaphoreType.DMA((2,)), pltpu.SemaphoreType.DMA((TOP_K,))],
        name="sc_dispatch")
    def run(x_hbm, pos_hbm, unused_hbm, zero_hbm, xs_hbm, rows_v, idx_v, load_sems, store_sems):
        wid = _sc_worker()
        pltpu.sync_copy(zero_hbm, rows_v.at[0])
        for r in range(n_zero_chunks // SC_WORKERS):
            start = pl.multiple_of((wid + r * SC_WORKERS) * ch, ch)
            pltpu.sync_copy(unused_hbm.at[pl.ds(start, ch)], idx_v.at[0, 0])
            pltpu.sync_copy(rows_v.at[0], xs_hbm.at[idx_v.at[0, 0]])

        def first_row(r):
            return pl.multiple_of((wid + r * SC_WORKERS) * ch, ch)

        def rows_copy(r, slot):
            return pltpu.make_async_copy(x_hbm.at[pl.ds(first_row(r), ch)], rows_v.at[slot], load_sems.at[slot])

        def start_load(r, slot):
            for k in range(TOP_K):
                pltpu.sync_copy(pos_hbm.at[k, pl.ds(first_row(r), ch)], idx_v.at[slot, k])
            rows_copy(r, slot).start()

        def has_chunk(r):
            return wid + r * SC_WORKERS < n_chunks

        start_load(0, 0)

        def body(r, carry):
            slot = r & 1

            @pl.when(has_chunk(r + 1))
            def _():
                start_load(r + 1, 1 - slot)

            @pl.when(has_chunk(r))
            def _():
                rows_copy(r, slot).wait()
                copies = [pltpu.async_copy(rows_v.at[slot], xs_hbm.at[idx_v.at[slot, k]], store_sems.at[k])
                          for k in range(TOP_K)]
                for cp in copies:
                    cp.wait()
            return carry

        lax.fori_loop(0, pl.cdiv(n_chunks, SC_WORKERS), body, 0)

    return run(xpk, pos_km, unused, zero_rows)


def _sc_collect(ys, pos_flat):
    ch = SC_COLLECT_ROWS
    per_worker = N_ASSIGN // SC_WORKERS
    mesh = plsc.VectorSubcoreMesh(core_axis_name="c", subcore_axis_name="s")

    @functools.partial(
        pl.kernel, mesh=mesh,
        out_type=jax.ShapeDtypeStruct((N_ASSIGN, D_MODEL), ys.dtype),
        scratch_types=[pltpu.VMEM((2, ch, D_MODEL), ys.dtype), pltpu.VMEM((2, ch), jnp.int32),
                       pltpu.SemaphoreType.DMA((2,))],
        name="sc_collect")
    def run(ys_hbm, pos_hbm, y4_hbm, rows_v, idx_v, sems):
        wid = _sc_worker()
        n_chunks = per_worker // ch

        def first_row(r):
            return pl.multiple_of(wid * per_worker + r * ch, 8)

        def gather(slot):
            return pltpu.make_async_copy(ys_hbm.at[idx_v.at[slot]], rows_v.at[slot], sems.at[slot])

        def start_gather(r, slot):
            pltpu.sync_copy(pos_hbm.at[pl.ds(first_row(r), ch)], idx_v.at[slot])
            gather(slot).start()

        start_gather(0, 0)

        def body(r, carry):
            slot = r & 1

            @pl.when(r + 1 < n_chunks)
            def _():
                start_gather(r + 1, 1 - slot)

            gather(slot).wait()
            pltpu.sync_copy(rows_v.at[slot], y4_hbm.at[pl.ds(first_row(r), ch)])
            return carry

        lax.fori_loop(0, n_chunks, body, 0)

    return run(ys, pos_flat)


def _ffn_norm_kernel(x_ref, y0_ref, y1_ref, y2_ref, y3_ref, gate_ref, g_ref, b_ref, o_ref):
    gate = gate_ref[...]
    ffn = (gate[:, 0:1] * y0_ref[0] + gate[:, 1:2] * y1_ref[0]
           + gate[:, 2:3] * y2_ref[0] + gate[:, 3:4] * y3_ref[0])
    o_ref[...] = _layer_norm(DEEPNORM_ALPHA * x_ref[...] + ffn, g_ref[0], b_ref[0])


def _ffn_norm(x1, y4, gate, ln_g, ln_b, layer, row_block, first_block, n_blocks):
    def rows(w):
        return pl.BlockSpec((row_block, w), lambda i: (first_block + i, 0))

    def plane(k):
        return pl.BlockSpec((1, row_block, D_MODEL), lambda i: (k, first_block + i, 0))

    vec = pl.BlockSpec((1, 1, D_MODEL), lambda i: (layer, 0, 0))
    return pl.pallas_call(
        _ffn_norm_kernel,
        grid=(n_blocks,),
        in_specs=[rows(D_MODEL), plane(0), plane(1), plane(2), plane(3), rows(TOP_K), vec, vec],
        out_specs=pl.BlockSpec((row_block, D_MODEL), lambda i: (i, 0)),
        out_shape=jax.ShapeDtypeStruct((n_blocks * row_block, D_MODEL), F32),
        compiler_params=_cparams("parallel"),
        name="ffn_norm",
    )(x1, y4, y4, y4, y4, gate, ln_g, ln_b)


def _rope_tables(pos):
    half = HD_RET // 2
    inv = ROPE_BASE ** (-jnp.arange(half, dtype=F32) / half)
    ang = pos.astype(F32)[:, None] * inv[None, :]
    cos, sin = jnp.cos(ang), jnp.sin(ang)
    return jnp.concatenate([cos, cos], -1), jnp.concatenate([-sin, sin], -1)


def _decay_tables(c):
    log_g = jnp.log1p(-jnp.exp2(-5.0 - jnp.arange(H_RET, dtype=F32)))
    idx = jnp.arange(c, dtype=F32)
    diff = idx[:, None] - idx[None, :]
    intra = jnp.where(diff[None] >= 0, jnp.exp(jnp.maximum(diff, 0.0)[None] * log_g[:, None, None]), 0.0)
    q_dec = jnp.exp((idx + 1.0)[None] * log_g[:, None])
    k_dec = jnp.exp((c - 1.0 - idx)[None] * log_g[:, None])
    c_dec = jnp.exp(c * log_g)
    return intra, q_dec, k_dec, c_dec


def _t5_bucket(dist):
    n = jnp.maximum(dist, 0)
    max_exact = NUM_BUCKETS // 2
    nf = jnp.maximum(n, 1).astype(F32)
    large = max_exact + (jnp.log(nf / max_exact) / math.log(REL_MAX_DIST / max_exact)
                         * (NUM_BUCKETS - max_exact)).astype(jnp.int32)
    large = jnp.minimum(large, NUM_BUCKETS - 1)
    return jnp.where(n < max_exact, n, large)


def _bias_table(rel_bias, bucket):
    hit = bucket[..., None] == jnp.arange(NUM_BUCKETS)
    per_head = rel_bias.astype(F32).T.reshape((H_SWA,) + (1,) * bucket.ndim + (NUM_BUCKETS,))
    return jnp.sum(jnp.where(hit[None], per_head, 0.0), axis=-1)


def kernel(x_prompt, x_sample, state_ret, cache_swa_k, cache_swa_v, w_in, w_out, ret_gn_gain, swa_sinks,
           rel_bias, ln1_g, ln1_b, ln2_g, ln2_b, router_w, router_b, w_gate_up, b_gate_up, w_down, b_down):
    x = jnp.concatenate([x_prompt.reshape(T_PROMPT, D_MODEL), x_sample.reshape(DEC_BATCH, D_MODEL)], axis=0)
    w_in_bf = w_in.astype(BF16)
    w_out_bf = w_out.astype(BF16)
    router_w_bf = router_w.astype(BF16)
    ret_gn_gain, ln1_g, ln1_b, ln2_g, ln2_b, router_b = (
        a.reshape(DEPTH, 1, a.shape[-1]) for a in (ret_gn_gain, ln1_g, ln1_b, ln2_g, ln2_b, router_b))
    b_gu4 = b_gate_up.reshape(DEPTH, N_EXPERTS, 1, 2 * D_FF)
    b_d4 = b_down.reshape(DEPTH, N_EXPERTS, 1, D_MODEL)
    cache_k = cache_swa_k.reshape(DEPTH, DEC_BATCH, WINDOW, 2 * HD_SWA)
    cache_v = cache_swa_v.reshape(DEPTH, DEC_BATCH, WINDOW, 2 * HD_SWA)

    cos_p, sin_p = _rope_tables(jnp.arange(SEQ, dtype=jnp.int32))
    cos_s, sin_s = _rope_tables(PAST_LEN + jnp.arange(1, dtype=jnp.int32))
    intra, q_dec, k_dec, c_dec = _decay_tables(RET_CHUNK)
    ones = jnp.ones((1, 1, HD_RET), F32)
    tables_p = (cos_p, sin_p, intra, q_dec[:, :, None] * ones, k_dec[:, :, None] * ones, c_dec[:, None, None] * ones)
    intra1, q_dec1, k_dec1, c_dec1 = _decay_tables(1)
    dec_s = jnp.stack([intra1[:, 0, 0], q_dec1[:, 0], k_dec1[:, 0], c_dec1], axis=1)
    i = jnp.arange(WINDOW)[:, None]
    j = jnp.arange(2 * WINDOW)[None, :]
    bias_p = _bias_table(rel_bias, _t5_bucket(WINDOW + i - j))
    js = jnp.arange(WINDOW + 1)
    bias_s = _bias_table(rel_bias, _t5_bucket(WINDOW - js))
    bias_c, bias_n = bias_s[:, :WINDOW], bias_s[:, WINDOW:]
    kv_of_head = jnp.arange(H_SWA) // G_SWA
    head_mask = (kv_of_head[:, None] == jnp.arange(KV_SWA)[None, :]).astype(F32)

    zero_rows = jnp.zeros((SC_DISPATCH_ROWS, D_MODEL // 2), jnp.uint32)

    sp = ss = ko = vo = tail = None
    for l in range(DEPTH):
        h = _in_proj(x, w_in_bf, l)
        o_r, sp = _ret_prompt(h, tables_p, ret_gn_gain, sp, l)
        o_s, tail = _swa_prompt(h, swa_sinks[l], bias_p, tail, l)
        o_r, ss = _ret_sample(h, dec_s, cos_s, sin_s, ret_gn_gain, state_ret, o_r, ss, l)
        hs = h[T_PROMPT:, 4 * RET_W:]
        q8 = hs[:, :SWA_Q_W].reshape(DEC_BATCH, H_SWA, HD_SWA)
        qm = (q8[:, :, None, :] * head_mask[None, :, :, None]).reshape(DEC_BATCH, H_SWA, 2 * HD_SWA)
        kn = hs[:, SWA_Q_W:SWA_Q_W + SWA_KV_W].reshape(DEC_BATCH, 1, SWA_KV_W)
        vn = hs[:, SWA_Q_W + SWA_KV_W:].reshape(DEC_BATCH, 1, SWA_KV_W)
        osm, ko, vo = _swa_sample(qm, kn, vn, cache_k, cache_v, bias_c, bias_n, swa_sinks[l][:, None], ko, vo, l)
        os_s = jnp.sum(osm.reshape(DEC_BATCH, H_SWA, KV_SWA, HD_SWA) * head_mask[None, :, :, None], axis=2)
        o_s = lax.dynamic_update_slice(o_s, os_s.reshape(DEC_BATCH, SWA_Q_W), (T_PROMPT, 0))

        x1, xpk, top_idx, gate, rank, counts = _out_proj(o_r, o_s, x, w_out_bf, ln1_g, ln1_b, router_w_bf,
                                                         router_b, l)
        pos_km, unused, block_expert, first, n_used = _route(top_idx, rank, counts)
        xs = _sc_dispatch(xpk, pos_km, unused, zero_rows)
        ys = _moe(block_expert, first, n_used, xs, w_gate_up, b_gu4, w_down, b_d4, l)
        y4 = _sc_collect(ys, pos_km.reshape(N_ASSIGN)).reshape(TOP_K, T_ALL, D_MODEL)
        if l < DEPTH - 1:
            x = _ffn_norm(x1, y4, gate, ln2_g, ln2_b, l, ROW_BLOCK, 0, T_ALL // ROW_BLOCK)
        else:
            yp = _ffn_norm(x1, y4, gate, ln2_g, ln2_b, l, FINAL_ROW_BLOCK, 0, T_PROMPT // FINAL_ROW_BLOCK)
            ys_out = _ffn_norm(x1, y4, gate, ln2_g, ln2_b, l, DEC_BATCH, T_PROMPT // DEC_BATCH, 1)

    yp = yp.reshape(BATCH, SEQ, D_MODEL)
    ys_out = ys_out.reshape(DEC_BATCH, 1, D_MODEL)
    kcp = tail[..., :SWA_KV_W].reshape(DEPTH, BATCH, WINDOW, KV_SWA, HD_SWA)
    vcp = tail[..., SWA_KV_W:].reshape(DEPTH, BATCH, WINDOW, KV_SWA, HD_SWA)
    return (yp, ys_out, sp, kcp, vcp, ss,
            ko.reshape(DEPTH, DEC_BATCH, WINDOW, KV_SWA, HD_SWA),
            vo.reshape(DEPTH, DEC_BATCH, WINDOW, KV_SWA, HD_SWA))
```

```python
import functools
import math

import jax
import jax.numpy as jnp
from jax import lax
from jax.experimental import pallas as pl
from jax.experimental.pallas import tpu as pltpu
from jax.experimental.pallas import tpu_sc as plsc

F32 = jnp.float32
BF16 = jnp.bfloat16

D_MODEL = 1024
BATCH = 8
SEQ = 2048
DEPTH = 4
DEC_BATCH = 128
PAST_LEN = 8192
H_RET = 4
HD_RET = 128
RET_CHUNK = 128
ROPE_BASE = 10000.0
H_SWA = 8
KV_SWA = 2
G_SWA = H_SWA // KV_SWA
HD_SWA = 64
WINDOW = 128
NUM_BUCKETS = 32
REL_MAX_DIST = 128
RET_W = H_RET * HD_RET
SWA_Q_W = H_SWA * HD_SWA
SWA_KV_W = KV_SWA * HD_SWA
D_IN = 4 * RET_W + SWA_Q_W + 2 * SWA_KV_W
N_EXPERTS = 32
TOP_K = 4
D_FF = D_MODEL
SWIGLU_LIMIT = 7.0
SWIGLU_ALPHA = 1.702
DEEPNORM_ALPHA = (2 * DEPTH) ** 0.25
LN_EPS = 1e-5
GN_EPS = 1e-5

T_PROMPT = BATCH * SEQ
T_ALL = T_PROMPT + DEC_BATCH
ROW_BLOCK = 384
SAMPLE_BLOCK = 8
MOE_ROWS = 256
MOE_CHUNK = 512
N_ASSIGN = T_ALL * TOP_K
MOE_BLOCKS = N_ASSIGN // MOE_ROWS + N_EXPERTS
MOE_SLOTS = MOE_BLOCKS * MOE_ROWS
N_UNUSED = MOE_SLOTS - N_ASSIGN
SC_CORES = 2
SC_WORKERS = SC_CORES * 16
SC_DISPATCH_ROWS = 64
SC_COLLECT_ROWS = 48
FINAL_ROW_BLOCK = 512
VMEM_LIMIT = 56 * 1024 * 1024


def _cparams(*sem):
    return pltpu.CompilerParams(dimension_semantics=sem, vmem_limit_bytes=VMEM_LIMIT)


def _in_proj_kernel(x_ref, w_ref, h_ref):
    h_ref[...] = jnp.dot(x_ref[...].astype(BF16), w_ref[0], preferred_element_type=F32)


def _in_proj(x, w_in_bf, layer):
    return pl.pallas_call(
        _in_proj_kernel,
        grid=(T_ALL // ROW_BLOCK,),
        in_specs=[pl.BlockSpec((ROW_BLOCK, D_MODEL), lambda i: (i, 0)),
                  pl.BlockSpec((1, D_MODEL, D_IN), lambda i: (layer, 0, 0))],
        out_specs=pl.BlockSpec((ROW_BLOCK, D_IN), lambda i: (i, 0)),
        out_shape=jax.ShapeDtypeStruct((T_ALL, D_IN), F32),
        compiler_params=_cparams("parallel"),
        name="in_proj",
    )(x, w_in_bf)


def _rotate(x, cos, sin_signed):
    return x * cos + pltpu.roll(x, HD_RET // 2, 1) * sin_signed


def _group_norm_gate(o, gain, g):
    mu = jnp.mean(o, -1, keepdims=True)
    var = jnp.mean(jnp.square(o - mu), -1, keepdims=True)
    return (o - mu) * lax.rsqrt(var + GN_EPS) * gain * (g * jax.nn.sigmoid(g))


def _layer_norm(y, g, b):
    mu = jnp.mean(y, -1, keepdims=True)
    var = jnp.mean(jnp.square(y - mu), -1, keepdims=True)
    return (y - mu) * lax.rsqrt(var + LN_EPS) * g + b


def _ret_prompt_kernel(q_ref, k_ref, v_ref, g_ref, cos_ref, sin_ref, intra_ref, qdec_ref, kdec_ref,
                       cdec_ref, gain_ref, o_ref, s_ref):
    @pl.when(pl.program_id(1) == 0)
    def _():
        s_ref[...] = jnp.zeros_like(s_ref)

    cos = cos_ref[...]
    sin = sin_ref[...]
    nt = (((1,), (1,)), ((), ()))
    tn = (((0,), (0,)), ((), ()))
    heads = [slice(h * HD_RET, (h + 1) * HD_RET) for h in range(H_RET)]
    vs, atts, crosses = [], [], []
    for h, sl in enumerate(heads):
        q = _rotate(q_ref[:, sl], cos, sin)
        k = _rotate(k_ref[:, sl], cos, sin) * (HD_RET ** -0.5)
        v = v_ref[:, sl].astype(BF16)
        s = s_ref[0, 0, h]
        atts.append(lax.dot_general(q.astype(BF16), k.astype(BF16), nt, preferred_element_type=F32))
        crosses.append(jnp.dot((q * qdec_ref[h]).astype(BF16), s.astype(BF16), preferred_element_type=F32))
        s_ref[0, 0, h] = s * cdec_ref[h] + lax.dot_general((k * kdec_ref[h]).astype(BF16), v, tn,
                                                          preferred_element_type=F32)
        vs.append(v)
    atts = [(att * intra_ref[h]).astype(BF16) for h, att in enumerate(atts)]
    outs = [jnp.dot(atts[h], vs[h], preferred_element_type=F32) + crosses[h] for h in range(H_RET)]
    for h, sl in enumerate(heads):
        o_ref[:, sl] = _group_norm_gate(outs[h], gain_ref[0, :, sl], g_ref[:, sl])


def _ret_prompt(h, tables, gain, s_prev, layer):
    cos, sin, intra, qdec, kdec, cdec = tables
    nc = SEQ // RET_CHUNK

    def col(j):
        return pl.BlockSpec((RET_CHUNK, RET_W), lambda b, c: (b * nc + c, j))

    def whole(a):
        return pl.BlockSpec(a.shape, lambda b, c: (0,) * a.ndim)

    in_specs = [col(0), col(1), col(2), col(3),
                pl.BlockSpec((RET_CHUNK, HD_RET), lambda b, c: (c, 0)),
                pl.BlockSpec((RET_CHUNK, HD_RET), lambda b, c: (c, 0)),
                whole(intra), whole(qdec), whole(kdec), whole(cdec),
                pl.BlockSpec((1, 1, RET_W), lambda b, c: (layer, 0, 0))]
    args = [h, h, h, h, cos, sin, intra, qdec, kdec, cdec, gain]
    aliases = {}
    if s_prev is not None:
        in_specs.append(pl.BlockSpec(memory_space=pl.ANY))
        args.append(s_prev)
        aliases = {len(args) - 1: 1}

    def body(*refs):
        _ret_prompt_kernel(*refs[:11], *refs[-2:])

    return pl.pallas_call(
        body,
        grid=(BATCH, nc),
        in_specs=in_specs,
        out_specs=[pl.BlockSpec((RET_CHUNK, RET_W), lambda b, c: (b * nc + c, 0)),
                   pl.BlockSpec((1, 1, H_RET, HD_RET, HD_RET), lambda b, c: (layer, b, 0, 0, 0))],
        out_shape=[jax.ShapeDtypeStruct((T_ALL, RET_W), F32),
                   jax.ShapeDtypeStruct((DEPTH, BATCH, H_RET, HD_RET, HD_RET), F32)],
        input_output_aliases=aliases,
        compiler_params=_cparams("parallel", "arbitrary"),
        name="ret_prompt",
    )(*args)


def _swa_prompt_kernel(sink_ref, q_ref, kvc_ref, kvp_ref, bias_ref, o_ref, tail_ref, lg_scr, p_scr):
    c = pl.program_id(1)

    @pl.when(c == pl.num_programs(1) - 1)
    def _():
        tail_ref[0, 0] = kvc_ref[...]

    row = lax.broadcasted_iota(jnp.int32, (WINDOW, 2 * WINDOW), 0)
    colj = lax.broadcasted_iota(jnp.int32, (WINDOW, 2 * WINDOW), 1)
    valid = (colj > row) & (colj <= row + WINDOW) & jnp.logical_or(c > 0, colj >= WINDOW)

    kv = jnp.concatenate([kvp_ref[...], kvc_ref[...]], axis=0)
    k_t = kv[:, :SWA_KV_W].T
    k_t_swapped = pltpu.roll(k_t, HD_SWA, 0)
    v = kv[:, SWA_KV_W:]
    v_swapped = pltpu.roll(v, HD_SWA, 1)
    k_row = lax.broadcasted_iota(jnp.int32, k_t.shape, 0)
    v_lane = lax.broadcasted_iota(jnp.int32, v.shape, 1)
    ones = jnp.ones(v.shape, BF16)
    k_pad, v_pad = {}, {}
    for kvh in range(KV_SWA):
        for par in range(2):
            k_src = k_t if kvh == par else k_t_swapped
            v_src = v if kvh == par else v_swapped
            k_pad[kvh, par] = jnp.where((k_row >= par * HD_SWA) & (k_row < (par + 1) * HD_SWA), k_src, 0.0).astype(BF16)
            v_pad[kvh, par] = jnp.where((v_lane >= par * HD_SWA) & (v_lane < (par + 1) * HD_SWA), v_src, 0.0).astype(BF16)
    for hd in range(H_SWA):
        pair, par, kvh = hd // 2, hd % 2, hd // G_SWA
        q2 = (q_ref[:, pair * 2 * HD_SWA:(pair + 1) * 2 * HD_SWA] * (HD_SWA ** -0.5)).astype(BF16)
        logits = jnp.dot(q2, k_pad[kvh, par], preferred_element_type=F32) + bias_ref[hd]
        lg_scr[hd] = jnp.where(valid, logits, -jnp.inf)
    sink_terms = []
    for hd in range(H_SWA):
        logits = lg_scr[hd]
        sink = sink_ref[hd]
        m = jnp.maximum(jnp.max(logits, -1, keepdims=True), sink)
        p_scr[hd] = jnp.exp(logits - m).astype(BF16)
        sink_terms.append(jnp.exp(sink - m))
    for pair in range(H_SWA // 2):
        acc = None
        for par in range(2):
            hd = 2 * pair + par
            p = p_scr[hd]
            den = jnp.dot(p, ones, preferred_element_type=F32) + sink_terms[hd]
            part = jnp.dot(p, v_pad[hd // G_SWA, par], preferred_element_type=F32) / den
            acc = part if acc is None else acc + part
        o_ref[:, pair * 2 * HD_SWA:(pair + 1) * 2 * HD_SWA] = acc


def _swa_prompt(h, sinks_l, bias, tail_prev, layer):
    nb = SEQ // WINDOW
    qcol = 4 * RET_W // SWA_Q_W
    kvcol = (4 * RET_W + SWA_Q_W) // (2 * SWA_KV_W)
    in_specs = [pl.BlockSpec(memory_space=pltpu.SMEM),
                pl.BlockSpec((WINDOW, SWA_Q_W), lambda b, c: (b * nb + c, qcol)),
                pl.BlockSpec((WINDOW, 2 * SWA_KV_W), lambda b, c: (b * nb + c, kvcol)),
                pl.BlockSpec((WINDOW, 2 * SWA_KV_W), lambda b, c: (b * nb + jnp.maximum(c - 1, 0), kvcol)),
                pl.BlockSpec((H_SWA, WINDOW, 2 * WINDOW), lambda b, c: (0, 0, 0))]
    args = [sinks_l, h, h, h, bias]
    aliases = {}
    if tail_prev is not None:
        in_specs.append(pl.BlockSpec(memory_space=pl.ANY))
        args.append(tail_prev)
        aliases = {5: 1}

    def body(*refs):
        _swa_prompt_kernel(*refs[:5], *refs[-4:])

    return pl.pallas_call(
        body,
        grid=(BATCH, nb),
        in_specs=in_specs,
        scratch_shapes=[pltpu.VMEM((H_SWA, WINDOW, 2 * WINDOW), F32), pltpu.VMEM((H_SWA, WINDOW, 2 * WINDOW), BF16)],
        out_specs=[pl.BlockSpec((WINDOW, SWA_Q_W), lambda b, c: (b * nb + c, 0)),
                   pl.BlockSpec((1, 1, WINDOW, 2 * SWA_KV_W), lambda b, c: (layer, b, 0, 0))],
        out_shape=[jax.ShapeDtypeStruct((T_ALL, SWA_Q_W), F32),
                   jax.ShapeDtypeStruct((DEPTH, BATCH, WINDOW, 2 * SWA_KV_W), F32)],
        input_output_aliases=aliases,
        compiler_params=_cparams("parallel", "arbitrary"),
        name="swa_prompt",
    )(*args)


def _ret_sample_kernel(dec_ref, q_ref, k_ref, v_ref, g_ref, cos_ref, sin_ref, gain_ref, st_ref, o_ref, so_ref):
    cos = cos_ref[...]
    sin = sin_ref[...]
    for h in range(H_RET):
        sl = slice(h * HD_RET, (h + 1) * HD_RET)
        intra, qdec, kdec, cdec = dec_ref[h, 0], dec_ref[h, 1], dec_ref[h, 2], dec_ref[h, 3]
        q = _rotate(q_ref[:, sl], cos, sin)
        k = _rotate(k_ref[:, sl], cos, sin) * (HD_RET ** -0.5)
        v = v_ref[:, sl]
        att = jnp.sum(q * k, -1, keepdims=True) * intra
        q_col = (q * qdec).T
        k_col = (k * kdec).T
        rows = []
        for b in range(SAMPLE_BLOCK):
            s = st_ref[0, b, h]
            v_row = v[b:b + 1, :]
            rows.append(att[b:b + 1, :] * v_row + jnp.sum(q_col[:, b:b + 1] * s, axis=0, keepdims=True))
            so_ref[0, b, h] = s * cdec + k_col[:, b:b + 1] * v_row
        o = jnp.concatenate(rows, axis=0)
        o_ref[:, sl] = _group_norm_gate(o, gain_ref[0, :, sl], g_ref[:, sl])


def _ret_sample(h, dec_s, cos_s, sin_s, gain, state_ret, o_r, st_prev, layer):
    base = T_PROMPT // SAMPLE_BLOCK

    def col(j):
        return pl.BlockSpec((SAMPLE_BLOCK, RET_W), lambda i: (base + i, j))

    in_specs = [pl.BlockSpec(memory_space=pltpu.SMEM),
                col(0), col(1), col(2), col(3),
                pl.BlockSpec((1, HD_RET), lambda i: (0, 0)),
                pl.BlockSpec((1, HD_RET), lambda i: (0, 0)),
                pl.BlockSpec((1, 1, RET_W), lambda i: (layer, 0, 0)),
                pl.BlockSpec((1, SAMPLE_BLOCK, H_RET, HD_RET, HD_RET), lambda i: (layer, i, 0, 0, 0)),
                pl.BlockSpec(memory_space=pl.ANY)]
    args = [dec_s, h, h, h, h, cos_s, sin_s, gain, state_ret, o_r]
    aliases = {9: 0}
    if st_prev is not None:
        in_specs.append(pl.BlockSpec(memory_space=pl.ANY))
        args.append(st_prev)
        aliases[10] = 1

    def body(*refs):
        _ret_sample_kernel(*refs[:9], *refs[-2:])

    return pl.pallas_call(
        body,
        grid=(DEC_BATCH // SAMPLE_BLOCK,),
        in_specs=in_specs,
        out_specs=[pl.BlockSpec((SAMPLE_BLOCK, RET_W), lambda i: (base + i, 0)),
                   pl.BlockSpec((1, SAMPLE_BLOCK, H_RET, HD_RET, HD_RET), lambda i: (layer, i, 0, 0, 0))],
        out_shape=[jax.ShapeDtypeStruct((T_ALL, RET_W), F32),
                   jax.ShapeDtypeStruct((DEPTH, DEC_BATCH, H_RET, HD_RET, HD_RET), F32)],
        input_output_aliases=aliases,
        compiler_params=_cparams("parallel"),
        name="ret_sample",
    )(*args)


def _swa_sample_kernel(qm_ref, kn_ref, vn_ref, kc_ref, vc_ref, bias_ref, biasn_ref, sink_ref,
                       o_ref, ko_ref, vo_ref):
    qm = qm_ref[...]
    kn = kn_ref[...]
    vn = vn_ref[...]
    kc = kc_ref[0]
    vc = vc_ref[0]
    scale = HD_SWA ** -0.5
    logits = jnp.einsum('bhd,bjd->bhj', qm.astype(BF16), kc.astype(BF16),
                        preferred_element_type=F32) * scale + bias_ref[...][None]
    j = lax.broadcasted_iota(jnp.int32, logits.shape, 2)
    logits = jnp.where(j >= 1, logits, -jnp.inf)
    ln = jnp.sum(qm * kn, -1, keepdims=True) * scale + biasn_ref[...][None]
    sink = sink_ref[...][None]
    m = jnp.maximum(jnp.maximum(jnp.max(logits, -1, keepdims=True), ln), sink)
    p = jnp.exp(logits - m)
    pn = jnp.exp(ln - m)
    denom = jnp.sum(p, -1, keepdims=True) + pn + jnp.exp(sink - m)
    w = p / denom
    o_ref[...] = (jnp.einsum('bhj,bjd->bhd', w.astype(BF16), vc.astype(BF16), preferred_element_type=F32)
                  + (pn / denom) * vn)
    ko_ref[0, :, 0:WINDOW - 1, :] = kc_ref[0, :, 1:WINDOW, :]
    ko_ref[0, :, WINDOW - 1:WINDOW, :] = kn
    vo_ref[0, :, 0:WINDOW - 1, :] = vc_ref[0, :, 1:WINDOW, :]
    vo_ref[0, :, WINDOW - 1:WINDOW, :] = vn


def _swa_sample(qm, kn, vn, cache_k, cache_v, bias_c, bias_n, sink_col, ko_prev, vo_prev, layer):
    sb = SAMPLE_BLOCK
    cache_spec = pl.BlockSpec((1, sb, WINDOW, 2 * HD_SWA), lambda i: (layer, i, 0, 0))
    in_specs = [pl.BlockSpec((sb, H_SWA, 2 * HD_SWA), lambda i: (i, 0, 0)),
                pl.BlockSpec((sb, 1, 2 * HD_SWA), lambda i: (i, 0, 0)),
                pl.BlockSpec((sb, 1, 2 * HD_SWA), lambda i: (i, 0, 0)),
                cache_spec, cache_spec,
                pl.BlockSpec((H_SWA, WINDOW), lambda i: (0, 0)),
                pl.BlockSpec((H_SWA, 1), lambda i: (0, 0)),
                pl.BlockSpec((H_SWA, 1), lambda i: (0, 0))]
    args = [qm, kn, vn, cache_k, cache_v, bias_c, bias_n, sink_col]
    aliases = {}
    if ko_prev is not None:
        in_specs += [pl.BlockSpec(memory_space=pl.ANY), pl.BlockSpec(memory_space=pl.ANY)]
        args += [ko_prev, vo_prev]
        aliases = {8: 1, 9: 2}

    def body(*refs):
        _swa_sample_kernel(*refs[:8], *refs[-3:])

    cshape = jax.ShapeDtypeStruct((DEPTH, DEC_BATCH, WINDOW, 2 * HD_SWA), F32)
    return pl.pallas_call(
        body,
        grid=(DEC_BATCH // sb,),
        in_specs=in_specs,
        out_specs=[pl.BlockSpec((sb, H_SWA, 2 * HD_SWA), lambda i: (i, 0, 0)), cache_spec, cache_spec],
        out_shape=[jax.ShapeDtypeStruct((DEC_BATCH, H_SWA, 2 * HD_SWA), F32), cshape, cshape],
        input_output_aliases=aliases,
        compiler_params=_cparams("parallel"),
        name="swa_sample",
    )(*args)


def _pack_bf16_pairs(x):
    w = x.shape[1] // 2
    lo = lax.bitcast_convert_type(x[:, :w].astype(BF16).astype(F32), jnp.uint32) >> 16
    hi = lax.bitcast_convert_type(x[:, w:].astype(BF16).astype(F32), jnp.uint32)
    return lo | hi


def _unpack_bf16_pairs(p):
    lo = lax.bitcast_convert_type(p << 16, F32).astype(BF16)
    hi = lax.bitcast_convert_type(p & jnp.uint32(0xFFFF0000), F32).astype(BF16)
    return lo, hi


def _out_proj_kernel(or_ref, os_ref, x_ref, wr_ref, ws_ref, g_ref, b_ref, rw_ref, rb_ref,
                     x1_ref, xpk_ref, idx_ref, gate_ref, rank_ref, cnt_ref):
    mix = (jnp.dot(or_ref[...].astype(BF16), wr_ref[0], preferred_element_type=F32)
           + jnp.dot(os_ref[...].astype(BF16), ws_ref[0], preferred_element_type=F32))
    x1 = _layer_norm(DEEPNORM_ALPHA * x_ref[...] + mix, g_ref[0], b_ref[0])
    x1_ref[...] = x1
    xpk_ref[...] = _pack_bf16_pairs(x1)
    logits = jnp.dot(x1.astype(BF16), rw_ref[0], preferred_element_type=F32) + rb_ref[0]
    lane = lax.broadcasted_iota(jnp.int32, logits.shape, 1)
    vals, hits = [], []
    for kk in range(TOP_K):
        m = jnp.max(logits, -1, keepdims=True)
        idx = jnp.min(jnp.where(logits == m, lane, N_EXPERTS), -1, keepdims=True)
        vals.append(m)
        hits.append(lane == idx)
        idx_ref[:, kk:kk + 1] = idx
        logits = jnp.where(lane == idx, -jnp.inf, logits)
    es = [jnp.exp(v - vals[0]) for v in vals]
    tot = es[0] + es[1] + es[2] + es[3]
    for kk in range(TOP_K):
        gate_ref[:, kk:kk + 1] = es[kk] / tot
    chosen = sum(h.astype(F32) for h in hits)
    rows = logits.shape[0]
    earlier = (lax.broadcasted_iota(jnp.int32, (rows, rows), 0)
               > lax.broadcasted_iota(jnp.int32, (rows, rows), 1)).astype(BF16)
    before = jnp.dot(earlier, chosen.astype(BF16), preferred_element_type=F32)
    for kk in range(TOP_K):
        rank_ref[:, kk:kk + 1] = jnp.sum(jnp.where(hits[kk], before, 0.0), -1, keepdims=True).astype(jnp.int32)
    cnt_ref[0] = jnp.sum(chosen, axis=0, keepdims=True).astype(jnp.int32)


def _out_proj(o_r, o_s, x, w_out_bf, ln_g, ln_b, router_w_bf, router_b, layer):
    def rows(w):
        return pl.BlockSpec((ROW_BLOCK, w), lambda i: (i, 0))

    def vec(w):
        return pl.BlockSpec((1, 1, w), lambda i: (layer, 0, 0))

    return pl.pallas_call(
        _out_proj_kernel,
        grid=(T_ALL // ROW_BLOCK,),
        in_specs=[rows(RET_W), rows(SWA_Q_W), rows(D_MODEL),
                  pl.BlockSpec((1, RET_W, D_MODEL), lambda i: (layer, 0, 0)),
                  pl.BlockSpec((1, SWA_Q_W, D_MODEL), lambda i: (layer, 1, 0)),
                  vec(D_MODEL), vec(D_MODEL),
                  pl.BlockSpec((1, D_MODEL, N_EXPERTS), lambda i: (layer, 0, 0)),
                  vec(N_EXPERTS)],
        out_specs=[rows(D_MODEL), rows(D_MODEL // 2), rows(TOP_K), rows(TOP_K), rows(TOP_K),
                   pl.BlockSpec((1, 1, N_EXPERTS), lambda i: (i, 0, 0))],
        out_shape=[jax.ShapeDtypeStruct((T_ALL, D_MODEL), F32),
                   jax.ShapeDtypeStruct((T_ALL, D_MODEL // 2), jnp.uint32),
                   jax.ShapeDtypeStruct((T_ALL, TOP_K), jnp.int32),
                   jax.ShapeDtypeStruct((T_ALL, TOP_K), F32),
                   jax.ShapeDtypeStruct((T_ALL, TOP_K), jnp.int32),
                   jax.ShapeDtypeStruct((T_ALL // ROW_BLOCK, 1, N_EXPERTS), jnp.int32)],
        compiler_params=_cparams("parallel"),
        name="out_proj",
    )(o_r, o_s, x, w_out_bf, w_out_bf, ln_g, ln_b, router_w_bf, router_b)


def _moe_kernel(start_ref, nblk_ref, xs_hbm, wgu_ref, bgu_ref, wd_ref, bd_ref, ys_hbm,
                wgu_bf, wd_bf, x_buf, y_buf, in_sems, out_sems, pending):
    e = pl.program_id(0)
    nblk = nblk_ref[e]
    n_full = nblk // (MOE_CHUNK // MOE_ROWS)
    has_tail = nblk % (MOE_CHUNK // MOE_ROWS) == 1
    base = start_ref[e]

    def in_copy(row0, rows, slot):
        src = xs_hbm.at[pl.ds(pl.multiple_of(row0, MOE_ROWS), rows)]
        return pltpu.make_async_copy(src, x_buf.at[slot, pl.ds(0, rows)], in_sems.at[slot])

    def out_copy(row0, rows, slot):
        dst = ys_hbm.at[pl.ds(pl.multiple_of(row0, MOE_ROWS), rows)]
        return pltpu.make_async_copy(y_buf.at[slot, pl.ds(0, rows)], dst, out_sems.at[slot])

    def drain(slot):
        for rows in (MOE_CHUNK, MOE_ROWS):
            @pl.when(pending[slot] == rows)
            def _():
                out_copy(0, rows, slot).wait()
        pending[slot] = 0

    def compute(rows, slot):
        lo, hi = _unpack_bf16_pairs(x_buf[slot, pl.ds(0, rows), :])
        half = D_MODEL // 2
        gu = (jnp.dot(lo, wgu_bf[:half, :], preferred_element_type=F32)
              + jnp.dot(hi, wgu_bf[half:, :], preferred_element_type=F32) + bgu_ref[0, 0])
        glu = jnp.minimum(gu[:, :D_FF], SWIGLU_LIMIT)
        lin = jnp.clip(gu[:, D_FF:], -SWIGLU_LIMIT, SWIGLU_LIMIT)
        act = glu * jax.nn.sigmoid(SWIGLU_ALPHA * glu) * (lin + 1.0)
        y_buf[slot, pl.ds(0, rows), :] = (jnp.dot(act.astype(BF16), wd_bf[...], preferred_element_type=F32)
                                          + bd_ref[0, 0])

    def run_chunk(row0, rows, slot):
        in_copy(row0, rows, slot).wait()
        drain(slot)
        compute(rows, slot)
        out_copy(row0, rows, slot).start()
        pending[slot] = rows

    @pl.when(e == 0)
    def _():
        pending[0] = 0
        pending[1] = 0

    @pl.when(nblk > 0)
    def _():
        @pl.when(n_full > 0)
        def _():
            in_copy(base, MOE_CHUNK, 0).start()

        @pl.when(n_full == 0)
        def _():
            in_copy(base, MOE_ROWS, 0).start()

        wgu_bf[...] = wgu_ref[0, 0].astype(BF16)
        wd_bf[...] = wd_ref[0, 0].astype(BF16)

        def body(c, carry):
            slot = c & 1
            row0 = pl.multiple_of(base + c * MOE_CHUNK, MOE_ROWS)

            @pl.when(c + 1 < n_full)
            def _():
                in_copy(row0 + MOE_CHUNK, MOE_CHUNK, 1 - slot).start()

            @pl.when(jnp.logical_and(c + 1 == n_full, has_tail))
            def _():
                in_copy(row0 + MOE_CHUNK, MOE_ROWS, 1 - slot).start()

            run_chunk(row0, MOE_CHUNK, slot)
            return carry

        lax.fori_loop(0, n_full, body, 0)

        @pl.when(has_tail)
        def _():
            run_chunk(pl.multiple_of(base + n_full * MOE_CHUNK, MOE_ROWS), MOE_ROWS, n_full & 1)

    @pl.when(e == pl.num_programs(0) - 1)
    def _():
        drain(0)
        drain(1)


def _moe(group_start, group_blocks, xs, w_gate_up, b_gate_up, w_down, b_down, layer):
    def wsel(e, st, nb):
        return (layer, e, 0, 0)

    return pl.pallas_call(
        _moe_kernel,
        grid_spec=pltpu.PrefetchScalarGridSpec(
            num_scalar_prefetch=2,
            grid=(N_EXPERTS,),
            in_specs=[pl.BlockSpec(memory_space=pl.ANY),
                      pl.BlockSpec((1, 1, D_MODEL, 2 * D_FF), wsel),
                      pl.BlockSpec((1, 1, 1, 2 * D_FF), wsel),
                      pl.BlockSpec((1, 1, D_FF, D_MODEL), wsel),
                      pl.BlockSpec((1, 1, 1, D_MODEL), wsel)],
            out_specs=pl.BlockSpec(memory_space=pl.ANY),
            scratch_shapes=[pltpu.VMEM((D_MODEL, 2 * D_FF), BF16), pltpu.VMEM((D_FF, D_MODEL), BF16),
                            pltpu.VMEM((2, MOE_CHUNK, D_MODEL // 2), jnp.uint32),
                            pltpu.VMEM((2, MOE_CHUNK, D_MODEL), F32),
                            pltpu.SemaphoreType.DMA((2,)), pltpu.SemaphoreType.DMA((2,)),
                            pltpu.SMEM((2,), jnp.int32)]),
        out_shape=jax.ShapeDtypeStruct((MOE_SLOTS, D_MODEL), F32),
        compiler_params=_cparams("arbitrary"),
        name="moe_experts",
    )(group_start, group_blocks, xs, w_gate_up, b_gate_up, w_down, b_down)


def _route(idx, rank, counts):
    experts = jnp.arange(N_EXPERTS, dtype=jnp.int32)
    counts = counts.reshape(T_ALL // ROW_BLOCK, N_EXPERTS)
    before_block = jnp.cumsum(counts, axis=0) - counts
    total = jnp.sum(counts, axis=0)
    padded = (total + MOE_ROWS - 1) // MOE_ROWS * MOE_ROWS
    pad_end = jnp.cumsum(padded)
    pad_start = pad_end - padded
    base = jnp.repeat(pad_start[None, :] + before_block, ROW_BLOCK, axis=0)
    pos = jnp.sum(jnp.where(idx[:, :, None] == experts[None, None, :], base[:, None, :], 0), axis=-1) + rank
    gap = padded - total
    gap_end = jnp.concatenate([jnp.cumsum(gap), jnp.full((1,), N_UNUSED, jnp.int32)])
    gap_start = jnp.concatenate([jnp.zeros((1,), jnp.int32), gap_end[:-1]])
    first_free = jnp.concatenate([pad_start + total, pad_end[-1:]])
    j = jnp.arange(N_UNUSED, dtype=jnp.int32)
    region = (j[:, None] >= gap_start[None, :]) & (j[:, None] < gap_end[None, :])
    unused = jnp.sum(jnp.where(region, first_free[None, :] + j[:, None] - gap_start[None, :], 0), axis=-1)
    return (pos.T.astype(jnp.int32), unused.astype(jnp.int32),
            pad_start.astype(jnp.int32), (padded // MOE_ROWS).astype(jnp.int32))


def _sc_worker():
    return lax.axis_index("s") * SC_CORES + lax.axis_index("c")


def _sc_dispatch(xpk, pos_km, unused, zero_rows):
    ch = SC_DISPATCH_ROWS
    n_chunks = T_ALL // ch
    n_zero_chunks = N_UNUSED // ch
    width = xpk.shape[1]
    mesh = plsc.VectorSubcoreMesh(core_axis_name="c", subcore_axis_name="s")

    @functools.partial(
        pl.kernel, mesh=mesh,
        out_type=jax.ShapeDtypeStruct((MOE_SLOTS, width), xpk.dtype),
        scratch_types=[pltpu.VMEM((2, ch, width), xpk.dtype), pltpu.VMEM((2, TOP_K, ch), jnp.int32),
                       pltpu.SemaphoreType.DMA((2,)), pltpu.SemaphoreType.DMA((TOP_K,))],
        name="sc_dispatch")
    def run(x_hbm, pos_hbm, unused_hbm, zero_hbm, xs_hbm, rows_v, idx_v, load_sems, store_sems):
        wid = _sc_worker()
        pltpu.sync_copy(zero_hbm, rows_v.at[0])
        for r in range(n_zero_chunks // SC_WORKERS):
            start = pl.multiple_of((wid + r * SC_WORKERS) * ch, ch)
            pltpu.sync_copy(unused_hbm.at[pl.ds(start, ch)], idx_v.at[0, 0])
            pltpu.sync_copy(rows_v.at[0], xs_hbm.at[idx_v.at[0, 0]])

        def first_row(r):
            return pl.multiple_of((wid + r * SC_WORKERS) * ch, ch)

        def rows_copy(r, slot):
            return pltpu.make_async_copy(x_hbm.at[pl.ds(first_row(r), ch)], rows_v.at[slot], load_sems.at[slot])

        def start_load(r, slot):
            for k in range(TOP_K):
                pltpu.sync_copy(pos_hbm.at[k, pl.ds(first_row(r), ch)], idx_v.at[slot, k])
            rows_copy(r, slot).start()

        def has_chunk(r):
            return wid + r * SC_WORKERS < n_chunks

        start_load(0, 0)

        def body(r, carry):
            slot = r & 1

            @pl.when(has_chunk(r + 1))
            def _():
                start_load(r + 1, 1 - slot)

            @pl.when(has_chunk(r))
            def _():
                rows_copy(r, slot).wait()
                copies = [pltpu.async_copy(rows_v.at[slot], xs_hbm.at[idx_v.at[slot, k]], store_sems.at[k])
                          for k in range(TOP_K)]
                for cp in copies:
                    cp.wait()
            return carry

        lax.fori_loop(0, pl.cdiv(n_chunks, SC_WORKERS), body, 0)

    return run(xpk, pos_km, unused, zero_rows)


def _sc_collect(ys, pos_flat):
    ch = SC_COLLECT_ROWS
    per_worker = N_ASSIGN // SC_WORKERS
    mesh = plsc.VectorSubcoreMesh(core_axis_name="c", subcore_axis_name="s")

    @functools.partial(
        pl.kernel, mesh=mesh,
        out_type=jax.ShapeDtypeStruct((N_ASSIGN, D_MODEL), ys.dtype),
        scratch_types=[pltpu.VMEM((2, ch, D_MODEL), ys.dtype), pltpu.VMEM((2, ch), jnp.int32),
                       pltpu.SemaphoreType.DMA((2,))],
        name="sc_collect")
    def run(ys_hbm, pos_hbm, y4_hbm, rows_v, idx_v, sems):
        wid = _sc_worker()
        n_chunks = per_worker // ch

        def first_row(r):
            return pl.multiple_of(wid * per_worker + r * ch, 8)

        def gather(slot):
            return pltpu.make_async_copy(ys_hbm.at[idx_v.at[slot]], rows_v.at[slot], sems.at[slot])

        def start_gather(r, slot):
            pltpu.sync_copy(pos_hbm.at[pl.ds(first_row(r), ch)], idx_v.at[slot])
            gather(slot).start()

        start_gather(0, 0)

        def body(r, carry):
            slot = r & 1

            @pl.when(r + 1 < n_chunks)
            def _():
                start_gather(r + 1, 1 - slot)

            gather(slot).wait()
            pltpu.sync_copy(rows_v.at[slot], y4_hbm.at[pl.ds(first_row(r), ch)])
            return carry

        lax.fori_loop(0, n_chunks, body, 0)

    return run(ys, pos_flat)


def _ffn_norm_kernel(x_ref, y0_ref, y1_ref, y2_ref, y3_ref, gate_ref, g_ref, b_ref, o_ref):
    gate = gate_ref[...]
    ffn = (gate[:, 0:1] * y0_ref[0] + gate[:, 1:2] * y1_ref[0]
           + gate[:, 2:3] * y2_ref[0] + gate[:, 3:4] * y3_ref[0])
    o_ref[...] = _layer_norm(DEEPNORM_ALPHA * x_ref[...] + ffn, g_ref[0], b_ref[0])


def _ffn_norm(x1, y4, gate, ln_g, ln_b, layer, row_block, first_block, n_blocks):
    def rows(w):
        return pl.BlockSpec((row_block, w), lambda i: (first_block + i, 0))

    def plane(k):
        return pl.BlockSpec((1, row_block, D_MODEL), lambda i: (k, first_block + i, 0))

    vec = pl.BlockSpec((1, 1, D_MODEL), lambda i: (layer, 0, 0))
    return pl.pallas_call(
        _ffn_norm_kernel,
        grid=(n_blocks,),
        in_specs=[rows(D_MODEL), plane(0), plane(1), plane(2), plane(3), rows(TOP_K), vec, vec],
        out_specs=pl.BlockSpec((row_block, D_MODEL), lambda i: (i, 0)),
        out_shape=jax.ShapeDtypeStruct((n_blocks * row_block, D_MODEL), F32),
        compiler_params=_cparams("parallel"),
        name="ffn_norm",
    )(x1, y4, y4, y4, y4, gate, ln_g, ln_b)


def _rope_tables(pos):
    half = HD_RET // 2
    inv = ROPE_BASE ** (-jnp.arange(half, dtype=F32) / half)
    ang = pos.astype(F32)[:, None] * inv[None, :]
    cos, sin = jnp.cos(ang), jnp.sin(ang)
    return jnp.concatenate([cos, cos], -1), jnp.concatenate([-sin, sin], -1)


def _decay_tables(c):
    log_g = jnp.log1p(-jnp.exp2(-5.0 - jnp.arange(H_RET, dtype=F32)))
    idx = jnp.arange(c, dtype=F32)
    diff = idx[:, None] - idx[None, :]
    intra = jnp.where(diff[None] >= 0, jnp.exp(jnp.maximum(diff, 0.0)[None] * log_g[:, None, None]), 0.0)
    q_dec = jnp.exp((idx + 1.0)[None] * log_g[:, None])
    k_dec = jnp.exp((c - 1.0 - idx)[None] * log_g[:, None])
    c_dec = jnp.exp(c * log_g)
    return intra, q_dec, k_dec, c_dec


def _t5_bucket(dist):
    n = jnp.maximum(dist, 0)
    max_exact = NUM_BUCKETS // 2
    nf = jnp.maximum(n, 1).astype(F32)
    large = max_exact + (jnp.log(nf / max_exact) / math.log(REL_MAX_DIST / max_exact)
                         * (NUM_BUCKETS - max_exact)).astype(jnp.int32)
    large = jnp.minimum(large, NUM_BUCKETS - 1)
    return jnp.where(n < max_exact, n, large)


def _bias_table(rel_bias, bucket):
    hit = bucket[..., None] == jnp.arange(NUM_BUCKETS)
    per_head = rel_bias.astype(F32).T.reshape((H_SWA,) + (1,) * bucket.ndim + (NUM_BUCKETS,))
    return jnp.sum(jnp.where(hit[None], per_head, 0.0), axis=-1)


def kernel(x_prompt, x_sample, state_ret, cache_swa_k, cache_swa_v, w_in, w_out, ret_gn_gain, swa_sinks,
           rel_bias, ln1_g, ln1_b, ln2_g, ln2_b, router_w, router_b, w_gate_up, b_gate_up, w_down, b_down):
    x = jnp.concatenate([x_prompt.reshape(T_PROMPT, D_MODEL), x_sample.reshape(DEC_BATCH, D_MODEL)], axis=0)
    w_in_bf = w_in.astype(BF16)
    w_out_bf = w_out.astype(BF16)
    router_w_bf = router_w.astype(BF16)
    ret_gn_gain, ln1_g, ln1_b, ln2_g, ln2_b, router_b = (
        a.reshape(DEPTH, 1, a.shape[-1]) for a in (ret_gn_gain, ln1_g, ln1_b, ln2_g, ln2_b, router_b))
    b_gu4 = b_gate_up.reshape(DEPTH, N_EXPERTS, 1, 2 * D_FF)
    b_d4 = b_down.reshape(DEPTH, N_EXPERTS, 1, D_MODEL)
    cache_k = cache_swa_k.reshape(DEPTH, DEC_BATCH, WINDOW, 2 * HD_SWA)
    cache_v = cache_swa_v.reshape(DEPTH, DEC_BATCH, WINDOW, 2 * HD_SWA)

    cos_p, sin_p = _rope_tables(jnp.arange(SEQ, dtype=jnp.int32))
    cos_s, sin_s = _rope_tables(PAST_LEN + jnp.arange(1, dtype=jnp.int32))
    intra, q_dec, k_dec, c_dec = _decay_tables(RET_CHUNK)
    ones = jnp.ones((1, 1, HD_RET), F32)
    tables_p = (cos_p, sin_p, intra, q_dec[:, :, None] * ones, k_dec[:, :, None] * ones, c_dec[:, None, None] * ones)
    intra1, q_dec1, k_dec1, c_dec1 = _decay_tables(1)
    dec_s = jnp.stack([intra1[:, 0, 0], q_dec1[:, 0], k_dec1[:, 0], c_dec1], axis=1)
    i = jnp.arange(WINDOW)[:, None]
    j = jnp.arange(2 * WINDOW)[None, :]
    bias_p = _bias_table(rel_bias, _t5_bucket(WINDOW + i - j))
    js = jnp.arange(WINDOW + 1)
    bias_s = _bias_table(rel_bias, _t5_bucket(WINDOW - js))
    bias_c, bias_n = bias_s[:, :WINDOW], bias_s[:, WINDOW:]
    kv_of_head = jnp.arange(H_SWA) // G_SWA
    head_mask = (kv_of_head[:, None] == jnp.arange(KV_SWA)[None, :]).astype(F32)

    zero_rows = jnp.zeros((SC_DISPATCH_ROWS, D_MODEL // 2), jnp.uint32)

    sp = ss = ko = vo = tail = None
    for l in range(DEPTH):
        h = _in_proj(x, w_in_bf, l)
        o_r, sp = _ret_prompt(h, tables_p, ret_gn_gain, sp, l)
        o_s, tail = _swa_prompt(h, swa_sinks[l], bias_p, tail, l)
        o_r, ss = _ret_sample(h, dec_s, cos_s, sin_s, ret_gn_gain, state_ret, o_r, ss, l)
        hs = h[T_PROMPT:, 4 * RET_W:]
        q8 = hs[:, :SWA_Q_W].reshape(DEC_BATCH, H_SWA, HD_SWA)
        qm = (q8[:, :, None, :] * head_mask[None, :, :, None]).reshape(DEC_BATCH, H_SWA, 2 * HD_SWA)
        kn = hs[:, SWA_Q_W:SWA_Q_W + SWA_KV_W].reshape(DEC_BATCH, 1, SWA_KV_W)
        vn = hs[:, SWA_Q_W + SWA_KV_W:].reshape(DEC_BATCH, 1, SWA_KV_W)
        osm, ko, vo = _swa_sample(qm, kn, vn, cache_k, cache_v, bias_c, bias_n, swa_sinks[l][:, None], ko, vo, l)
        os_s = jnp.sum(osm.reshape(DEC_BATCH, H_SWA, KV_SWA, HD_SWA) * head_mask[None, :, :, None], axis=2)
        o_s = lax.dynamic_update_slice(o_s, os_s.reshape(DEC_BATCH, SWA_Q_W), (T_PROMPT, 0))

        x1, xpk, top_idx, gate, rank, counts = _out_proj(o_r, o_s, x, w_out_bf, ln1_g, ln1_b, router_w_bf,
                                                         router_b, l)
        pos_km, unused, group_start, group_blocks = _route(top_idx, rank, counts)
        xs = _sc_dispatch(xpk, pos_km, unused, zero_rows)
        ys = _moe(group_start, group_blocks, xs, w_gate_up, b_gu4, w_down, b_d4, l)
        y4 = _sc_collect(ys, pos_km.reshape(N_ASSIGN)).reshape(TOP_K, T_ALL, D_MODEL)
        if l < DEPTH - 1:
            x = _ffn_norm(x1, y4, gate, ln2_g, ln2_b, l, ROW_BLOCK, 0, T_ALL // ROW_BLOCK)
        else:
            yp = _ffn_norm(x1, y4, gate, ln2_g, ln2_b, l, FINAL_ROW_BLOCK, 0, T_PROMPT // FINAL_ROW_BLOCK)
            ys_out = _ffn_norm(x1, y4, gate, ln2_g, ln2_b, l, DEC_BATCH, T_PROMPT // DEC_BATCH, 1)

    yp = yp.reshape(BATCH, SEQ, D_MODEL)
    ys_out = ys_out.reshape(DEC_BATCH, 1, D_MODEL)
    kcp = tail[..., :SWA_KV_W].reshape(DEPTH, BATCH, WINDOW, KV_SWA, HD_SWA)
    vcp = tail[..., SWA_KV_W:].reshape(DEPTH, BATCH, WINDOW, KV_SWA, HD_SWA)
    return (yp, ys_out, sp, kcp, vcp, ss,
            ko.reshape(DEPTH, DEC_BATCH, WINDOW, KV_SWA, HD_SWA),
            vo.reshape(DEPTH, DEC_BATCH, WINDOW, KV_SWA, HD_SWA))
```

```python
import functools
import math

import jax
import jax.numpy as jnp
from jax import lax
from jax.experimental import pallas as pl
from jax.experimental.pallas import tpu as pltpu
from jax.experimental.pallas import tpu_sc as plsc

F32 = jnp.float32
BF16 = jnp.bfloat16

D_MODEL = 1024
BATCH = 8
SEQ = 2048
DEPTH = 4
DEC_BATCH = 128
PAST_LEN = 8192
H_RET = 4
HD_RET = 128
RET_CHUNK = 128
ROPE_BASE = 10000.0
H_SWA = 8
KV_SWA = 2
G_SWA = H_SWA // KV_SWA
HD_SWA = 64
WINDOW = 128
NUM_BUCKETS = 32
REL_MAX_DIST = 128
RET_W = H_RET * HD_RET
SWA_Q_W = H_SWA * HD_SWA
SWA_KV_W = KV_SWA * HD_SWA
D_IN = 4 * RET_W + SWA_Q_W + 2 * SWA_KV_W
N_EXPERTS = 32
TOP_K = 4
D_FF = D_MODEL
SWIGLU_LIMIT = 7.0
SWIGLU_ALPHA = 1.702
DEEPNORM_ALPHA = (2 * DEPTH) ** 0.25
LN_EPS = 1e-5
GN_EPS = 1e-5

T_PROMPT = BATCH * SEQ
T_ALL = T_PROMPT + DEC_BATCH
ROW_BLOCK = 384
SAMPLE_BLOCK = 8
MOE_ROWS = 256
MOE_CHUNK = 512
N_ASSIGN = T_ALL * TOP_K
MOE_BLOCKS = N_ASSIGN // MOE_ROWS + N_EXPERTS
MOE_SLOTS = MOE_BLOCKS * MOE_ROWS
N_UNUSED = MOE_SLOTS - N_ASSIGN
SC_CORES = 2
SC_WORKERS = SC_CORES * 16
SC_DISPATCH_ROWS = 64
SC_COLLECT_ROWS = 48
FINAL_ROW_BLOCK = 512
VMEM_LIMIT = 56 * 1024 * 1024


def _cparams(*sem):
    return pltpu.CompilerParams(dimension_semantics=sem, vmem_limit_bytes=VMEM_LIMIT)


def _in_proj_kernel(x_ref, w_ref, h_ref):
    h_ref[...] = jnp.dot(x_ref[...].astype(BF16), w_ref[0], preferred_element_type=F32)


def _in_proj(x, w_in_bf, layer):
    return pl.pallas_call(
        _in_proj_kernel,
        grid=(T_ALL // ROW_BLOCK,),
        in_specs=[pl.BlockSpec((ROW_BLOCK, D_MODEL), lambda i: (i, 0)),
                  pl.BlockSpec((1, D_MODEL, D_IN), lambda i: (layer, 0, 0))],
        out_specs=pl.BlockSpec((ROW_BLOCK, D_IN), lambda i: (i, 0)),
        out_shape=jax.ShapeDtypeStruct((T_ALL, D_IN), F32),
        compiler_params=_cparams("parallel"),
        name="in_proj",
    )(x, w_in_bf)


def _rotate(x, cos, sin_signed):
    return x * cos + pltpu.roll(x, HD_RET // 2, 1) * sin_signed


def _group_norm_gate(o, gain, g):
    mu = jnp.mean(o, -1, keepdims=True)
    var = jnp.mean(jnp.square(o - mu), -1, keepdims=True)
    return (o - mu) * lax.rsqrt(var + GN_EPS) * gain * (g * jax.nn.sigmoid(g))


def _layer_norm(y, g, b):
    mu = jnp.mean(y, -1, keepdims=True)
    var = jnp.mean(jnp.square(y - mu), -1, keepdims=True)
    return (y - mu) * lax.rsqrt(var + LN_EPS) * g + b


def _ret_prompt_kernel(q_ref, k_ref, v_ref, g_ref, cos_ref, sin_ref, intra_ref, qdec_ref, kdec_ref,
                       cdec_ref, gain_ref, o_ref, s_ref):
    @pl.when(pl.program_id(1) == 0)
    def _():
        s_ref[...] = jnp.zeros_like(s_ref)

    cos = cos_ref[...]
    sin = sin_ref[...]
    nt = (((1,), (1,)), ((), ()))
    tn = (((0,), (0,)), ((), ()))
    heads = [slice(h * HD_RET, (h + 1) * HD_RET) for h in range(H_RET)]
    vs, atts, crosses = [], [], []
    for h, sl in enumerate(heads):
        q = _rotate(q_ref[:, sl], cos, sin)
        k = _rotate(k_ref[:, sl], cos, sin) * (HD_RET ** -0.5)
        v = v_ref[:, sl].astype(BF16)
        s = s_ref[0, 0, h]
        atts.append(lax.dot_general(q.astype(BF16), k.astype(BF16), nt, preferred_element_type=F32))
        crosses.append(jnp.dot((q * qdec_ref[h]).astype(BF16), s.astype(BF16), preferred_element_type=F32))
        s_ref[0, 0, h] = s * cdec_ref[h] + lax.dot_general((k * kdec_ref[h]).astype(BF16), v, tn,
                                                          preferred_element_type=F32)
        vs.append(v)
    atts = [(att * intra_ref[h]).astype(BF16) for h, att in enumerate(atts)]
    outs = [jnp.dot(atts[h], vs[h], preferred_element_type=F32) + crosses[h] for h in range(H_RET)]
    for h, sl in enumerate(heads):
        o_ref[:, sl] = _group_norm_gate(outs[h], gain_ref[0, :, sl], g_ref[:, sl])


def _ret_prompt(h, tables, gain, s_prev, layer):
    cos, sin, intra, qdec, kdec, cdec = tables
    nc = SEQ // RET_CHUNK

    def col(j):
        return pl.BlockSpec((RET_CHUNK, RET_W), lambda b, c: (b * nc + c, j))

    def whole(a):
        return pl.BlockSpec(a.shape, lambda b, c: (0,) * a.ndim)

    in_specs = [col(0), col(1), col(2), col(3),
                pl.BlockSpec((RET_CHUNK, HD_RET), lambda b, c: (c, 0)),
                pl.BlockSpec((RET_CHUNK, HD_RET), lambda b, c: (c, 0)),
                whole(intra), whole(qdec), whole(kdec), whole(cdec),
                pl.BlockSpec((1, 1, RET_W), lambda b, c: (layer, 0, 0))]
    args = [h, h, h, h, cos, sin, intra, qdec, kdec, cdec, gain]
    aliases = {}
    if s_prev is not None:
        in_specs.append(pl.BlockSpec(memory_space=pl.ANY))
        args.append(s_prev)
        aliases = {len(args) - 1: 1}

    def body(*refs):
        _ret_prompt_kernel(*refs[:11], *refs[-2:])

    return pl.pallas_call(
        body,
        grid=(BATCH, nc),
        in_specs=in_specs,
        out_specs=[pl.BlockSpec((RET_CHUNK, RET_W), lambda b, c: (b * nc + c, 0)),
                   pl.BlockSpec((1, 1, H_RET, HD_RET, HD_RET), lambda b, c: (layer, b, 0, 0, 0))],
        out_shape=[jax.ShapeDtypeStruct((T_ALL, RET_W), F32),
                   jax.ShapeDtypeStruct((DEPTH, BATCH, H_RET, HD_RET, HD_RET), F32)],
        input_output_aliases=aliases,
        compiler_params=_cparams("parallel", "arbitrary"),
        name="ret_prompt",
    )(*args)


def _swa_prompt_kernel(sink_ref, q_ref, kvc_ref, kvp_ref, bias_ref, o_ref, tail_ref, lg_scr, p_scr):
    c = pl.program_id(1)

    @pl.when(c == pl.num_programs(1) - 1)
    def _():
        tail_ref[0, 0] = kvc_ref[...]

    row = lax.broadcasted_iota(jnp.int32, (WINDOW, 2 * WINDOW), 0)
    colj = lax.broadcasted_iota(jnp.int32, (WINDOW, 2 * WINDOW), 1)
    valid = (colj > row) & (colj <= row + WINDOW) & jnp.logical_or(c > 0, colj >= WINDOW)

    kv = jnp.concatenate([kvp_ref[...], kvc_ref[...]], axis=0)
    k_t = kv[:, :SWA_KV_W].T
    k_t_swapped = pltpu.roll(k_t, HD_SWA, 0)
    v = kv[:, SWA_KV_W:]
    v_swapped = pltpu.roll(v, HD_SWA, 1)
    k_row = lax.broadcasted_iota(jnp.int32, k_t.shape, 0)
    v_lane = lax.broadcasted_iota(jnp.int32, v.shape, 1)
    ones = jnp.ones(v.shape, BF16)
    k_pad, v_pad = {}, {}
    for kvh in range(KV_SWA):
        for par in range(2):
            k_src = k_t if kvh == par else k_t_swapped
            v_src = v if kvh == par else v_swapped
            k_pad[kvh, par] = jnp.where((k_row >= par * HD_SWA) & (k_row < (par + 1) * HD_SWA), k_src, 0.0).astype(BF16)
            v_pad[kvh, par] = jnp.where((v_lane >= par * HD_SWA) & (v_lane < (par + 1) * HD_SWA), v_src, 0.0).astype(BF16)
    for hd in range(H_SWA):
        pair, par, kvh = hd // 2, hd % 2, hd // G_SWA
        q2 = (q_ref[:, pair * 2 * HD_SWA:(pair + 1) * 2 * HD_SWA] * (HD_SWA ** -0.5)).astype(BF16)
        logits = jnp.dot(q2, k_pad[kvh, par], preferred_element_type=F32) + bias_ref[hd]
        lg_scr[hd] = jnp.where(valid, logits, -jnp.inf)
    sink_terms = []
    for hd in range(H_SWA):
        logits = lg_scr[hd]
        sink = sink_ref[hd]
        m = jnp.maximum(jnp.max(logits, -1, keepdims=True), sink)
        p_scr[hd] = jnp.exp(logits - m).astype(BF16)
        sink_terms.append(jnp.exp(sink - m))
    for pair in range(H_SWA // 2):
        acc = None
        for par in range(2):
            hd = 2 * pair + par
            p = p_scr[hd]
            den = jnp.dot(p, ones, preferred_element_type=F32) + sink_terms[hd]
            part = jnp.dot(p, v_pad[hd // G_SWA, par], preferred_element_type=F32) / den
            acc = part if acc is None else acc + part
        o_ref[:, pair * 2 * HD_SWA:(pair + 1) * 2 * HD_SWA] = acc


def _swa_prompt(h, sinks_l, bias, tail_prev, layer):
    nb = SEQ // WINDOW
    qcol = 4 * RET_W // SWA_Q_W
    kvcol = (4 * RET_W + SWA_Q_W) // (2 * SWA_KV_W)
    in_specs = [pl.BlockSpec(memory_space=pltpu.SMEM),
                pl.BlockSpec((WINDOW, SWA_Q_W), lambda b, c: (b * nb + c, qcol)),
                pl.BlockSpec((WINDOW, 2 * SWA_KV_W), lambda b, c: (b * nb + c, kvcol)),
                pl.BlockSpec((WINDOW, 2 * SWA_KV_W), lambda b, c: (b * nb + jnp.maximum(c - 1, 0), kvcol)),
                pl.BlockSpec((H_SWA, WINDOW, 2 * WINDOW), lambda b, c: (0, 0, 0))]
    args = [sinks_l, h, h, h, bias]
    aliases = {}
    if tail_prev is not None:
        in_specs.append(pl.BlockSpec(memory_space=pl.ANY))
        args.append(tail_prev)
        aliases = {5: 1}

    def body(*refs):
        _swa_prompt_kernel(*refs[:5], *refs[-4:])

    return pl.pallas_call(
        body,
        grid=(BATCH, nb),
        in_specs=in_specs,
        scratch_shapes=[pltpu.VMEM((H_SWA, WINDOW, 2 * WINDOW), F32), pltpu.VMEM((H_SWA, WINDOW, 2 * WINDOW), BF16)],
        out_specs=[pl.BlockSpec((WINDOW, SWA_Q_W), lambda b, c: (b * nb + c, 0)),
                   pl.BlockSpec((1, 1, WINDOW, 2 * SWA_KV_W), lambda b, c: (layer, b, 0, 0))],
        out_shape=[jax.ShapeDtypeStruct((T_ALL, SWA_Q_W), F32),
                   jax.ShapeDtypeStruct((DEPTH, BATCH, WINDOW, 2 * SWA_KV_W), F32)],
        input_output_aliases=aliases,
        compiler_params=_cparams("parallel", "arbitrary"),
        name="swa_prompt",
    )(*args)


def _ret_sample_kernel(dec_ref, q_ref, k_ref, v_ref, g_ref, cos_ref, sin_ref, gain_ref, st_ref, o_ref, so_ref):
    cos = cos_ref[...]
    sin = sin_ref[...]
    for h in range(H_RET):
        sl = slice(h * HD_RET, (h + 1) * HD_RET)
        intra, qdec, kdec, cdec = dec_ref[h, 0], dec_ref[h, 1], dec_ref[h, 2], dec_ref[h, 3]
        q = _rotate(q_ref[:, sl], cos, sin)
        k = _rotate(k_ref[:, sl], cos, sin) * (HD_RET ** -0.5)
        v = v_ref[:, sl]
        att = jnp.sum(q * k, -1, keepdims=True) * intra
        q_col = (q * qdec).T
        k_col = (k * kdec).T
        rows = []
        for b in range(SAMPLE_BLOCK):
            s = st_ref[0, b, h]
            v_row = v[b:b + 1, :]
            rows.append(att[b:b + 1, :] * v_row + jnp.sum(q_col[:, b:b + 1] * s, axis=0, keepdims=True))
            so_ref[0, b, h] = s * cdec + k_col[:, b:b + 1] * v_row
        o = jnp.concatenate(rows, axis=0)
        o_ref[:, sl] = _group_norm_gate(o, gain_ref[0, :, sl], g_ref[:, sl])


def _ret_sample(h, dec_s, cos_s, sin_s, gain, state_ret, o_r, st_prev, layer):
    base = T_PROMPT // SAMPLE_BLOCK

    def col(j):
        return pl.BlockSpec((SAMPLE_BLOCK, RET_W), lambda i: (base + i, j))

    in_specs = [pl.BlockSpec(memory_space=pltpu.SMEM),
                col(0), col(1), col(2), col(3),
                pl.BlockSpec((1, HD_RET), lambda i: (0, 0)),
                pl.BlockSpec((1, HD_RET), lambda i: (0, 0)),
                pl.BlockSpec((1, 1, RET_W), lambda i: (layer, 0, 0)),
                pl.BlockSpec((1, SAMPLE_BLOCK, H_RET, HD_RET, HD_RET), lambda i: (layer, i, 0, 0, 0)),
                pl.BlockSpec(memory_space=pl.ANY)]
    args = [dec_s, h, h, h, h, cos_s, sin_s, gain, state_ret, o_r]
    aliases = {9: 0}
    if st_prev is not None:
        in_specs.append(pl.BlockSpec(memory_space=pl.ANY))
        args.append(st_prev)
        aliases[10] = 1

    def body(*refs):
        _ret_sample_kernel(*refs[:9], *refs[-2:])

    return pl.pallas_call(
        body,
        grid=(DEC_BATCH // SAMPLE_BLOCK,),
        in_specs=in_specs,
        out_specs=[pl.BlockSpec((SAMPLE_BLOCK, RET_W), lambda i: (base + i, 0)),
                   pl.BlockSpec((1, SAMPLE_BLOCK, H_RET, HD_RET, HD_RET), lambda i: (layer, i, 0, 0, 0))],
        out_shape=[jax.ShapeDtypeStruct((T_ALL, RET_W), F32),
                   jax.ShapeDtypeStruct((DEPTH, DEC_BATCH, H_RET, HD_RET, HD_RET), F32)],
        input_output_aliases=aliases,
        compiler_params=_cparams("parallel"),
        name="ret_sample",
    )(*args)


def _swa_sample_kernel(qm_ref, kn_ref, vn_ref, kc_ref, vc_ref, bias_ref, biasn_ref, sink_ref,
                       o_ref, ko_ref, vo_ref):
    qm = qm_ref[...]
    kn = kn_ref[...]
    vn = vn_ref[...]
    kc = kc_ref[0]
    vc = vc_ref[0]
    scale = HD_SWA ** -0.5
    logits = jnp.einsum('bhd,bjd->bhj', qm.astype(BF16), kc.astype(BF16),
                        preferred_element_type=F32) * scale + bias_ref[...][None]
    j = lax.broadcasted_iota(jnp.int32, logits.shape, 2)
    logits = jnp.where(j >= 1, logits, -jnp.inf)
    ln = jnp.sum(qm * kn, -1, keepdims=True) * scale + biasn_ref[...][None]
    sink = sink_ref[...][None]
    m = jnp.maximum(jnp.maximum(jnp.max(logits, -1, keepdims=True), ln), sink)
    p = jnp.exp(logits - m)
    pn = jnp.exp(ln - m)
    denom = jnp.sum(p, -1, keepdims=True) + pn + jnp.exp(sink - m)
    w = p / denom
    o_ref[...] = (jnp.einsum('bhj,bjd->bhd', w.astype(BF16), vc.astype(BF16), preferred_element_type=F32)
                  + (pn / denom) * vn)
    ko_ref[0, :, 0:WINDOW - 1, :] = kc_ref[0, :, 1:WINDOW, :]
    ko_ref[0, :, WINDOW - 1:WINDOW, :] = kn
    vo_ref[0, :, 0:WINDOW - 1, :] = vc_ref[0, :, 1:WINDOW, :]
    vo_ref[0, :, WINDOW - 1:WINDOW, :] = vn


def _swa_sample(qm, kn, vn, cache_k, cache_v, bias_c, bias_n, sink_col, ko_prev, vo_prev, layer):
    sb = SAMPLE_BLOCK
    cache_spec = pl.BlockSpec((1, sb, WINDOW, 2 * HD_SWA), lambda i: (layer, i, 0, 0))
    in_specs = [pl.BlockSpec((sb, H_SWA, 2 * HD_SWA), lambda i: (i, 0, 0)),
                pl.BlockSpec((sb, 1, 2 * HD_SWA), lambda i: (i, 0, 0)),
                pl.BlockSpec((sb, 1, 2 * HD_SWA), lambda i: (i, 0, 0)),
                cache_spec, cache_spec,
                pl.BlockSpec((H_SWA, WINDOW), lambda i: (0, 0)),
                pl.BlockSpec((H_SWA, 1), lambda i: (0, 0)),
                pl.BlockSpec((H_SWA, 1), lambda i: (0, 0))]
    args = [qm, kn, vn, cache_k, cache_v, bias_c, bias_n, sink_col]
    aliases = {}
    if ko_prev is not None:
        in_specs += [pl.BlockSpec(memory_space=pl.ANY), pl.BlockSpec(memory_space=pl.ANY)]
        args += [ko_prev, vo_prev]
        aliases = {8: 1, 9: 2}

    def body(*refs):
        _swa_sample_kernel(*refs[:8], *refs[-3:])

    cshape = jax.ShapeDtypeStruct((DEPTH, DEC_BATCH, WINDOW, 2 * HD_SWA), F32)
    return pl.pallas_call(
        body,
        grid=(DEC_BATCH // sb,),
        in_specs=in_specs,
        out_specs=[pl.BlockSpec((sb, H_SWA, 2 * HD_SWA), lambda i: (i, 0, 0)), cache_spec, cache_spec],
        out_shape=[jax.ShapeDtypeStruct((DEC_BATCH, H_SWA, 2 * HD_SWA), F32), cshape, cshape],
        input_output_aliases=aliases,
        compiler_params=_cparams("parallel"),
        name="swa_sample",
    )(*args)


def _pack_bf16_pairs(x):
    w = x.shape[1] // 2
    lo = lax.bitcast_convert_type(x[:, :w].astype(BF16).astype(F32), jnp.uint32) >> 16
    hi = lax.bitcast_convert_type(x[:, w:].astype(BF16).astype(F32), jnp.uint32)
    return lo | hi


def _unpack_bf16_pairs(p):
    lo = lax.bitcast_convert_type(p << 16, F32).astype(BF16)
    hi = lax.bitcast_convert_type(p & jnp.uint32(0xFFFF0000), F32).astype(BF16)
    return lo, hi


def _out_proj_kernel(or_ref, os_ref, x_ref, wr_ref, ws_ref, g_ref, b_ref, rw_ref, rb_ref,
                     x1_ref, xpk_ref, idx_ref, gate_ref, rank_ref, cnt_ref):
    mix = (jnp.dot(or_ref[...].astype(BF16), wr_ref[0], preferred_element_type=F32)
           + jnp.dot(os_ref[...].astype(BF16), ws_ref[0], preferred_element_type=F32))
    x1 = _layer_norm(DEEPNORM_ALPHA * x_ref[...] + mix, g_ref[0], b_ref[0])
    x1_ref[...] = x1
    xpk_ref[...] = _pack_bf16_pairs(x1)
    logits = jnp.dot(x1.astype(BF16), rw_ref[0], preferred_element_type=F32) + rb_ref[0]
    lane = lax.broadcasted_iota(jnp.int32, logits.shape, 1)
    vals, hits = [], []
    for kk in range(TOP_K):
        m = jnp.max(logits, -1, keepdims=True)
        idx = jnp.min(jnp.where(logits == m, lane, N_EXPERTS), -1, keepdims=True)
        vals.append(m)
        hits.append(lane == idx)
        idx_ref[:, kk:kk + 1] = idx
        logits = jnp.where(lane == idx, -jnp.inf, logits)
    es = [jnp.exp(v - vals[0]) for v in vals]
    tot = es[0] + es[1] + es[2] + es[3]
    for kk in range(TOP_K):
        gate_ref[:, kk:kk + 1] = es[kk] / tot
    chosen = sum(h.astype(F32) for h in hits)
    rows = logits.shape[0]
    earlier = (lax.broadcasted_iota(jnp.int32, (rows, rows), 0)
               > lax.broadcasted_iota(jnp.int32, (rows, rows), 1)).astype(BF16)
    before = jnp.dot(earlier, chosen.astype(BF16), preferred_element_type=F32)
    for kk in range(TOP_K):
        rank_ref[:, kk:kk + 1] = jnp.sum(jnp.where(hits[kk], before, 0.0), -1, keepdims=True).astype(jnp.int32)
    cnt_ref[0] = jnp.sum(chosen, axis=0, keepdims=True).astype(jnp.int32)


def _out_proj(o_r, o_s, x, w_out_bf, ln_g, ln_b, router_w_bf, router_b, layer):
    def rows(w):
        return pl.BlockSpec((ROW_BLOCK, w), lambda i: (i, 0))

    def vec(w):
        return pl.BlockSpec((1, 1, w), lambda i: (layer, 0, 0))

    return pl.pallas_call(
        _out_proj_kernel,
        grid=(T_ALL // ROW_BLOCK,),
        in_specs=[rows(RET_W), rows(SWA_Q_W), rows(D_MODEL),
                  pl.BlockSpec((1, RET_W, D_MODEL), lambda i: (layer, 0, 0)),
                  pl.BlockSpec((1, SWA_Q_W, D_MODEL), lambda i: (layer, 1, 0)),
                  vec(D_MODEL), vec(D_MODEL),
                  pl.BlockSpec((1, D_MODEL, N_EXPERTS), lambda i: (layer, 0, 0)),
                  vec(N_EXPERTS)],
        out_specs=[rows(D_MODEL), rows(D_MODEL // 2), rows(TOP_K), rows(TOP_K), rows(TOP_K),
                   pl.BlockSpec((1, 1, N_EXPERTS), lambda i: (i, 0, 0))],
        out_shape=[jax.ShapeDtypeStruct((T_ALL, D_MODEL), F32),
                   jax.ShapeDtypeStruct((T_ALL, D_MODEL // 2), jnp.uint32),
                   jax.ShapeDtypeStruct((T_ALL, TOP_K), jnp.int32),
                   jax.ShapeDtypeStruct((T_ALL, TOP_K), F32),
                   jax.ShapeDtypeStruct((T_ALL, TOP_K), jnp.int32),
                   jax.ShapeDtypeStruct((T_ALL // ROW_BLOCK, 1, N_EXPERTS), jnp.int32)],
        compiler_params=_cparams("parallel"),
        name="out_proj",
    )(o_r, o_s, x, w_out_bf, w_out_bf, ln_g, ln_b, router_w_bf, router_b)


def _moe_kernel(start_ref, nblk_ref, next_ref, xs_hbm, wgu_ref, bgu_ref, wd_ref, bd_ref, ys_hbm,
                wgu_bf, wd_bf, x_buf, y_buf, in_sems, out_sems, pending, ready):
    e = pl.program_id(0)
    per_chunk = MOE_CHUNK // MOE_ROWS
    nblk = nblk_ref[e]
    n_full = nblk // per_chunk
    has_tail = nblk % per_chunk == 1
    base = start_ref[e]

    def in_copy(row0, rows, slot):
        src = xs_hbm.at[pl.ds(pl.multiple_of(row0, MOE_ROWS), rows)]
        return pltpu.make_async_copy(src, x_buf.at[slot, pl.ds(0, rows)], in_sems.at[slot])

    def out_copy(row0, rows, slot):
        dst = ys_hbm.at[pl.ds(pl.multiple_of(row0, MOE_ROWS), rows)]
        return pltpu.make_async_copy(y_buf.at[slot, pl.ds(0, rows)], dst, out_sems.at[slot])

    def start_first_chunk(expert, slot):
        @pl.when(nblk_ref[expert] >= per_chunk)
        def _():
            in_copy(start_ref[expert], MOE_CHUNK, slot).start()

        @pl.when(nblk_ref[expert] < per_chunk)
        def _():
            in_copy(start_ref[expert], MOE_ROWS, slot).start()

    def prefetch_next_expert(slot):
        nxt = next_ref[e]

        @pl.when(nxt < N_EXPERTS)
        def _():
            start_first_chunk(nxt, slot)
            ready[0] = slot

    def drain(slot):
        for rows in (MOE_CHUNK, MOE_ROWS):
            @pl.when(pending[slot] == rows)
            def _():
                out_copy(0, rows, slot).wait()
        pending[slot] = 0

    def compute(rows, slot):
        x = jnp.concatenate(_unpack_bf16_pairs(x_buf[slot, pl.ds(0, rows), :]), axis=1)
        gu = jnp.dot(x, wgu_bf[...], preferred_element_type=F32) + bgu_ref[0, 0]
        glu = jnp.minimum(gu[:, :D_FF], SWIGLU_LIMIT)
        lin = jnp.clip(gu[:, D_FF:], -SWIGLU_LIMIT, SWIGLU_LIMIT)
        act = glu * jax.nn.sigmoid(SWIGLU_ALPHA * glu) * (lin + 1.0)
        y = jnp.dot(act.astype(BF16), wd_bf[...], preferred_element_type=F32) + bd_ref[0, 0]
        y_buf[slot, pl.ds(0, rows), :] = _pack_bf16_pairs(y)

    def run_chunk(row0, rows, slot):
        in_copy(row0, rows, slot).wait()
        drain(slot)
        compute(rows, slot)
        out_copy(row0, rows, slot).start()
        pending[slot] = rows

    @pl.when(e == 0)
    def _():
        pending[0] = 0
        pending[1] = 0
        ready[0] = -1

    @pl.when(nblk > 0)
    def _():
        @pl.when(ready[0] < 0)
        def _():
            start_first_chunk(e, 0)
            ready[0] = 0

        first_slot = ready[0]
        wgu_bf[...] = wgu_ref[0, 0].astype(BF16)
        wd_bf[...] = wd_ref[0, 0].astype(BF16)

        def body(c, carry):
            slot = (first_slot + c) & 1
            row0 = pl.multiple_of(base + c * MOE_CHUNK, MOE_ROWS)

            @pl.when(c + 1 < n_full)
            def _():
                in_copy(row0 + MOE_CHUNK, MOE_CHUNK, 1 - slot).start()

            @pl.when(jnp.logical_and(c + 1 == n_full, has_tail))
            def _():
                in_copy(row0 + MOE_CHUNK, MOE_ROWS, 1 - slot).start()

            @pl.when(jnp.logical_and(c + 1 == n_full, jnp.logical_not(has_tail)))
            def _():
                prefetch_next_expert(1 - slot)

            run_chunk(row0, MOE_CHUNK, slot)
            return carry

        lax.fori_loop(0, n_full, body, 0)

        @pl.when(has_tail)
        def _():
            slot = (first_slot + n_full) & 1
            prefetch_next_expert(1 - slot)
            run_chunk(pl.multiple_of(base + n_full * MOE_CHUNK, MOE_ROWS), MOE_ROWS, slot)

    @pl.when(e == pl.num_programs(0) - 1)
    def _():
        drain(0)
        drain(1)


def _moe(group_start, group_blocks, group_next, xs, w_gate_up, b_gate_up, w_down, b_down, layer):
    def wsel(e, st, nb, nx):
        return (layer, e, 0, 0)

    return pl.pallas_call(
        _moe_kernel,
        grid_spec=pltpu.PrefetchScalarGridSpec(
            num_scalar_prefetch=3,
            grid=(N_EXPERTS,),
            in_specs=[pl.BlockSpec(memory_space=pl.ANY),
                      pl.BlockSpec((1, 1, D_MODEL, 2 * D_FF), wsel),
                      pl.BlockSpec((1, 1, 1, 2 * D_FF), wsel),
                      pl.BlockSpec((1, 1, D_FF, D_MODEL), wsel),
                      pl.BlockSpec((1, 1, 1, D_MODEL), wsel)],
            out_specs=pl.BlockSpec(memory_space=pl.ANY),
            scratch_shapes=[pltpu.VMEM((D_MODEL, 2 * D_FF), BF16), pltpu.VMEM((D_FF, D_MODEL), BF16),
                            pltpu.VMEM((2, MOE_CHUNK, D_MODEL // 2), jnp.uint32),
                            pltpu.VMEM((2, MOE_CHUNK, D_MODEL // 2), jnp.uint32),
                            pltpu.SemaphoreType.DMA((2,)), pltpu.SemaphoreType.DMA((2,)),
                            pltpu.SMEM((2,), jnp.int32), pltpu.SMEM((1,), jnp.int32)]),
        out_shape=jax.ShapeDtypeStruct((MOE_SLOTS, D_MODEL // 2), jnp.uint32),
        compiler_params=_cparams("arbitrary"),
        name="moe_experts",
    )(group_start, group_blocks, group_next, xs, w_gate_up, b_gate_up, w_down, b_down)


def _route(idx, rank, counts):
    experts = jnp.arange(N_EXPERTS, dtype=jnp.int32)
    counts = counts.reshape(T_ALL // ROW_BLOCK, N_EXPERTS)
    before_block = jnp.cumsum(counts, axis=0) - counts
    total = jnp.sum(counts, axis=0)
    padded = (total + MOE_ROWS - 1) // MOE_ROWS * MOE_ROWS
    pad_end = jnp.cumsum(padded)
    pad_start = pad_end - padded
    base = jnp.repeat(pad_start[None, :] + before_block, ROW_BLOCK, axis=0)
    pos = jnp.sum(jnp.where(idx[:, :, None] == experts[None, None, :], base[:, None, :], 0), axis=-1) + rank
    gap = padded - total
    gap_end = jnp.concatenate([jnp.cumsum(gap), jnp.full((1,), N_UNUSED, jnp.int32)])
    gap_start = jnp.concatenate([jnp.zeros((1,), jnp.int32), gap_end[:-1]])
    first_free = jnp.concatenate([pad_start + total, pad_end[-1:]])
    j = jnp.arange(N_UNUSED, dtype=jnp.int32)
    region = (j[:, None] >= gap_start[None, :]) & (j[:, None] < gap_end[None, :])
    unused = jnp.sum(jnp.where(region, first_free[None, :] + j[:, None] - gap_start[None, :], 0), axis=-1)
    nonempty = jnp.where(total > 0, experts, N_EXPERTS)
    following = jnp.concatenate([lax.cummin(nonempty, reverse=True)[1:], jnp.full((1,), N_EXPERTS, jnp.int32)])
    return (pos.T.astype(jnp.int32), unused.astype(jnp.int32),
            pad_start.astype(jnp.int32), (padded // MOE_ROWS).astype(jnp.int32), following.astype(jnp.int32))


def _sc_worker():
    return lax.axis_index("s") * SC_CORES + lax.axis_index("c")


def _sc_dispatch(xpk, pos_km, unused, zero_rows):
    ch = SC_DISPATCH_ROWS
    n_chunks = T_ALL // ch
    n_zero_chunks = N_UNUSED // ch
    width = xpk.shape[1]
    mesh = plsc.VectorSubcoreMesh(core_axis_name="c", subcore_axis_name="s")

    @functools.partial(
        pl.kernel, mesh=mesh,
        out_type=jax.ShapeDtypeStruct((MOE_SLOTS, width), xpk.dtype),
        scratch_types=[pltpu.VMEM((2, ch, width), xpk.dtype), pltpu.VMEM((2, TOP_K, ch), jnp.int32),
                       pltpu.SemaphoreType.DMA((2,)), pltpu.SemaphoreType.DMA((TOP_K,))],
        name="sc_dispatch")
    def run(x_hbm, pos_hbm, unused_hbm, zero_hbm, xs_hbm, rows_v, idx_v, load_sems, store_sems):
        wid = _sc_worker()
        pltpu.sync_copy(zero_hbm, rows_v.at[0])
        for r in range(n_zero_chunks // SC_WORKERS):
            start = pl.multiple_of((wid + r * SC_WORKERS) * ch, ch)
            pltpu.sync_copy(unused_hbm.at[pl.ds(start, ch)], idx_v.at[0, 0])
            pltpu.sync_copy(rows_v.at[0], xs_hbm.at[idx_v.at[0, 0]])

        def first_row(r):
            return pl.multiple_of((wid + r * SC_WORKERS) * ch, ch)

        def rows_copy(r, slot):
            return pltpu.make_async_copy(x_hbm.at[pl.ds(first_row(r), ch)], rows_v.at[slot], load_sems.at[slot])

        def start_load(r, slot):
            for k in range(TOP_K):
                pltpu.sync_copy(pos_hbm.at[k, pl.ds(first_row(r), ch)], idx_v.at[slot, k])
            rows_copy(r, slot).start()

        def has_chunk(r):
            return wid + r * SC_WORKERS < n_chunks

        start_load(0, 0)

        def body(r, carry):
            slot = r & 1

            @pl.when(has_chunk(r + 1))
            def _():
                start_load(r + 1, 1 - slot)

            @pl.when(has_chunk(r))
            def _():
                rows_copy(r, slot).wait()
                copies = [pltpu.async_copy(rows_v.at[slot], xs_hbm.at[idx_v.at[slot, k]], store_sems.at[k])
                          for k in range(TOP_K)]
                for cp in copies:
                    cp.wait()
            return carry

        lax.fori_loop(0, pl.cdiv(n_chunks, SC_WORKERS), body, 0)

    return run(xpk, pos_km, unused, zero_rows)


def _sc_collect(ys, pos_flat):
    ch = SC_COLLECT_ROWS
    per_worker = N_ASSIGN // SC_WORKERS
    mesh = plsc.VectorSubcoreMesh(core_axis_name="c", subcore_axis_name="s")

    @functools.partial(
        pl.kernel, mesh=mesh,
        out_type=jax.ShapeDtypeStruct((N_ASSIGN, ys.shape[1]), ys.dtype),
        scratch_types=[pltpu.VMEM((2, ch, ys.shape[1]), ys.dtype), pltpu.VMEM((2, ch), jnp.int32),
                       pltpu.SemaphoreType.DMA((2,))],
        name="sc_collect")
    def run(ys_hbm, pos_hbm, y4_hbm, rows_v, idx_v, sems):
        wid = _sc_worker()
        n_chunks = per_worker // ch

        def first_row(r):
            return pl.multiple_of(wid * per_worker + r * ch, 8)

        def gather(slot):
            return pltpu.make_async_copy(ys_hbm.at[idx_v.at[slot]], rows_v.at[slot], sems.at[slot])

        def start_gather(r, slot):
            pltpu.sync_copy(pos_hbm.at[pl.ds(first_row(r), ch)], idx_v.at[slot])
            gather(slot).start()

        start_gather(0, 0)

        def body(r, carry):
            slot = r & 1

            @pl.when(r + 1 < n_chunks)
            def _():
                start_gather(r + 1, 1 - slot)

            gather(slot).wait()
            pltpu.sync_copy(rows_v.at[slot], y4_hbm.at[pl.ds(first_row(r), ch)])
            return carry

        lax.fori_loop(0, n_chunks, body, 0)

    return run(ys, pos_flat)


def _ffn_norm_kernel(x_ref, y0_ref, y1_ref, y2_ref, y3_ref, gate_ref, g_ref, b_ref, o_ref):
    gate = gate_ref[...]
    lo = hi = None
    for k, y_ref in enumerate((y0_ref, y1_ref, y2_ref, y3_ref)):
        p = y_ref[0]
        g = gate[:, k:k + 1]
        lo_k = g * lax.bitcast_convert_type(p << 16, F32)
        hi_k = g * lax.bitcast_convert_type(p & jnp.uint32(0xFFFF0000), F32)
        lo = lo_k if lo is None else lo + lo_k
        hi = hi_k if hi is None else hi + hi_k
    ffn = jnp.concatenate([lo, hi], axis=1)
    o_ref[...] = _layer_norm(DEEPNORM_ALPHA * x_ref[...] + ffn, g_ref[0], b_ref[0])


def _ffn_norm(x1, y4, gate, ln_g, ln_b, layer, row_block, first_block, n_blocks):
    def rows(w):
        return pl.BlockSpec((row_block, w), lambda i: (first_block + i, 0))

    def plane(k):
        return pl.BlockSpec((1, row_block, D_MODEL // 2), lambda i: (k, first_block + i, 0))

    vec = pl.BlockSpec((1, 1, D_MODEL), lambda i: (layer, 0, 0))
    return pl.pallas_call(
        _ffn_norm_kernel,
        grid=(n_blocks,),
        in_specs=[rows(D_MODEL), plane(0), plane(1), plane(2), plane(3), rows(TOP_K), vec, vec],
        out_specs=pl.BlockSpec((row_block, D_MODEL), lambda i: (i, 0)),
        out_shape=jax.ShapeDtypeStruct((n_blocks * row_block, D_MODEL), F32),
        compiler_params=_cparams("parallel"),
        name="ffn_norm",
    )(x1, y4, y4, y4, y4, gate, ln_g, ln_b)


def _rope_tables(pos):
    half = HD_RET // 2
    inv = ROPE_BASE ** (-jnp.arange(half, dtype=F32) / half)
    ang = pos.astype(F32)[:, None] * inv[None, :]
    cos, sin = jnp.cos(ang), jnp.sin(ang)
    return jnp.concatenate([cos, cos], -1), jnp.concatenate([-sin, sin], -1)


def _decay_tables(c):
    log_g = jnp.log1p(-jnp.exp2(-5.0 - jnp.arange(H_RET, dtype=F32)))
    idx = jnp.arange(c, dtype=F32)
    diff = idx[:, None] - idx[None, :]
    intra = jnp.where(diff[None] >= 0, jnp.exp(jnp.maximum(diff, 0.0)[None] * log_g[:, None, None]), 0.0)
    q_dec = jnp.exp((idx + 1.0)[None] * log_g[:, None])
    k_dec = jnp.exp((c - 1.0 - idx)[None] * log_g[:, None])
    c_dec = jnp.exp(c * log_g)
    return intra, q_dec, k_dec, c_dec


def _t5_bucket(dist):
    n = jnp.maximum(dist, 0)
    max_exact = NUM_BUCKETS // 2
    nf = jnp.maximum(n, 1).astype(F32)
    large = max_exact + (jnp.log(nf / max_exact) / math.log(REL_MAX_DIST / max_exact)
                         * (NUM_BUCKETS - max_exact)).astype(jnp.int32)
    large = jnp.minimum(large, NUM_BUCKETS - 1)
    return jnp.where(n < max_exact, n, large)


def _bias_table(rel_bias, bucket):
    hit = bucket[..., None] == jnp.arange(NUM_BUCKETS)
    per_head = rel_bias.astype(F32).T.reshape((H_SWA,) + (1,) * bucket.ndim + (NUM_BUCKETS,))
    return jnp.sum(jnp.where(hit[None], per_head, 0.0), axis=-1)


def kernel(x_prompt, x_sample, state_ret, cache_swa_k, cache_swa_v, w_in, w_out, ret_gn_gain, swa_sinks,
           rel_bias, ln1_g, ln1_b, ln2_g, ln2_b, router_w, router_b, w_gate_up, b_gate_up, w_down, b_down):
    x = jnp.concatenate([x_prompt.reshape(T_PROMPT, D_MODEL), x_sample.reshape(DEC_BATCH, D_MODEL)], axis=0)
    w_in_bf = w_in.astype(BF16)
    w_out_bf = w_out.astype(BF16)
    router_w_bf = router_w.astype(BF16)
    ret_gn_gain, ln1_g, ln1_b, ln2_g, ln2_b, router_b = (
        a.reshape(DEPTH, 1, a.shape[-1]) for a in (ret_gn_gain, ln1_g, ln1_b, ln2_g, ln2_b, router_b))
    b_gu4 = b_gate_up.reshape(DEPTH, N_EXPERTS, 1, 2 * D_FF)
    b_d4 = b_down.reshape(DEPTH, N_EXPERTS, 1, D_MODEL)
    cache_k = cache_swa_k.reshape(DEPTH, DEC_BATCH, WINDOW, 2 * HD_SWA)
    cache_v = cache_swa_v.reshape(DEPTH, DEC_BATCH, WINDOW, 2 * HD_SWA)

    cos_p, sin_p = _rope_tables(jnp.arange(SEQ, dtype=jnp.int32))
    cos_s, sin_s = _rope_tables(PAST_LEN + jnp.arange(1, dtype=jnp.int32))
    intra, q_dec, k_dec, c_dec = _decay_tables(RET_CHUNK)
    ones = jnp.ones((1, 1, HD_RET), F32)
    tables_p = (cos_p, sin_p, intra, q_dec[:, :, None] * ones, k_dec[:, :, None] * ones, c_dec[:, None, None] * ones)
    intra1, q_dec1, k_dec1, c_dec1 = _decay_tables(1)
    dec_s = jnp.stack([intra1[:, 0, 0], q_dec1[:, 0], k_dec1[:, 0], c_dec1], axis=1)
    i = jnp.arange(WINDOW)[:, None]
    j = jnp.arange(2 * WINDOW)[None, :]
    bias_p = _bias_table(rel_bias, _t5_bucket(WINDOW + i - j))
    js = jnp.arange(WINDOW + 1)
    bias_s = _bias_table(rel_bias, _t5_bucket(WINDOW - js))
    bias_c, bias_n = bias_s[:, :WINDOW], bias_s[:, WINDOW:]
    kv_of_head = jnp.arange(H_SWA) // G_SWA
    head_mask = (kv_of_head[:, None] == jnp.arange(KV_SWA)[None, :]).astype(F32)

    zero_rows = jnp.zeros((SC_DISPATCH_ROWS, D_MODEL // 2), jnp.uint32)

    sp = ss = ko = vo = tail = None
    for l in range(DEPTH):
        h = _in_proj(x, w_in_bf, l)
        o_r, sp = _ret_prompt(h, tables_p, ret_gn_gain, sp, l)
        o_s, tail = _swa_prompt(h, swa_sinks[l], bias_p, tail, l)
        o_r, ss = _ret_sample(h, dec_s, cos_s, sin_s, ret_gn_gain, state_ret, o_r, ss, l)
        hs = h[T_PROMPT:, 4 * RET_W:]
        q8 = hs[:, :SWA_Q_W].reshape(DEC_BATCH, H_SWA, HD_SWA)
        qm = (q8[:, :, None, :] * head_mask[None, :, :, None]).reshape(DEC_BATCH, H_SWA, 2 * HD_SWA)
        kn = hs[:, SWA_Q_W:SWA_Q_W + SWA_KV_W].reshape(DEC_BATCH, 1, SWA_KV_W)
        vn = hs[:, SWA_Q_W + SWA_KV_W:].reshape(DEC_BATCH, 1, SWA_KV_W)
        osm, ko, vo = _swa_sample(qm, kn, vn, cache_k, cache_v, bias_c, bias_n, swa_sinks[l][:, None], ko, vo, l)
        os_s = jnp.sum(osm.reshape(DEC_BATCH, H_SWA, KV_SWA, HD_SWA) * head_mask[None, :, :, None], axis=2)
        o_s = lax.dynamic_update_slice(o_s, os_s.reshape(DEC_BATCH, SWA_Q_W), (T_PROMPT, 0))

        x1, xpk, top_idx, gate, rank, counts = _out_proj(o_r, o_s, x, w_out_bf, ln1_g, ln1_b, router_w_bf,
                                                         router_b, l)
        pos_km, unused, group_start, group_blocks, group_next = _route(top_idx, rank, counts)
        xs = _sc_dispatch(xpk, pos_km, unused, zero_rows)
        ys = _moe(group_start, group_blocks, group_next, xs, w_gate_up, b_gu4, w_down, b_d4, l)
        y4 = _sc_collect(ys, pos_km.reshape(N_ASSIGN)).reshape(TOP_K, T_ALL, D_MODEL // 2)
        if l < DEPTH - 1:
            x = _ffn_norm(x1, y4, gate, ln2_g, ln2_b, l, ROW_BLOCK, 0, T_ALL // ROW_BLOCK)
        else:
            yp = _ffn_norm(x1, y4, gate, ln2_g, ln2_b, l, FINAL_ROW_BLOCK, 0, T_PROMPT // FINAL_ROW_BLOCK)
            ys_out = _ffn_norm(x1, y4, gate, ln2_g, ln2_b, l, DEC_BATCH, T_PROMPT // DEC_BATCH, 1)

    yp = yp.reshape(BATCH, SEQ, D_MODEL)
    ys_out = ys_out.reshape(DEC_BATCH, 1, D_MODEL)
    kcp = tail[..., :SWA_KV_W].reshape(DEPTH, BATCH, WINDOW, KV_SWA, HD_SWA)
    vcp = tail[..., SWA_KV_W:].reshape(DEPTH, BATCH, WINDOW, KV_SWA, HD_SWA)
    return (yp, ys_out, sp, kcp, vcp, ss,
            ko.reshape(DEPTH, DEC_BATCH, WINDOW, KV_SWA, HD_SWA),
            vo.reshape(DEPTH, DEC_BATCH, WINDOW, KV_SWA, HD_SWA))
```

```python
import functools
import math

import jax
import jax.numpy as jnp
from jax import lax
from jax.experimental import pallas as pl
from jax.experimental.pallas import tpu as pltpu
from jax.experimental.pallas import tpu_sc as plsc

F32 = jnp.float32
BF16 = jnp.bfloat16

D_MODEL = 1024
BATCH = 8
SEQ = 2048
DEPTH = 4
DEC_BATCH = 128
PAST_LEN = 8192
H_RET = 4
HD_RET = 128
RET_CHUNK = 128
ROPE_BASE = 10000.0
H_SWA = 8
KV_SWA = 2
G_SWA = H_SWA // KV_SWA
HD_SWA = 64
WINDOW = 128
NUM_BUCKETS = 32
REL_MAX_DIST = 128
RET_W = H_RET * HD_RET
SWA_Q_W = H_SWA * HD_SWA
SWA_KV_W = KV_SWA * HD_SWA
D_IN = 4 * RET_W + SWA_Q_W + 2 * SWA_KV_W
N_EXPERTS = 32
TOP_K = 4
D_FF = D_MODEL
SWIGLU_LIMIT = 7.0
SWIGLU_ALPHA = 1.702
DEEPNORM_ALPHA = (2 * DEPTH) ** 0.25
LN_EPS = 1e-5
GN_EPS = 1e-5

T_PROMPT = BATCH * SEQ
T_ALL = T_PROMPT + DEC_BATCH
ROW_BLOCK = 384
SAMPLE_BLOCK = 8
MOE_ROWS = 256
MOE_CHUNK = 512
N_ASSIGN = T_ALL * TOP_K
MOE_BLOCKS = N_ASSIGN // MOE_ROWS + N_EXPERTS
MOE_SLOTS = MOE_BLOCKS * MOE_ROWS
N_UNUSED = MOE_SLOTS - N_ASSIGN
SC_CORES = 2
SC_WORKERS = SC_CORES * 16
SC_DISPATCH_ROWS = 64
SC_COLLECT_ROWS = 48
FINAL_ROW_BLOCK = 512
VMEM_LIMIT = 56 * 1024 * 1024


def _cparams(*sem):
    return pltpu.CompilerParams(dimension_semantics=sem, vmem_limit_bytes=VMEM_LIMIT)


def _in_proj_kernel(x_ref, w_ref, h_ref):
    h_ref[...] = jnp.dot(x_ref[...].astype(BF16), w_ref[0], preferred_element_type=F32)


def _in_proj(x, w_in_bf, layer):
    return pl.pallas_call(
        _in_proj_kernel,
        grid=(T_ALL // ROW_BLOCK,),
        in_specs=[pl.BlockSpec((ROW_BLOCK, D_MODEL), lambda i: (i, 0)),
                  pl.BlockSpec((1, D_MODEL, D_IN), lambda i: (layer, 0, 0))],
        out_specs=pl.BlockSpec((ROW_BLOCK, D_IN), lambda i: (i, 0)),
        out_shape=jax.ShapeDtypeStruct((T_ALL, D_IN), F32),
        compiler_params=_cparams("parallel"),
        name="in_proj",
    )(x, w_in_bf)


def _rotate(x, cos, sin_signed):
    return x * cos + pltpu.roll(x, HD_RET // 2, 1) * sin_signed


def _group_norm_gate(o, gain, g):
    mu = jnp.mean(o, -1, keepdims=True)
    var = jnp.mean(jnp.square(o - mu), -1, keepdims=True)
    return (o - mu) * lax.rsqrt(var + GN_EPS) * gain * (g * jax.nn.sigmoid(g))


def _layer_norm(y, g, b):
    mu = jnp.mean(y, -1, keepdims=True)
    var = jnp.mean(jnp.square(y - mu), -1, keepdims=True)
    return (y - mu) * lax.rsqrt(var + LN_EPS) * g + b


def _ret_prompt_stages(q_ref, k_ref, v_ref, g_ref, cos_ref, sin_ref, intra_ref, qdec_ref, kdec_ref,
                       cdec_ref, gain_ref, o_ref, s_ref):
    @pl.when(pl.program_id(1) == 0)
    def _():
        s_ref[...] = jnp.zeros_like(s_ref)

    cos = cos_ref[...]
    sin = sin_ref[...]
    nt = (((1,), (1,)), ((), ()))
    tn = (((0,), (0,)), ((), ()))
    heads = [slice(h * HD_RET, (h + 1) * HD_RET) for h in range(H_RET)]
    vs, atts, crosses = [], [], []
    for h, sl in enumerate(heads):
        q = _rotate(q_ref[:, sl], cos, sin)
        k = _rotate(k_ref[:, sl], cos, sin) * (HD_RET ** -0.5)
        v = v_ref[:, sl].astype(BF16)
        s = s_ref[0, 0, h]
        atts.append(lax.dot_general(q.astype(BF16), k.astype(BF16), nt, preferred_element_type=F32))
        crosses.append(jnp.dot((q * qdec_ref[h]).astype(BF16), s.astype(BF16), preferred_element_type=F32))
        s_ref[0, 0, h] = s * cdec_ref[h] + lax.dot_general((k * kdec_ref[h]).astype(BF16), v, tn,
                                                          preferred_element_type=F32)
        vs.append(v)
    yield
    atts = [(att * intra_ref[h]).astype(BF16) for h, att in enumerate(atts)]
    yield
    outs = [jnp.dot(atts[h], vs[h], preferred_element_type=F32) + crosses[h] for h in range(H_RET)]
    yield
    for h, sl in enumerate(heads):
        o_ref[:, sl] = _group_norm_gate(outs[h], gain_ref[0, :, sl], g_ref[:, sl])


def _swa_prompt_stages(sink_ref, q_ref, kvc_ref, kvp_ref, bias_ref, o_ref, tail_ref, lg_scr, p_scr):
    c = pl.program_id(1)

    @pl.when(c == pl.num_programs(1) - 1)
    def _():
        tail_ref[0, 0] = kvc_ref[...]

    row = lax.broadcasted_iota(jnp.int32, (WINDOW, 2 * WINDOW), 0)
    colj = lax.broadcasted_iota(jnp.int32, (WINDOW, 2 * WINDOW), 1)
    valid = (colj > row) & (colj <= row + WINDOW) & jnp.logical_or(c > 0, colj >= WINDOW)

    kv = jnp.concatenate([kvp_ref[...], kvc_ref[...]], axis=0)
    k_t = kv[:, :SWA_KV_W].T
    k_t_swapped = pltpu.roll(k_t, HD_SWA, 0)
    v = kv[:, SWA_KV_W:]
    v_swapped = pltpu.roll(v, HD_SWA, 1)
    k_row = lax.broadcasted_iota(jnp.int32, k_t.shape, 0)
    v_lane = lax.broadcasted_iota(jnp.int32, v.shape, 1)
    ones = jnp.ones(v.shape, BF16)
    k_pad, v_pad = {}, {}
    for kvh in range(KV_SWA):
        for par in range(2):
            k_src = k_t if kvh == par else k_t_swapped
            v_src = v if kvh == par else v_swapped
            k_pad[kvh, par] = jnp.where((k_row >= par * HD_SWA) & (k_row < (par + 1) * HD_SWA), k_src, 0.0).astype(BF16)
            v_pad[kvh, par] = jnp.where((v_lane >= par * HD_SWA) & (v_lane < (par + 1) * HD_SWA), v_src, 0.0).astype(BF16)
    for hd in range(H_SWA):
        pair, par, kvh = hd // 2, hd % 2, hd // G_SWA
        q2 = (q_ref[:, pair * 2 * HD_SWA:(pair + 1) * 2 * HD_SWA] * (HD_SWA ** -0.5)).astype(BF16)
        logits = jnp.dot(q2, k_pad[kvh, par], preferred_element_type=F32) + bias_ref[hd]
        lg_scr[hd] = jnp.where(valid, logits, -jnp.inf)
    yield
    sink_terms = []
    for hd in range(H_SWA):
        logits = lg_scr[hd]
        sink = sink_ref[hd]
        m = jnp.maximum(jnp.max(logits, -1, keepdims=True), sink)
        p_scr[hd] = jnp.exp(logits - m).astype(BF16)
        sink_terms.append(jnp.exp(sink - m))
    yield
    for pair in range(H_SWA // 2):
        acc = None
        for par in range(2):
            hd = 2 * pair + par
            p = p_scr[hd]
            den = jnp.dot(p, ones, preferred_element_type=F32) + sink_terms[hd]
            part = jnp.dot(p, v_pad[hd // G_SWA, par], preferred_element_type=F32) / den
            acc = part if acc is None else acc + part
        o_ref[:, pair * 2 * HD_SWA:(pair + 1) * 2 * HD_SWA] = acc


def _mix_prompt_kernel(*refs):
    n_ret_in, n_swa_in = 11, 5
    ret_in = refs[:n_ret_in]
    swa_in = refs[n_ret_in:n_ret_in + n_swa_in]
    o_r_ref, s_ref, o_s_ref, tail_ref, lg_scr, p_scr = refs[-6:]
    groups = [_ret_prompt_stages(*ret_in, o_r_ref, s_ref),
              _swa_prompt_stages(*swa_in, o_s_ref, tail_ref, lg_scr, p_scr)]
    while groups:
        for g in list(groups):
            if next(g, groups) is groups:
                groups.remove(g)


def _mix_prompt(h, tables, gain, sinks_l, bias, s_prev, tail_prev, layer):
    cos, sin, intra, qdec, kdec, cdec = tables
    nb = SEQ // WINDOW
    qcol = 4 * RET_W // SWA_Q_W
    kvcol = (4 * RET_W + SWA_Q_W) // (2 * SWA_KV_W)

    def col(j):
        return pl.BlockSpec((RET_CHUNK, RET_W), lambda b, c: (b * nb + c, j))

    def whole(a):
        return pl.BlockSpec(a.shape, lambda b, c: (0,) * a.ndim)

    in_specs = [col(0), col(1), col(2), col(3),
                pl.BlockSpec((RET_CHUNK, HD_RET), lambda b, c: (c, 0)),
                pl.BlockSpec((RET_CHUNK, HD_RET), lambda b, c: (c, 0)),
                whole(intra), whole(qdec), whole(kdec), whole(cdec),
                pl.BlockSpec((1, 1, RET_W), lambda b, c: (layer, 0, 0)),
                pl.BlockSpec(memory_space=pltpu.SMEM),
                pl.BlockSpec((WINDOW, SWA_Q_W), lambda b, c: (b * nb + c, qcol)),
                pl.BlockSpec((WINDOW, 2 * SWA_KV_W), lambda b, c: (b * nb + c, kvcol)),
                pl.BlockSpec((WINDOW, 2 * SWA_KV_W), lambda b, c: (b * nb + jnp.maximum(c - 1, 0), kvcol)),
                pl.BlockSpec((H_SWA, WINDOW, 2 * WINDOW), lambda b, c: (0, 0, 0))]
    args = [h, h, h, h, cos, sin, intra, qdec, kdec, cdec, gain, sinks_l, h, h, h, bias]
    aliases = {}
    if s_prev is not None:
        in_specs += [pl.BlockSpec(memory_space=pl.ANY), pl.BlockSpec(memory_space=pl.ANY)]
        args += [s_prev, tail_prev]
        aliases = {16: 1, 17: 3}

    return pl.pallas_call(
        _mix_prompt_kernel,
        grid=(BATCH, nb),
        in_specs=in_specs,
        scratch_shapes=[pltpu.VMEM((H_SWA, WINDOW, 2 * WINDOW), F32), pltpu.VMEM((H_SWA, WINDOW, 2 * WINDOW), BF16)],
        out_specs=[pl.BlockSpec((RET_CHUNK, RET_W), lambda b, c: (b * nb + c, 0)),
                   pl.BlockSpec((1, 1, H_RET, HD_RET, HD_RET), lambda b, c: (layer, b, 0, 0, 0)),
                   pl.BlockSpec((WINDOW, SWA_Q_W), lambda b, c: (b * nb + c, 0)),
                   pl.BlockSpec((1, 1, WINDOW, 2 * SWA_KV_W), lambda b, c: (layer, b, 0, 0))],
        out_shape=[jax.ShapeDtypeStruct((T_ALL, RET_W), F32),
                   jax.ShapeDtypeStruct((DEPTH, BATCH, H_RET, HD_RET, HD_RET), F32),
                   jax.ShapeDtypeStruct((T_ALL, SWA_Q_W), F32),
                   jax.ShapeDtypeStruct((DEPTH, BATCH, WINDOW, 2 * SWA_KV_W), F32)],
        input_output_aliases=aliases,
        compiler_params=_cparams("parallel", "arbitrary"),
        name="mix_prompt",
    )(*args)


def _ret_sample_kernel(dec_ref, q_ref, k_ref, v_ref, g_ref, cos_ref, sin_ref, gain_ref, st_ref, o_ref, so_ref):
    cos = cos_ref[...]
    sin = sin_ref[...]
    for h in range(H_RET):
        sl = slice(h * HD_RET, (h + 1) * HD_RET)
        intra, qdec, kdec, cdec = dec_ref[h, 0], dec_ref[h, 1], dec_ref[h, 2], dec_ref[h, 3]
        q = _rotate(q_ref[:, sl], cos, sin)
        k = _rotate(k_ref[:, sl], cos, sin) * (HD_RET ** -0.5)
        v = v_ref[:, sl]
        att = jnp.sum(q * k, -1, keepdims=True) * intra
        q_col = (q * qdec).T
        k_col = (k * kdec).T
        rows = []
        for b in range(SAMPLE_BLOCK):
            s = st_ref[0, b, h]
            v_row = v[b:b + 1, :]
            rows.append(att[b:b + 1, :] * v_row + jnp.sum(q_col[:, b:b + 1] * s, axis=0, keepdims=True))
            so_ref[0, b, h] = s * cdec + k_col[:, b:b + 1] * v_row
        o = jnp.concatenate(rows, axis=0)
        o_ref[:, sl] = _group_norm_gate(o, gain_ref[0, :, sl], g_ref[:, sl])


def _ret_sample(h, dec_s, cos_s, sin_s, gain, state_ret, o_r, st_prev, layer):
    base = T_PROMPT // SAMPLE_BLOCK

    def col(j):
        return pl.BlockSpec((SAMPLE_BLOCK, RET_W), lambda i: (base + i, j))

    in_specs = [pl.BlockSpec(memory_space=pltpu.SMEM),
                col(0), col(1), col(2), col(3),
                pl.BlockSpec((1, HD_RET), lambda i: (0, 0)),
                pl.BlockSpec((1, HD_RET), lambda i: (0, 0)),
                pl.BlockSpec((1, 1, RET_W), lambda i: (layer, 0, 0)),
                pl.BlockSpec((1, SAMPLE_BLOCK, H_RET, HD_RET, HD_RET), lambda i: (layer, i, 0, 0, 0)),
                pl.BlockSpec(memory_space=pl.ANY)]
    args = [dec_s, h, h, h, h, cos_s, sin_s, gain, state_ret, o_r]
    aliases = {9: 0}
    if st_prev is not None:
        in_specs.append(pl.BlockSpec(memory_space=pl.ANY))
        args.append(st_prev)
        aliases[10] = 1

    def body(*refs):
        _ret_sample_kernel(*refs[:9], *refs[-2:])

    return pl.pallas_call(
        body,
        grid=(DEC_BATCH // SAMPLE_BLOCK,),
        in_specs=in_specs,
        out_specs=[pl.BlockSpec((SAMPLE_BLOCK, RET_W), lambda i: (base + i, 0)),
                   pl.BlockSpec((1, SAMPLE_BLOCK, H_RET, HD_RET, HD_RET), lambda i: (layer, i, 0, 0, 0))],
        out_shape=[jax.ShapeDtypeStruct((T_ALL, RET_W), F32),
                   jax.ShapeDtypeStruct((DEPTH, DEC_BATCH, H_RET, HD_RET, HD_RET), F32)],
        input_output_aliases=aliases,
        compiler_params=_cparams("parallel"),
        name="ret_sample",
    )(*args)


def _swa_sample_kernel(qm_ref, kn_ref, vn_ref, kc_ref, vc_ref, bias_ref, biasn_ref, sink_ref,
                       o_ref, ko_ref, vo_ref):
    qm = qm_ref[...]
    kn = kn_ref[...]
    vn = vn_ref[...]
    kc = kc_ref[0]
    vc = vc_ref[0]
    scale = HD_SWA ** -0.5
    logits = jnp.einsum('bhd,bjd->bhj', qm.astype(BF16), kc.astype(BF16),
                        preferred_element_type=F32) * scale + bias_ref[...][None]
    j = lax.broadcasted_iota(jnp.int32, logits.shape, 2)
    logits = jnp.where(j >= 1, logits, -jnp.inf)
    ln = jnp.sum(qm * kn, -1, keepdims=True) * scale + biasn_ref[...][None]
    sink = sink_ref[...][None]
    m = jnp.maximum(jnp.maximum(jnp.max(logits, -1, keepdims=True), ln), sink)
    p = jnp.exp(logits - m)
    pn = jnp.exp(ln - m)
    denom = jnp.sum(p, -1, keepdims=True) + pn + jnp.exp(sink - m)
    w = p / denom
    o_ref[...] = (jnp.einsum('bhj,bjd->bhd', w.astype(BF16), vc.astype(BF16), preferred_element_type=F32)
                  + (pn / denom) * vn)
    ko_ref[0, :, 0:WINDOW - 1, :] = kc_ref[0, :, 1:WINDOW, :]
    ko_ref[0, :, WINDOW - 1:WINDOW, :] = kn
    vo_ref[0, :, 0:WINDOW - 1, :] = vc_ref[0, :, 1:WINDOW, :]
    vo_ref[0, :, WINDOW - 1:WINDOW, :] = vn


def _swa_sample(qm, kn, vn, cache_k, cache_v, bias_c, bias_n, sink_col, ko_prev, vo_prev, layer):
    sb = SAMPLE_BLOCK
    cache_spec = pl.BlockSpec((1, sb, WINDOW, 2 * HD_SWA), lambda i: (layer, i, 0, 0))
    in_specs = [pl.BlockSpec((sb, H_SWA, 2 * HD_SWA), lambda i: (i, 0, 0)),
                pl.BlockSpec((sb, 1, 2 * HD_SWA), lambda i: (i, 0, 0)),
                pl.BlockSpec((sb, 1, 2 * HD_SWA), lambda i: (i, 0, 0)),
                cache_spec, cache_spec,
                pl.BlockSpec((H_SWA, WINDOW), lambda i: (0, 0)),
                pl.BlockSpec((H_SWA, 1), lambda i: (0, 0)),
                pl.BlockSpec((H_SWA, 1), lambda i: (0, 0))]
    args = [qm, kn, vn, cache_k, cache_v, bias_c, bias_n, sink_col]
    aliases = {}
    if ko_prev is not None:
        in_specs += [pl.BlockSpec(memory_space=pl.ANY), pl.BlockSpec(memory_space=pl.ANY)]
        args += [ko_prev, vo_prev]
        aliases = {8: 1, 9: 2}

    def body(*refs):
        _swa_sample_kernel(*refs[:8], *refs[-3:])

    cshape = jax.ShapeDtypeStruct((DEPTH, DEC_BATCH, WINDOW, 2 * HD_SWA), F32)
    return pl.pallas_call(
        body,
        grid=(DEC_BATCH // sb,),
        in_specs=in_specs,
        out_specs=[pl.BlockSpec((sb, H_SWA, 2 * HD_SWA), lambda i: (i, 0, 0)), cache_spec, cache_spec],
        out_shape=[jax.ShapeDtypeStruct((DEC_BATCH, H_SWA, 2 * HD_SWA), F32), cshape, cshape],
        input_output_aliases=aliases,
        compiler_params=_cparams("parallel"),
        name="swa_sample",
    )(*args)


def _pack_bf16_pairs(x):
    w = x.shape[1] // 2
    lo = lax.bitcast_convert_type(x[:, :w].astype(BF16).astype(F32), jnp.uint32) >> 16
    hi = lax.bitcast_convert_type(x[:, w:].astype(BF16).astype(F32), jnp.uint32)
    return lo | hi


def _unpack_bf16_pairs(p):
    lo = lax.bitcast_convert_type(p << 16, F32).astype(BF16)
    hi = lax.bitcast_convert_type(p & jnp.uint32(0xFFFF0000), F32).astype(BF16)
    return lo, hi


def _out_proj_kernel(or_ref, os_ref, x_ref, wr_ref, ws_ref, g_ref, b_ref, rw_ref, rb_ref,
                     x1_ref, xpk_ref, idx_ref, gate_ref, rank_ref, cnt_ref):
    mix = (jnp.dot(or_ref[...].astype(BF16), wr_ref[0], preferred_element_type=F32)
           + jnp.dot(os_ref[...].astype(BF16), ws_ref[0], preferred_element_type=F32))
    x1 = _layer_norm(DEEPNORM_ALPHA * x_ref[...] + mix, g_ref[0], b_ref[0])
    x1_ref[...] = x1
    xpk_ref[...] = _pack_bf16_pairs(x1)
    nt = (((1,), (1,)), ((), ()))
    logits = lax.dot_general(rw_ref[0], x1.astype(BF16), nt, preferred_element_type=F32) + rb_ref[0]
    rows = logits.shape[1]
    expert = lax.broadcasted_iota(jnp.int32, logits.shape, 0)
    vals, hits = [], []
    for kk in range(TOP_K):
        m = jnp.max(logits, 0, keepdims=True)
        idx = jnp.min(jnp.where(logits == m, expert, N_EXPERTS), 0, keepdims=True)
        vals.append(m)
        hits.append(expert == idx)
        idx_ref[kk:kk + 1, :] = idx
        logits = jnp.where(expert == idx, -jnp.inf, logits)
    es = [jnp.exp(v - vals[0]) for v in vals]
    tot = es[0] + es[1] + es[2] + es[3]
    gate_ref[...] = jnp.concatenate([e / tot for e in es] + [jnp.zeros((8 - TOP_K, rows), F32)], axis=0)
    chosen = sum(h.astype(F32) for h in hits)
    earlier = (lax.broadcasted_iota(jnp.int32, (rows, rows), 0)
               < lax.broadcasted_iota(jnp.int32, (rows, rows), 1)).astype(BF16)
    before = jnp.dot(chosen.astype(BF16), earlier, preferred_element_type=F32)
    for kk in range(TOP_K):
        rank_ref[kk:kk + 1, :] = jnp.sum(jnp.where(hits[kk], before, 0.0), 0, keepdims=True).astype(jnp.int32)
    cnt_ref[0] = jnp.sum(chosen, axis=1, keepdims=True).astype(jnp.int32)


def _out_proj(o_r, o_s, x, w_out_bf, ln_g, ln_b, router_w_bf, router_b, layer):
    def rows(w):
        return pl.BlockSpec((ROW_BLOCK, w), lambda i: (i, 0))

    def vec(w):
        return pl.BlockSpec((1, 1, w), lambda i: (layer, 0, 0))

    def cols(h):
        return pl.BlockSpec((h, ROW_BLOCK), lambda i: (0, i))

    return pl.pallas_call(
        _out_proj_kernel,
        grid=(T_ALL // ROW_BLOCK,),
        in_specs=[rows(RET_W), rows(SWA_Q_W), rows(D_MODEL),
                  pl.BlockSpec((1, RET_W, D_MODEL), lambda i: (layer, 0, 0)),
                  pl.BlockSpec((1, SWA_Q_W, D_MODEL), lambda i: (layer, 1, 0)),
                  vec(D_MODEL), vec(D_MODEL),
                  pl.BlockSpec((1, N_EXPERTS, D_MODEL), lambda i: (layer, 0, 0)),
                  pl.BlockSpec((1, N_EXPERTS, 1), lambda i: (layer, 0, 0))],
        out_specs=[rows(D_MODEL), rows(D_MODEL // 2), cols(TOP_K), cols(8), cols(TOP_K),
                   pl.BlockSpec((1, N_EXPERTS, 1), lambda i: (i, 0, 0))],
        out_shape=[jax.ShapeDtypeStruct((T_ALL, D_MODEL), F32),
                   jax.ShapeDtypeStruct((T_ALL, D_MODEL // 2), jnp.uint32),
                   jax.ShapeDtypeStruct((TOP_K, T_ALL), jnp.int32),
                   jax.ShapeDtypeStruct((8, T_ALL), F32),
                   jax.ShapeDtypeStruct((TOP_K, T_ALL), jnp.int32),
                   jax.ShapeDtypeStruct((T_ALL // ROW_BLOCK, N_EXPERTS, 1), jnp.int32)],
        compiler_params=_cparams("parallel"),
        name="out_proj",
    )(o_r, o_s, x, w_out_bf, w_out_bf, ln_g, ln_b, router_w_bf, router_b)


def _moe_kernel(start_ref, nblk_ref, next_ref, xs_hbm, wgu_ref, bgu_ref, wd_ref, bd_ref, ys_hbm,
                wgu_bf, wd_bf, x_buf, y_buf, in_sems, out_sems, pending, ready):
    e = pl.program_id(0)
    per_chunk = MOE_CHUNK // MOE_ROWS
    nblk = nblk_ref[e]
    n_full = nblk // per_chunk
    has_tail = nblk % per_chunk == 1
    base = start_ref[e]

    def in_copy(row0, rows, slot):
        src = xs_hbm.at[pl.ds(pl.multiple_of(row0, MOE_ROWS), rows)]
        return pltpu.make_async_copy(src, x_buf.at[slot, pl.ds(0, rows)], in_sems.at[slot])

    def out_copy(row0, rows, slot):
        dst = ys_hbm.at[pl.ds(pl.multiple_of(row0, MOE_ROWS), rows)]
        return pltpu.make_async_copy(y_buf.at[slot, pl.ds(0, rows)], dst, out_sems.at[slot])

    def start_first_chunk(expert, slot):
        @pl.when(nblk_ref[expert] >= per_chunk)
        def _():
            in_copy(start_ref[expert], MOE_CHUNK, slot).start()

        @pl.when(nblk_ref[expert] < per_chunk)
        def _():
            in_copy(start_ref[expert], MOE_ROWS, slot).start()

    def prefetch_next_expert(slot):
        nxt = next_ref[e]

        @pl.when(nxt < N_EXPERTS)
        def _():
            start_first_chunk(nxt, slot)
            ready[0] = slot

    def drain(slot):
        for rows in (MOE_CHUNK, MOE_ROWS):
            @pl.when(pending[slot] == rows)
            def _():
                out_copy(0, rows, slot).wait()
        pending[slot] = 0

    def compute(rows, slot):
        x = jnp.concatenate(_unpack_bf16_pairs(x_buf[slot, pl.ds(0, rows), :]), axis=1)
        gu = jnp.dot(x, wgu_bf[...], preferred_element_type=F32) + bgu_ref[0, 0]
        glu = jnp.minimum(gu[:, :D_FF], SWIGLU_LIMIT)
        lin = jnp.clip(gu[:, D_FF:], -SWIGLU_LIMIT, SWIGLU_LIMIT)
        act = glu * jax.nn.sigmoid(SWIGLU_ALPHA * glu) * (lin + 1.0)
        y = jnp.dot(act.astype(BF16), wd_bf[...], preferred_element_type=F32) + bd_ref[0, 0]
        y_buf[slot, pl.ds(0, rows), :] = _pack_bf16_pairs(y)

    def run_chunk(row0, rows, slot):
        in_copy(row0, rows, slot).wait()
        drain(slot)
        compute(rows, slot)
        out_copy(row0, rows, slot).start()
        pending[slot] = rows

    @pl.when(e == 0)
    def _():
        pending[0] = 0
        pending[1] = 0
        ready[0] = -1

    @pl.when(nblk > 0)
    def _():
        @pl.when(ready[0] < 0)
        def _():
            start_first_chunk(e, 0)
            ready[0] = 0

        first_slot = ready[0]
        wgu_bf[...] = wgu_ref[0, 0].astype(BF16)
        wd_bf[...] = wd_ref[0, 0].astype(BF16)

        def body(c, carry):
            slot = (first_slot + c) & 1
            row0 = pl.multiple_of(base + c * MOE_CHUNK, MOE_ROWS)

            @pl.when(c + 1 < n_full)
            def _():
                in_copy(row0 + MOE_CHUNK, MOE_CHUNK, 1 - slot).start()

            @pl.when(jnp.logical_and(c + 1 == n_full, has_tail))
            def _():
                in_copy(row0 + MOE_CHUNK, MOE_ROWS, 1 - slot).start()

            @pl.when(jnp.logical_and(c + 1 == n_full, jnp.logical_not(has_tail)))
            def _():
                prefetch_next_expert(1 - slot)

            run_chunk(row0, MOE_CHUNK, slot)
            return carry

        lax.fori_loop(0, n_full, body, 0)

        @pl.when(has_tail)
        def _():
            slot = (first_slot + n_full) & 1
            prefetch_next_expert(1 - slot)
            run_chunk(pl.multiple_of(base + n_full * MOE_CHUNK, MOE_ROWS), MOE_ROWS, slot)

    @pl.when(e == pl.num_programs(0) - 1)
    def _():
        drain(0)
        drain(1)


def _moe(group_start, group_blocks, group_next, xs, w_gate_up, b_gate_up, w_down, b_down, layer):
    def wsel(e, st, nb, nx):
        return (layer, e, 0, 0)

    return pl.pallas_call(
        _moe_kernel,
        grid_spec=pltpu.PrefetchScalarGridSpec(
            num_scalar_prefetch=3,
            grid=(N_EXPERTS,),
            in_specs=[pl.BlockSpec(memory_space=pl.ANY),
                      pl.BlockSpec((1, 1, D_MODEL, 2 * D_FF), wsel),
                      pl.BlockSpec((1, 1, 1, 2 * D_FF), wsel),
                      pl.BlockSpec((1, 1, D_FF, D_MODEL), wsel),
                      pl.BlockSpec((1, 1, 1, D_MODEL), wsel)],
            out_specs=pl.BlockSpec(memory_space=pl.ANY),
            scratch_shapes=[pltpu.VMEM((D_MODEL, 2 * D_FF), BF16), pltpu.VMEM((D_FF, D_MODEL), BF16),
                            pltpu.VMEM((2, MOE_CHUNK, D_MODEL // 2), jnp.uint32),
                            pltpu.VMEM((2, MOE_CHUNK, D_MODEL // 2), jnp.uint32),
                            pltpu.SemaphoreType.DMA((2,)), pltpu.SemaphoreType.DMA((2,)),
                            pltpu.SMEM((2,), jnp.int32), pltpu.SMEM((1,), jnp.int32)]),
        out_shape=jax.ShapeDtypeStruct((MOE_SLOTS, D_MODEL // 2), jnp.uint32),
        compiler_params=_cparams("arbitrary"),
        name="moe_experts",
    )(group_start, group_blocks, group_next, xs, w_gate_up, b_gate_up, w_down, b_down)


def _route(idx, rank, counts):
    experts = jnp.arange(N_EXPERTS, dtype=jnp.int32)
    counts = counts.reshape(T_ALL // ROW_BLOCK, N_EXPERTS)
    before_block = jnp.cumsum(counts, axis=0) - counts
    total = jnp.sum(counts, axis=0)
    padded = (total + MOE_ROWS - 1) // MOE_ROWS * MOE_ROWS
    pad_end = jnp.cumsum(padded)
    pad_start = pad_end - padded
    base = jnp.repeat((pad_start[None, :] + before_block).T, ROW_BLOCK, axis=1)
    pos = jnp.sum(jnp.where(idx[:, None, :] == experts[None, :, None], base[None], 0), axis=1) + rank
    gap = padded - total
    gap_end = jnp.concatenate([jnp.cumsum(gap), jnp.full((1,), N_UNUSED, jnp.int32)])
    gap_start = jnp.concatenate([jnp.zeros((1,), jnp.int32), gap_end[:-1]])
    first_free = jnp.concatenate([pad_start + total, pad_end[-1:]])
    j = jnp.arange(N_UNUSED, dtype=jnp.int32)
    region = (j[:, None] >= gap_start[None, :]) & (j[:, None] < gap_end[None, :])
    unused = jnp.sum(jnp.where(region, first_free[None, :] + j[:, None] - gap_start[None, :], 0), axis=-1)
    nonempty = jnp.where(total > 0, experts, N_EXPERTS)
    following = jnp.concatenate([lax.cummin(nonempty, reverse=True)[1:], jnp.full((1,), N_EXPERTS, jnp.int32)])
    return (pos.astype(jnp.int32), unused.astype(jnp.int32),
            pad_start.astype(jnp.int32), (padded // MOE_ROWS).astype(jnp.int32), following.astype(jnp.int32))


def _sc_worker():
    return lax.axis_index("s") * SC_CORES + lax.axis_index("c")


def _sc_dispatch(xpk, pos_km, unused, zero_rows):
    ch = SC_DISPATCH_ROWS
    n_chunks = T_ALL // ch
    n_zero_chunks = N_UNUSED // ch
    width = xpk.shape[1]
    mesh = plsc.VectorSubcoreMesh(core_axis_name="c", subcore_axis_name="s")

    @functools.partial(
        pl.kernel, mesh=mesh,
        out_type=jax.ShapeDtypeStruct((MOE_SLOTS, width), xpk.dtype),
        scratch_types=[pltpu.VMEM((2, ch, width), xpk.dtype), pltpu.VMEM((2, TOP_K, ch), jnp.int32),
                       pltpu.SemaphoreType.DMA((2,)), pltpu.SemaphoreType.DMA((TOP_K,))],
        name="sc_dispatch")
    def run(x_hbm, pos_hbm, unused_hbm, zero_hbm, xs_hbm, rows_v, idx_v, load_sems, store_sems):
        wid = _sc_worker()
        pltpu.sync_copy(zero_hbm, rows_v.at[0])
        for r in range(n_zero_chunks // SC_WORKERS):
            start = pl.multiple_of((wid + r * SC_WORKERS) * ch, ch)
            pltpu.sync_copy(unused_hbm.at[pl.ds(start, ch)], idx_v.at[0, 0])
            pltpu.sync_copy(rows_v.at[0], xs_hbm.at[idx_v.at[0, 0]])

        def first_row(r):
            return pl.multiple_of((wid + r * SC_WORKERS) * ch, ch)

        def rows_copy(r, slot):
            return pltpu.make_async_copy(x_hbm.at[pl.ds(first_row(r), ch)], rows_v.at[slot], load_sems.at[slot])

        def start_load(r, slot):
            for k in range(TOP_K):
                pltpu.sync_copy(pos_hbm.at[k, pl.ds(first_row(r), ch)], idx_v.at[slot, k])
            rows_copy(r, slot).start()

        def has_chunk(r):
            return wid + r * SC_WORKERS < n_chunks

        start_load(0, 0)

        def body(r, carry):
            slot = r & 1

            @pl.when(has_chunk(r + 1))
            def _():
                start_load(r + 1, 1 - slot)

            @pl.when(has_chunk(r))
            def _():
                rows_copy(r, slot).wait()
                copies = [pltpu.async_copy(rows_v.at[slot], xs_hbm.at[idx_v.at[slot, k]], store_sems.at[k])
                          for k in range(TOP_K)]
                for cp in copies:
                    cp.wait()
            return carry

        lax.fori_loop(0, pl.cdiv(n_chunks, SC_WORKERS), body, 0)

    return run(xpk, pos_km, unused, zero_rows)


def _sc_collect(ys, pos_flat):
    ch = SC_COLLECT_ROWS
    per_worker = N_ASSIGN // SC_WORKERS
    mesh = plsc.VectorSubcoreMesh(core_axis_name="c", subcore_axis_name="s")

    @functools.partial(
        pl.kernel, mesh=mesh,
        out_type=jax.ShapeDtypeStruct((N_ASSIGN, ys.shape[1]), ys.dtype),
        scratch_types=[pltpu.VMEM((2, ch, ys.shape[1]), ys.dtype), pltpu.VMEM((2, ch), jnp.int32),
                       pltpu.SemaphoreType.DMA((2,))],
        name="sc_collect")
    def run(ys_hbm, pos_hbm, y4_hbm, rows_v, idx_v, sems):
        wid = _sc_worker()
        n_chunks = per_worker // ch

        def first_row(r):
            return pl.multiple_of(wid * per_worker + r * ch, 8)

        def gather(slot):
            return pltpu.make_async_copy(ys_hbm.at[idx_v.at[slot]], rows_v.at[slot], sems.at[slot])

        def start_gather(r, slot):
            pltpu.sync_copy(pos_hbm.at[pl.ds(first_row(r), ch)], idx_v.at[slot])
            gather(slot).start()

        start_gather(0, 0)

        def body(r, carry):
            slot = r & 1

            @pl.when(r + 1 < n_chunks)
            def _():
                start_gather(r + 1, 1 - slot)

            gather(slot).wait()
            pltpu.sync_copy(rows_v.at[slot], y4_hbm.at[pl.ds(first_row(r), ch)])
            return carry

        lax.fori_loop(0, n_chunks, body, 0)

    return run(ys, pos_flat)


def _ffn_norm_kernel(x_ref, y0_ref, y1_ref, y2_ref, y3_ref, gate_ref, g_ref, b_ref, o_ref):
    gate = gate_ref[...].T
    lo = hi = None
    for k, y_ref in enumerate((y0_ref, y1_ref, y2_ref, y3_ref)):
        p = y_ref[0]
        g = gate[:, k:k + 1]
        lo_k = g * lax.bitcast_convert_type(p << 16, F32)
        hi_k = g * lax.bitcast_convert_type(p & jnp.uint32(0xFFFF0000), F32)
        lo = lo_k if lo is None else lo + lo_k
        hi = hi_k if hi is None else hi + hi_k
    ffn = jnp.concatenate([lo, hi], axis=1)
    o_ref[...] = _layer_norm(DEEPNORM_ALPHA * x_ref[...] + ffn, g_ref[0], b_ref[0])


def _ffn_norm(x1, y4, gate, ln_g, ln_b, layer, row_block, first_block, n_blocks):
    def rows(w):
        return pl.BlockSpec((row_block, w), lambda i: (first_block + i, 0))

    def plane(k):
        return pl.BlockSpec((1, row_block, D_MODEL // 2), lambda i: (k, first_block + i, 0))

    vec = pl.BlockSpec((1, 1, D_MODEL), lambda i: (layer, 0, 0))
    return pl.pallas_call(
        _ffn_norm_kernel,
        grid=(n_blocks,),
        in_specs=[rows(D_MODEL), plane(0), plane(1), plane(2), plane(3),
                  pl.BlockSpec((8, row_block), lambda i: (0, first_block + i)), vec, vec],
        out_specs=pl.BlockSpec((row_block, D_MODEL), lambda i: (i, 0)),
        out_shape=jax.ShapeDtypeStruct((n_blocks * row_block, D_MODEL), F32),
        compiler_params=_cparams("parallel"),
        name="ffn_norm",
    )(x1, y4, y4, y4, y4, gate, ln_g, ln_b)


def _rope_tables(pos):
    half = HD_RET // 2
    inv = ROPE_BASE ** (-jnp.arange(half, dtype=F32) / half)
    ang = pos.astype(F32)[:, None] * inv[None, :]
    cos, sin = jnp.cos(ang), jnp.sin(ang)
    return jnp.concatenate([cos, cos], -1), jnp.concatenate([-sin, sin], -1)


def _decay_tables(c):
    log_g = jnp.log1p(-jnp.exp2(-5.0 - jnp.arange(H_RET, dtype=F32)))
    idx = jnp.arange(c, dtype=F32)
    diff = idx[:, None] - idx[None, :]
    intra = jnp.where(diff[None] >= 0, jnp.exp(jnp.maximum(diff, 0.0)[None] * log_g[:, None, None]), 0.0)
    q_dec = jnp.exp((idx + 1.0)[None] * log_g[:, None])
    k_dec = jnp.exp((c - 1.0 - idx)[None] * log_g[:, None])
    c_dec = jnp.exp(c * log_g)
    return intra, q_dec, k_dec, c_dec


def _t5_bucket(dist):
    n = jnp.maximum(dist, 0)
    max_exact = NUM_BUCKETS // 2
    nf = jnp.maximum(n, 1).astype(F32)
    large = max_exact + (jnp.log(nf / max_exact) / math.log(REL_MAX_DIST / max_exact)
                         * (NUM_BUCKETS - max_exact)).astype(jnp.int32)
    large = jnp.minimum(large, NUM_BUCKETS - 1)
    return jnp.where(n < max_exact, n, large)


def _bias_table(rel_bias, bucket):
    hit = bucket[..., None] == jnp.arange(NUM_BUCKETS)
    per_head = rel_bias.astype(F32).T.reshape((H_SWA,) + (1,) * bucket.ndim + (NUM_BUCKETS,))
    return jnp.sum(jnp.where(hit[None], per_head, 0.0), axis=-1)


def kernel(x_prompt, x_sample, state_ret, cache_swa_k, cache_swa_v, w_in, w_out, ret_gn_gain, swa_sinks,
           rel_bias, ln1_g, ln1_b, ln2_g, ln2_b, router_w, router_b, w_gate_up, b_gate_up, w_down, b_down):
    x = jnp.concatenate([x_prompt.reshape(T_PROMPT, D_MODEL), x_sample.reshape(DEC_BATCH, D_MODEL)], axis=0)
    w_in_bf = w_in.astype(BF16)
    w_out_bf = w_out.astype(BF16)
    router_w_bf = jnp.swapaxes(router_w, 1, 2).astype(BF16)
    router_b = router_b.reshape(DEPTH, N_EXPERTS, 1)
    ret_gn_gain, ln1_g, ln1_b, ln2_g, ln2_b = (
        a.reshape(DEPTH, 1, a.shape[-1]) for a in (ret_gn_gain, ln1_g, ln1_b, ln2_g, ln2_b))
    b_gu4 = b_gate_up.reshape(DEPTH, N_EXPERTS, 1, 2 * D_FF)
    b_d4 = b_down.reshape(DEPTH, N_EXPERTS, 1, D_MODEL)
    cache_k = cache_swa_k.reshape(DEPTH, DEC_BATCH, WINDOW, 2 * HD_SWA)
    cache_v = cache_swa_v.reshape(DEPTH, DEC_BATCH, WINDOW, 2 * HD_SWA)

    cos_p, sin_p = _rope_tables(jnp.arange(SEQ, dtype=jnp.int32))
    cos_s, sin_s = _rope_tables(PAST_LEN + jnp.arange(1, dtype=jnp.int32))
    intra, q_dec, k_dec, c_dec = _decay_tables(RET_CHUNK)
    ones = jnp.ones((1, 1, HD_RET), F32)
    tables_p = (cos_p, sin_p, intra, q_dec[:, :, None] * ones, k_dec[:, :, None] * ones, c_dec[:, None, None] * ones)
    intra1, q_dec1, k_dec1, c_dec1 = _decay_tables(1)
    dec_s = jnp.stack([intra1[:, 0, 0], q_dec1[:, 0], k_dec1[:, 0], c_dec1], axis=1)
    i = jnp.arange(WINDOW)[:, None]
    j = jnp.arange(2 * WINDOW)[None, :]
    bias_p = _bias_table(rel_bias, _t5_bucket(WINDOW + i - j))
    js = jnp.arange(WINDOW + 1)
    bias_s = _bias_table(rel_bias, _t5_bucket(WINDOW - js))
    bias_c, bias_n = bias_s[:, :WINDOW], bias_s[:, WINDOW:]
    kv_of_head = jnp.arange(H_SWA) // G_SWA
    head_mask = (kv_of_head[:, None] == jnp.arange(KV_SWA)[None, :]).astype(F32)

    zero_rows = jnp.zeros((SC_DISPATCH_ROWS, D_MODEL // 2), jnp.uint32)

    sp = ss = ko = vo = tail = None
    for l in range(DEPTH):
        h = _in_proj(x, w_in_bf, l)
        o_r, sp, o_s, tail = _mix_prompt(h, tables_p, ret_gn_gain, swa_sinks[l], bias_p, sp, tail, l)
        o_r, ss = _ret_sample(h, dec_s, cos_s, sin_s, ret_gn_gain, state_ret, o_r, ss, l)
        hs = h[T_PROMPT:, 4 * RET_W:]
        q8 = hs[:, :SWA_Q_W].reshape(DEC_BATCH, H_SWA, HD_SWA)
        qm = (q8[:, :, None, :] * head_mask[None, :, :, None]).reshape(DEC_BATCH, H_SWA, 2 * HD_SWA)
        kn = hs[:, SWA_Q_W:SWA_Q_W + SWA_KV_W].reshape(DEC_BATCH, 1, SWA_KV_W)
        vn = hs[:, SWA_Q_W + SWA_KV_W:].reshape(DEC_BATCH, 1, SWA_KV_W)
        osm, ko, vo = _swa_sample(qm, kn, vn, cache_k, cache_v, bias_c, bias_n, swa_sinks[l][:, None], ko, vo, l)
        os_s = jnp.sum(osm.reshape(DEC_BATCH, H_SWA, KV_SWA, HD_SWA) * head_mask[None, :, :, None], axis=2)
        o_s = lax.dynamic_update_slice(o_s, os_s.reshape(DEC_BATCH, SWA_Q_W), (T_PROMPT, 0))

        x1, xpk, top_idx, gate, rank, counts = _out_proj(o_r, o_s, x, w_out_bf, ln1_g, ln1_b, router_w_bf,
                                                         router_b, l)
        pos_km, unused, group_start, group_blocks, group_next = _route(top_idx, rank, counts)
        xs = _sc_dispatch(xpk, pos_km, unused, zero_rows)
        ys = _moe(group_start, group_blocks, group_next, xs, w_gate_up, b_gu4, w_down, b_d4, l)
        y4 = _sc_collect(ys, pos_km.reshape(N_ASSIGN)).reshape(TOP_K, T_ALL, D_MODEL // 2)
        if l < DEPTH - 1:
            x = _ffn_norm(x1, y4, gate, ln2_g, ln2_b, l, ROW_BLOCK, 0, T_ALL // ROW_BLOCK)
        else:
            yp = _ffn_norm(x1, y4, gate, ln2_g, ln2_b, l, FINAL_ROW_BLOCK, 0, T_PROMPT // FINAL_ROW_BLOCK)
            ys_out = _ffn_norm(x1, y4, gate, ln2_g, ln2_b, l, DEC_BATCH, T_PROMPT // DEC_BATCH, 1)

    yp = yp.reshape(BATCH, SEQ, D_MODEL)
    ys_out = ys_out.reshape(DEC_BATCH, 1, D_MODEL)
    kcp = tail[..., :SWA_KV_W].reshape(DEPTH, BATCH, WINDOW, KV_SWA, HD_SWA)
    vcp = tail[..., SWA_KV_W:].reshape(DEPTH, BATCH, WINDOW, KV_SWA, HD_SWA)
    return (yp, ys_out, sp, kcp, vcp, ss,
            ko.reshape(DEPTH, DEC_BATCH, WINDOW, KV_SWA, HD_SWA),
            vo.reshape(DEPTH, DEC_BATCH, WINDOW, KV_SWA, HD_SWA))
```

```python
import functools
import math

import jax
import jax.numpy as jnp
from jax import lax
from jax.experimental import pallas as pl
from jax.experimental.pallas import tpu as pltpu
from jax.experimental.pallas import tpu_sc as plsc

F32 = jnp.float32
BF16 = jnp.bfloat16

D_MODEL = 1024
BATCH = 8
SEQ = 2048
DEPTH = 4
DEC_BATCH = 128
PAST_LEN = 8192
H_RET = 4
HD_RET = 128
RET_CHUNK = 128
ROPE_BASE = 10000.0
H_SWA = 8
KV_SWA = 2
G_SWA = H_SWA // KV_SWA
HD_SWA = 64
WINDOW = 128
NUM_BUCKETS = 32
REL_MAX_DIST = 128
RET_W = H_RET * HD_RET
SWA_Q_W = H_SWA * HD_SWA
SWA_KV_W = KV_SWA * HD_SWA
D_IN = 4 * RET_W + SWA_Q_W + 2 * SWA_KV_W
H_MAIN_W = 4 * RET_W + SWA_Q_W
N_EXPERTS = 32
TOP_K = 4
D_FF = D_MODEL
SWIGLU_LIMIT = 7.0
SWIGLU_ALPHA = 1.702
DEEPNORM_ALPHA = (2 * DEPTH) ** 0.25
LN_EPS = 1e-5
GN_EPS = 1e-5

T_PROMPT = BATCH * SEQ
T_ALL = T_PROMPT + DEC_BATCH
ROW_BLOCK = 384
SAMPLE_BLOCK = 8
MOE_ROWS = 256
MOE_CHUNK = 512
N_ASSIGN = T_ALL * TOP_K
MOE_BLOCKS = N_ASSIGN // MOE_ROWS + N_EXPERTS
MOE_SLOTS = MOE_BLOCKS * MOE_ROWS
N_UNUSED = MOE_SLOTS - N_ASSIGN
SC_CORES = 2
SC_WORKERS = SC_CORES * 16
SC_DISPATCH_ROWS = 64
SC_COLLECT_ROWS = 48
FINAL_ROW_BLOCK = 512
NORM_SUB_ROWS = 128
VMEM_LIMIT = 56 * 1024 * 1024


def _cparams(*sem):
    return pltpu.CompilerParams(dimension_semantics=sem, vmem_limit_bytes=VMEM_LIMIT)


def _project_in(x, w_ref, hm_ref, hkv_ref):
    h = jnp.dot(x.astype(BF16), w_ref[0], preferred_element_type=F32)
    hm_ref[...] = h[:, :H_MAIN_W].astype(BF16)
    hkv_ref[...] = h[:, H_MAIN_W:]


def _in_proj_kernel(x_ref, w_ref, hm_ref, hkv_ref):
    _project_in(x_ref[...], w_ref, hm_ref, hkv_ref)


_IN_PROJ_OUT_SPECS = [pl.BlockSpec((ROW_BLOCK, H_MAIN_W), lambda i: (i, 0)),
                      pl.BlockSpec((ROW_BLOCK, 2 * SWA_KV_W), lambda i: (i, 0))]
_IN_PROJ_OUT_SHAPES = [jax.ShapeDtypeStruct((T_ALL, H_MAIN_W), BF16),
                       jax.ShapeDtypeStruct((T_ALL, 2 * SWA_KV_W), F32)]


def _in_proj(x, w_in_bf, layer):
    return pl.pallas_call(
        _in_proj_kernel,
        grid=(T_ALL // ROW_BLOCK,),
        in_specs=[pl.BlockSpec((ROW_BLOCK, D_MODEL), lambda i: (i, 0)),
                  pl.BlockSpec((1, D_MODEL, D_IN), lambda i: (layer, 0, 0))],
        out_specs=_IN_PROJ_OUT_SPECS,
        out_shape=_IN_PROJ_OUT_SHAPES,
        compiler_params=_cparams("parallel"),
        name="in_proj",
    )(x, w_in_bf)


def _rotate(x, cos, sin_signed):
    return x * cos + pltpu.roll(x, HD_RET // 2, 1) * sin_signed


def _group_norm_gate(o, gain, g):
    mu = jnp.mean(o, -1, keepdims=True)
    var = jnp.mean(jnp.square(o - mu), -1, keepdims=True)
    return (o - mu) * lax.rsqrt(var + GN_EPS) * gain * (g * jax.nn.sigmoid(g))


def _layer_norm(y, g, b):
    mu = jnp.mean(y, -1, keepdims=True)
    var = jnp.mean(jnp.square(y - mu), -1, keepdims=True)
    return (y - mu) * lax.rsqrt(var + LN_EPS) * g + b


def _ret_prompt_stages(q_ref, k_ref, v_ref, g_ref, cos_ref, sin_ref, intra_ref, qdec_ref, kdec_ref,
                       cdec_ref, gain_ref, o_ref, s_ref):
    @pl.when(pl.program_id(1) == 0)
    def _():
        s_ref[...] = jnp.zeros_like(s_ref)

    cos = cos_ref[...]
    sin = sin_ref[...]
    nt = (((1,), (1,)), ((), ()))
    tn = (((0,), (0,)), ((), ()))
    heads = [slice(h * HD_RET, (h + 1) * HD_RET) for h in range(H_RET)]
    vs, atts, crosses = [], [], []
    for h, sl in enumerate(heads):
        q = _rotate(q_ref[:, sl].astype(F32), cos, sin)
        k = _rotate(k_ref[:, sl].astype(F32), cos, sin) * (HD_RET ** -0.5)
        v = v_ref[:, sl]
        s = s_ref[0, 0, h]
        atts.append(lax.dot_general(q.astype(BF16), k.astype(BF16), nt, preferred_element_type=F32))
        crosses.append(jnp.dot((q * qdec_ref[h]).astype(BF16), s.astype(BF16), preferred_element_type=F32))
        s_ref[0, 0, h] = s * cdec_ref[h] + lax.dot_general((k * kdec_ref[h]).astype(BF16), v, tn,
                                                          preferred_element_type=F32)
        vs.append(v)
    yield
    atts = [(att * intra_ref[h]).astype(BF16) for h, att in enumerate(atts)]
    yield
    outs = [jnp.dot(atts[h], vs[h], preferred_element_type=F32) + crosses[h] for h in range(H_RET)]
    yield
    for h, sl in enumerate(heads):
        o_ref[:, sl] = _group_norm_gate(outs[h], gain_ref[0, :, sl], g_ref[:, sl].astype(F32))


def _swa_prompt_stages(sink_ref, q_ref, kvc_ref, kvp_ref, bias_ref, o_ref, tail_ref, lg_scr, p_scr):
    c = pl.program_id(1)

    @pl.when(c == pl.num_programs(1) - 1)
    def _():
        tail_ref[0, 0] = kvc_ref[...].T

    row = lax.broadcasted_iota(jnp.int32, (WINDOW, 2 * WINDOW), 0)
    colj = lax.broadcasted_iota(jnp.int32, (WINDOW, 2 * WINDOW), 1)
    valid = (colj > row) & (colj <= row + WINDOW) & jnp.logical_or(c > 0, colj >= WINDOW)

    kv = jnp.concatenate([kvp_ref[...], kvc_ref[...]], axis=0)
    k_t = kv[:, :SWA_KV_W].T
    k_t_swapped = pltpu.roll(k_t, HD_SWA, 0)
    v = kv[:, SWA_KV_W:]
    v_swapped = pltpu.roll(v, HD_SWA, 1)
    k_row = lax.broadcasted_iota(jnp.int32, k_t.shape, 0)
    v_lane = lax.broadcasted_iota(jnp.int32, v.shape, 1)
    ones = jnp.ones(v.shape, BF16)
    k_pad, v_pad = {}, {}
    for kvh in range(KV_SWA):
        for par in range(2):
            k_src = k_t if kvh == par else k_t_swapped
            v_src = v if kvh == par else v_swapped
            k_pad[kvh, par] = jnp.where((k_row >= par * HD_SWA) & (k_row < (par + 1) * HD_SWA), k_src, 0.0).astype(BF16)
            v_pad[kvh, par] = jnp.where((v_lane >= par * HD_SWA) & (v_lane < (par + 1) * HD_SWA), v_src, 0.0).astype(BF16)
    for hd in range(H_SWA):
        pair, par, kvh = hd // 2, hd % 2, hd // G_SWA
        q2 = (q_ref[:, pair * 2 * HD_SWA:(pair + 1) * 2 * HD_SWA] * (HD_SWA ** -0.5)).astype(BF16)
        logits = jnp.dot(q2, k_pad[kvh, par], preferred_element_type=F32) + bias_ref[hd]
        lg_scr[hd] = jnp.where(valid, logits, -jnp.inf)
    yield
    sink_terms = []
    for hd in range(H_SWA):
        logits = lg_scr[hd]
        sink = sink_ref[hd]
        m = jnp.maximum(jnp.max(logits, -1, keepdims=True), sink)
        p_scr[hd] = jnp.exp(logits - m).astype(BF16)
        sink_terms.append(jnp.exp(sink - m))
    yield
    for pair in range(H_SWA // 2):
        acc = None
        for par in range(2):
            hd = 2 * pair + par
            p = p_scr[hd]
            den = jnp.dot(p, ones, preferred_element_type=F32) + sink_terms[hd]
            part = jnp.dot(p, v_pad[hd // G_SWA, par], preferred_element_type=F32) / den
            acc = part if acc is None else acc + part
        o_ref[:, pair * 2 * HD_SWA:(pair + 1) * 2 * HD_SWA] = acc


def _mix_prompt_kernel(*refs):
    n_ret_in, n_swa_in = 11, 5
    ret_in = refs[:n_ret_in]
    swa_in = refs[n_ret_in:n_ret_in + n_swa_in]
    o_r_ref, s_ref, o_s_ref, tail_ref, lg_scr, p_scr = refs[-6:]
    groups = [_ret_prompt_stages(*ret_in, o_r_ref, s_ref),
              _swa_prompt_stages(*swa_in, o_s_ref, tail_ref, lg_scr, p_scr)]
    while groups:
        for g in list(groups):
            if next(g, groups) is groups:
                groups.remove(g)


def _mix_prompt(h, h_kv, tables, gain, sinks_l, bias, s_prev, tail_prev, layer):
    cos, sin, intra, qdec, kdec, cdec = tables
    nb = SEQ // WINDOW
    qcol = 4 * RET_W // SWA_Q_W
    kvcol = 0

    def col(j):
        return pl.BlockSpec((RET_CHUNK, RET_W), lambda b, c: (b * nb + c, j))

    def whole(a):
        return pl.BlockSpec(a.shape, lambda b, c: (0,) * a.ndim)

    in_specs = [col(0), col(1), col(2), col(3),
                pl.BlockSpec((RET_CHUNK, HD_RET), lambda b, c: (c, 0)),
                pl.BlockSpec((RET_CHUNK, HD_RET), lambda b, c: (c, 0)),
                whole(intra), whole(qdec), whole(kdec), whole(cdec),
                pl.BlockSpec((1, 1, RET_W), lambda b, c: (layer, 0, 0)),
                pl.BlockSpec(memory_space=pltpu.SMEM),
                pl.BlockSpec((WINDOW, SWA_Q_W), lambda b, c: (b * nb + c, qcol)),
                pl.BlockSpec((WINDOW, 2 * SWA_KV_W), lambda b, c: (b * nb + c, kvcol)),
                pl.BlockSpec((WINDOW, 2 * SWA_KV_W), lambda b, c: (b * nb + jnp.maximum(c - 1, 0), kvcol)),
                pl.BlockSpec((H_SWA, WINDOW, 2 * WINDOW), lambda b, c: (0, 0, 0))]
    args = [h, h, h, h, cos, sin, intra, qdec, kdec, cdec, gain, sinks_l, h, h_kv, h_kv, bias]
    aliases = {}
    if s_prev is not None:
        in_specs += [pl.BlockSpec(memory_space=pl.ANY), pl.BlockSpec(memory_space=pl.ANY)]
        args += [s_prev, tail_prev]
        aliases = {16: 1, 17: 3}

    return pl.pallas_call(
        _mix_prompt_kernel,
        grid=(BATCH, nb),
        in_specs=in_specs,
        scratch_shapes=[pltpu.VMEM((H_SWA, WINDOW, 2 * WINDOW), F32), pltpu.VMEM((H_SWA, WINDOW, 2 * WINDOW), BF16)],
        out_specs=[pl.BlockSpec((RET_CHUNK, RET_W), lambda b, c: (b * nb + c, 0)),
                   pl.BlockSpec((1, 1, H_RET, HD_RET, HD_RET), lambda b, c: (layer, b, 0, 0, 0)),
                   pl.BlockSpec((WINDOW, SWA_Q_W), lambda b, c: (b * nb + c, 0)),
                   pl.BlockSpec((1, 1, 2 * SWA_KV_W, WINDOW), lambda b, c: (layer, b, 0, 0))],
        out_shape=[jax.ShapeDtypeStruct((T_ALL, RET_W), F32),
                   jax.ShapeDtypeStruct((DEPTH, BATCH, H_RET, HD_RET, HD_RET), F32),
                   jax.ShapeDtypeStruct((T_ALL, SWA_Q_W), F32),
                   jax.ShapeDtypeStruct((DEPTH, BATCH, 2 * SWA_KV_W, WINDOW), F32)],
        input_output_aliases=aliases,
        compiler_params=_cparams("parallel", "arbitrary"),
        name="mix_prompt",
    )(*args)


def _ret_sample_kernel(dec_ref, q_ref, k_ref, v_ref, g_ref, cos_ref, sin_ref, gain_ref, st_ref, o_ref, so_ref):
    cos = cos_ref[...]
    sin = sin_ref[...]
    for h in range(H_RET):
        sl = slice(h * HD_RET, (h + 1) * HD_RET)
        intra, qdec, kdec, cdec = dec_ref[h, 0], dec_ref[h, 1], dec_ref[h, 2], dec_ref[h, 3]
        q = _rotate(q_ref[:, sl], cos, sin)
        k = _rotate(k_ref[:, sl], cos, sin) * (HD_RET ** -0.5)
        v = v_ref[:, sl]
        att = jnp.sum(q * k, -1, keepdims=True) * intra
        q_col = (q * qdec).T
        k_col = (k * kdec).T
        rows = []
        for b in range(SAMPLE_BLOCK):
            s = st_ref[0, b, h]
            v_row = v[b:b + 1, :]
            rows.append(att[b:b + 1, :] * v_row + jnp.sum(q_col[:, b:b + 1] * s, axis=0, keepdims=True))
            so_ref[0, b, h] = s * cdec + k_col[:, b:b + 1] * v_row
        o = jnp.concatenate(rows, axis=0)
        o_ref[:, sl] = _group_norm_gate(o, gain_ref[0, :, sl], g_ref[:, sl])


def _ret_sample(h, dec_s, cos_s, sin_s, gain, state_ret, o_r, st_prev, layer):
    base = T_PROMPT // SAMPLE_BLOCK

    def col(j):
        return pl.BlockSpec((SAMPLE_BLOCK, RET_W), lambda i: (i, j))

    in_specs = [pl.BlockSpec(memory_space=pltpu.SMEM),
                col(0), col(1), col(2), col(3),
                pl.BlockSpec((1, HD_RET), lambda i: (0, 0)),
                pl.BlockSpec((1, HD_RET), lambda i: (0, 0)),
                pl.BlockSpec((1, 1, RET_W), lambda i: (layer, 0, 0)),
                pl.BlockSpec((1, SAMPLE_BLOCK, H_RET, HD_RET, HD_RET), lambda i: (layer, i, 0, 0, 0)),
                pl.BlockSpec(memory_space=pl.ANY)]
    args = [dec_s, h, h, h, h, cos_s, sin_s, gain, state_ret, o_r]
    aliases = {9: 0}
    if st_prev is not None:
        in_specs.append(pl.BlockSpec(memory_space=pl.ANY))
        args.append(st_prev)
        aliases[10] = 1

    def body(*refs):
        _ret_sample_kernel(*refs[:9], *refs[-2:])

    return pl.pallas_call(
        body,
        grid=(DEC_BATCH // SAMPLE_BLOCK,),
        in_specs=in_specs,
        out_specs=[pl.BlockSpec((SAMPLE_BLOCK, RET_W), lambda i: (base + i, 0)),
                   pl.BlockSpec((1, SAMPLE_BLOCK, H_RET, HD_RET, HD_RET), lambda i: (layer, i, 0, 0, 0))],
        out_shape=[jax.ShapeDtypeStruct((T_ALL, RET_W), F32),
                   jax.ShapeDtypeStruct((DEPTH, DEC_BATCH, H_RET, HD_RET, HD_RET), F32)],
        input_output_aliases=aliases,
        compiler_params=_cparams("parallel"),
        name="ret_sample",
    )(*args)


def _swa_sample_kernel(qm_ref, kn_ref, vn_ref, knc_ref, vnc_ref, kc_ref, vc_ref, bias_ref, biasn_ref, sink_ref,
                       o_ref, ko_ref, vo_ref):
    qm = qm_ref[...]
    kn = kn_ref[...]
    vn = vn_ref[...]
    kc = kc_ref[0]
    vc = vc_ref[0]
    scale = HD_SWA ** -0.5
    logits = jnp.einsum('bhd,bdj->bhj', qm.astype(BF16), kc.astype(BF16),
                        preferred_element_type=F32) * scale + bias_ref[...][None]
    j = lax.broadcasted_iota(jnp.int32, logits.shape, 2)
    logits = jnp.where(j >= 1, logits, -jnp.inf)
    ln = jnp.sum(qm * kn, -1, keepdims=True) * scale + biasn_ref[...][None]
    sink = sink_ref[...][None]
    m = jnp.maximum(jnp.maximum(jnp.max(logits, -1, keepdims=True), ln), sink)
    p = jnp.exp(logits - m)
    pn = jnp.exp(ln - m)
    denom = jnp.sum(p, -1, keepdims=True) + pn + jnp.exp(sink - m)
    w = p / denom
    o_ref[...] = (jnp.einsum('bhj,bdj->bhd', w.astype(BF16), vc.astype(BF16), preferred_element_type=F32)
                  + (pn / denom) * vn)
    last = lax.broadcasted_iota(jnp.int32, kc.shape, 2) == WINDOW - 1
    ko_ref[0] = jnp.where(last, knc_ref[...], pltpu.roll(kc, WINDOW - 1, 2))
    vo_ref[0] = jnp.where(last, vnc_ref[...], pltpu.roll(vc, WINDOW - 1, 2))


def _swa_sample(qm, kn, vn, cache_k, cache_v, bias_c, bias_n, sink_col, ko_prev, vo_prev, layer):
    sb = SAMPLE_BLOCK
    cache_spec = pl.BlockSpec((1, sb, 2 * HD_SWA, WINDOW), lambda i: (layer, i, 0, 0))
    row_spec = pl.BlockSpec((sb, 1, 2 * HD_SWA), lambda i: (i, 0, 0))
    col_spec = pl.BlockSpec((sb, 2 * HD_SWA, 1), lambda i: (i, 0, 0))
    in_specs = [pl.BlockSpec((sb, H_SWA, 2 * HD_SWA), lambda i: (i, 0, 0)),
                row_spec, row_spec, col_spec, col_spec,
                cache_spec, cache_spec,
                pl.BlockSpec((H_SWA, WINDOW), lambda i: (0, 0)),
                pl.BlockSpec((H_SWA, 1), lambda i: (0, 0)),
                pl.BlockSpec((H_SWA, 1), lambda i: (0, 0))]
    args = [qm, kn, vn, kn.reshape(DEC_BATCH, 2 * HD_SWA, 1), vn.reshape(DEC_BATCH, 2 * HD_SWA, 1),
            cache_k, cache_v, bias_c, bias_n, sink_col]
    aliases = {}
    if ko_prev is not None:
        in_specs += [pl.BlockSpec(memory_space=pl.ANY), pl.BlockSpec(memory_space=pl.ANY)]
        args += [ko_prev, vo_prev]
        aliases = {10: 1, 11: 2}

    def body(*refs):
        _swa_sample_kernel(*refs[:10], *refs[-3:])

    cshape = jax.ShapeDtypeStruct((DEPTH, DEC_BATCH, 2 * HD_SWA, WINDOW), F32)
    return pl.pallas_call(
        body,
        grid=(DEC_BATCH // sb,),
        in_specs=in_specs,
        out_specs=[pl.BlockSpec((sb, H_SWA, 2 * HD_SWA), lambda i: (i, 0, 0)), cache_spec, cache_spec],
        out_shape=[jax.ShapeDtypeStruct((DEC_BATCH, H_SWA, 2 * HD_SWA), F32), cshape, cshape],
        input_output_aliases=aliases,
        compiler_params=_cparams("parallel"),
        name="swa_sample",
    )(*args)


def _pack_bf16_pairs(x):
    w = x.shape[1] // 2
    lo = lax.bitcast_convert_type(x[:, :w].astype(BF16).astype(F32), jnp.uint32) >> 16
    hi = lax.bitcast_convert_type(x[:, w:].astype(BF16).astype(F32), jnp.uint32)
    return lo | hi


def _unpack_bf16_pairs(p):
    lo = lax.bitcast_convert_type(p << 16, F32).astype(BF16)
    hi = lax.bitcast_convert_type(p & jnp.uint32(0xFFFF0000), F32).astype(BF16)
    return lo, hi


def _out_proj_kernel(or_ref, os_ref, x_ref, wr_ref, ws_ref, g_ref, b_ref, rw_ref, rb_ref,
                     x1_ref, xpk_ref, idx_ref, gate_ref, rank_ref, cnt_ref):
    mix = (jnp.dot(or_ref[...].astype(BF16), wr_ref[0], preferred_element_type=F32)
           + jnp.dot(os_ref[...].astype(BF16), ws_ref[0], preferred_element_type=F32))
    x1 = _layer_norm(DEEPNORM_ALPHA * x_ref[...] + mix, g_ref[0], b_ref[0])
    x1_ref[...] = x1
    xpk_ref[...] = _pack_bf16_pairs(x1)
    nt = (((1,), (1,)), ((), ()))
    logits = lax.dot_general(rw_ref[0], x1.astype(BF16), nt, preferred_element_type=F32) + rb_ref[0]
    rows = logits.shape[1]
    expert = lax.broadcasted_iota(jnp.int32, logits.shape, 0)
    vals, hits = [], []
    for kk in range(TOP_K):
        m = jnp.max(logits, 0, keepdims=True)
        idx = jnp.min(jnp.where(logits == m, expert, N_EXPERTS), 0, keepdims=True)
        vals.append(m)
        hits.append(expert == idx)
        idx_ref[kk:kk + 1, :] = idx
        logits = jnp.where(expert == idx, -jnp.inf, logits)
    es = [jnp.exp(v - vals[0]) for v in vals]
    tot = es[0] + es[1] + es[2] + es[3]
    gate_ref[...] = jnp.concatenate([e / tot for e in es] + [jnp.zeros((8 - TOP_K, rows), F32)], axis=0)
    chosen = sum(h.astype(F32) for h in hits)
    earlier = (lax.broadcasted_iota(jnp.int32, (rows, rows), 0)
               < lax.broadcasted_iota(jnp.int32, (rows, rows), 1)).astype(BF16)
    before = jnp.dot(chosen.astype(BF16), earlier, preferred_element_type=F32)
    for kk in range(TOP_K):
        rank_ref[kk:kk + 1, :] = jnp.sum(jnp.where(hits[kk], before, 0.0), 0, keepdims=True).astype(jnp.int32)
    cnt_ref[0] = jnp.sum(chosen, axis=1, keepdims=True).astype(jnp.int32)


def _out_proj(o_r, o_s, x, w_out_bf, ln_g, ln_b, router_w_bf, router_b, layer):
    def rows(w):
        return pl.BlockSpec((ROW_BLOCK, w), lambda i: (i, 0))

    def vec(w):
        return pl.BlockSpec((1, 1, w), lambda i: (layer, 0, 0))

    def cols(h):
        return pl.BlockSpec((h, ROW_BLOCK), lambda i: (0, i))

    return pl.pallas_call(
        _out_proj_kernel,
        grid=(T_ALL // ROW_BLOCK,),
        in_specs=[rows(RET_W), rows(SWA_Q_W), rows(D_MODEL),
                  pl.BlockSpec((1, RET_W, D_MODEL), lambda i: (layer, 0, 0)),
                  pl.BlockSpec((1, SWA_Q_W, D_MODEL), lambda i: (layer, 1, 0)),
                  vec(D_MODEL), vec(D_MODEL),
                  pl.BlockSpec((1, N_EXPERTS, D_MODEL), lambda i: (layer, 0, 0)),
                  pl.BlockSpec((1, N_EXPERTS, 1), lambda i: (layer, 0, 0))],
        out_specs=[rows(D_MODEL), rows(D_MODEL // 2), cols(TOP_K), cols(8), cols(TOP_K),
                   pl.BlockSpec((1, N_EXPERTS, 1), lambda i: (i, 0, 0))],
        out_shape=[jax.ShapeDtypeStruct((T_ALL, D_MODEL), F32),
                   jax.ShapeDtypeStruct((T_ALL, D_MODEL // 2), jnp.uint32),
                   jax.ShapeDtypeStruct((TOP_K, T_ALL), jnp.int32),
                   jax.ShapeDtypeStruct((8, T_ALL), F32),
                   jax.ShapeDtypeStruct((TOP_K, T_ALL), jnp.int32),
                   jax.ShapeDtypeStruct((T_ALL // ROW_BLOCK, N_EXPERTS, 1), jnp.int32)],
        compiler_params=_cparams("parallel"),
        name="out_proj",
    )(o_r, o_s, x, w_out_bf, w_out_bf, ln_g, ln_b, router_w_bf, router_b)


def _moe_kernel(start_ref, nblk_ref, next_ref, xs_hbm, wgu_ref, bgu_ref, wd_ref, bd_ref, ys_hbm,
                wgu_bf, wd_bf, x_buf, y_buf, in_sems, out_sems, pending, ready):
    e = pl.program_id(0)
    per_chunk = MOE_CHUNK // MOE_ROWS
    nblk = nblk_ref[e]
    n_full = nblk // per_chunk
    has_tail = nblk % per_chunk == 1
    base = start_ref[e]

    def in_copy(row0, rows, slot):
        src = xs_hbm.at[pl.ds(pl.multiple_of(row0, MOE_ROWS), rows)]
        return pltpu.make_async_copy(src, x_buf.at[slot, pl.ds(0, rows)], in_sems.at[slot])

    def out_copy(row0, rows, slot):
        dst = ys_hbm.at[pl.ds(pl.multiple_of(row0, MOE_ROWS), rows)]
        return pltpu.make_async_copy(y_buf.at[slot, pl.ds(0, rows)], dst, out_sems.at[slot])

    def start_first_chunk(expert, slot):
        @pl.when(nblk_ref[expert] >= per_chunk)
        def _():
            in_copy(start_ref[expert], MOE_CHUNK, slot).start()

        @pl.when(nblk_ref[expert] < per_chunk)
        def _():
            in_copy(start_ref[expert], MOE_ROWS, slot).start()

    def prefetch_next_expert(slot):
        nxt = next_ref[e]

        @pl.when(nxt < N_EXPERTS)
        def _():
            start_first_chunk(nxt, slot)
            ready[0] = slot

    def drain(slot):
        for rows in (MOE_CHUNK, MOE_ROWS):
            @pl.when(pending[slot] == rows)
            def _():
                out_copy(0, rows, slot).wait()
        pending[slot] = 0

    def compute(rows, slot):
        x = jnp.concatenate(_unpack_bf16_pairs(x_buf[slot, pl.ds(0, rows), :]), axis=1)
        gu = jnp.dot(x, wgu_bf[...], preferred_element_type=F32) + bgu_ref[0, 0]
        glu = jnp.minimum(gu[:, :D_FF], SWIGLU_LIMIT)
        lin = jnp.clip(gu[:, D_FF:], -SWIGLU_LIMIT, SWIGLU_LIMIT)
        act = glu * jax.nn.sigmoid(SWIGLU_ALPHA * glu) * (lin + 1.0)
        y = jnp.dot(act.astype(BF16), wd_bf[...], preferred_element_type=F32) + bd_ref[0, 0]
        y_buf[slot, pl.ds(0, rows), :] = _pack_bf16_pairs(y)

    def run_chunk(row0, rows, slot):
        in_copy(row0, rows, slot).wait()
        drain(slot)
        compute(rows, slot)
        out_copy(row0, rows, slot).start()
        pending[slot] = rows

    @pl.when(e == 0)
    def _():
        pending[0] = 0
        pending[1] = 0
        ready[0] = -1

    @pl.when(nblk > 0)
    def _():
        @pl.when(ready[0] < 0)
        def _():
            start_first_chunk(e, 0)
            ready[0] = 0

        first_slot = ready[0]
        wgu_bf[...] = wgu_ref[0, 0].astype(BF16)
        wd_bf[...] = wd_ref[0, 0].astype(BF16)

        def body(c, carry):
            slot = (first_slot + c) & 1
            row0 = pl.multiple_of(base + c * MOE_CHUNK, MOE_ROWS)

            @pl.when(c + 1 < n_full)
            def _():
                in_copy(row0 + MOE_CHUNK, MOE_CHUNK, 1 - slot).start()

            @pl.when(jnp.logical_and(c + 1 == n_full, has_tail))
            def _():
                in_copy(row0 + MOE_CHUNK, MOE_ROWS, 1 - slot).start()

            @pl.when(jnp.logical_and(c + 1 == n_full, jnp.logical_not(has_tail)))
            def _():
                prefetch_next_expert(1 - slot)

            run_chunk(row0, MOE_CHUNK, slot)
            return carry

        lax.fori_loop(0, n_full, body, 0)

        @pl.when(has_tail)
        def _():
            slot = (first_slot + n_full) & 1
            prefetch_next_expert(1 - slot)
            run_chunk(pl.multiple_of(base + n_full * MOE_CHUNK, MOE_ROWS), MOE_ROWS, slot)

    @pl.when(e == pl.num_programs(0) - 1)
    def _():
        drain(0)
        drain(1)


def _moe(group_start, group_blocks, group_next, xs, w_gate_up, b_gate_up, w_down, b_down, layer):
    def wsel(e, st, nb, nx):
        return (layer, e, 0, 0)

    return pl.pallas_call(
        _moe_kernel,
        grid_spec=pltpu.PrefetchScalarGridSpec(
            num_scalar_prefetch=3,
            grid=(N_EXPERTS,),
            in_specs=[pl.BlockSpec(memory_space=pl.ANY),
                      pl.BlockSpec((1, 1, D_MODEL, 2 * D_FF), wsel),
                      pl.BlockSpec((1, 1, 1, 2 * D_FF), wsel),
                      pl.BlockSpec((1, 1, D_FF, D_MODEL), wsel),
                      pl.BlockSpec((1, 1, 1, D_MODEL), wsel)],
            out_specs=pl.BlockSpec(memory_space=pl.ANY),
            scratch_shapes=[pltpu.VMEM((D_MODEL, 2 * D_FF), BF16), pltpu.VMEM((D_FF, D_MODEL), BF16),
                            pltpu.VMEM((2, MOE_CHUNK, D_MODEL // 2), jnp.uint32),
                            pltpu.VMEM((2, MOE_CHUNK, D_MODEL // 2), jnp.uint32),
                            pltpu.SemaphoreType.DMA((2,)), pltpu.SemaphoreType.DMA((2,)),
                            pltpu.SMEM((2,), jnp.int32), pltpu.SMEM((1,), jnp.int32)]),
        out_shape=jax.ShapeDtypeStruct((MOE_SLOTS, D_MODEL // 2), jnp.uint32),
        compiler_params=_cparams("arbitrary"),
        name="moe_experts",
    )(group_start, group_blocks, group_next, xs, w_gate_up, b_gate_up, w_down, b_down)


def _route(idx, rank, counts):
    experts = jnp.arange(N_EXPERTS, dtype=jnp.int32)
    counts = counts.reshape(T_ALL // ROW_BLOCK, N_EXPERTS)
    before_block = jnp.cumsum(counts, axis=0) - counts
    total = jnp.sum(counts, axis=0)
    padded = (total + MOE_ROWS - 1) // MOE_ROWS * MOE_ROWS
    pad_end = jnp.cumsum(padded)
    pad_start = pad_end - padded
    base = jnp.repeat((pad_start[None, :] + before_block).T, ROW_BLOCK, axis=1)
    pos = jnp.sum(jnp.where(idx[:, None, :] == experts[None, :, None], base[None], 0), axis=1) + rank
    gap = padded - total
    gap_end = jnp.concatenate([jnp.cumsum(gap), jnp.full((1,), N_UNUSED, jnp.int32)])
    gap_start = jnp.concatenate([jnp.zeros((1,), jnp.int32), gap_end[:-1]])
    first_free = jnp.concatenate([pad_start + total, pad_end[-1:]])
    j = jnp.arange(N_UNUSED, dtype=jnp.int32)
    region = (j[:, None] >= gap_start[None, :]) & (j[:, None] < gap_end[None, :])
    unused = jnp.sum(jnp.where(region, first_free[None, :] + j[:, None] - gap_start[None, :], 0), axis=-1)
    nonempty = jnp.where(total > 0, experts, N_EXPERTS)
    following = jnp.concatenate([lax.cummin(nonempty, reverse=True)[1:], jnp.full((1,), N_EXPERTS, jnp.int32)])
    return (pos.astype(jnp.int32), unused.astype(jnp.int32),
            pad_start.astype(jnp.int32), (padded // MOE_ROWS).astype(jnp.int32), following.astype(jnp.int32))


def _sc_worker():
    return lax.axis_index("s") * SC_CORES + lax.axis_index("c")


def _sc_dispatch(xpk, pos_km, unused, zero_rows):
    ch = SC_DISPATCH_ROWS
    n_chunks = T_ALL // ch
    n_zero_chunks = N_UNUSED // ch
    width = xpk.shape[1]
    mesh = plsc.VectorSubcoreMesh(core_axis_name="c", subcore_axis_name="s")

    @functools.partial(
        pl.kernel, mesh=mesh,
        out_type=jax.ShapeDtypeStruct((MOE_SLOTS, width), xpk.dtype),
        scratch_types=[pltpu.VMEM((2, ch, width), xpk.dtype), pltpu.VMEM((2, TOP_K, ch), jnp.int32),
                       pltpu.SemaphoreType.DMA((2,)), pltpu.SemaphoreType.DMA((TOP_K,))],
        name="sc_dispatch")
    def run(x_hbm, pos_hbm, unused_hbm, zero_hbm, xs_hbm, rows_v, idx_v, load_sems, store_sems):
        wid = _sc_worker()
        pltpu.sync_copy(zero_hbm, rows_v.at[0])
        for r in range(n_zero_chunks // SC_WORKERS):
            start = pl.multiple_of((wid + r * SC_WORKERS) * ch, ch)
            pltpu.sync_copy(unused_hbm.at[pl.ds(start, ch)], idx_v.at[0, 0])
            pltpu.sync_copy(rows_v.at[0], xs_hbm.at[idx_v.at[0, 0]])

        def first_row(r):
            return pl.multiple_of((wid + r * SC_WORKERS) * ch, ch)

        def rows_copy(r, slot):
            return pltpu.make_async_copy(x_hbm.at[pl.ds(first_row(r), ch)], rows_v.at[slot], load_sems.at[slot])

        def start_load(r, slot):
            for k in range(TOP_K):
                pltpu.sync_copy(pos_hbm.at[k, pl.ds(first_row(r), ch)], idx_v.at[slot, k])
            rows_copy(r, slot).start()

        def has_chunk(r):
            return wid + r * SC_WORKERS < n_chunks

        start_load(0, 0)

        def body(r, carry):
            slot = r & 1

            @pl.when(has_chunk(r + 1))
            def _():
                start_load(r + 1, 1 - slot)

            @pl.when(has_chunk(r))
            def _():
                rows_copy(r, slot).wait()
                copies = [pltpu.async_copy(rows_v.at[slot], xs_hbm.at[idx_v.at[slot, k]], store_sems.at[k])
                          for k in range(TOP_K)]
                for cp in copies:
                    cp.wait()
            return carry

        lax.fori_loop(0, pl.cdiv(n_chunks, SC_WORKERS), body, 0)

    return run(xpk, pos_km, unused, zero_rows)


def _sc_collect(ys, pos_flat):
    ch = SC_COLLECT_ROWS
    per_worker = N_ASSIGN // SC_WORKERS
    mesh = plsc.VectorSubcoreMesh(core_axis_name="c", subcore_axis_name="s")

    @functools.partial(
        pl.kernel, mesh=mesh,
        out_type=jax.ShapeDtypeStruct((N_ASSIGN, ys.shape[1]), ys.dtype),
        scratch_types=[pltpu.VMEM((2, ch, ys.shape[1]), ys.dtype), pltpu.VMEM((2, ch), jnp.int32),
                       pltpu.SemaphoreType.DMA((2,))],
        name="sc_collect")
    def run(ys_hbm, pos_hbm, y4_hbm, rows_v, idx_v, sems):
        wid = _sc_worker()
        n_chunks = per_worker // ch

        def first_row(r):
            return pl.multiple_of(wid * per_worker + r * ch, 8)

        def gather(slot):
            return pltpu.make_async_copy(ys_hbm.at[idx_v.at[slot]], rows_v.at[slot], sems.at[slot])

        def start_gather(r, slot):
            pltpu.sync_copy(pos_hbm.at[pl.ds(first_row(r), ch)], idx_v.at[slot])
            gather(slot).start()

        start_gather(0, 0)

        def body(r, carry):
            slot = r & 1

            @pl.when(r + 1 < n_chunks)
            def _():
                start_gather(r + 1, 1 - slot)

            gather(slot).wait()
            pltpu.sync_copy(rows_v.at[slot], y4_hbm.at[pl.ds(first_row(r), ch)])
            return carry

        lax.fori_loop(0, n_chunks, body, 0)

    return run(ys, pos_flat)


def _combine_norm(x_ref, y_refs, gate_ref, g_ref, b_ref, rows=slice(None)):
    gate = gate_ref[:, rows].T
    lo = hi = None
    for k, y_ref in enumerate(y_refs):
        p = y_ref[0, rows, :]
        g = gate[:, k:k + 1]
        lo_k = g * lax.bitcast_convert_type(p << 16, F32)
        hi_k = g * lax.bitcast_convert_type(p & jnp.uint32(0xFFFF0000), F32)
        lo = lo_k if lo is None else lo + lo_k
        hi = hi_k if hi is None else hi + hi_k
    ffn = jnp.concatenate([lo, hi], axis=1)
    return _layer_norm(DEEPNORM_ALPHA * x_ref[rows, :] + ffn, g_ref[0], b_ref[0])


def _ffn_norm_kernel(x_ref, y0_ref, y1_ref, y2_ref, y3_ref, gate_ref, g_ref, b_ref, o_ref):
    o_ref[...] = _combine_norm(x_ref, (y0_ref, y1_ref, y2_ref, y3_ref), gate_ref, g_ref, b_ref)


def _norm_proj_kernel(x_ref, y0_ref, y1_ref, y2_ref, y3_ref, gate_ref, g_ref, b_ref, w_ref,
                      x2_ref, hm_ref, hkv_ref):
    for r in range(ROW_BLOCK // NORM_SUB_ROWS):
        rows = pl.ds(r * NORM_SUB_ROWS, NORM_SUB_ROWS)
        x2 = _combine_norm(x_ref, (y0_ref, y1_ref, y2_ref, y3_ref), gate_ref, g_ref, b_ref, rows)
        x2_ref[rows, :] = x2
        _project_in(x2, w_ref, hm_ref.at[rows, :], hkv_ref.at[rows, :])


def _norm_proj(x1, y4, gate, ln_g, ln_b, w_in_bf, layer):
    rows = pl.BlockSpec((ROW_BLOCK, D_MODEL), lambda i: (i, 0))

    def plane(k):
        return pl.BlockSpec((1, ROW_BLOCK, D_MODEL // 2), lambda i: (k, i, 0))

    vec = pl.BlockSpec((1, 1, D_MODEL), lambda i: (layer, 0, 0))
    return pl.pallas_call(
        _norm_proj_kernel,
        grid=(T_ALL // ROW_BLOCK,),
        in_specs=[rows, plane(0), plane(1), plane(2), plane(3),
                  pl.BlockSpec((8, ROW_BLOCK), lambda i: (0, i)), vec, vec,
                  pl.BlockSpec((1, D_MODEL, D_IN), lambda i: (layer + 1, 0, 0))],
        out_specs=[rows] + _IN_PROJ_OUT_SPECS,
        out_shape=[jax.ShapeDtypeStruct((T_ALL, D_MODEL), F32)] + _IN_PROJ_OUT_SHAPES,
        compiler_params=_cparams("parallel"),
        name="norm_proj",
    )(x1, y4, y4, y4, y4, gate, ln_g, ln_b, w_in_bf)


def _ffn_norm(x1, y4, gate, ln_g, ln_b, layer, row_block, first_block, n_blocks):
    def rows(w):
        return pl.BlockSpec((row_block, w), lambda i: (first_block + i, 0))

    def plane(k):
        return pl.BlockSpec((1, row_block, D_MODEL // 2), lambda i: (k, first_block + i, 0))

    vec = pl.BlockSpec((1, 1, D_MODEL), lambda i: (layer, 0, 0))
    return pl.pallas_call(
        _ffn_norm_kernel,
        grid=(n_blocks,),
        in_specs=[rows(D_MODEL), plane(0), plane(1), plane(2), plane(3),
                  pl.BlockSpec((8, row_block), lambda i: (0, first_block + i)), vec, vec],
        out_specs=pl.BlockSpec((row_block, D_MODEL), lambda i: (i, 0)),
        out_shape=jax.ShapeDtypeStruct((n_blocks * row_block, D_MODEL), F32),
        compiler_params=_cparams("parallel"),
        name="ffn_norm",
    )(x1, y4, y4, y4, y4, gate, ln_g, ln_b)


def _rope_tables(pos):
    half = HD_RET // 2
    inv = ROPE_BASE ** (-jnp.arange(half, dtype=F32) / half)
    ang = pos.astype(F32)[:, None] * inv[None, :]
    cos, sin = jnp.cos(ang), jnp.sin(ang)
    return jnp.concatenate([cos, cos], -1), jnp.concatenate([-sin, sin], -1)


def _decay_tables(c):
    log_g = jnp.log1p(-jnp.exp2(-5.0 - jnp.arange(H_RET, dtype=F32)))
    idx = jnp.arange(c, dtype=F32)
    diff = idx[:, None] - idx[None, :]
    intra = jnp.where(diff[None] >= 0, jnp.exp(jnp.maximum(diff, 0.0)[None] * log_g[:, None, None]), 0.0)
    q_dec = jnp.exp((idx + 1.0)[None] * log_g[:, None])
    k_dec = jnp.exp((c - 1.0 - idx)[None] * log_g[:, None])
    c_dec = jnp.exp(c * log_g)
    return intra, q_dec, k_dec, c_dec


def _t5_bucket(dist):
    n = jnp.maximum(dist, 0)
    max_exact = NUM_BUCKETS // 2
    nf = jnp.maximum(n, 1).astype(F32)
    large = max_exact + (jnp.log(nf / max_exact) / math.log(REL_MAX_DIST / max_exact)
                         * (NUM_BUCKETS - max_exact)).astype(jnp.int32)
    large = jnp.minimum(large, NUM_BUCKETS - 1)
    return jnp.where(n < max_exact, n, large)


def _bias_table(rel_bias, bucket):
    hit = bucket[..., None] == jnp.arange(NUM_BUCKETS)
    per_head = rel_bias.astype(F32).T.reshape((H_SWA,) + (1,) * bucket.ndim + (NUM_BUCKETS,))
    return jnp.sum(jnp.where(hit[None], per_head, 0.0), axis=-1)


def kernel(x_prompt, x_sample, state_ret, cache_swa_k, cache_swa_v, w_in, w_out, ret_gn_gain, swa_sinks,
           rel_bias, ln1_g, ln1_b, ln2_g, ln2_b, router_w, router_b, w_gate_up, b_gate_up, w_down, b_down):
    x = jnp.concatenate([x_prompt.reshape(T_PROMPT, D_MODEL), x_sample.reshape(DEC_BATCH, D_MODEL)], axis=0)
    w_in_bf = w_in.astype(BF16)
    w_out_bf = w_out.astype(BF16)
    router_w_bf = jnp.swapaxes(router_w, 1, 2).astype(BF16)
    router_b = router_b.reshape(DEPTH, N_EXPERTS, 1)
    ret_gn_gain, ln1_g, ln1_b, ln2_g, ln2_b = (
        a.reshape(DEPTH, 1, a.shape[-1]) for a in (ret_gn_gain, ln1_g, ln1_b, ln2_g, ln2_b))
    b_gu4 = b_gate_up.reshape(DEPTH, N_EXPERTS, 1, 2 * D_FF)
    b_d4 = b_down.reshape(DEPTH, N_EXPERTS, 1, D_MODEL)
    cache_k = jnp.transpose(cache_swa_k, (0, 1, 3, 4, 2)).reshape(DEPTH, DEC_BATCH, 2 * HD_SWA, WINDOW)
    cache_v = jnp.transpose(cache_swa_v, (0, 1, 3, 4, 2)).reshape(DEPTH, DEC_BATCH, 2 * HD_SWA, WINDOW)

    cos_p, sin_p = _rope_tables(jnp.arange(SEQ, dtype=jnp.int32))
    cos_s, sin_s = _rope_tables(PAST_LEN + jnp.arange(1, dtype=jnp.int32))
    intra, q_dec, k_dec, c_dec = _decay_tables(RET_CHUNK)
    ones = jnp.ones((1, 1, HD_RET), F32)
    tables_p = (cos_p, sin_p, intra, q_dec[:, :, None] * ones, k_dec[:, :, None] * ones, c_dec[:, None, None] * ones)
    intra1, q_dec1, k_dec1, c_dec1 = _decay_tables(1)
    dec_s = jnp.stack([intra1[:, 0, 0], q_dec1[:, 0], k_dec1[:, 0], c_dec1], axis=1)
    i = jnp.arange(WINDOW)[:, None]
    j = jnp.arange(2 * WINDOW)[None, :]
    bias_p = _bias_table(rel_bias, _t5_bucket(WINDOW + i - j))
    js = jnp.arange(WINDOW + 1)
    bias_s = _bias_table(rel_bias, _t5_bucket(WINDOW - js))
    bias_c, bias_n = bias_s[:, :WINDOW], bias_s[:, WINDOW:]
    kv_of_head = jnp.arange(H_SWA) // G_SWA
    head_mask = (kv_of_head[:, None] == jnp.arange(KV_SWA)[None, :]).astype(F32)

    zero_rows = jnp.zeros((SC_DISPATCH_ROWS, D_MODEL // 2), jnp.uint32)

    sp = ss = ko = vo = tail = None
    h, h_kv = _in_proj(x, w_in_bf, 0)
    for l in range(DEPTH):
        o_r, sp, o_s, tail = _mix_prompt(h, h_kv, tables_p, ret_gn_gain, swa_sinks[l], bias_p, sp, tail, l)
        hs = h[T_PROMPT:].astype(F32)
        kvs = h_kv[T_PROMPT:]
        o_r, ss = _ret_sample(hs, dec_s, cos_s, sin_s, ret_gn_gain, state_ret, o_r, ss, l)
        q8 = hs[:, 4 * RET_W:].reshape(DEC_BATCH, H_SWA, HD_SWA)
        qm = (q8[:, :, None, :] * head_mask[None, :, :, None]).reshape(DEC_BATCH, H_SWA, 2 * HD_SWA)
        kn = kvs[:, :SWA_KV_W].reshape(DEC_BATCH, 1, SWA_KV_W)
        vn = kvs[:, SWA_KV_W:].reshape(DEC_BATCH, 1, SWA_KV_W)
        osm, ko, vo = _swa_sample(qm, kn, vn, cache_k, cache_v, bias_c, bias_n, swa_sinks[l][:, None], ko, vo, l)
        os_s = jnp.sum(osm.reshape(DEC_BATCH, H_SWA, KV_SWA, HD_SWA) * head_mask[None, :, :, None], axis=2)
        o_s = lax.dynamic_update_slice(o_s, os_s.reshape(DEC_BATCH, SWA_Q_W), (T_PROMPT, 0))

        x1, xpk, top_idx, gate, rank, counts = _out_proj(o_r, o_s, x, w_out_bf, ln1_g, ln1_b, router_w_bf,
                                                         router_b, l)
        pos_km, unused, group_start, group_blocks, group_next = _route(top_idx, rank, counts)
        xs = _sc_dispatch(xpk, pos_km, unused, zero_rows)
        ys = _moe(group_start, group_blocks, group_next, xs, w_gate_up, b_gu4, w_down, b_d4, l)
        y4 = _sc_collect(ys, pos_km.reshape(N_ASSIGN)).reshape(TOP_K, T_ALL, D_MODEL // 2)
        if l < DEPTH - 1:
            x, h, h_kv = _norm_proj(x1, y4, gate, ln2_g, ln2_b, w_in_bf, l)
        else:
            yp = _ffn_norm(x1, y4, gate, ln2_g, ln2_b, l, FINAL_ROW_BLOCK, 0, T_PROMPT // FINAL_ROW_BLOCK)
            ys_out = _ffn_norm(x1, y4, gate, ln2_g, ln2_b, l, DEC_BATCH, T_PROMPT // DEC_BATCH, 1)

    yp = yp.reshape(BATCH, SEQ, D_MODEL)
    ys_out = ys_out.reshape(DEC_BATCH, 1, D_MODEL)

    def window_major(t):
        return jnp.transpose(t.reshape(DEPTH, t.shape[1], KV_SWA, HD_SWA, WINDOW), (0, 1, 4, 2, 3))

    return (yp, ys_out, sp, window_major(tail[:, :, :SWA_KV_W]), window_major(tail[:, :, SWA_KV_W:]), ss,
            window_major(ko), window_major(vo))
```

```python
import functools
import math

import jax
import jax.numpy as jnp
from jax import lax
from jax.experimental import pallas as pl
from jax.experimental.pallas import tpu as pltpu
from jax.experimental.pallas import tpu_sc as plsc

F32 = jnp.float32
BF16 = jnp.bfloat16

D_MODEL = 1024
BATCH = 8
SEQ = 2048
DEPTH = 4
DEC_BATCH = 128
PAST_LEN = 8192
H_RET = 4
HD_RET = 128
RET_CHUNK = 128
ROPE_BASE = 10000.0
H_SWA = 8
KV_SWA = 2
G_SWA = H_SWA // KV_SWA
HD_SWA = 64
WINDOW = 128
NUM_BUCKETS = 32
REL_MAX_DIST = 128
RET_W = H_RET * HD_RET
SWA_Q_W = H_SWA * HD_SWA
SWA_KV_W = KV_SWA * HD_SWA
D_IN = 4 * RET_W + SWA_Q_W + 2 * SWA_KV_W
H_MAIN_W = 4 * RET_W + SWA_Q_W
N_EXPERTS = 32
TOP_K = 4
D_FF = D_MODEL
SWIGLU_LIMIT = 7.0
SWIGLU_ALPHA = 1.702
DEEPNORM_ALPHA = (2 * DEPTH) ** 0.25
LN_EPS = 1e-5
GN_EPS = 1e-5

T_PROMPT = BATCH * SEQ
T_ALL = T_PROMPT + DEC_BATCH
ROW_BLOCK = 384
SAMPLE_BLOCK = 8
MOE_ROWS = 128
MOE_CHUNK = 1024
MOE_PIECE = 512
N_ASSIGN = T_ALL * TOP_K
MOE_BLOCKS = N_ASSIGN // MOE_ROWS + N_EXPERTS
MOE_SLOTS = MOE_BLOCKS * MOE_ROWS
N_UNUSED = MOE_SLOTS - N_ASSIGN
SC_CORES = 2
SC_WORKERS = SC_CORES * 16
SC_DISPATCH_ROWS = 64
SC_COLLECT_ROWS = 48
FINAL_ROW_BLOCK = 512
NORM_SUB_ROWS = 128
VMEM_LIMIT = 56 * 1024 * 1024


def _cparams(*sem):
    return pltpu.CompilerParams(dimension_semantics=sem, vmem_limit_bytes=VMEM_LIMIT)


def _project_in(x, w_ref, hm_ref, hkv_ref):
    h = jnp.dot(x.astype(BF16), w_ref[0], preferred_element_type=F32)
    hm_ref[...] = h[:, :H_MAIN_W].astype(BF16)
    hkv_ref[...] = h[:, H_MAIN_W:]


def _in_proj_kernel(x_ref, w_ref, hm_ref, hkv_ref):
    _project_in(x_ref[...], w_ref, hm_ref, hkv_ref)


_IN_PROJ_OUT_SPECS = [pl.BlockSpec((ROW_BLOCK, H_MAIN_W), lambda i: (i, 0)),
                      pl.BlockSpec((ROW_BLOCK, 2 * SWA_KV_W), lambda i: (i, 0))]
_IN_PROJ_OUT_SHAPES = [jax.ShapeDtypeStruct((T_ALL, H_MAIN_W), BF16),
                       jax.ShapeDtypeStruct((T_ALL, 2 * SWA_KV_W), F32)]


def _in_proj(x, w_in_bf, layer):
    return pl.pallas_call(
        _in_proj_kernel,
        grid=(T_ALL // ROW_BLOCK,),
        in_specs=[pl.BlockSpec((ROW_BLOCK, D_MODEL), lambda i: (i, 0)),
                  pl.BlockSpec((1, D_MODEL, D_IN), lambda i: (layer, 0, 0))],
        out_specs=_IN_PROJ_OUT_SPECS,
        out_shape=_IN_PROJ_OUT_SHAPES,
        compiler_params=_cparams("parallel"),
        name="in_proj",
    )(x, w_in_bf)


def _rotate(x, cos, sin_signed):
    return x * cos + pltpu.roll(x, HD_RET // 2, 1) * sin_signed


def _group_norm_gate(o, gain, g):
    mu = jnp.mean(o, -1, keepdims=True)
    var = jnp.mean(jnp.square(o - mu), -1, keepdims=True)
    return (o - mu) * lax.rsqrt(var + GN_EPS) * gain * (g * jax.nn.sigmoid(g))


def _layer_norm(y, g, b):
    mu = jnp.mean(y, -1, keepdims=True)
    var = jnp.mean(jnp.square(y - mu), -1, keepdims=True)
    return (y - mu) * lax.rsqrt(var + LN_EPS) * g + b


def _ret_prompt_stages(q_ref, k_ref, v_ref, g_ref, cos_ref, sin_ref, intra_ref, qdec_ref, kdec_ref,
                       cdec_ref, gain_ref, o_ref, s_ref):
    @pl.when(pl.program_id(1) == 0)
    def _():
        s_ref[...] = jnp.zeros_like(s_ref)

    cos = cos_ref[...]
    sin = sin_ref[...]
    nt = (((1,), (1,)), ((), ()))
    tn = (((0,), (0,)), ((), ()))
    heads = [slice(h * HD_RET, (h + 1) * HD_RET) for h in range(H_RET)]
    vs, atts, crosses = [], [], []
    for h, sl in enumerate(heads):
        q = _rotate(q_ref[:, sl].astype(F32), cos, sin)
        k = _rotate(k_ref[:, sl].astype(F32), cos, sin) * (HD_RET ** -0.5)
        v = v_ref[:, sl]
        s = s_ref[0, 0, h]
        atts.append(lax.dot_general(q.astype(BF16), k.astype(BF16), nt, preferred_element_type=F32))
        crosses.append(jnp.dot((q * qdec_ref[h]).astype(BF16), s.astype(BF16), preferred_element_type=F32))
        s_ref[0, 0, h] = s * cdec_ref[h] + lax.dot_general((k * kdec_ref[h]).astype(BF16), v, tn,
                                                          preferred_element_type=F32)
        vs.append(v)
    yield
    atts = [(att * intra_ref[h]).astype(BF16) for h, att in enumerate(atts)]
    yield
    outs = [jnp.dot(atts[h], vs[h], preferred_element_type=F32) + crosses[h] for h in range(H_RET)]
    yield
    for h, sl in enumerate(heads):
        o_ref[:, sl] = _group_norm_gate(outs[h], gain_ref[0, :, sl], g_ref[:, sl].astype(F32))


def _swa_prompt_stages(sink_ref, q_ref, kvc_ref, kvp_ref, bias_ref, o_ref, tail_ref, lg_scr, p_scr):
    c = pl.program_id(1)

    @pl.when(c == pl.num_programs(1) - 1)
    def _():
        tail_ref[0, 0] = kvc_ref[...].T

    kv = jnp.concatenate([kvp_ref[...], kvc_ref[...]], axis=0)
    k_t = kv[:, :SWA_KV_W].T
    k_t_swapped = pltpu.roll(k_t, HD_SWA, 0)
    v = kv[:, SWA_KV_W:]
    v_swapped = pltpu.roll(v, HD_SWA, 1)
    k_row = lax.broadcasted_iota(jnp.int32, k_t.shape, 0)
    v_lane = lax.broadcasted_iota(jnp.int32, v.shape, 1)
    ones = jnp.ones(v.shape, BF16)
    k_pad, v_pad = {}, {}
    for kvh in range(KV_SWA):
        for par in range(2):
            k_src = k_t if kvh == par else k_t_swapped
            v_src = v if kvh == par else v_swapped
            k_pad[kvh, par] = jnp.where((k_row >= par * HD_SWA) & (k_row < (par + 1) * HD_SWA), k_src, 0.0).astype(BF16)
            v_pad[kvh, par] = jnp.where((v_lane >= par * HD_SWA) & (v_lane < (par + 1) * HD_SWA), v_src, 0.0).astype(BF16)
    for hd in range(H_SWA):
        pair, par, kvh = hd // 2, hd % 2, hd // G_SWA
        q2 = (q_ref[:, pair * 2 * HD_SWA:(pair + 1) * 2 * HD_SWA] * (HD_SWA ** -0.5)).astype(BF16)
        lg_scr[hd] = jnp.dot(q2, k_pad[kvh, par], preferred_element_type=F32) + bias_ref[0, hd]
    yield
    sink_terms = []
    for hd in range(H_SWA):
        logits = lg_scr[hd]
        sink = sink_ref[hd]
        m = jnp.maximum(jnp.max(logits, -1, keepdims=True), sink)
        p_scr[hd] = jnp.exp(logits - m).astype(BF16)
        sink_terms.append(jnp.exp(sink - m))
    yield
    for pair in range(H_SWA // 2):
        acc = None
        for par in range(2):
            hd = 2 * pair + par
            p = p_scr[hd]
            den = jnp.dot(p, ones, preferred_element_type=F32) + sink_terms[hd]
            part = jnp.dot(p, v_pad[hd // G_SWA, par], preferred_element_type=F32) / den
            acc = part if acc is None else acc + part
        o_ref[:, pair * 2 * HD_SWA:(pair + 1) * 2 * HD_SWA] = acc


def _mix_prompt_kernel(*refs):
    n_ret_in, n_swa_in = 11, 5
    ret_in = refs[:n_ret_in]
    swa_in = refs[n_ret_in:n_ret_in + n_swa_in]
    o_r_ref, s_ref, o_s_ref, tail_ref, lg_scr, p_scr = refs[-6:]
    groups = [_ret_prompt_stages(*ret_in, o_r_ref, s_ref),
              _swa_prompt_stages(*swa_in, o_s_ref, tail_ref, lg_scr, p_scr)]
    while groups:
        for g in list(groups):
            if next(g, groups) is groups:
                groups.remove(g)


def _mix_prompt(h, h_kv, tables, gain, sinks_l, bias, s_prev, tail_prev, layer):
    cos, sin, intra, qdec, kdec, cdec = tables
    nb = SEQ // WINDOW
    qcol = 4 * RET_W // SWA_Q_W
    kvcol = 0

    def col(j):
        return pl.BlockSpec((RET_CHUNK, RET_W), lambda b, c: (b * nb + c, j))

    def whole(a):
        return pl.BlockSpec(a.shape, lambda b, c: (0,) * a.ndim)

    in_specs = [col(0), col(1), col(2), col(3),
                pl.BlockSpec((RET_CHUNK, HD_RET), lambda b, c: (c, 0)),
                pl.BlockSpec((RET_CHUNK, HD_RET), lambda b, c: (c, 0)),
                whole(intra), whole(qdec), whole(kdec), whole(cdec),
                pl.BlockSpec((1, 1, RET_W), lambda b, c: (layer, 0, 0)),
                pl.BlockSpec(memory_space=pltpu.SMEM),
                pl.BlockSpec((WINDOW, SWA_Q_W), lambda b, c: (b * nb + c, qcol)),
                pl.BlockSpec((WINDOW, 2 * SWA_KV_W), lambda b, c: (b * nb + c, kvcol)),
                pl.BlockSpec((WINDOW, 2 * SWA_KV_W), lambda b, c: (b * nb + jnp.maximum(c - 1, 0), kvcol)),
                pl.BlockSpec((1, H_SWA, WINDOW, 2 * WINDOW), lambda b, c: (jnp.minimum(c, 1), 0, 0, 0))]
    args = [h, h, h, h, cos, sin, intra, qdec, kdec, cdec, gain, sinks_l, h, h_kv, h_kv, bias]
    aliases = {}
    if s_prev is not None:
        in_specs += [pl.BlockSpec(memory_space=pl.ANY), pl.BlockSpec(memory_space=pl.ANY)]
        args += [s_prev, tail_prev]
        aliases = {16: 1, 17: 3}

    return pl.pallas_call(
        _mix_prompt_kernel,
        grid=(BATCH, nb),
        in_specs=in_specs,
        scratch_shapes=[pltpu.VMEM((H_SWA, WINDOW, 2 * WINDOW), F32), pltpu.VMEM((H_SWA, WINDOW, 2 * WINDOW), BF16)],
        out_specs=[pl.BlockSpec((RET_CHUNK, RET_W), lambda b, c: (b * nb + c, 0)),
                   pl.BlockSpec((1, 1, H_RET, HD_RET, HD_RET), lambda b, c: (layer, b, 0, 0, 0)),
                   pl.BlockSpec((WINDOW, SWA_Q_W), lambda b, c: (b * nb + c, 0)),
                   pl.BlockSpec((1, 1, 2 * SWA_KV_W, WINDOW), lambda b, c: (layer, b, 0, 0))],
        out_shape=[jax.ShapeDtypeStruct((T_ALL, RET_W), F32),
                   jax.ShapeDtypeStruct((DEPTH, BATCH, H_RET, HD_RET, HD_RET), F32),
                   jax.ShapeDtypeStruct((T_ALL, SWA_Q_W), F32),
                   jax.ShapeDtypeStruct((DEPTH, BATCH, 2 * SWA_KV_W, WINDOW), F32)],
        input_output_aliases=aliases,
        compiler_params=_cparams("parallel", "arbitrary"),
        name="mix_prompt",
    )(*args)


def _ret_sample_kernel(dec_ref, q_ref, k_ref, v_ref, g_ref, cos_ref, sin_ref, gain_ref, st_ref, o_ref, so_ref):
    cos = cos_ref[...]
    sin = sin_ref[...]
    for h in range(H_RET):
        sl = slice(h * HD_RET, (h + 1) * HD_RET)
        intra, qdec, kdec, cdec = dec_ref[h, 0], dec_ref[h, 1], dec_ref[h, 2], dec_ref[h, 3]
        q = _rotate(q_ref[:, sl], cos, sin)
        k = _rotate(k_ref[:, sl], cos, sin) * (HD_RET ** -0.5)
        v = v_ref[:, sl]
        att = jnp.sum(q * k, -1, keepdims=True) * intra
        q_col = (q * qdec).T
        k_col = (k * kdec).T
        rows = []
        for b in range(SAMPLE_BLOCK):
            s = st_ref[0, b, h]
            v_row = v[b:b + 1, :]
            rows.append(att[b:b + 1, :] * v_row + jnp.sum(q_col[:, b:b + 1] * s, axis=0, keepdims=True))
            so_ref[0, b, h] = s * cdec + k_col[:, b:b + 1] * v_row
        o = jnp.concatenate(rows, axis=0)
        o_ref[:, sl] = _group_norm_gate(o, gain_ref[0, :, sl], g_ref[:, sl])


def _ret_sample(h, dec_s, cos_s, sin_s, gain, state_ret, o_r, st_prev, layer):
    base = T_PROMPT // SAMPLE_BLOCK

    def col(j):
        return pl.BlockSpec((SAMPLE_BLOCK, RET_W), lambda i: (i, j))

    in_specs = [pl.BlockSpec(memory_space=pltpu.SMEM),
                col(0), col(1), col(2), col(3),
                pl.BlockSpec((1, HD_RET), lambda i: (0, 0)),
                pl.BlockSpec((1, HD_RET), lambda i: (0, 0)),
                pl.BlockSpec((1, 1, RET_W), lambda i: (layer, 0, 0)),
                pl.BlockSpec((1, SAMPLE_BLOCK, H_RET, HD_RET, HD_RET), lambda i: (layer, i, 0, 0, 0)),
                pl.BlockSpec(memory_space=pl.ANY)]
    args = [dec_s, h, h, h, h, cos_s, sin_s, gain, state_ret, o_r]
    aliases = {9: 0}
    if st_prev is not None:
        in_specs.append(pl.BlockSpec(memory_space=pl.ANY))
        args.append(st_prev)
        aliases[10] = 1

    def body(*refs):
        _ret_sample_kernel(*refs[:9], *refs[-2:])

    return pl.pallas_call(
        body,
        grid=(DEC_BATCH // SAMPLE_BLOCK,),
        in_specs=in_specs,
        out_specs=[pl.BlockSpec((SAMPLE_BLOCK, RET_W), lambda i: (base + i, 0)),
                   pl.BlockSpec((1, SAMPLE_BLOCK, H_RET, HD_RET, HD_RET), lambda i: (layer, i, 0, 0, 0))],
        out_shape=[jax.ShapeDtypeStruct((T_ALL, RET_W), F32),
                   jax.ShapeDtypeStruct((DEPTH, DEC_BATCH, H_RET, HD_RET, HD_RET), F32)],
        input_output_aliases=aliases,
        compiler_params=_cparams("parallel"),
        name="ret_sample",
    )(*args)


def _swa_sample_kernel(qm_ref, kn_ref, vn_ref, knc_ref, vnc_ref, kc_ref, vc_ref, bias_ref, biasn_ref, sink_ref,
                       o_ref, ko_ref, vo_ref):
    qm = qm_ref[...]
    kn = kn_ref[...]
    vn = vn_ref[...]
    kc = kc_ref[0]
    vc = vc_ref[0]
    scale = HD_SWA ** -0.5
    logits = jnp.einsum('bhd,bdj->bhj', qm.astype(BF16), kc.astype(BF16),
                        preferred_element_type=F32) * scale + bias_ref[...][None]
    j = lax.broadcasted_iota(jnp.int32, logits.shape, 2)
    logits = jnp.where(j >= 1, logits, -jnp.inf)
    ln = jnp.sum(qm * kn, -1, keepdims=True) * scale + biasn_ref[...][None]
    sink = sink_ref[...][None]
    m = jnp.maximum(jnp.maximum(jnp.max(logits, -1, keepdims=True), ln), sink)
    p = jnp.exp(logits - m)
    pn = jnp.exp(ln - m)
    denom = jnp.sum(p, -1, keepdims=True) + pn + jnp.exp(sink - m)
    w = p / denom
    o_ref[...] = (jnp.einsum('bhj,bdj->bhd', w.astype(BF16), vc.astype(BF16), preferred_element_type=F32)
                  + (pn / denom) * vn)
    last = lax.broadcasted_iota(jnp.int32, kc.shape, 2) == WINDOW - 1
    ko_ref[0] = jnp.where(last, knc_ref[...], pltpu.roll(kc, WINDOW - 1, 2))
    vo_ref[0] = jnp.where(last, vnc_ref[...], pltpu.roll(vc, WINDOW - 1, 2))


def _swa_sample(qm, kn, vn, cache_k, cache_v, bias_c, bias_n, sink_col, ko_prev, vo_prev, layer):
    sb = SAMPLE_BLOCK
    cache_spec = pl.BlockSpec((1, sb, 2 * HD_SWA, WINDOW), lambda i: (layer, i, 0, 0))
    row_spec = pl.BlockSpec((sb, 1, 2 * HD_SWA), lambda i: (i, 0, 0))
    col_spec = pl.BlockSpec((sb, 2 * HD_SWA, 1), lambda i: (i, 0, 0))
    in_specs = [pl.BlockSpec((sb, H_SWA, 2 * HD_SWA), lambda i: (i, 0, 0)),
                row_spec, row_spec, col_spec, col_spec,
                cache_spec, cache_spec,
                pl.BlockSpec((H_SWA, WINDOW), lambda i: (0, 0)),
                pl.BlockSpec((H_SWA, 1), lambda i: (0, 0)),
                pl.BlockSpec((H_SWA, 1), lambda i: (0, 0))]
    args = [qm, kn, vn, kn.reshape(DEC_BATCH, 2 * HD_SWA, 1), vn.reshape(DEC_BATCH, 2 * HD_SWA, 1),
            cache_k, cache_v, bias_c, bias_n, sink_col]
    aliases = {}
    if ko_prev is not None:
        in_specs += [pl.BlockSpec(memory_space=pl.ANY), pl.BlockSpec(memory_space=pl.ANY)]
        args += [ko_prev, vo_prev]
        aliases = {10: 1, 11: 2}

    def body(*refs):
        _swa_sample_kernel(*refs[:10], *refs[-3:])

    cshape = jax.ShapeDtypeStruct((DEPTH, DEC_BATCH, 2 * HD_SWA, WINDOW), F32)
    return pl.pallas_call(
        body,
        grid=(DEC_BATCH // sb,),
        in_specs=in_specs,
        out_specs=[pl.BlockSpec((sb, H_SWA, 2 * HD_SWA), lambda i: (i, 0, 0)), cache_spec, cache_spec],
        out_shape=[jax.ShapeDtypeStruct((DEC_BATCH, H_SWA, 2 * HD_SWA), F32), cshape, cshape],
        input_output_aliases=aliases,
        compiler_params=_cparams("parallel"),
        name="swa_sample",
    )(*args)


def _pack_bf16_pairs(x):
    w = x.shape[1] // 2
    lo = lax.bitcast_convert_type(x[:, :w].astype(BF16).astype(F32), jnp.uint32) >> 16
    hi = lax.bitcast_convert_type(x[:, w:].astype(BF16).astype(F32), jnp.uint32)
    return lo | hi


def _unpack_bf16_pairs(p):
    lo = lax.bitcast_convert_type(p << 16, F32).astype(BF16)
    hi = lax.bitcast_convert_type(p & jnp.uint32(0xFFFF0000), F32).astype(BF16)
    return lo, hi


def _out_proj_kernel(or_ref, os_ref, x_ref, wr_ref, ws_ref, g_ref, b_ref, rw_ref, rb_ref,
                     x1_ref, xpk_ref, idx_ref, gate_ref, rank_ref, cnt_ref):
    mix = (jnp.dot(or_ref[...].astype(BF16), wr_ref[0], preferred_element_type=F32)
           + jnp.dot(os_ref[...].astype(BF16), ws_ref[0], preferred_element_type=F32))
    x1 = _layer_norm(DEEPNORM_ALPHA * x_ref[...] + mix, g_ref[0], b_ref[0])
    x1_ref[...] = x1
    xpk_ref[...] = _pack_bf16_pairs(x1)
    nt = (((1,), (1,)), ((), ()))
    logits = lax.dot_general(rw_ref[0], x1.astype(BF16), nt, preferred_element_type=F32) + rb_ref[0]
    rows = logits.shape[1]
    expert = lax.broadcasted_iota(jnp.int32, logits.shape, 0)
    vals, hits = [], []
    for kk in range(TOP_K):
        m = jnp.max(logits, 0, keepdims=True)
        idx = jnp.min(jnp.where(logits == m, expert, N_EXPERTS), 0, keepdims=True)
        vals.append(m)
        hits.append(expert == idx)
        idx_ref[kk:kk + 1, :] = idx
        logits = jnp.where(expert == idx, -jnp.inf, logits)
    es = [jnp.exp(v - vals[0]) for v in vals]
    tot = es[0] + es[1] + es[2] + es[3]
    gate_ref[...] = jnp.concatenate([e / tot for e in es] + [jnp.zeros((8 - TOP_K, rows), F32)], axis=0)
    chosen = sum(h.astype(F32) for h in hits)
    earlier = (lax.broadcasted_iota(jnp.int32, (rows, rows), 0)
               < lax.broadcasted_iota(jnp.int32, (rows, rows), 1)).astype(BF16)
    before = jnp.dot(chosen.astype(BF16), earlier, preferred_element_type=F32)
    for kk in range(TOP_K):
        rank_ref[kk:kk + 1, :] = jnp.sum(jnp.where(hits[kk], before, 0.0), 0, keepdims=True).astype(jnp.int32)
    cnt_ref[0] = jnp.sum(chosen, axis=1, keepdims=True).astype(jnp.int32)


def _out_proj(o_r, o_s, x, w_out_bf, ln_g, ln_b, router_w_bf, router_b, layer):
    def rows(w):
        return pl.BlockSpec((ROW_BLOCK, w), lambda i: (i, 0))

    def vec(w):
        return pl.BlockSpec((1, 1, w), lambda i: (layer, 0, 0))

    def cols(h):
        return pl.BlockSpec((h, ROW_BLOCK), lambda i: (0, i))

    return pl.pallas_call(
        _out_proj_kernel,
        grid=(T_ALL // ROW_BLOCK,),
        in_specs=[rows(RET_W), rows(SWA_Q_W), rows(D_MODEL),
                  pl.BlockSpec((1, RET_W, D_MODEL), lambda i: (layer, 0, 0)),
                  pl.BlockSpec((1, SWA_Q_W, D_MODEL), lambda i: (layer, 1, 0)),
                  vec(D_MODEL), vec(D_MODEL),
                  pl.BlockSpec((1, N_EXPERTS, D_MODEL), lambda i: (layer, 0, 0)),
                  pl.BlockSpec((1, N_EXPERTS, 1), lambda i: (layer, 0, 0))],
        out_specs=[rows(D_MODEL), rows(D_MODEL // 2), cols(TOP_K), cols(8), cols(TOP_K),
                   pl.BlockSpec((1, N_EXPERTS, 1), lambda i: (i, 0, 0))],
        out_shape=[jax.ShapeDtypeStruct((T_ALL, D_MODEL), F32),
                   jax.ShapeDtypeStruct((T_ALL, D_MODEL // 2), jnp.uint32),
                   jax.ShapeDtypeStruct((TOP_K, T_ALL), jnp.int32),
                   jax.ShapeDtypeStruct((8, T_ALL), F32),
                   jax.ShapeDtypeStruct((TOP_K, T_ALL), jnp.int32),
                   jax.ShapeDtypeStruct((T_ALL // ROW_BLOCK, N_EXPERTS, 1), jnp.int32)],
        compiler_params=_cparams("parallel"),
        name="out_proj",
    )(o_r, o_s, x, w_out_bf, w_out_bf, ln_g, ln_b, router_w_bf, router_b)


def _moe_kernel(start_ref, nblk_ref, next_ref, xs_hbm, wgu_ref, bgu_ref, wd_ref, bd_ref, ys_hbm,
                wgu_bf, wd_bf, x_buf, y_buf, in_sems, out_sems, pending, ready):
    e = pl.program_id(0)
    per_chunk = MOE_CHUNK // MOE_ROWS
    tail_sizes = [t * MOE_ROWS for t in range(1, per_chunk)]
    nblk = nblk_ref[e]
    n_full = nblk // per_chunk
    tail_blocks = nblk % per_chunk
    base = start_ref[e]

    def in_copy(row0, rows, slot):
        src = xs_hbm.at[pl.ds(pl.multiple_of(row0, MOE_ROWS), rows)]
        return pltpu.make_async_copy(src, x_buf.at[slot, pl.ds(0, rows)], in_sems.at[slot])

    def out_copy(row0, rows, slot):
        dst = ys_hbm.at[pl.ds(pl.multiple_of(row0, MOE_ROWS), rows)]
        return pltpu.make_async_copy(y_buf.at[slot, pl.ds(0, rows)], dst, out_sems.at[slot])

    def start_first_chunk(expert, slot):
        @pl.when(nblk_ref[expert] >= per_chunk)
        def _():
            in_copy(start_ref[expert], MOE_CHUNK, slot).start()

        for rows in tail_sizes:
            @pl.when(nblk_ref[expert] == rows // MOE_ROWS)
            def _():
                in_copy(start_ref[expert], rows, slot).start()

    def prefetch_next_expert(slot):
        nxt = next_ref[e]

        @pl.when(nxt < N_EXPERTS)
        def _():
            start_first_chunk(nxt, slot)
            ready[0] = slot

    def drain(slot):
        for rows in [MOE_CHUNK] + tail_sizes:
            @pl.when(pending[slot] == rows)
            def _():
                out_copy(0, rows, slot).wait()
        pending[slot] = 0

    def compute(rows, slot):
        for r0 in range(0, rows, MOE_PIECE):
            piece = pl.ds(r0, min(MOE_PIECE, rows - r0))
            x = jnp.concatenate(_unpack_bf16_pairs(x_buf[slot, piece, :]), axis=1)
            gu = jnp.dot(x, wgu_bf[...], preferred_element_type=F32) + bgu_ref[0, 0]
            glu = jnp.minimum(gu[:, :D_FF], SWIGLU_LIMIT)
            lin = jnp.clip(gu[:, D_FF:], -SWIGLU_LIMIT, SWIGLU_LIMIT)
            act = glu * jax.nn.sigmoid(SWIGLU_ALPHA * glu) * (lin + 1.0)
            y = jnp.dot(act.astype(BF16), wd_bf[...], preferred_element_type=F32) + bd_ref[0, 0]
            y_buf[slot, piece, :] = _pack_bf16_pairs(y)

    def run_chunk(row0, rows, slot):
        in_copy(row0, rows, slot).wait()
        drain(slot)
        compute(rows, slot)
        out_copy(row0, rows, slot).start()
        pending[slot] = rows

    @pl.when(e == 0)
    def _():
        pending[0] = 0
        pending[1] = 0
        ready[0] = -1

    @pl.when(nblk > 0)
    def _():
        @pl.when(ready[0] < 0)
        def _():
            start_first_chunk(e, 0)
            ready[0] = 0

        first_slot = ready[0]
        wgu_bf[...] = wgu_ref[0, 0].astype(BF16)
        wd_bf[...] = wd_ref[0, 0].astype(BF16)

        def body(c, carry):
            slot = (first_slot + c) & 1
            row0 = pl.multiple_of(base + c * MOE_CHUNK, MOE_ROWS)

            @pl.when(c + 1 < n_full)
            def _():
                in_copy(row0 + MOE_CHUNK, MOE_CHUNK, 1 - slot).start()

            for rows in tail_sizes:
                @pl.when(jnp.logical_and(c + 1 == n_full, tail_blocks == rows // MOE_ROWS))
                def _():
                    in_copy(row0 + MOE_CHUNK, rows, 1 - slot).start()

            @pl.when(jnp.logical_and(c + 1 == n_full, tail_blocks == 0))
            def _():
                prefetch_next_expert(1 - slot)

            run_chunk(row0, MOE_CHUNK, slot)
            return carry

        lax.fori_loop(0, n_full, body, 0)

        for rows in tail_sizes:
            @pl.when(tail_blocks == rows // MOE_ROWS)
            def _():
                slot = (first_slot + n_full) & 1
                prefetch_next_expert(1 - slot)
                run_chunk(pl.multiple_of(base + n_full * MOE_CHUNK, MOE_ROWS), rows, slot)

    @pl.when(e == pl.num_programs(0) - 1)
    def _():
        drain(0)
        drain(1)


def _moe(group_start, group_blocks, group_next, xs, w_gate_up, b_gate_up, w_down, b_down, layer):
    def wsel(e, st, nb, nx):
        return (layer, e, 0, 0)

    return pl.pallas_call(
        _moe_kernel,
        grid_spec=pltpu.PrefetchScalarGridSpec(
            num_scalar_prefetch=3,
            grid=(N_EXPERTS,),
            in_specs=[pl.BlockSpec(memory_space=pl.ANY),
                      pl.BlockSpec((1, 1, D_MODEL, 2 * D_FF), wsel),
                      pl.BlockSpec((1, 1, 1, 2 * D_FF), wsel),
                      pl.BlockSpec((1, 1, D_FF, D_MODEL), wsel),
                      pl.BlockSpec((1, 1, 1, D_MODEL), wsel)],
            out_specs=pl.BlockSpec(memory_space=pl.ANY),
            scratch_shapes=[pltpu.VMEM((D_MODEL, 2 * D_FF), BF16), pltpu.VMEM((D_FF, D_MODEL), BF16),
                            pltpu.VMEM((2, MOE_CHUNK, D_MODEL // 2), jnp.uint32),
                            pltpu.VMEM((2, MOE_CHUNK, D_MODEL // 2), jnp.uint32),
                            pltpu.SemaphoreType.DMA((2,)), pltpu.SemaphoreType.DMA((2,)),
                            pltpu.SMEM((2,), jnp.int32), pltpu.SMEM((1,), jnp.int32)]),
        out_shape=jax.ShapeDtypeStruct((MOE_SLOTS, D_MODEL // 2), jnp.uint32),
        compiler_params=_cparams("arbitrary"),
        name="moe_experts",
    )(group_start, group_blocks, group_next, xs, w_gate_up, b_gate_up, w_down, b_down)


def _route(idx, rank, counts):
    experts = jnp.arange(N_EXPERTS, dtype=jnp.int32)
    counts = counts.reshape(T_ALL // ROW_BLOCK, N_EXPERTS)
    before_block = jnp.cumsum(counts, axis=0) - counts
    total = jnp.sum(counts, axis=0)
    padded = (total + MOE_ROWS - 1) // MOE_ROWS * MOE_ROWS
    pad_end = jnp.cumsum(padded)
    pad_start = pad_end - padded
    base = jnp.repeat((pad_start[None, :] + before_block).T, ROW_BLOCK, axis=1)
    pos = jnp.sum(jnp.where(idx[:, None, :] == experts[None, :, None], base[None], 0), axis=1) + rank
    gap = padded - total
    gap_end = jnp.concatenate([jnp.cumsum(gap), jnp.full((1,), N_UNUSED, jnp.int32)])
    gap_start = jnp.concatenate([jnp.zeros((1,), jnp.int32), gap_end[:-1]])
    first_free = jnp.concatenate([pad_start + total, pad_end[-1:]])
    j = jnp.arange(N_UNUSED, dtype=jnp.int32)
    region = (j[:, None] >= gap_start[None, :]) & (j[:, None] < gap_end[None, :])
    unused = jnp.sum(jnp.where(region, first_free[None, :] + j[:, None] - gap_start[None, :], 0), axis=-1)
    nonempty = jnp.where(total > 0, experts, N_EXPERTS)
    following = jnp.concatenate([lax.cummin(nonempty, reverse=True)[1:], jnp.full((1,), N_EXPERTS, jnp.int32)])
    return (pos.astype(jnp.int32), unused.astype(jnp.int32),
            pad_start.astype(jnp.int32), (padded // MOE_ROWS).astype(jnp.int32), following.astype(jnp.int32))


def _sc_worker():
    return lax.axis_index("s") * SC_CORES + lax.axis_index("c")


def _sc_dispatch(xpk, pos_km, unused, zero_rows):
    ch = SC_DISPATCH_ROWS
    n_chunks = T_ALL // ch
    n_zero_chunks = N_UNUSED // ch
    width = xpk.shape[1]
    mesh = plsc.VectorSubcoreMesh(core_axis_name="c", subcore_axis_name="s")

    @functools.partial(
        pl.kernel, mesh=mesh,
        out_type=jax.ShapeDtypeStruct((MOE_SLOTS, width), xpk.dtype),
        scratch_types=[pltpu.VMEM((2, ch, width), xpk.dtype), pltpu.VMEM((2, TOP_K, ch), jnp.int32),
                       pltpu.SemaphoreType.DMA((2,)), pltpu.SemaphoreType.DMA((TOP_K,))],
        name="sc_dispatch")
    def run(x_hbm, pos_hbm, unused_hbm, zero_hbm, xs_hbm, rows_v, idx_v, load_sems, store_sems):
        wid = _sc_worker()
        pltpu.sync_copy(zero_hbm, rows_v.at[0])
        for r in range(n_zero_chunks // SC_WORKERS):
            start = pl.multiple_of((wid + r * SC_WORKERS) * ch, ch)
            pltpu.sync_copy(unused_hbm.at[pl.ds(start, ch)], idx_v.at[0, 0])
            pltpu.sync_copy(rows_v.at[0], xs_hbm.at[idx_v.at[0, 0]])

        def first_row(r):
            return pl.multiple_of((wid + r * SC_WORKERS) * ch, ch)

        def rows_copy(r, slot):
            return pltpu.make_async_copy(x_hbm.at[pl.ds(first_row(r), ch)], rows_v.at[slot], load_sems.at[slot])

        def start_load(r, slot):
            for k in range(TOP_K):
                pltpu.sync_copy(pos_hbm.at[k, pl.ds(first_row(r), ch)], idx_v.at[slot, k])
            rows_copy(r, slot).start()

        def has_chunk(r):
            return wid + r * SC_WORKERS < n_chunks

        start_load(0, 0)

        def body(r, carry):
            slot = r & 1

            @pl.when(has_chunk(r + 1))
            def _():
                start_load(r + 1, 1 - slot)

            @pl.when(has_chunk(r))
            def _():
                rows_copy(r, slot).wait()
                copies = [pltpu.async_copy(rows_v.at[slot], xs_hbm.at[idx_v.at[slot, k]], store_sems.at[k])
                          for k in range(TOP_K)]
                for cp in copies:
                    cp.wait()
            return carry

        lax.fori_loop(0, pl.cdiv(n_chunks, SC_WORKERS), body, 0)

    return run(xpk, pos_km, unused, zero_rows)


def _sc_collect(ys, pos_flat):
    ch = SC_COLLECT_ROWS
    per_worker = N_ASSIGN // SC_WORKERS
    mesh = plsc.VectorSubcoreMesh(core_axis_name="c", subcore_axis_name="s")

    @functools.partial(
        pl.kernel, mesh=mesh,
        out_type=jax.ShapeDtypeStruct((N_ASSIGN, ys.shape[1]), ys.dtype),
        scratch_types=[pltpu.VMEM((2, ch, ys.shape[1]), ys.dtype), pltpu.VMEM((2, ch), jnp.int32),
                       pltpu.SemaphoreType.DMA((2,))],
        name="sc_collect")
    def run(ys_hbm, pos_hbm, y4_hbm, rows_v, idx_v, sems):
        wid = _sc_worker()
        n_chunks = per_worker // ch

        def first_row(r):
            return pl.multiple_of(wid * per_worker + r * ch, 8)

        def gather(slot):
            return pltpu.make_async_copy(ys_hbm.at[idx_v.at[slot]], rows_v.at[slot], sems.at[slot])

        def start_gather(r, slot):
            pltpu.sync_copy(pos_hbm.at[pl.ds(first_row(r), ch)], idx_v.at[slot])
            gather(slot).start()

        start_gather(0, 0)

        def body(r, carry):
            slot = r & 1

            @pl.when(r + 1 < n_chunks)
            def _():
                start_gather(r + 1, 1 - slot)

            gather(slot).wait()
            pltpu.sync_copy(rows_v.at[slot], y4_hbm.at[pl.ds(first_row(r), ch)])
            return carry

        lax.fori_loop(0, n_chunks, body, 0)

    return run(ys, pos_flat)


def _combine_norm(x_ref, y_refs, gate_ref, g_ref, b_ref, rows=slice(None)):
    gate = gate_ref[:, rows].T
    lo = hi = None
    for k, y_ref in enumerate(y_refs):
        p = y_ref[0, rows, :]
        g = gate[:, k:k + 1]
        lo_k = g * lax.bitcast_convert_type(p << 16, F32)
        hi_k = g * lax.bitcast_convert_type(p & jnp.uint32(0xFFFF0000), F32)
        lo = lo_k if lo is None else lo + lo_k
        hi = hi_k if hi is None else hi + hi_k
    ffn = jnp.concatenate([lo, hi], axis=1)
    return _layer_norm(DEEPNORM_ALPHA * x_ref[rows, :] + ffn, g_ref[0], b_ref[0])


def _ffn_norm_kernel(x_ref, y0_ref, y1_ref, y2_ref, y3_ref, gate_ref, g_ref, b_ref, o_ref):
    o_ref[...] = _combine_norm(x_ref, (y0_ref, y1_ref, y2_ref, y3_ref), gate_ref, g_ref, b_ref)


def _norm_proj_kernel(x_ref, y0_ref, y1_ref, y2_ref, y3_ref, gate_ref, g_ref, b_ref, w_ref,
                      x2_ref, hm_ref, hkv_ref):
    for r in range(ROW_BLOCK // NORM_SUB_ROWS):
        rows = pl.ds(r * NORM_SUB_ROWS, NORM_SUB_ROWS)
        x2 = _combine_norm(x_ref, (y0_ref, y1_ref, y2_ref, y3_ref), gate_ref, g_ref, b_ref, rows)
        x2_ref[rows, :] = x2
        _project_in(x2, w_ref, hm_ref.at[rows, :], hkv_ref.at[rows, :])


def _norm_proj(x1, y4, gate, ln_g, ln_b, w_in_bf, layer):
    rows = pl.BlockSpec((ROW_BLOCK, D_MODEL), lambda i: (i, 0))

    def plane(k):
        return pl.BlockSpec((1, ROW_BLOCK, D_MODEL // 2), lambda i: (k, i, 0))

    vec = pl.BlockSpec((1, 1, D_MODEL), lambda i: (layer, 0, 0))
    return pl.pallas_call(
        _norm_proj_kernel,
        grid=(T_ALL // ROW_BLOCK,),
        in_specs=[rows, plane(0), plane(1), plane(2), plane(3),
                  pl.BlockSpec((8, ROW_BLOCK), lambda i: (0, i)), vec, vec,
                  pl.BlockSpec((1, D_MODEL, D_IN), lambda i: (layer + 1, 0, 0))],
        out_specs=[rows] + _IN_PROJ_OUT_SPECS,
        out_shape=[jax.ShapeDtypeStruct((T_ALL, D_MODEL), F32)] + _IN_PROJ_OUT_SHAPES,
        compiler_params=_cparams("parallel"),
        name="norm_proj",
    )(x1, y4, y4, y4, y4, gate, ln_g, ln_b, w_in_bf)


def _ffn_norm(x1, y4, gate, ln_g, ln_b, layer, row_block, first_block, n_blocks):
    def rows(w):
        return pl.BlockSpec((row_block, w), lambda i: (first_block + i, 0))

    def plane(k):
        return pl.BlockSpec((1, row_block, D_MODEL // 2), lambda i: (k, first_block + i, 0))

    vec = pl.BlockSpec((1, 1, D_MODEL), lambda i: (layer, 0, 0))
    return pl.pallas_call(
        _ffn_norm_kernel,
        grid=(n_blocks,),
        in_specs=[rows(D_MODEL), plane(0), plane(1), plane(2), plane(3),
                  pl.BlockSpec((8, row_block), lambda i: (0, first_block + i)), vec, vec],
        out_specs=pl.BlockSpec((row_block, D_MODEL), lambda i: (i, 0)),
        out_shape=jax.ShapeDtypeStruct((n_blocks * row_block, D_MODEL), F32),
        compiler_params=_cparams("parallel"),
        name="ffn_norm",
    )(x1, y4, y4, y4, y4, gate, ln_g, ln_b)


def _rope_tables(pos):
    half = HD_RET // 2
    inv = ROPE_BASE ** (-jnp.arange(half, dtype=F32) / half)
    ang = pos.astype(F32)[:, None] * inv[None, :]
    cos, sin = jnp.cos(ang), jnp.sin(ang)
    return jnp.concatenate([cos, cos], -1), jnp.concatenate([-sin, sin], -1)


def _decay_tables(c):
    log_g = jnp.log1p(-jnp.exp2(-5.0 - jnp.arange(H_RET, dtype=F32)))
    idx = jnp.arange(c, dtype=F32)
    diff = idx[:, None] - idx[None, :]
    intra = jnp.where(diff[None] >= 0, jnp.exp(jnp.maximum(diff, 0.0)[None] * log_g[:, None, None]), 0.0)
    q_dec = jnp.exp((idx + 1.0)[None] * log_g[:, None])
    k_dec = jnp.exp((c - 1.0 - idx)[None] * log_g[:, None])
    c_dec = jnp.exp(c * log_g)
    return intra, q_dec, k_dec, c_dec


def _t5_bucket(dist):
    n = jnp.maximum(dist, 0)
    max_exact = NUM_BUCKETS // 2
    nf = jnp.maximum(n, 1).astype(F32)
    large = max_exact + (jnp.log(nf / max_exact) / math.log(REL_MAX_DIST / max_exact)
                         * (NUM_BUCKETS - max_exact)).astype(jnp.int32)
    large = jnp.minimum(large, NUM_BUCKETS - 1)
    return jnp.where(n < max_exact, n, large)


def _bias_table(rel_bias, bucket):
    hit = bucket[..., None] == jnp.arange(NUM_BUCKETS)
    per_head = rel_bias.astype(F32).T.reshape((H_SWA,) + (1,) * bucket.ndim + (NUM_BUCKETS,))
    return jnp.sum(jnp.where(hit[None], per_head, 0.0), axis=-1)


def kernel(x_prompt, x_sample, state_ret, cache_swa_k, cache_swa_v, w_in, w_out, ret_gn_gain, swa_sinks,
           rel_bias, ln1_g, ln1_b, ln2_g, ln2_b, router_w, router_b, w_gate_up, b_gate_up, w_down, b_down):
    x = jnp.concatenate([x_prompt.reshape(T_PROMPT, D_MODEL), x_sample.reshape(DEC_BATCH, D_MODEL)], axis=0)
    w_in_bf = w_in.astype(BF16)
    w_out_bf = w_out.astype(BF16)
    router_w_bf = jnp.swapaxes(router_w, 1, 2).astype(BF16)
    router_b = router_b.reshape(DEPTH, N_EXPERTS, 1)
    ret_gn_gain, ln1_g, ln1_b, ln2_g, ln2_b = (
        a.reshape(DEPTH, 1, a.shape[-1]) for a in (ret_gn_gain, ln1_g, ln1_b, ln2_g, ln2_b))
    b_gu4 = b_gate_up.reshape(DEPTH, N_EXPERTS, 1, 2 * D_FF)
    b_d4 = b_down.reshape(DEPTH, N_EXPERTS, 1, D_MODEL)
    cache_k = jnp.transpose(cache_swa_k, (0, 1, 3, 4, 2)).reshape(DEPTH, DEC_BATCH, 2 * HD_SWA, WINDOW)
    cache_v = jnp.transpose(cache_swa_v, (0, 1, 3, 4, 2)).reshape(DEPTH, DEC_BATCH, 2 * HD_SWA, WINDOW)

    cos_p, sin_p = _rope_tables(jnp.arange(SEQ, dtype=jnp.int32))
    cos_s, sin_s = _rope_tables(PAST_LEN + jnp.arange(1, dtype=jnp.int32))
    intra, q_dec, k_dec, c_dec = _decay_tables(RET_CHUNK)
    ones = jnp.ones((1, 1, HD_RET), F32)
    tables_p = (cos_p, sin_p, intra, q_dec[:, :, None] * ones, k_dec[:, :, None] * ones, c_dec[:, None, None] * ones)
    intra1, q_dec1, k_dec1, c_dec1 = _decay_tables(1)
    dec_s = jnp.stack([intra1[:, 0, 0], q_dec1[:, 0], k_dec1[:, 0], c_dec1], axis=1)
    i = jnp.arange(WINDOW)[:, None]
    j = jnp.arange(2 * WINDOW)[None, :]
    in_window = (WINDOW + i - j >= 0) & (WINDOW + i - j < WINDOW)
    has_prev = jnp.array([False, True])[:, None, None, None]
    allowed = in_window[None, None] & (has_prev | (j >= WINDOW)[None, None])
    bias_p = jnp.where(allowed, _bias_table(rel_bias, _t5_bucket(WINDOW + i - j))[None], -jnp.inf)
    js = jnp.arange(WINDOW + 1)
    bias_s = _bias_table(rel_bias, _t5_bucket(WINDOW - js))
    bias_c, bias_n = bias_s[:, :WINDOW], bias_s[:, WINDOW:]
    kv_of_head = jnp.arange(H_SWA) // G_SWA
    head_mask = (kv_of_head[:, None] == jnp.arange(KV_SWA)[None, :]).astype(F32)

    zero_rows = jnp.zeros((SC_DISPATCH_ROWS, D_MODEL // 2), jnp.uint32)

    sp = ss = ko = vo = tail = None
    h, h_kv = _in_proj(x, w_in_bf, 0)
    for l in range(DEPTH):
        o_r, sp, o_s, tail = _mix_prompt(h, h_kv, tables_p, ret_gn_gain, swa_sinks[l], bias_p, sp, tail, l)
        hs = h[T_PROMPT:].astype(F32)
        kvs = h_kv[T_PROMPT:]
        o_r, ss = _ret_sample(hs, dec_s, cos_s, sin_s, ret_gn_gain, state_ret, o_r, ss, l)
        q8 = hs[:, 4 * RET_W:].reshape(DEC_BATCH, H_SWA, HD_SWA)
        qm = (q8[:, :, None, :] * head_mask[None, :, :, None]).reshape(DEC_BATCH, H_SWA, 2 * HD_SWA)
        kn = kvs[:, :SWA_KV_W].reshape(DEC_BATCH, 1, SWA_KV_W)
        vn = kvs[:, SWA_KV_W:].reshape(DEC_BATCH, 1, SWA_KV_W)
        osm, ko, vo = _swa_sample(qm, kn, vn, cache_k, cache_v, bias_c, bias_n, swa_sinks[l][:, None], ko, vo, l)
        os_s = jnp.sum(osm.reshape(DEC_BATCH, H_SWA, KV_SWA, HD_SWA) * head_mask[None, :, :, None], axis=2)
        o_s = lax.dynamic_update_slice(o_s, os_s.reshape(DEC_BATCH, SWA_Q_W), (T_PROMPT, 0))

        x1, xpk, top_idx, gate, rank, counts = _out_proj(o_r, o_s, x, w_out_bf, ln1_g, ln1_b, router_w_bf,
                                                         router_b, l)
        pos_km, unused, group_start, group_blocks, group_next = _route(top_idx, rank, counts)
        xs = _sc_dispatch(xpk, pos_km, unused, zero_rows)
        ys = _moe(group_start, group_blocks, group_next, xs, w_gate_up, b_gu4, w_down, b_d4, l)
        y4 = _sc_collect(ys, pos_km.reshape(N_ASSIGN)).reshape(TOP_K, T_ALL, D_MODEL // 2)
        if l < DEPTH - 1:
            x, h, h_kv = _norm_proj(x1, y4, gate, ln2_g, ln2_b, w_in_bf, l)
        else:
            yp = _ffn_norm(x1, y4, gate, ln2_g, ln2_b, l, FINAL_ROW_BLOCK, 0, T_PROMPT // FINAL_ROW_BLOCK)
            ys_out = _ffn_norm(x1, y4, gate, ln2_g, ln2_b, l, DEC_BATCH, T_PROMPT // DEC_BATCH, 1)

    yp = yp.reshape(BATCH, SEQ, D_MODEL)
    ys_out = ys_out.reshape(DEC_BATCH, 1, D_MODEL)

    def window_major(t):
        return jnp.transpose(t.reshape(DEPTH, t.shape[1], KV_SWA, HD_SWA, WINDOW), (0, 1, 4, 2, 3))

    return (yp, ys_out, sp, window_major(tail[:, :, :SWA_KV_W]), window_major(tail[:, :, SWA_KV_W:]), ss,
            window_major(ko), window_major(vo))
```

```python
import functools
import math

import jax
import jax.numpy as jnp
from jax import lax
from jax.experimental import pallas as pl
from jax.experimental.pallas import tpu as pltpu
from jax.experimental.pallas import tpu_sc as plsc

F32 = jnp.float32
BF16 = jnp.bfloat16

D_MODEL = 1024
BATCH = 8
SEQ = 2048
DEPTH = 4
DEC_BATCH = 128
PAST_LEN = 8192
H_RET = 4
HD_RET = 128
RET_CHUNK = 128
ROPE_BASE = 10000.0
H_SWA = 8
KV_SWA = 2
G_SWA = H_SWA // KV_SWA
HD_SWA = 64
WINDOW = 128
NUM_BUCKETS = 32
REL_MAX_DIST = 128
RET_W = H_RET * HD_RET
SWA_Q_W = H_SWA * HD_SWA
SWA_KV_W = KV_SWA * HD_SWA
D_IN = 4 * RET_W + SWA_Q_W + 2 * SWA_KV_W
H_MAIN_W = 4 * RET_W + SWA_Q_W
N_EXPERTS = 32
TOP_K = 4
D_FF = D_MODEL
SWIGLU_LIMIT = 7.0
SWIGLU_ALPHA = 1.702
DEEPNORM_ALPHA = (2 * DEPTH) ** 0.25
LN_EPS = 1e-5
GN_EPS = 1e-5

T_PROMPT = BATCH * SEQ
T_ALL = T_PROMPT + DEC_BATCH
ROW_BLOCK = 384
SAMPLE_BLOCK = 8
MOE_ROWS = 128
MOE_CHUNK = 1024
MOE_PIECE = 1024
N_ASSIGN = T_ALL * TOP_K
MOE_BLOCKS = N_ASSIGN // MOE_ROWS + N_EXPERTS
MOE_SLOTS = MOE_BLOCKS * MOE_ROWS
N_UNUSED = MOE_SLOTS - N_ASSIGN
SC_CORES = 2
SC_WORKERS = SC_CORES * 16
SC_DISPATCH_ROWS = 64
SC_COLLECT_ROWS = 48
FINAL_ROW_BLOCK = 512
NORM_SUB_ROWS = 128
MIX_BLOCKS = 4
VMEM_LIMIT = 56 * 1024 * 1024


def _cparams(*sem):
    return pltpu.CompilerParams(dimension_semantics=sem, vmem_limit_bytes=VMEM_LIMIT)


def _project_in(x, w_ref, hm_ref, hkv_ref):
    h = jnp.dot(x.astype(BF16), w_ref[0], preferred_element_type=F32)
    hm_ref[...] = h[:, :H_MAIN_W].astype(BF16)
    hkv_ref[...] = h[:, H_MAIN_W:]


def _in_proj_kernel(x_ref, w_ref, hm_ref, hkv_ref):
    _project_in(x_ref[...], w_ref, hm_ref, hkv_ref)


_IN_PROJ_OUT_SPECS = [pl.BlockSpec((ROW_BLOCK, H_MAIN_W), lambda i: (i, 0)),
                      pl.BlockSpec((ROW_BLOCK, 2 * SWA_KV_W), lambda i: (i, 0))]
_IN_PROJ_OUT_SHAPES = [jax.ShapeDtypeStruct((T_ALL, H_MAIN_W), BF16),
                       jax.ShapeDtypeStruct((T_ALL, 2 * SWA_KV_W), F32)]


def _in_proj(x, w_in_bf, layer):
    return pl.pallas_call(
        _in_proj_kernel,
        grid=(T_ALL // ROW_BLOCK,),
        in_specs=[pl.BlockSpec((ROW_BLOCK, D_MODEL), lambda i: (i, 0)),
                  pl.BlockSpec((1, D_MODEL, D_IN), lambda i: (layer, 0, 0))],
        out_specs=_IN_PROJ_OUT_SPECS,
        out_shape=_IN_PROJ_OUT_SHAPES,
        compiler_params=_cparams("parallel"),
        name="in_proj",
    )(x, w_in_bf)


def _rotate(x, cos, sin_signed):
    return x * cos + pltpu.roll(x, HD_RET // 2, 1) * sin_signed


def _group_norm_gate(o, gain, g):
    mu = jnp.mean(o, -1, keepdims=True)
    var = jnp.mean(jnp.square(o - mu), -1, keepdims=True)
    return (o - mu) * lax.rsqrt(var + GN_EPS) * gain * (g * jax.nn.sigmoid(g))


def _layer_norm(y, g, b):
    mu = jnp.mean(y, -1, keepdims=True)
    var = jnp.mean(jnp.square(y - mu), -1, keepdims=True)
    return (y - mu) * lax.rsqrt(var + LN_EPS) * g + b


def _ret_prompt_stages(q_ref, k_ref, v_ref, g_ref, cos_ref, sin_ref, intra_ref, qdec_ref, kdec_ref,
                       cdec_ref, gain_ref, o_ref, s_ref, rows):
    cos = cos_ref[rows, :]
    sin = sin_ref[rows, :]
    nt = (((1,), (1,)), ((), ()))
    tn = (((0,), (0,)), ((), ()))
    heads = [slice(h * HD_RET, (h + 1) * HD_RET) for h in range(H_RET)]
    vs, atts, crosses = [], [], []
    for h, sl in enumerate(heads):
        q = _rotate(q_ref[rows, sl].astype(F32), cos, sin)
        k = _rotate(k_ref[rows, sl].astype(F32), cos, sin) * (HD_RET ** -0.5)
        v = v_ref[rows, sl]
        s = s_ref[0, 0, h]
        atts.append(lax.dot_general(q.astype(BF16), k.astype(BF16), nt, preferred_element_type=F32))
        crosses.append(jnp.dot((q * qdec_ref[h]).astype(BF16), s.astype(BF16), preferred_element_type=F32))
        s_ref[0, 0, h] = s * cdec_ref[h] + lax.dot_general((k * kdec_ref[h]).astype(BF16), v, tn,
                                                          preferred_element_type=F32)
        vs.append(v)
    yield
    atts = [(att * intra_ref[h]).astype(BF16) for h, att in enumerate(atts)]
    yield
    outs = [jnp.dot(atts[h], vs[h], preferred_element_type=F32) + crosses[h] for h in range(H_RET)]
    yield
    for h, sl in enumerate(heads):
        o_ref[rows, sl] = _group_norm_gate(outs[h], gain_ref[0, :, sl], g_ref[rows, sl].astype(F32))


def _swa_prompt_stages(sink_ref, q_ref, kv_prev, kv_cur, bias, o_ref, rows, lg_scr, p_scr):
    kv = jnp.concatenate([kv_prev, kv_cur], axis=0)
    k_t = kv[:, :SWA_KV_W].T
    k_t_swapped = pltpu.roll(k_t, HD_SWA, 0)
    v = kv[:, SWA_KV_W:]
    v_swapped = pltpu.roll(v, HD_SWA, 1)
    k_row = lax.broadcasted_iota(jnp.int32, k_t.shape, 0)
    v_lane = lax.broadcasted_iota(jnp.int32, v.shape, 1)
    ones = jnp.ones(v.shape, BF16)
    k_pad, v_pad = {}, {}
    for kvh in range(KV_SWA):
        for par in range(2):
            k_src = k_t if kvh == par else k_t_swapped
            v_src = v if kvh == par else v_swapped
            k_pad[kvh, par] = jnp.where((k_row >= par * HD_SWA) & (k_row < (par + 1) * HD_SWA), k_src, 0.0).astype(BF16)
            v_pad[kvh, par] = jnp.where((v_lane >= par * HD_SWA) & (v_lane < (par + 1) * HD_SWA), v_src, 0.0).astype(BF16)
    for hd in range(H_SWA):
        pair, par, kvh = hd // 2, hd % 2, hd // G_SWA
        q2 = (q_ref[rows, pair * 2 * HD_SWA:(pair + 1) * 2 * HD_SWA] * (HD_SWA ** -0.5)).astype(BF16)
        lg_scr[hd] = jnp.dot(q2, k_pad[kvh, par], preferred_element_type=F32) + bias(hd)
    yield
    sink_terms = []
    for hd in range(H_SWA):
        logits = lg_scr[hd]
        sink = sink_ref[hd]
        m = jnp.maximum(jnp.max(logits, -1, keepdims=True), sink)
        p_scr[hd] = jnp.exp(logits - m).astype(BF16)
        sink_terms.append(jnp.exp(sink - m))
    yield
    for pair in range(H_SWA // 2):
        acc = None
        for par in range(2):
            hd = 2 * pair + par
            p = p_scr[hd]
            den = jnp.dot(p, ones, preferred_element_type=F32) + sink_terms[hd]
            part = jnp.dot(p, v_pad[hd // G_SWA, par], preferred_element_type=F32) / den
            acc = part if acc is None else acc + part
        o_ref[rows, pair * 2 * HD_SWA:(pair + 1) * 2 * HD_SWA] = acc


def _mix_prompt_kernel(*refs):
    n_ret_in = 11
    ret_in = refs[:n_ret_in]
    sink_ref, q_ref, kvc_ref, kvp_ref, bias_ref = refs[n_ret_in:n_ret_in + 5]
    o_r_ref, s_ref, o_s_ref, tail_ref, lg_scr, p_scr = refs[-6:]
    c = pl.program_id(1)

    @pl.when(c == 0)
    def _():
        s_ref[...] = jnp.zeros_like(s_ref)

    @pl.when(c == pl.num_programs(1) - 1)
    def _():
        tail_ref[0, 0] = kvc_ref[pl.ds((MIX_BLOCKS - 1) * WINDOW, WINDOW), :].T

    groups = []
    for j in range(MIX_BLOCKS):
        rows = pl.ds(j * WINDOW, WINDOW)
        kv_cur = kvc_ref[rows, :]
        if j == 0:
            kv_prev = kvp_ref[...]
            table = jnp.minimum(c, 1)
        else:
            kv_prev = kvc_ref[pl.ds((j - 1) * WINDOW, WINDOW), :]
            table = 1
        groups.append(_ret_prompt_stages(*ret_in, o_r_ref, s_ref, rows))

        def bias(hd, table=table):
            return bias_ref[table, hd]

        groups.append(_swa_prompt_stages(sink_ref, q_ref, kv_prev, kv_cur, bias, o_s_ref, rows,
                                         lg_scr.at[j], p_scr.at[j]))
    while groups:
        for g in list(groups):
            if next(g, groups) is groups:
                groups.remove(g)


def _mix_prompt(h, h_kv, tables, gain, sinks_l, bias, s_prev, tail_prev, layer):
    cos, sin, intra, qdec, kdec, cdec = tables
    nb = SEQ // WINDOW
    ns = nb // MIX_BLOCKS
    rows = MIX_BLOCKS * WINDOW
    qcol = 4 * RET_W // SWA_Q_W

    def col(j):
        return pl.BlockSpec((rows, RET_W), lambda b, c: (b * ns + c, j))

    def whole(a):
        return pl.BlockSpec(a.shape, lambda b, c: (0,) * a.ndim)

    in_specs = [col(0), col(1), col(2), col(3),
                pl.BlockSpec((rows, HD_RET), lambda b, c: (c, 0)),
                pl.BlockSpec((rows, HD_RET), lambda b, c: (c, 0)),
                whole(intra), whole(qdec), whole(kdec), whole(cdec),
                pl.BlockSpec((1, 1, RET_W), lambda b, c: (layer, 0, 0)),
                pl.BlockSpec(memory_space=pltpu.SMEM),
                pl.BlockSpec((rows, SWA_Q_W), lambda b, c: (b * ns + c, qcol)),
                pl.BlockSpec((rows, 2 * SWA_KV_W), lambda b, c: (b * ns + c, 0)),
                pl.BlockSpec((WINDOW, 2 * SWA_KV_W), lambda b, c: (b * nb + jnp.maximum(MIX_BLOCKS * c - 1, 0), 0)),
                whole(bias)]
    args = [h, h, h, h, cos, sin, intra, qdec, kdec, cdec, gain, sinks_l, h, h_kv, h_kv, bias]
    aliases = {}
    if s_prev is not None:
        in_specs += [pl.BlockSpec(memory_space=pl.ANY), pl.BlockSpec(memory_space=pl.ANY)]
        args += [s_prev, tail_prev]
        aliases = {16: 1, 17: 3}

    return pl.pallas_call(
        _mix_prompt_kernel,
        grid=(BATCH, ns),
        in_specs=in_specs,
        scratch_shapes=[pltpu.VMEM((MIX_BLOCKS, H_SWA, WINDOW, 2 * WINDOW), F32),
                        pltpu.VMEM((MIX_BLOCKS, H_SWA, WINDOW, 2 * WINDOW), BF16)],
        out_specs=[pl.BlockSpec((rows, RET_W), lambda b, c: (b * ns + c, 0)),
                   pl.BlockSpec((1, 1, H_RET, HD_RET, HD_RET), lambda b, c: (layer, b, 0, 0, 0)),
                   pl.BlockSpec((rows, SWA_Q_W), lambda b, c: (b * ns + c, 0)),
                   pl.BlockSpec((1, 1, 2 * SWA_KV_W, WINDOW), lambda b, c: (layer, b, 0, 0))],
        out_shape=[jax.ShapeDtypeStruct((T_ALL, RET_W), F32),
                   jax.ShapeDtypeStruct((DEPTH, BATCH, H_RET, HD_RET, HD_RET), F32),
                   jax.ShapeDtypeStruct((T_ALL, SWA_Q_W), F32),
                   jax.ShapeDtypeStruct((DEPTH, BATCH, 2 * SWA_KV_W, WINDOW), F32)],
        input_output_aliases=aliases,
        compiler_params=_cparams("parallel", "arbitrary"),
        name="mix_prompt",
    )(*args)


def _ret_sample_kernel(dec_ref, q_ref, k_ref, v_ref, g_ref, cos_ref, sin_ref, gain_ref, st_ref, o_ref, so_ref):
    cos = cos_ref[...]
    sin = sin_ref[...]
    for h in range(H_RET):
        sl = slice(h * HD_RET, (h + 1) * HD_RET)
        intra, qdec, kdec, cdec = dec_ref[h, 0], dec_ref[h, 1], dec_ref[h, 2], dec_ref[h, 3]
        q = _rotate(q_ref[:, sl], cos, sin)
        k = _rotate(k_ref[:, sl], cos, sin) * (HD_RET ** -0.5)
        v = v_ref[:, sl]
        att = jnp.sum(q * k, -1, keepdims=True) * intra
        q_col = (q * qdec).T
        k_col = (k * kdec).T
        rows = []
        for b in range(SAMPLE_BLOCK):
            s = st_ref[0, b, h]
            v_row = v[b:b + 1, :]
            rows.append(att[b:b + 1, :] * v_row + jnp.sum(q_col[:, b:b + 1] * s, axis=0, keepdims=True))
            so_ref[0, b, h] = s * cdec + k_col[:, b:b + 1] * v_row
        o = jnp.concatenate(rows, axis=0)
        o_ref[:, sl] = _group_norm_gate(o, gain_ref[0, :, sl], g_ref[:, sl])


def _ret_sample(h, dec_s, cos_s, sin_s, gain, state_ret, o_r, st_prev, layer):
    base = T_PROMPT // SAMPLE_BLOCK

    def col(j):
        return pl.BlockSpec((SAMPLE_BLOCK, RET_W), lambda i: (i, j))

    in_specs = [pl.BlockSpec(memory_space=pltpu.SMEM),
                col(0), col(1), col(2), col(3),
                pl.BlockSpec((1, HD_RET), lambda i: (0, 0)),
                pl.BlockSpec((1, HD_RET), lambda i: (0, 0)),
                pl.BlockSpec((1, 1, RET_W), lambda i: (layer, 0, 0)),
                pl.BlockSpec((1, SAMPLE_BLOCK, H_RET, HD_RET, HD_RET), lambda i: (layer, i, 0, 0, 0)),
                pl.BlockSpec(memory_space=pl.ANY)]
    args = [dec_s, h, h, h, h, cos_s, sin_s, gain, state_ret, o_r]
    aliases = {9: 0}
    if st_prev is not None:
        in_specs.append(pl.BlockSpec(memory_space=pl.ANY))
        args.append(st_prev)
        aliases[10] = 1

    def body(*refs):
        _ret_sample_kernel(*refs[:9], *refs[-2:])

    return pl.pallas_call(
        body,
        grid=(DEC_BATCH // SAMPLE_BLOCK,),
        in_specs=in_specs,
        out_specs=[pl.BlockSpec((SAMPLE_BLOCK, RET_W), lambda i: (base + i, 0)),
                   pl.BlockSpec((1, SAMPLE_BLOCK, H_RET, HD_RET, HD_RET), lambda i: (layer, i, 0, 0, 0))],
        out_shape=[jax.ShapeDtypeStruct((T_ALL, RET_W), F32),
                   jax.ShapeDtypeStruct((DEPTH, DEC_BATCH, H_RET, HD_RET, HD_RET), F32)],
        input_output_aliases=aliases,
        compiler_params=_cparams("parallel"),
        name="ret_sample",
    )(*args)


def _swa_sample_kernel(qm_ref, kn_ref, vn_ref, knc_ref, vnc_ref, kc_ref, vc_ref, bias_ref, biasn_ref, sink_ref,
                       o_ref, ko_ref, vo_ref):
    qm = qm_ref[...]
    kn = kn_ref[...]
    vn = vn_ref[...]
    kc = kc_ref[0]
    vc = vc_ref[0]
    scale = HD_SWA ** -0.5
    logits = jnp.einsum('bhd,bdj->bhj', qm.astype(BF16), kc.astype(BF16),
                        preferred_element_type=F32) * scale + bias_ref[...][None]
    j = lax.broadcasted_iota(jnp.int32, logits.shape, 2)
    logits = jnp.where(j >= 1, logits, -jnp.inf)
    ln = jnp.sum(qm * kn, -1, keepdims=True) * scale + biasn_ref[...][None]
    sink = sink_ref[...][None]
    m = jnp.maximum(jnp.maximum(jnp.max(logits, -1, keepdims=True), ln), sink)
    p = jnp.exp(logits - m)
    pn = jnp.exp(ln - m)
    denom = jnp.sum(p, -1, keepdims=True) + pn + jnp.exp(sink - m)
    w = p / denom
    o_ref[...] = (jnp.einsum('bhj,bdj->bhd', w.astype(BF16), vc.astype(BF16), preferred_element_type=F32)
                  + (pn / denom) * vn)
    last = lax.broadcasted_iota(jnp.int32, kc.shape, 2) == WINDOW - 1
    ko_ref[0] = jnp.where(last, knc_ref[...], pltpu.roll(kc, WINDOW - 1, 2))
    vo_ref[0] = jnp.where(last, vnc_ref[...], pltpu.roll(vc, WINDOW - 1, 2))


def _swa_sample(qm, kn, vn, cache_k, cache_v, bias_c, bias_n, sink_col, ko_prev, vo_prev, layer):
    sb = SAMPLE_BLOCK
    cache_spec = pl.BlockSpec((1, sb, 2 * HD_SWA, WINDOW), lambda i: (layer, i, 0, 0))
    row_spec = pl.BlockSpec((sb, 1, 2 * HD_SWA), lambda i: (i, 0, 0))
    col_spec = pl.BlockSpec((sb, 2 * HD_SWA, 1), lambda i: (i, 0, 0))
    in_specs = [pl.BlockSpec((sb, H_SWA, 2 * HD_SWA), lambda i: (i, 0, 0)),
                row_spec, row_spec, col_spec, col_spec,
                cache_spec, cache_spec,
                pl.BlockSpec((H_SWA, WINDOW), lambda i: (0, 0)),
                pl.BlockSpec((H_SWA, 1), lambda i: (0, 0)),
                pl.BlockSpec((H_SWA, 1), lambda i: (0, 0))]
    args = [qm, kn, vn, kn.reshape(DEC_BATCH, 2 * HD_SWA, 1), vn.reshape(DEC_BATCH, 2 * HD_SWA, 1),
            cache_k, cache_v, bias_c, bias_n, sink_col]
    aliases = {}
    if ko_prev is not None:
        in_specs += [pl.BlockSpec(memory_space=pl.ANY), pl.BlockSpec(memory_space=pl.ANY)]
        args += [ko_prev, vo_prev]
        aliases = {10: 1, 11: 2}

    def body(*refs):
        _swa_sample_kernel(*refs[:10], *refs[-3:])

    cshape = jax.ShapeDtypeStruct((DEPTH, DEC_BATCH, 2 * HD_SWA, WINDOW), F32)
    return pl.pallas_call(
        body,
        grid=(DEC_BATCH // sb,),
        in_specs=in_specs,
        out_specs=[pl.BlockSpec((sb, H_SWA, 2 * HD_SWA), lambda i: (i, 0, 0)), cache_spec, cache_spec],
        out_shape=[jax.ShapeDtypeStruct((DEC_BATCH, H_SWA, 2 * HD_SWA), F32), cshape, cshape],
        input_output_aliases=aliases,
        compiler_params=_cparams("parallel"),
        name="swa_sample",
    )(*args)


def _pack_bf16_pairs(x):
    w = x.shape[1] // 2
    lo = lax.bitcast_convert_type(x[:, :w].astype(BF16).astype(F32), jnp.uint32) >> 16
    hi = lax.bitcast_convert_type(x[:, w:].astype(BF16).astype(F32), jnp.uint32)
    return lo | hi


def _unpack_bf16_pairs(p):
    lo = lax.bitcast_convert_type(p << 16, F32).astype(BF16)
    hi = lax.bitcast_convert_type(p & jnp.uint32(0xFFFF0000), F32).astype(BF16)
    return lo, hi


def _out_proj_kernel(or_ref, os_ref, x_ref, wr_ref, ws_ref, g_ref, b_ref, rw_ref, rb_ref,
                     x1_ref, xpk_ref, idx_ref, gate_ref, rank_ref, cnt_ref):
    mix = (jnp.dot(or_ref[...].astype(BF16), wr_ref[0], preferred_element_type=F32)
           + jnp.dot(os_ref[...].astype(BF16), ws_ref[0], preferred_element_type=F32))
    x1 = _layer_norm(DEEPNORM_ALPHA * x_ref[...] + mix, g_ref[0], b_ref[0])
    x1_ref[...] = x1
    xpk_ref[...] = _pack_bf16_pairs(x1)
    nt = (((1,), (1,)), ((), ()))
    logits = lax.dot_general(rw_ref[0], x1.astype(BF16), nt, preferred_element_type=F32) + rb_ref[0]
    rows = logits.shape[1]
    expert = lax.broadcasted_iota(jnp.int32, logits.shape, 0)
    vals, hits = [], []
    for kk in range(TOP_K):
        m = jnp.max(logits, 0, keepdims=True)
        idx = jnp.min(jnp.where(logits == m, expert, N_EXPERTS), 0, keepdims=True)
        vals.append(m)
        hits.append(expert == idx)
        idx_ref[kk:kk + 1, :] = idx
        logits = jnp.where(expert == idx, -jnp.inf, logits)
    es = [jnp.exp(v - vals[0]) for v in vals]
    tot = es[0] + es[1] + es[2] + es[3]
    gate_ref[...] = jnp.concatenate([e / tot for e in es] + [jnp.zeros((8 - TOP_K, rows), F32)], axis=0)
    chosen = sum(h.astype(F32) for h in hits)
    earlier = (lax.broadcasted_iota(jnp.int32, (rows, rows), 0)
               < lax.broadcasted_iota(jnp.int32, (rows, rows), 1)).astype(BF16)
    before = jnp.dot(chosen.astype(BF16), earlier, preferred_element_type=F32)
    for kk in range(TOP_K):
        rank_ref[kk:kk + 1, :] = jnp.sum(jnp.where(hits[kk], before, 0.0), 0, keepdims=True).astype(jnp.int32)
    cnt_ref[0] = jnp.sum(chosen, axis=1, keepdims=True).astype(jnp.int32)


def _out_proj(o_r, o_s, x, w_out_bf, ln_g, ln_b, router_w_bf, router_b, layer):
    def rows(w):
        return pl.BlockSpec((ROW_BLOCK, w), lambda i: (i, 0))

    def vec(w):
        return pl.BlockSpec((1, 1, w), lambda i: (layer, 0, 0))

    def cols(h):
        return pl.BlockSpec((h, ROW_BLOCK), lambda i: (0, i))

    return pl.pallas_call(
        _out_proj_kernel,
        grid=(T_ALL // ROW_BLOCK,),
        in_specs=[rows(RET_W), rows(SWA_Q_W), rows(D_MODEL),
                  pl.BlockSpec((1, RET_W, D_MODEL), lambda i: (layer, 0, 0)),
                  pl.BlockSpec((1, SWA_Q_W, D_MODEL), lambda i: (layer, 1, 0)),
                  vec(D_MODEL), vec(D_MODEL),
                  pl.BlockSpec((1, N_EXPERTS, D_MODEL), lambda i: (layer, 0, 0)),
                  pl.BlockSpec((1, N_EXPERTS, 1), lambda i: (layer, 0, 0))],
        out_specs=[rows(D_MODEL), rows(D_MODEL // 2), cols(TOP_K), cols(8), cols(TOP_K),
                   pl.BlockSpec((1, N_EXPERTS, 1), lambda i: (i, 0, 0))],
        out_shape=[jax.ShapeDtypeStruct((T_ALL, D_MODEL), F32),
                   jax.ShapeDtypeStruct((T_ALL, D_MODEL // 2), jnp.uint32),
                   jax.ShapeDtypeStruct((TOP_K, T_ALL), jnp.int32),
                   jax.ShapeDtypeStruct((8, T_ALL), F32),
                   jax.ShapeDtypeStruct((TOP_K, T_ALL), jnp.int32),
                   jax.ShapeDtypeStruct((T_ALL // ROW_BLOCK, N_EXPERTS, 1), jnp.int32)],
        compiler_params=_cparams("parallel"),
        name="out_proj",
    )(o_r, o_s, x, w_out_bf, w_out_bf, ln_g, ln_b, router_w_bf, router_b)


def _moe_kernel(start_ref, nblk_ref, next_ref, xs_hbm, wgu_ref, bgu_ref, wd_ref, bd_ref, ys_hbm,
                wgu_bf, wd_bf, x_buf, y_buf, in_sems, out_sems, pending, ready):
    e = pl.program_id(0)
    per_chunk = MOE_CHUNK // MOE_ROWS
    tail_sizes = [t * MOE_ROWS for t in range(1, per_chunk)]
    nblk = nblk_ref[e]
    n_full = nblk // per_chunk
    tail_blocks = nblk % per_chunk
    base = start_ref[e]

    def in_copy(row0, rows, slot):
        src = xs_hbm.at[pl.ds(pl.multiple_of(row0, MOE_ROWS), rows)]
        return pltpu.make_async_copy(src, x_buf.at[slot, pl.ds(0, rows)], in_sems.at[slot])

    def out_copy(row0, rows, slot):
        dst = ys_hbm.at[pl.ds(pl.multiple_of(row0, MOE_ROWS), rows)]
        return pltpu.make_async_copy(y_buf.at[slot, pl.ds(0, rows)], dst, out_sems.at[slot])

    def start_first_chunk(expert, slot):
        @pl.when(nblk_ref[expert] >= per_chunk)
        def _():
            in_copy(start_ref[expert], MOE_CHUNK, slot).start()

        for rows in tail_sizes:
            @pl.when(nblk_ref[expert] == rows // MOE_ROWS)
            def _():
                in_copy(start_ref[expert], rows, slot).start()

    def prefetch_next_expert(slot):
        nxt = next_ref[e]

        @pl.when(nxt < N_EXPERTS)
        def _():
            start_first_chunk(nxt, slot)
            ready[0] = slot

    def drain(slot):
        for rows in [MOE_CHUNK] + tail_sizes:
            @pl.when(pending[slot] == rows)
            def _():
                out_copy(0, rows, slot).wait()
        pending[slot] = 0

    def compute(rows, slot):
        for r0 in range(0, rows, MOE_PIECE):
            piece = pl.ds(r0, min(MOE_PIECE, rows - r0))
            x = jnp.concatenate(_unpack_bf16_pairs(x_buf[slot, piece, :]), axis=1)
            gu = jnp.dot(x, wgu_bf[...], preferred_element_type=F32) + bgu_ref[0, 0]
            glu = jnp.minimum(gu[:, :D_FF], SWIGLU_LIMIT)
            lin = jnp.clip(gu[:, D_FF:], -SWIGLU_LIMIT, SWIGLU_LIMIT)
            act = glu * jax.nn.sigmoid(SWIGLU_ALPHA * glu) * (lin + 1.0)
            y = jnp.dot(act.astype(BF16), wd_bf[...], preferred_element_type=F32) + bd_ref[0, 0]
            y_buf[slot, piece, :] = _pack_bf16_pairs(y)

    def run_chunk(row0, rows, slot):
        in_copy(row0, rows, slot).wait()
        drain(slot)
        compute(rows, slot)
        out_copy(row0, rows, slot).start()
        pending[slot] = rows

    @pl.when(e == 0)
    def _():
        pending[0] = 0
        pending[1] = 0
        ready[0] = -1

    @pl.when(nblk > 0)
    def _():
        @pl.when(ready[0] < 0)
        def _():
            start_first_chunk(e, 0)
            ready[0] = 0

        first_slot = ready[0]
        wgu_bf[...] = wgu_ref[0, 0].astype(BF16)
        wd_bf[...] = wd_ref[0, 0].astype(BF16)

        def body(c, carry):
            slot = (first_slot + c) & 1
            row0 = pl.multiple_of(base + c * MOE_CHUNK, MOE_ROWS)

            @pl.when(c + 1 < n_full)
            def _():
                in_copy(row0 + MOE_CHUNK, MOE_CHUNK, 1 - slot).start()

            for rows in tail_sizes:
                @pl.when(jnp.logical_and(c + 1 == n_full, tail_blocks == rows // MOE_ROWS))
                def _():
                    in_copy(row0 + MOE_CHUNK, rows, 1 - slot).start()

            @pl.when(jnp.logical_and(c + 1 == n_full, tail_blocks == 0))
            def _():
                prefetch_next_expert(1 - slot)

            run_chunk(row0, MOE_CHUNK, slot)
            return carry

        lax.fori_loop(0, n_full, body, 0)

        for rows in tail_sizes:
            @pl.when(tail_blocks == rows // MOE_ROWS)
            def _():
                slot = (first_slot + n_full) & 1
                prefetch_next_expert(1 - slot)
                run_chunk(pl.multiple_of(base + n_full * MOE_CHUNK, MOE_ROWS), rows, slot)

    @pl.when(e == pl.num_programs(0) - 1)
    def _():
        drain(0)
        drain(1)


def _moe(group_start, group_blocks, group_next, xs, w_gate_up, b_gate_up, w_down, b_down, layer):
    def wsel(e, st, nb, nx):
        return (layer, e, 0, 0)

    return pl.pallas_call(
        _moe_kernel,
        grid_spec=pltpu.PrefetchScalarGridSpec(
            num_scalar_prefetch=3,
            grid=(N_EXPERTS,),
            in_specs=[pl.BlockSpec(memory_space=pl.ANY),
                      pl.BlockSpec((1, 1, D_MODEL, 2 * D_FF), wsel),
                      pl.BlockSpec((1, 1, 1, 2 * D_FF), wsel),
                      pl.BlockSpec((1, 1, D_FF, D_MODEL), wsel),
                      pl.BlockSpec((1, 1, 1, D_MODEL), wsel)],
            out_specs=pl.BlockSpec(memory_space=pl.ANY),
            scratch_shapes=[pltpu.VMEM((D_MODEL, 2 * D_FF), BF16), pltpu.VMEM((D_FF, D_MODEL), BF16),
                            pltpu.VMEM((2, MOE_CHUNK, D_MODEL // 2), jnp.uint32),
                            pltpu.VMEM((2, MOE_CHUNK, D_MODEL // 2), jnp.uint32),
                            pltpu.SemaphoreType.DMA((2,)), pltpu.SemaphoreType.DMA((2,)),
                            pltpu.SMEM((2,), jnp.int32), pltpu.SMEM((1,), jnp.int32)]),
        out_shape=jax.ShapeDtypeStruct((MOE_SLOTS, D_MODEL // 2), jnp.uint32),
        compiler_params=_cparams("arbitrary"),
        name="moe_experts",
    )(group_start, group_blocks, group_next, xs, w_gate_up, b_gate_up, w_down, b_down)


def _route(idx, rank, counts):
    experts = jnp.arange(N_EXPERTS, dtype=jnp.int32)
    counts = counts.reshape(T_ALL // ROW_BLOCK, N_EXPERTS)
    before_block = jnp.cumsum(counts, axis=0) - counts
    total = jnp.sum(counts, axis=0)
    padded = (total + MOE_ROWS - 1) // MOE_ROWS * MOE_ROWS
    pad_end = jnp.cumsum(padded)
    pad_start = pad_end - padded
    base = jnp.repeat((pad_start[None, :] + before_block).T, ROW_BLOCK, axis=1)
    pos = jnp.sum(jnp.where(idx[:, None, :] == experts[None, :, None], base[None], 0), axis=1) + rank
    gap = padded - total
    gap_end = jnp.concatenate([jnp.cumsum(gap), jnp.full((1,), N_UNUSED, jnp.int32)])
    gap_start = jnp.concatenate([jnp.zeros((1,), jnp.int32), gap_end[:-1]])
    first_free = jnp.concatenate([pad_start + total, pad_end[-1:]])
    j = jnp.arange(N_UNUSED, dtype=jnp.int32)
    region = (j[:, None] >= gap_start[None, :]) & (j[:, None] < gap_end[None, :])
    unused = jnp.sum(jnp.where(region, first_free[None, :] + j[:, None] - gap_start[None, :], 0), axis=-1)
    nonempty = jnp.where(total > 0, experts, N_EXPERTS)
    following = jnp.concatenate([lax.cummin(nonempty, reverse=True)[1:], jnp.full((1,), N_EXPERTS, jnp.int32)])
    return (pos.astype(jnp.int32), unused.astype(jnp.int32),
            pad_start.astype(jnp.int32), (padded // MOE_ROWS).astype(jnp.int32), following.astype(jnp.int32))


def _sc_worker():
    return lax.axis_index("s") * SC_CORES + lax.axis_index("c")


def _sc_dispatch(xpk, pos_km, unused, zero_rows):
    ch = SC_DISPATCH_ROWS
    n_chunks = T_ALL // ch
    n_zero_chunks = N_UNUSED // ch
    width = xpk.shape[1]
    mesh = plsc.VectorSubcoreMesh(core_axis_name="c", subcore_axis_name="s")

    @functools.partial(
        pl.kernel, mesh=mesh,
        out_type=jax.ShapeDtypeStruct((MOE_SLOTS, width), xpk.dtype),
        scratch_types=[pltpu.VMEM((2, ch, width), xpk.dtype), pltpu.VMEM((2, TOP_K, ch), jnp.int32),
                       pltpu.SemaphoreType.DMA((2,)), pltpu.SemaphoreType.DMA((TOP_K,))],
        name="sc_dispatch")
    def run(x_hbm, pos_hbm, unused_hbm, zero_hbm, xs_hbm, rows_v, idx_v, load_sems, store_sems):
        wid = _sc_worker()
        pltpu.sync_copy(zero_hbm, rows_v.at[0])
        for r in range(n_zero_chunks // SC_WORKERS):
            start = pl.multiple_of((wid + r * SC_WORKERS) * ch, ch)
            pltpu.sync_copy(unused_hbm.at[pl.ds(start, ch)], idx_v.at[0, 0])
            pltpu.sync_copy(rows_v.at[0], xs_hbm.at[idx_v.at[0, 0]])

        def first_row(r):
            return pl.multiple_of((wid + r * SC_WORKERS) * ch, ch)

        def rows_copy(r, slot):
            return pltpu.make_async_copy(x_hbm.at[pl.ds(first_row(r), ch)], rows_v.at[slot], load_sems.at[slot])

        def start_load(r, slot):
            for k in range(TOP_K):
                pltpu.sync_copy(pos_hbm.at[k, pl.ds(first_row(r), ch)], idx_v.at[slot, k])
            rows_copy(r, slot).start()

        def has_chunk(r):
            return wid + r * SC_WORKERS < n_chunks

        start_load(0, 0)

        def body(r, carry):
            slot = r & 1

            @pl.when(has_chunk(r + 1))
            def _():
                start_load(r + 1, 1 - slot)

            @pl.when(has_chunk(r))
            def _():
                rows_copy(r, slot).wait()
                copies = [pltpu.async_copy(rows_v.at[slot], xs_hbm.at[idx_v.at[slot, k]], store_sems.at[k])
                          for k in range(TOP_K)]
                for cp in copies:
                    cp.wait()
            return carry

        lax.fori_loop(0, pl.cdiv(n_chunks, SC_WORKERS), body, 0)

    return run(xpk, pos_km, unused, zero_rows)


def _sc_collect(ys, pos_flat):
    ch = SC_COLLECT_ROWS
    per_worker = N_ASSIGN // SC_WORKERS
    mesh = plsc.VectorSubcoreMesh(core_axis_name="c", subcore_axis_name="s")

    @functools.partial(
        pl.kernel, mesh=mesh,
        out_type=jax.ShapeDtypeStruct((N_ASSIGN, ys.shape[1]), ys.dtype),
        scratch_types=[pltpu.VMEM((2, ch, ys.shape[1]), ys.dtype), pltpu.VMEM((2, ch), jnp.int32),
                       pltpu.SemaphoreType.DMA((2,))],
        name="sc_collect")
    def run(ys_hbm, pos_hbm, y4_hbm, rows_v, idx_v, sems):
        wid = _sc_worker()
        n_chunks = per_worker // ch

        def first_row(r):
            return pl.multiple_of(wid * per_worker + r * ch, 8)

        def gather(slot):
            return pltpu.make_async_copy(ys_hbm.at[idx_v.at[slot]], rows_v.at[slot], sems.at[slot])

        def start_gather(r, slot):
            pltpu.sync_copy(pos_hbm.at[pl.ds(first_row(r), ch)], idx_v.at[slot])
            gather(slot).start()

        start_gather(0, 0)

        def body(r, carry):
            slot = r & 1

            @pl.when(r + 1 < n_chunks)
            def _():
                start_gather(r + 1, 1 - slot)

            gather(slot).wait()
            pltpu.sync_copy(rows_v.at[slot], y4_hbm.at[pl.ds(first_row(r), ch)])
            return carry

        lax.fori_loop(0, n_chunks, body, 0)

    return run(ys, pos_flat)


def _combine_norm(x_ref, y_refs, gate_ref, g_ref, b_ref, rows=slice(None)):
    gate = gate_ref[:, rows].T
    lo = hi = None
    for k, y_ref in enumerate(y_refs):
        p = y_ref[0, rows, :]
        g = gate[:, k:k + 1]
        lo_k = g * lax.bitcast_convert_type(p << 16, F32)
        hi_k = g * lax.bitcast_convert_type(p & jnp.uint32(0xFFFF0000), F32)
        lo = lo_k if lo is None else lo + lo_k
        hi = hi_k if hi is None else hi + hi_k
    ffn = jnp.concatenate([lo, hi], axis=1)
    return _layer_norm(DEEPNORM_ALPHA * x_ref[rows, :] + ffn, g_ref[0], b_ref[0])


def _ffn_norm_kernel(x_ref, y0_ref, y1_ref, y2_ref, y3_ref, gate_ref, g_ref, b_ref, o_ref):
    o_ref[...] = _combine_norm(x_ref, (y0_ref, y1_ref, y2_ref, y3_ref), gate_ref, g_ref, b_ref)


def _norm_proj_kernel(x_ref, y0_ref, y1_ref, y2_ref, y3_ref, gate_ref, g_ref, b_ref, w_ref,
                      x2_ref, hm_ref, hkv_ref):
    for r in range(ROW_BLOCK // NORM_SUB_ROWS):
        rows = pl.ds(r * NORM_SUB_ROWS, NORM_SUB_ROWS)
        x2 = _combine_norm(x_ref, (y0_ref, y1_ref, y2_ref, y3_ref), gate_ref, g_ref, b_ref, rows)
        x2_ref[rows, :] = x2
        _project_in(x2, w_ref, hm_ref.at[rows, :], hkv_ref.at[rows, :])


def _norm_proj(x1, y4, gate, ln_g, ln_b, w_in_bf, layer):
    rows = pl.BlockSpec((ROW_BLOCK, D_MODEL), lambda i: (i, 0))

    def plane(k):
        return pl.BlockSpec((1, ROW_BLOCK, D_MODEL // 2), lambda i: (k, i, 0))

    vec = pl.BlockSpec((1, 1, D_MODEL), lambda i: (layer, 0, 0))
    return pl.pallas_call(
        _norm_proj_kernel,
        grid=(T_ALL // ROW_BLOCK,),
        in_specs=[rows, plane(0), plane(1), plane(2), plane(3),
                  pl.BlockSpec((8, ROW_BLOCK), lambda i: (0, i)), vec, vec,
                  pl.BlockSpec((1, D_MODEL, D_IN), lambda i: (layer + 1, 0, 0))],
        out_specs=[rows] + _IN_PROJ_OUT_SPECS,
        out_shape=[jax.ShapeDtypeStruct((T_ALL, D_MODEL), F32)] + _IN_PROJ_OUT_SHAPES,
        compiler_params=_cparams("parallel"),
        name="norm_proj",
    )(x1, y4, y4, y4, y4, gate, ln_g, ln_b, w_in_bf)


def _ffn_norm(x1, y4, gate, ln_g, ln_b, layer, row_block, first_block, n_blocks):
    def rows(w):
        return pl.BlockSpec((row_block, w), lambda i: (first_block + i, 0))

    def plane(k):
        return pl.BlockSpec((1, row_block, D_MODEL // 2), lambda i: (k, first_block + i, 0))

    vec = pl.BlockSpec((1, 1, D_MODEL), lambda i: (layer, 0, 0))
    return pl.pallas_call(
        _ffn_norm_kernel,
        grid=(n_blocks,),
        in_specs=[rows(D_MODEL), plane(0), plane(1), plane(2), plane(3),
                  pl.BlockSpec((8, row_block), lambda i: (0, first_block + i)), vec, vec],
        out_specs=pl.BlockSpec((row_block, D_MODEL), lambda i: (i, 0)),
        out_shape=jax.ShapeDtypeStruct((n_blocks * row_block, D_MODEL), F32),
        compiler_params=_cparams("parallel"),
        name="ffn_norm",
    )(x1, y4, y4, y4, y4, gate, ln_g, ln_b)


def _rope_tables(pos):
    half = HD_RET // 2
    inv = ROPE_BASE ** (-jnp.arange(half, dtype=F32) / half)
    ang = pos.astype(F32)[:, None] * inv[None, :]
    cos, sin = jnp.cos(ang), jnp.sin(ang)
    return jnp.concatenate([cos, cos], -1), jnp.concatenate([-sin, sin], -1)


def _decay_tables(c):
    log_g = jnp.log1p(-jnp.exp2(-5.0 - jnp.arange(H_RET, dtype=F32)))
    idx = jnp.arange(c, dtype=F32)
    diff = idx[:, None] - idx[None, :]
    intra = jnp.where(diff[None] >= 0, jnp.exp(jnp.maximum(diff, 0.0)[None] * log_g[:, None, None]), 0.0)
    q_dec = jnp.exp((idx + 1.0)[None] * log_g[:, None])
    k_dec = jnp.exp((c - 1.0 - idx)[None] * log_g[:, None])
    c_dec = jnp.exp(c * log_g)
    return intra, q_dec, k_dec, c_dec


def _t5_bucket(dist):
    n = jnp.maximum(dist, 0)
    max_exact = NUM_BUCKETS // 2
    nf = jnp.maximum(n, 1).astype(F32)
    large = max_exact + (jnp.log(nf / max_exact) / math.log(REL_MAX_DIST / max_exact)
                         * (NUM_BUCKETS - max_exact)).astype(jnp.int32)
    large = jnp.minimum(large, NUM_BUCKETS - 1)
    return jnp.where(n < max_exact, n, large)


def _bias_table(rel_bias, bucket):
    hit = bucket[..., None] == jnp.arange(NUM_BUCKETS)
    per_head = rel_bias.astype(F32).T.reshape((H_SWA,) + (1,) * bucket.ndim + (NUM_BUCKETS,))
    return jnp.sum(jnp.where(hit[None], per_head, 0.0), axis=-1)


def kernel(x_prompt, x_sample, state_ret, cache_swa_k, cache_swa_v, w_in, w_out, ret_gn_gain, swa_sinks,
           rel_bias, ln1_g, ln1_b, ln2_g, ln2_b, router_w, router_b, w_gate_up, b_gate_up, w_down, b_down):
    x = jnp.concatenate([x_prompt.reshape(T_PROMPT, D_MODEL), x_sample.reshape(DEC_BATCH, D_MODEL)], axis=0)
    w_in_bf = w_in.astype(BF16)
    w_out_bf = w_out.astype(BF16)
    router_w_bf = jnp.swapaxes(router_w, 1, 2).astype(BF16)
    router_b = router_b.reshape(DEPTH, N_EXPERTS, 1)
    ret_gn_gain, ln1_g, ln1_b, ln2_g, ln2_b = (
        a.reshape(DEPTH, 1, a.shape[-1]) for a in (ret_gn_gain, ln1_g, ln1_b, ln2_g, ln2_b))
    b_gu4 = b_gate_up.reshape(DEPTH, N_EXPERTS, 1, 2 * D_FF)
    b_d4 = b_down.reshape(DEPTH, N_EXPERTS, 1, D_MODEL)
    cache_k = jnp.transpose(cache_swa_k, (0, 1, 3, 4, 2)).reshape(DEPTH, DEC_BATCH, 2 * HD_SWA, WINDOW)
    cache_v = jnp.transpose(cache_swa_v, (0, 1, 3, 4, 2)).reshape(DEPTH, DEC_BATCH, 2 * HD_SWA, WINDOW)

    cos_p, sin_p = _rope_tables(jnp.arange(SEQ, dtype=jnp.int32))
    cos_s, sin_s = _rope_tables(PAST_LEN + jnp.arange(1, dtype=jnp.int32))
    intra, q_dec, k_dec, c_dec = _decay_tables(RET_CHUNK)
    ones = jnp.ones((1, 1, HD_RET), F32)
    tables_p = (cos_p, sin_p, intra, q_dec[:, :, None] * ones, k_dec[:, :, None] * ones, c_dec[:, None, None] * ones)
    intra1, q_dec1, k_dec1, c_dec1 = _decay_tables(1)
    dec_s = jnp.stack([intra1[:, 0, 0], q_dec1[:, 0], k_dec1[:, 0], c_dec1], axis=1)
    i = jnp.arange(WINDOW)[:, None]
    j = jnp.arange(2 * WINDOW)[None, :]
    in_window = (WINDOW + i - j >= 0) & (WINDOW + i - j < WINDOW)
    has_prev = jnp.array([False, True])[:, None, None, None]
    allowed = in_window[None, None] & (has_prev | (j >= WINDOW)[None, None])
    bias_p = jnp.where(allowed, _bias_table(rel_bias, _t5_bucket(WINDOW + i - j))[None], -jnp.inf)
    js = jnp.arange(WINDOW + 1)
    bias_s = _bias_table(rel_bias, _t5_bucket(WINDOW - js))
    bias_c, bias_n = bias_s[:, :WINDOW], bias_s[:, WINDOW:]
    kv_of_head = jnp.arange(H_SWA) // G_SWA
    head_mask = (kv_of_head[:, None] == jnp.arange(KV_SWA)[None, :]).astype(F32)

    zero_rows = jnp.zeros((SC_DISPATCH_ROWS, D_MODEL // 2), jnp.uint32)

    sp = ss = ko = vo = tail = None
    h, h_kv = _in_proj(x, w_in_bf, 0)
    for l in range(DEPTH):
        o_r, sp, o_s, tail = _mix_prompt(h, h_kv, tables_p, ret_gn_gain, swa_sinks[l], bias_p, sp, tail, l)
        hs = h[T_PROMPT:].astype(F32)
        kvs = h_kv[T_PROMPT:]
        o_r, ss = _ret_sample(hs, dec_s, cos_s, sin_s, ret_gn_gain, state_ret, o_r, ss, l)
        q8 = hs[:, 4 * RET_W:].reshape(DEC_BATCH, H_SWA, HD_SWA)
        qm = (q8[:, :, None, :] * head_mask[None, :, :, None]).reshape(DEC_BATCH, H_SWA, 2 * HD_SWA)
        kn = kvs[:, :SWA_KV_W].reshape(DEC_BATCH, 1, SWA_KV_W)
        vn = kvs[:, SWA_KV_W:].reshape(DEC_BATCH, 1, SWA_KV_W)
        osm, ko, vo = _swa_sample(qm, kn, vn, cache_k, cache_v, bias_c, bias_n, swa_sinks[l][:, None], ko, vo, l)
        os_s = jnp.sum(osm.reshape(DEC_BATCH, H_SWA, KV_SWA, HD_SWA) * head_mask[None, :, :, None], axis=2)
        o_s = lax.dynamic_update_slice(o_s, os_s.reshape(DEC_BATCH, SWA_Q_W), (T_PROMPT, 0))

        x1, xpk, top_idx, gate, rank, counts = _out_proj(o_r, o_s, x, w_out_bf, ln1_g, ln1_b, router_w_bf,
                                                         router_b, l)
        pos_km, unused, group_start, group_blocks, group_next = _route(top_idx, rank, counts)
        xs = _sc_dispatch(xpk, pos_km, unused, zero_rows)
        ys = _moe(group_start, group_blocks, group_next, xs, w_gate_up, b_gu4, w_down, b_d4, l)
        y4 = _sc_collect(ys, pos_km.reshape(N_ASSIGN)).reshape(TOP_K, T_ALL, D_MODEL // 2)
        if l < DEPTH - 1:
            x, h, h_kv = _norm_proj(x1, y4, gate, ln2_g, ln2_b, w_in_bf, l)
        else:
            yp = _ffn_norm(x1, y4, gate, ln2_g, ln2_b, l, FINAL_ROW_BLOCK, 0, T_PROMPT // FINAL_ROW_BLOCK)
            ys_out = _ffn_norm(x1, y4, gate, ln2_g, ln2_b, l, DEC_BATCH, T_PROMPT // DEC_BATCH, 1)

    yp = yp.reshape(BATCH, SEQ, D_MODEL)
    ys_out = ys_out.reshape(DEC_BATCH, 1, D_MODEL)

    def window_major(t):
        return jnp.transpose(t.reshape(DEPTH, t.shape[1], KV_SWA, HD_SWA, WINDOW), (0, 1, 4, 2, 3))

    return (yp, ys_out, sp, window_major(tail[:, :, :SWA_KV_W]), window_major(tail[:, :, SWA_KV_W:]), ss,
            window_major(ko), window_major(vo))
```

```python
import functools
import math

import jax
import jax.numpy as jnp
from jax import lax
from jax.experimental import pallas as pl
from jax.experimental.pallas import tpu as pltpu
from jax.experimental.pallas import tpu_sc as plsc

F32 = jnp.float32
BF16 = jnp.bfloat16

D_MODEL = 1024
BATCH = 8
SEQ = 2048
DEPTH = 4
DEC_BATCH = 128
PAST_LEN = 8192
H_RET = 4
HD_RET = 128
RET_CHUNK = 128
ROPE_BASE = 10000.0
H_SWA = 8
KV_SWA = 2
G_SWA = H_SWA // KV_SWA
HD_SWA = 64
WINDOW = 128
NUM_BUCKETS = 32
REL_MAX_DIST = 128
RET_W = H_RET * HD_RET
SWA_Q_W = H_SWA * HD_SWA
SWA_KV_W = KV_SWA * HD_SWA
D_IN = 4 * RET_W + SWA_Q_W + 2 * SWA_KV_W
MIX_W = RET_W + SWA_Q_W
H_MAIN_W = 4 * RET_W + SWA_Q_W
N_EXPERTS = 32
TOP_K = 4
D_FF = D_MODEL
SWIGLU_LIMIT = 7.0
SWIGLU_ALPHA = 1.702
DEEPNORM_ALPHA = (2 * DEPTH) ** 0.25
LN_EPS = 1e-5
GN_EPS = 1e-5

T_PROMPT = BATCH * SEQ
T_ALL = T_PROMPT + DEC_BATCH
ROW_BLOCK = 384
SAMPLE_BLOCK = 8
MOE_ROWS = 128
MOE_CHUNK = 1024
MOE_PIECE = 1024
N_ASSIGN = T_ALL * TOP_K
MOE_BLOCKS = N_ASSIGN // MOE_ROWS + N_EXPERTS
MOE_SLOTS = MOE_BLOCKS * MOE_ROWS
N_UNUSED = MOE_SLOTS - N_ASSIGN
SC_CORES = 2
SC_WORKERS = SC_CORES * 16
SC_DISPATCH_ROWS = 64
SC_COLLECT_ROWS = 48
FINAL_ROW_BLOCK = 512
NORM_SUB_ROWS = 128
MIX_BLOCKS = 8
VMEM_LIMIT = 56 * 1024 * 1024


def _cparams(*sem):
    return pltpu.CompilerParams(dimension_semantics=sem, vmem_limit_bytes=VMEM_LIMIT)


def _project_in(x, w_ref, hm_ref, hkv_ref):
    h = jnp.dot(x.astype(BF16), w_ref[0], preferred_element_type=F32)
    hm_ref[...] = h[:, :H_MAIN_W].astype(BF16)
    hkv_ref[...] = h[:, H_MAIN_W:]


def _in_proj_kernel(x_ref, w_ref, hm_ref, hkv_ref):
    _project_in(x_ref[...], w_ref, hm_ref, hkv_ref)


_IN_PROJ_OUT_SPECS = [pl.BlockSpec((ROW_BLOCK, H_MAIN_W), lambda i: (i, 0)),
                      pl.BlockSpec((ROW_BLOCK, 2 * SWA_KV_W), lambda i: (i, 0))]
_IN_PROJ_OUT_SHAPES = [jax.ShapeDtypeStruct((T_ALL, H_MAIN_W), BF16),
                       jax.ShapeDtypeStruct((T_ALL, 2 * SWA_KV_W), F32)]


def _in_proj(x, w_in_bf, layer):
    return pl.pallas_call(
        _in_proj_kernel,
        grid=(T_ALL // ROW_BLOCK,),
        in_specs=[pl.BlockSpec((ROW_BLOCK, D_MODEL), lambda i: (i, 0)),
                  pl.BlockSpec((1, D_MODEL, D_IN), lambda i: (layer, 0, 0))],
        out_specs=_IN_PROJ_OUT_SPECS,
        out_shape=_IN_PROJ_OUT_SHAPES,
        compiler_params=_cparams("parallel"),
        name="in_proj",
    )(x, w_in_bf)


def _rotate(x, cos, sin_signed):
    return x * cos + pltpu.roll(x, HD_RET // 2, 1) * sin_signed


def _group_norm_gate(o, gain, g):
    mu = jnp.mean(o, -1, keepdims=True)
    var = jnp.mean(jnp.square(o - mu), -1, keepdims=True)
    return (o - mu) * lax.rsqrt(var + GN_EPS) * gain * (g * jax.nn.sigmoid(g))


def _layer_norm(y, g, b):
    mu = jnp.mean(y, -1, keepdims=True)
    var = jnp.mean(jnp.square(y - mu), -1, keepdims=True)
    return (y - mu) * lax.rsqrt(var + LN_EPS) * g + b


def _ret_prompt_stages(q_ref, k_ref, v_ref, g_ref, cos_ref, sin_ref, intra_ref, qdec_ref, kdec_ref,
                       cdec_ref, gain_ref, o_ref, s_ref, rows):
    cos = cos_ref[rows, :]
    sin = sin_ref[rows, :]
    nt = (((1,), (1,)), ((), ()))
    tn = (((0,), (0,)), ((), ()))
    heads = [slice(h * HD_RET, (h + 1) * HD_RET) for h in range(H_RET)]
    vs, atts, crosses = [], [], []
    for h, sl in enumerate(heads):
        q = _rotate(q_ref[rows, sl].astype(F32), cos, sin)
        k = _rotate(k_ref[rows, sl].astype(F32), cos, sin) * (HD_RET ** -0.5)
        v = v_ref[rows, sl]
        s = s_ref[0, 0, h]
        atts.append(lax.dot_general(q.astype(BF16), k.astype(BF16), nt, preferred_element_type=F32))
        crosses.append(jnp.dot((q * qdec_ref[h]).astype(BF16), s.astype(BF16), preferred_element_type=F32))
        s_ref[0, 0, h] = s * cdec_ref[h] + lax.dot_general((k * kdec_ref[h]).astype(BF16), v, tn,
                                                          preferred_element_type=F32)
        vs.append(v)
    yield
    atts = [(att * intra_ref[h]).astype(BF16) for h, att in enumerate(atts)]
    yield
    outs = [jnp.dot(atts[h], vs[h], preferred_element_type=F32) + crosses[h] for h in range(H_RET)]
    yield
    for h, sl in enumerate(heads):
        o_ref[rows, sl] = _group_norm_gate(outs[h], gain_ref[0, :, sl], g_ref[rows, sl].astype(F32))


def _swa_prompt_stages(sink_ref, q_ref, kv_prev, kv_cur, bias, o_ref, rows, lg_scr, p_scr):
    kv = jnp.concatenate([kv_prev, kv_cur], axis=0)
    k_t = kv[:, :SWA_KV_W].T
    k_t_swapped = pltpu.roll(k_t, HD_SWA, 0)
    v = kv[:, SWA_KV_W:]
    v_swapped = pltpu.roll(v, HD_SWA, 1)
    k_row = lax.broadcasted_iota(jnp.int32, k_t.shape, 0)
    v_lane = lax.broadcasted_iota(jnp.int32, v.shape, 1)
    ones = jnp.ones(v.shape, BF16)
    k_pad, v_pad = {}, {}
    for kvh in range(KV_SWA):
        for par in range(2):
            k_src = k_t if kvh == par else k_t_swapped
            v_src = v if kvh == par else v_swapped
            k_pad[kvh, par] = jnp.where((k_row >= par * HD_SWA) & (k_row < (par + 1) * HD_SWA), k_src, 0.0).astype(BF16)
            v_pad[kvh, par] = jnp.where((v_lane >= par * HD_SWA) & (v_lane < (par + 1) * HD_SWA), v_src, 0.0).astype(BF16)
    for hd in range(H_SWA):
        pair, par, kvh = hd // 2, hd % 2, hd // G_SWA
        q2 = (q_ref[rows, pair * 2 * HD_SWA:(pair + 1) * 2 * HD_SWA] * (HD_SWA ** -0.5)).astype(BF16)
        lg_scr[hd] = jnp.dot(q2, k_pad[kvh, par], preferred_element_type=F32) + bias(hd)
    yield
    sink_terms = []
    for hd in range(H_SWA):
        logits = lg_scr[hd]
        sink = sink_ref[hd]
        m = jnp.maximum(jnp.max(logits, -1, keepdims=True), sink)
        p_scr[hd] = jnp.exp(logits - m).astype(BF16)
        sink_terms.append(jnp.exp(sink - m))
    yield
    for pair in range(H_SWA // 2):
        acc = None
        for par in range(2):
            hd = 2 * pair + par
            p = p_scr[hd]
            den = jnp.dot(p, ones, preferred_element_type=F32) + sink_terms[hd]
            part = jnp.dot(p, v_pad[hd // G_SWA, par], preferred_element_type=F32) / den
            acc = part if acc is None else acc + part
        o_ref[rows, pair * 2 * HD_SWA:(pair + 1) * 2 * HD_SWA] = acc


def _mix_prompt_kernel(*refs):
    n_ret_in = 11
    ret_in = refs[:n_ret_in]
    sink_ref, q_ref, kvc_ref, kvp_ref, bias_ref = refs[n_ret_in:n_ret_in + 5]
    o_ref, s_ref, tail_ref, lg_scr, p_scr = refs[-5:]
    o_r_ref = o_ref.at[:, pl.ds(0, RET_W)]
    o_s_ref = o_ref.at[:, pl.ds(RET_W, SWA_Q_W)]
    c = pl.program_id(1)

    @pl.when(c == 0)
    def _():
        s_ref[...] = jnp.zeros_like(s_ref)

    @pl.when(c == pl.num_programs(1) - 1)
    def _():
        tail_ref[0, 0] = kvc_ref[pl.ds((MIX_BLOCKS - 1) * WINDOW, WINDOW), :].T

    groups = []
    for j in range(MIX_BLOCKS):
        rows = pl.ds(j * WINDOW, WINDOW)
        kv_cur = kvc_ref[rows, :]
        if j == 0:
            kv_prev = kvp_ref[...]
            table = jnp.minimum(c, 1)
        else:
            kv_prev = kvc_ref[pl.ds((j - 1) * WINDOW, WINDOW), :]
            table = 1
        groups.append(_ret_prompt_stages(*ret_in, o_r_ref, s_ref, rows))

        def bias(hd, table=table):
            return bias_ref[table, hd]

        groups.append(_swa_prompt_stages(sink_ref, q_ref, kv_prev, kv_cur, bias, o_s_ref, rows,
                                         lg_scr.at[j], p_scr.at[j]))
    while groups:
        for g in list(groups):
            if next(g, groups) is groups:
                groups.remove(g)


def _mix_prompt(h, h_kv, tables, gain, sinks_l, bias, s_prev, tail_prev, layer):
    cos, sin, intra, qdec, kdec, cdec = tables
    nb = SEQ // WINDOW
    ns = nb // MIX_BLOCKS
    rows = MIX_BLOCKS * WINDOW
    qcol = 4 * RET_W // SWA_Q_W

    def col(j):
        return pl.BlockSpec((rows, RET_W), lambda b, c: (b * ns + c, j))

    def whole(a):
        return pl.BlockSpec(a.shape, lambda b, c: (0,) * a.ndim)

    in_specs = [col(0), col(1), col(2), col(3),
                pl.BlockSpec((rows, HD_RET), lambda b, c: (c, 0)),
                pl.BlockSpec((rows, HD_RET), lambda b, c: (c, 0)),
                whole(intra), whole(qdec), whole(kdec), whole(cdec),
                pl.BlockSpec((1, 1, RET_W), lambda b, c: (layer, 0, 0)),
                pl.BlockSpec(memory_space=pltpu.SMEM),
                pl.BlockSpec((rows, SWA_Q_W), lambda b, c: (b * ns + c, qcol)),
                pl.BlockSpec((rows, 2 * SWA_KV_W), lambda b, c: (b * ns + c, 0)),
                pl.BlockSpec((WINDOW, 2 * SWA_KV_W), lambda b, c: (b * nb + jnp.maximum(MIX_BLOCKS * c - 1, 0), 0)),
                whole(bias)]
    args = [h, h, h, h, cos, sin, intra, qdec, kdec, cdec, gain, sinks_l, h, h_kv, h_kv, bias]
    aliases = {}
    if s_prev is not None:
        in_specs += [pl.BlockSpec(memory_space=pl.ANY), pl.BlockSpec(memory_space=pl.ANY)]
        args += [s_prev, tail_prev]
        aliases = {16: 1, 17: 2}

    return pl.pallas_call(
        _mix_prompt_kernel,
        grid=(BATCH, ns),
        in_specs=in_specs,
        scratch_shapes=[pltpu.VMEM((MIX_BLOCKS, H_SWA, WINDOW, 2 * WINDOW), F32),
                        pltpu.VMEM((MIX_BLOCKS, H_SWA, WINDOW, 2 * WINDOW), BF16)],
        out_specs=[pl.BlockSpec((rows, MIX_W), lambda b, c: (b * ns + c, 0)),
                   pl.BlockSpec((1, 1, H_RET, HD_RET, HD_RET), lambda b, c: (layer, b, 0, 0, 0)),
                   pl.BlockSpec((1, 1, 2 * SWA_KV_W, WINDOW), lambda b, c: (layer, b, 0, 0))],
        out_shape=[jax.ShapeDtypeStruct((T_ALL, MIX_W), F32),
                   jax.ShapeDtypeStruct((DEPTH, BATCH, H_RET, HD_RET, HD_RET), F32),
                   jax.ShapeDtypeStruct((DEPTH, BATCH, 2 * SWA_KV_W, WINDOW), F32)],
        input_output_aliases=aliases,
        compiler_params=_cparams("parallel", "arbitrary"),
        name="mix_prompt",
    )(*args)


def _ret_sample_kernel(dec_ref, q_ref, k_ref, v_ref, g_ref, cos_ref, sin_ref, gain_ref, st_ref, o_ref, so_ref):
    cos = cos_ref[...]
    sin = sin_ref[...]
    for h in range(H_RET):
        sl = slice(h * HD_RET, (h + 1) * HD_RET)
        intra, qdec, kdec, cdec = dec_ref[h, 0], dec_ref[h, 1], dec_ref[h, 2], dec_ref[h, 3]
        q = _rotate(q_ref[:, sl], cos, sin)
        k = _rotate(k_ref[:, sl], cos, sin) * (HD_RET ** -0.5)
        v = v_ref[:, sl]
        att = jnp.sum(q * k, -1, keepdims=True) * intra
        q_col = (q * qdec).T
        k_col = (k * kdec).T
        rows = []
        for b in range(SAMPLE_BLOCK):
            s = st_ref[0, b, h]
            v_row = v[b:b + 1, :]
            rows.append(att[b:b + 1, :] * v_row + jnp.sum(q_col[:, b:b + 1] * s, axis=0, keepdims=True))
            so_ref[0, b, h] = s * cdec + k_col[:, b:b + 1] * v_row
        o = jnp.concatenate(rows, axis=0)
        o_ref[:, sl] = _group_norm_gate(o, gain_ref[0, :, sl], g_ref[:, sl])


def _ret_sample(h, dec_s, cos_s, sin_s, gain, state_ret, o_r, st_prev, layer):
    base = T_PROMPT // SAMPLE_BLOCK

    def col(j):
        return pl.BlockSpec((SAMPLE_BLOCK, RET_W), lambda i: (i, j))

    in_specs = [pl.BlockSpec(memory_space=pltpu.SMEM),
                col(0), col(1), col(2), col(3),
                pl.BlockSpec((1, HD_RET), lambda i: (0, 0)),
                pl.BlockSpec((1, HD_RET), lambda i: (0, 0)),
                pl.BlockSpec((1, 1, RET_W), lambda i: (layer, 0, 0)),
                pl.BlockSpec((1, SAMPLE_BLOCK, H_RET, HD_RET, HD_RET), lambda i: (layer, i, 0, 0, 0)),
                pl.BlockSpec(memory_space=pl.ANY)]
    args = [dec_s, h, h, h, h, cos_s, sin_s, gain, state_ret, o_r]
    aliases = {9: 0}
    if st_prev is not None:
        in_specs.append(pl.BlockSpec(memory_space=pl.ANY))
        args.append(st_prev)
        aliases[10] = 1

    def body(*refs):
        _ret_sample_kernel(*refs[:9], *refs[-2:])

    return pl.pallas_call(
        body,
        grid=(DEC_BATCH // SAMPLE_BLOCK,),
        in_specs=in_specs,
        out_specs=[pl.BlockSpec((SAMPLE_BLOCK, RET_W), lambda i: (base + i, 0)),
                   pl.BlockSpec((1, SAMPLE_BLOCK, H_RET, HD_RET, HD_RET), lambda i: (layer, i, 0, 0, 0))],
        out_shape=[jax.ShapeDtypeStruct((T_ALL, MIX_W), F32),
                   jax.ShapeDtypeStruct((DEPTH, DEC_BATCH, H_RET, HD_RET, HD_RET), F32)],
        input_output_aliases=aliases,
        compiler_params=_cparams("parallel"),
        name="ret_sample",
    )(*args)


def _swa_sample_kernel(qm_ref, kn_ref, vn_ref, knc_ref, vnc_ref, kc_ref, vc_ref, bias_ref, biasn_ref, sink_ref,
                       o_ref, ko_ref, vo_ref):
    qm = qm_ref[...]
    kn = kn_ref[...]
    vn = vn_ref[...]
    kc = kc_ref[0]
    vc = vc_ref[0]
    scale = HD_SWA ** -0.5
    logits = jnp.einsum('bhd,bdj->bhj', qm.astype(BF16), kc.astype(BF16),
                        preferred_element_type=F32) * scale + bias_ref[...][None]
    j = lax.broadcasted_iota(jnp.int32, logits.shape, 2)
    logits = jnp.where(j >= 1, logits, -jnp.inf)
    ln = jnp.sum(qm * kn, -1, keepdims=True) * scale + biasn_ref[...][None]
    sink = sink_ref[...][None]
    m = jnp.maximum(jnp.maximum(jnp.max(logits, -1, keepdims=True), ln), sink)
    p = jnp.exp(logits - m)
    pn = jnp.exp(ln - m)
    denom = jnp.sum(p, -1, keepdims=True) + pn + jnp.exp(sink - m)
    w = p / denom
    o_ref[...] = (jnp.einsum('bhj,bdj->bhd', w.astype(BF16), vc.astype(BF16), preferred_element_type=F32)
                  + (pn / denom) * vn)
    last = lax.broadcasted_iota(jnp.int32, kc.shape, 2) == WINDOW - 1
    ko_ref[0] = jnp.where(last, knc_ref[...], pltpu.roll(kc, WINDOW - 1, 2))
    vo_ref[0] = jnp.where(last, vnc_ref[...], pltpu.roll(vc, WINDOW - 1, 2))


def _swa_sample(qm, kn, vn, cache_k, cache_v, bias_c, bias_n, sink_col, ko_prev, vo_prev, layer):
    sb = SAMPLE_BLOCK
    cache_spec = pl.BlockSpec((1, sb, 2 * HD_SWA, WINDOW), lambda i: (layer, i, 0, 0))
    row_spec = pl.BlockSpec((sb, 1, 2 * HD_SWA), lambda i: (i, 0, 0))
    col_spec = pl.BlockSpec((sb, 2 * HD_SWA, 1), lambda i: (i, 0, 0))
    in_specs = [pl.BlockSpec((sb, H_SWA, 2 * HD_SWA), lambda i: (i, 0, 0)),
                row_spec, row_spec, col_spec, col_spec,
                cache_spec, cache_spec,
                pl.BlockSpec((H_SWA, WINDOW), lambda i: (0, 0)),
                pl.BlockSpec((H_SWA, 1), lambda i: (0, 0)),
                pl.BlockSpec((H_SWA, 1), lambda i: (0, 0))]
    args = [qm, kn, vn, kn.reshape(DEC_BATCH, 2 * HD_SWA, 1), vn.reshape(DEC_BATCH, 2 * HD_SWA, 1),
            cache_k, cache_v, bias_c, bias_n, sink_col]
    aliases = {}
    if ko_prev is not None:
        in_specs += [pl.BlockSpec(memory_space=pl.ANY), pl.BlockSpec(memory_space=pl.ANY)]
        args += [ko_prev, vo_prev]
        aliases = {10: 1, 11: 2}

    def body(*refs):
        _swa_sample_kernel(*refs[:10], *refs[-3:])

    cshape = jax.ShapeDtypeStruct((DEPTH, DEC_BATCH, 2 * HD_SWA, WINDOW), F32)
    return pl.pallas_call(
        body,
        grid=(DEC_BATCH // sb,),
        in_specs=in_specs,
        out_specs=[pl.BlockSpec((sb, H_SWA, 2 * HD_SWA), lambda i: (i, 0, 0)), cache_spec, cache_spec],
        out_shape=[jax.ShapeDtypeStruct((DEC_BATCH, H_SWA, 2 * HD_SWA), F32), cshape, cshape],
        input_output_aliases=aliases,
        compiler_params=_cparams("parallel"),
        name="swa_sample",
    )(*args)


def _pack_bf16_pairs(x):
    w = x.shape[1] // 2
    lo = lax.bitcast_convert_type(x[:, :w].astype(BF16).astype(F32), jnp.uint32) >> 16
    hi = lax.bitcast_convert_type(x[:, w:].astype(BF16).astype(F32), jnp.uint32)
    return lo | hi


def _unpack_bf16_pairs(p):
    lo = lax.bitcast_convert_type(p << 16, F32).astype(BF16)
    hi = lax.bitcast_convert_type(p & jnp.uint32(0xFFFF0000), F32).astype(BF16)
    return lo, hi


def _out_proj_kernel(o_ref, x_ref, w_ref, g_ref, b_ref, rw_ref, rb_ref,
                     x1_ref, xpk_ref, idx_ref, gate_ref, rank_ref, cnt_ref):
    nt = (((1,), (1,)), ((), ()))
    mix = jnp.dot(o_ref[...].astype(BF16), w_ref[0], preferred_element_type=F32)
    x1 = _layer_norm(DEEPNORM_ALPHA * x_ref[...] + mix, g_ref[0], b_ref[0])
    x1_ref[...] = x1
    xpk_ref[...] = _pack_bf16_pairs(x1)
    logits = lax.dot_general(rw_ref[0], x1.astype(BF16), nt, preferred_element_type=F32) + rb_ref[0]
    rows = logits.shape[1]
    expert = lax.broadcasted_iota(jnp.int32, logits.shape, 0)
    vals, hits = [], []
    for kk in range(TOP_K):
        m = jnp.max(logits, 0, keepdims=True)
        idx = jnp.min(jnp.where(logits == m, expert, N_EXPERTS), 0, keepdims=True)
        vals.append(m)
        hits.append(expert == idx)
        idx_ref[kk:kk + 1, :] = idx
        logits = jnp.where(expert == idx, -jnp.inf, logits)
    es = [jnp.exp(v - vals[0]) for v in vals]
    tot = es[0] + es[1] + es[2] + es[3]
    gate_ref[...] = jnp.concatenate([e / tot for e in es] + [jnp.zeros((8 - TOP_K, rows), F32)], axis=0)
    chosen = sum(h.astype(F32) for h in hits)
    earlier = (lax.broadcasted_iota(jnp.int32, (rows, rows), 0)
               < lax.broadcasted_iota(jnp.int32, (rows, rows), 1)).astype(BF16)
    before = jnp.dot(chosen.astype(BF16), earlier, preferred_element_type=F32)
    for kk in range(TOP_K):
        rank_ref[kk:kk + 1, :] = jnp.sum(jnp.where(hits[kk], before, 0.0), 0, keepdims=True).astype(jnp.int32)
    cnt_ref[0] = jnp.sum(chosen, axis=1, keepdims=True).astype(jnp.int32)


def _out_proj(o, x, w_out_bf, ln_g, ln_b, router_w_bf, router_b, layer):
    def rows(w):
        return pl.BlockSpec((ROW_BLOCK, w), lambda i: (i, 0))

    def vec(w):
        return pl.BlockSpec((1, 1, w), lambda i: (layer, 0, 0))

    def cols(h):
        return pl.BlockSpec((h, ROW_BLOCK), lambda i: (0, i))

    return pl.pallas_call(
        _out_proj_kernel,
        grid=(T_ALL // ROW_BLOCK,),
        in_specs=[rows(MIX_W), rows(D_MODEL),
                  pl.BlockSpec((1, MIX_W, D_MODEL), lambda i: (layer, 0, 0)),
                  vec(D_MODEL), vec(D_MODEL),
                  pl.BlockSpec((1, N_EXPERTS, D_MODEL), lambda i: (layer, 0, 0)),
                  pl.BlockSpec((1, N_EXPERTS, 1), lambda i: (layer, 0, 0))],
        out_specs=[rows(D_MODEL), rows(D_MODEL // 2), cols(TOP_K), cols(8), cols(TOP_K),
                   pl.BlockSpec((1, N_EXPERTS, 1), lambda i: (i, 0, 0))],
        out_shape=[jax.ShapeDtypeStruct((T_ALL, D_MODEL), F32),
                   jax.ShapeDtypeStruct((T_ALL, D_MODEL // 2), jnp.uint32),
                   jax.ShapeDtypeStruct((TOP_K, T_ALL), jnp.int32),
                   jax.ShapeDtypeStruct((8, T_ALL), F32),
                   jax.ShapeDtypeStruct((TOP_K, T_ALL), jnp.int32),
                   jax.ShapeDtypeStruct((T_ALL // ROW_BLOCK, N_EXPERTS, 1), jnp.int32)],
        compiler_params=_cparams("parallel"),
        name="out_proj",
    )(o, x, w_out_bf, ln_g, ln_b, router_w_bf, router_b)


def _moe_kernel(start_ref, nblk_ref, next_ref, xs_hbm, wgu_ref, bgu_ref, wd_ref, bd_ref, ys_hbm,
                wgu_bf, wd_bf, x_buf, y_buf, in_sems, out_sems, pending, ready):
    e = pl.program_id(0)
    per_chunk = MOE_CHUNK // MOE_ROWS
    tail_sizes = [t * MOE_ROWS for t in range(1, per_chunk)]
    nblk = nblk_ref[e]
    n_full = nblk // per_chunk
    tail_blocks = nblk % per_chunk
    base = start_ref[e]

    def in_copy(row0, rows, slot):
        src = xs_hbm.at[pl.ds(pl.multiple_of(row0, MOE_ROWS), rows)]
        return pltpu.make_async_copy(src, x_buf.at[slot, pl.ds(0, rows)], in_sems.at[slot])

    def out_copy(row0, rows, slot):
        dst = ys_hbm.at[pl.ds(pl.multiple_of(row0, MOE_ROWS), rows)]
        return pltpu.make_async_copy(y_buf.at[slot, pl.ds(0, rows)], dst, out_sems.at[slot])

    def start_first_chunk(expert, slot):
        @pl.when(nblk_ref[expert] >= per_chunk)
        def _():
            in_copy(start_ref[expert], MOE_CHUNK, slot).start()

        for rows in tail_sizes:
            @pl.when(nblk_ref[expert] == rows // MOE_ROWS)
            def _():
                in_copy(start_ref[expert], rows, slot).start()

    def prefetch_next_expert(slot):
        nxt = next_ref[e]

        @pl.when(nxt < N_EXPERTS)
        def _():
            start_first_chunk(nxt, slot)
            ready[0] = slot

    def drain(slot):
        for rows in [MOE_CHUNK] + tail_sizes:
            @pl.when(pending[slot] == rows)
            def _():
                out_copy(0, rows, slot).wait()
        pending[slot] = 0

    def compute(rows, slot):
        for r0 in range(0, rows, MOE_PIECE):
            piece = pl.ds(r0, min(MOE_PIECE, rows - r0))
            x = jnp.concatenate(_unpack_bf16_pairs(x_buf[slot, piece, :]), axis=1)
            gu = jnp.dot(x, wgu_bf[...], preferred_element_type=F32) + bgu_ref[0, 0]
            glu = jnp.minimum(gu[:, :D_FF], SWIGLU_LIMIT)
            lin = jnp.clip(gu[:, D_FF:], -SWIGLU_LIMIT, SWIGLU_LIMIT)
            act = glu * jax.nn.sigmoid(SWIGLU_ALPHA * glu) * (lin + 1.0)
            y = jnp.dot(act.astype(BF16), wd_bf[...], preferred_element_type=F32) + bd_ref[0, 0]
            y_buf[slot, piece, :] = _pack_bf16_pairs(y)

    def run_chunk(row0, rows, slot):
        in_copy(row0, rows, slot).wait()
        drain(slot)
        compute(rows, slot)
        out_copy(row0, rows, slot).start()
        pending[slot] = rows

    @pl.when(e == 0)
    def _():
        pending[0] = 0
        pending[1] = 0
        ready[0] = -1

    @pl.when(nblk > 0)
    def _():
        @pl.when(ready[0] < 0)
        def _():
            start_first_chunk(e, 0)
            ready[0] = 0

        first_slot = ready[0]
        wgu_bf[...] = wgu_ref[0, 0].astype(BF16)
        wd_bf[...] = wd_ref[0, 0].astype(BF16)

        def body(c, carry):
            slot = (first_slot + c) & 1
            row0 = pl.multiple_of(base + c * MOE_CHUNK, MOE_ROWS)

            @pl.when(c + 1 < n_full)
            def _():
                in_copy(row0 + MOE_CHUNK, MOE_CHUNK, 1 - slot).start()

            for rows in tail_sizes:
                @pl.when(jnp.logical_and(c + 1 == n_full, tail_blocks == rows // MOE_ROWS))
                def _():
                    in_copy(row0 + MOE_CHUNK, rows, 1 - slot).start()

            @pl.when(jnp.logical_and(c + 1 == n_full, tail_blocks == 0))
            def _():
                prefetch_next_expert(1 - slot)

            run_chunk(row0, MOE_CHUNK, slot)
            return carry

        lax.fori_loop(0, n_full, body, 0)

        for rows in tail_sizes:
            @pl.when(tail_blocks == rows // MOE_ROWS)
            def _():
                slot = (first_slot + n_full) & 1
                prefetch_next_expert(1 - slot)
                run_chunk(pl.multiple_of(base + n_full * MOE_CHUNK, MOE_ROWS), rows, slot)

    @pl.when(e == pl.num_programs(0) - 1)
    def _():
        drain(0)
        drain(1)


def _moe(group_start, group_blocks, group_next, xs, w_gate_up, b_gate_up, w_down, b_down, layer):
    def wsel(e, st, nb, nx):
        return (layer, e, 0, 0)

    return pl.pallas_call(
        _moe_kernel,
        grid_spec=pltpu.PrefetchScalarGridSpec(
            num_scalar_prefetch=3,
            grid=(N_EXPERTS,),
            in_specs=[pl.BlockSpec(memory_space=pl.ANY),
                      pl.BlockSpec((1, 1, D_MODEL, 2 * D_FF), wsel),
                      pl.BlockSpec((1, 1, 1, 2 * D_FF), wsel),
                      pl.BlockSpec((1, 1, D_FF, D_MODEL), wsel),
                      pl.BlockSpec((1, 1, 1, D_MODEL), wsel)],
            out_specs=pl.BlockSpec(memory_space=pl.ANY),
            scratch_shapes=[pltpu.VMEM((D_MODEL, 2 * D_FF), BF16), pltpu.VMEM((D_FF, D_MODEL), BF16),
                            pltpu.VMEM((2, MOE_CHUNK, D_MODEL // 2), jnp.uint32),
                            pltpu.VMEM((2, MOE_CHUNK, D_MODEL // 2), jnp.uint32),
                            pltpu.SemaphoreType.DMA((2,)), pltpu.SemaphoreType.DMA((2,)),
                            pltpu.SMEM((2,), jnp.int32), pltpu.SMEM((1,), jnp.int32)]),
        out_shape=jax.ShapeDtypeStruct((MOE_SLOTS, D_MODEL // 2), jnp.uint32),
        compiler_params=_cparams("arbitrary"),
        name="moe_experts",
    )(group_start, group_blocks, group_next, xs, w_gate_up, b_gate_up, w_down, b_down)


def _route(idx, rank, counts):
    experts = jnp.arange(N_EXPERTS, dtype=jnp.int32)
    counts = counts.reshape(T_ALL // ROW_BLOCK, N_EXPERTS)
    before_block = jnp.cumsum(counts, axis=0) - counts
    total = jnp.sum(counts, axis=0)
    padded = (total + MOE_ROWS - 1) // MOE_ROWS * MOE_ROWS
    pad_end = jnp.cumsum(padded)
    pad_start = pad_end - padded
    base = jnp.repeat((pad_start[None, :] + before_block).T, ROW_BLOCK, axis=1)
    pos = jnp.sum(jnp.where(idx[:, None, :] == experts[None, :, None], base[None], 0), axis=1) + rank
    gap = padded - total
    gap_end = jnp.concatenate([jnp.cumsum(gap), jnp.full((1,), N_UNUSED, jnp.int32)])
    gap_start = jnp.concatenate([jnp.zeros((1,), jnp.int32), gap_end[:-1]])
    first_free = jnp.concatenate([pad_start + total, pad_end[-1:]])
    j = jnp.arange(N_UNUSED, dtype=jnp.int32)
    region = (j[:, None] >= gap_start[None, :]) & (j[:, None] < gap_end[None, :])
    unused = jnp.sum(jnp.where(region, first_free[None, :] + j[:, None] - gap_start[None, :], 0), axis=-1)
    nonempty = jnp.where(total > 0, experts, N_EXPERTS)
    following = jnp.concatenate([lax.cummin(nonempty, reverse=True)[1:], jnp.full((1,), N_EXPERTS, jnp.int32)])
    return (pos.astype(jnp.int32), unused.astype(jnp.int32),
            pad_start.astype(jnp.int32), (padded // MOE_ROWS).astype(jnp.int32), following.astype(jnp.int32))


def _sc_worker():
    return lax.axis_index("s") * SC_CORES + lax.axis_index("c")


def _sc_dispatch(xpk, pos_km, unused, zero_rows):
    ch = SC_DISPATCH_ROWS
    n_chunks = T_ALL // ch
    n_zero_chunks = N_UNUSED // ch
    width = xpk.shape[1]
    mesh = plsc.VectorSubcoreMesh(core_axis_name="c", subcore_axis_name="s")

    @functools.partial(
        pl.kernel, mesh=mesh,
        out_type=jax.ShapeDtypeStruct((MOE_SLOTS, width), xpk.dtype),
        scratch_types=[pltpu.VMEM((2, ch, width), xpk.dtype), pltpu.VMEM((2, TOP_K, ch), jnp.int32),
                       pltpu.SemaphoreType.DMA((2,)), pltpu.SemaphoreType.DMA((TOP_K,))],
        name="sc_dispatch")
    def run(x_hbm, pos_hbm, unused_hbm, zero_hbm, xs_hbm, rows_v, idx_v, load_sems, store_sems):
        wid = _sc_worker()
        pltpu.sync_copy(zero_hbm, rows_v.at[0])
        for r in range(n_zero_chunks // SC_WORKERS):
            start = pl.multiple_of((wid + r * SC_WORKERS) * ch, ch)
            pltpu.sync_copy(unused_hbm.at[pl.ds(start, ch)], idx_v.at[0, 0])
            pltpu.sync_copy(rows_v.at[0], xs_hbm.at[idx_v.at[0, 0]])

        def first_row(r):
            return pl.multiple_of((wid + r * SC_WORKERS) * ch, ch)

        def rows_copy(r, slot):
            return pltpu.make_async_copy(x_hbm.at[pl.ds(first_row(r), ch)], rows_v.at[slot], load_sems.at[slot])

        def start_load(r, slot):
            for k in range(TOP_K):
                pltpu.sync_copy(pos_hbm.at[k, pl.ds(first_row(r), ch)], idx_v.at[slot, k])
            rows_copy(r, slot).start()

        def has_chunk(r):
            return wid + r * SC_WORKERS < n_chunks

        start_load(0, 0)

        def body(r, carry):
            slot = r & 1

            @pl.when(has_chunk(r + 1))
            def _():
                start_load(r + 1, 1 - slot)

            @pl.when(has_chunk(r))
            def _():
                rows_copy(r, slot).wait()
                copies = [pltpu.async_copy(rows_v.at[slot], xs_hbm.at[idx_v.at[slot, k]], store_sems.at[k])
                          for k in range(TOP_K)]
                for cp in copies:
                    cp.wait()
            return carry

        lax.fori_loop(0, pl.cdiv(n_chunks, SC_WORKERS), body, 0)

    return run(xpk, pos_km, unused, zero_rows)


def _sc_collect(ys, pos_flat):
    ch = SC_COLLECT_ROWS
    per_worker = N_ASSIGN // SC_WORKERS
    mesh = plsc.VectorSubcoreMesh(core_axis_name="c", subcore_axis_name="s")

    @functools.partial(
        pl.kernel, mesh=mesh,
        out_type=jax.ShapeDtypeStruct((N_ASSIGN, ys.shape[1]), ys.dtype),
        scratch_types=[pltpu.VMEM((2, ch, ys.shape[1]), ys.dtype), pltpu.VMEM((2, ch), jnp.int32),
                       pltpu.SemaphoreType.DMA((2,))],
        name="sc_collect")
    def run(ys_hbm, pos_hbm, y4_hbm, rows_v, idx_v, sems):
        wid = _sc_worker()
        n_chunks = per_worker // ch

        def first_row(r):
            return pl.multiple_of(wid * per_worker + r * ch, 8)

        def gather(slot):
            return pltpu.make_async_copy(ys_hbm.at[idx_v.at[slot]], rows_v.at[slot], sems.at[slot])

        def start_gather(r, slot):
            pltpu.sync_copy(pos_hbm.at[pl.ds(first_row(r), ch)], idx_v.at[slot])
            gather(slot).start()

        start_gather(0, 0)

        def body(r, carry):
            slot = r & 1

            @pl.when(r + 1 < n_chunks)
            def _():
                start_gather(r + 1, 1 - slot)

            gather(slot).wait()
            pltpu.sync_copy(rows_v.at[slot], y4_hbm.at[pl.ds(first_row(r), ch)])
            return carry

        lax.fori_loop(0, n_chunks, body, 0)

    return run(ys, pos_flat)


def _combine_norm(x_ref, y_refs, gate_ref, g_ref, b_ref, rows=slice(None)):
    gate = gate_ref[:, rows].T
    lo = hi = None
    for k, y_ref in enumerate(y_refs):
        p = y_ref[0, rows, :]
        g = gate[:, k:k + 1]
        lo_k = g * lax.bitcast_convert_type(p << 16, F32)
        hi_k = g * lax.bitcast_convert_type(p & jnp.uint32(0xFFFF0000), F32)
        lo = lo_k if lo is None else lo + lo_k
        hi = hi_k if hi is None else hi + hi_k
    ffn = jnp.concatenate([lo, hi], axis=1)
    return _layer_norm(DEEPNORM_ALPHA * x_ref[rows, :] + ffn, g_ref[0], b_ref[0])


def _ffn_norm_kernel(x_ref, y0_ref, y1_ref, y2_ref, y3_ref, gate_ref, g_ref, b_ref, o_ref):
    o_ref[...] = _combine_norm(x_ref, (y0_ref, y1_ref, y2_ref, y3_ref), gate_ref, g_ref, b_ref)


def _norm_proj_kernel(x_ref, y0_ref, y1_ref, y2_ref, y3_ref, gate_ref, g_ref, b_ref, w_ref,
                      x2_ref, hm_ref, hkv_ref):
    for r in range(ROW_BLOCK // NORM_SUB_ROWS):
        rows = pl.ds(r * NORM_SUB_ROWS, NORM_SUB_ROWS)
        x2 = _combine_norm(x_ref, (y0_ref, y1_ref, y2_ref, y3_ref), gate_ref, g_ref, b_ref, rows)
        x2_ref[rows, :] = x2
        _project_in(x2, w_ref, hm_ref.at[rows, :], hkv_ref.at[rows, :])


def _norm_proj(x1, y4, gate, ln_g, ln_b, w_in_bf, layer):
    rows = pl.BlockSpec((ROW_BLOCK, D_MODEL), lambda i: (i, 0))

    def plane(k):
        return pl.BlockSpec((1, ROW_BLOCK, D_MODEL // 2), lambda i: (k, i, 0))

    vec = pl.BlockSpec((1, 1, D_MODEL), lambda i: (layer, 0, 0))
    return pl.pallas_call(
        _norm_proj_kernel,
        grid=(T_ALL // ROW_BLOCK,),
        in_specs=[rows, plane(0), plane(1), plane(2), plane(3),
                  pl.BlockSpec((8, ROW_BLOCK), lambda i: (0, i)), vec, vec,
                  pl.BlockSpec((1, D_MODEL, D_IN), lambda i: (layer + 1, 0, 0))],
        out_specs=[rows] + _IN_PROJ_OUT_SPECS,
        out_shape=[jax.ShapeDtypeStruct((T_ALL, D_MODEL), F32)] + _IN_PROJ_OUT_SHAPES,
        compiler_params=_cparams("parallel"),
        name="norm_proj",
    )(x1, y4, y4, y4, y4, gate, ln_g, ln_b, w_in_bf)


def _ffn_norm(x1, y4, gate, ln_g, ln_b, layer, row_block, first_block, n_blocks):
    def rows(w):
        return pl.BlockSpec((row_block, w), lambda i: (first_block + i, 0))

    def plane(k):
        return pl.BlockSpec((1, row_block, D_MODEL // 2), lambda i: (k, first_block + i, 0))

    vec = pl.BlockSpec((1, 1, D_MODEL), lambda i: (layer, 0, 0))
    return pl.pallas_call(
        _ffn_norm_kernel,
        grid=(n_blocks,),
        in_specs=[rows(D_MODEL), plane(0), plane(1), plane(2), plane(3),
                  pl.BlockSpec((8, row_block), lambda i: (0, first_block + i)), vec, vec],
        out_specs=pl.BlockSpec((row_block, D_MODEL), lambda i: (i, 0)),
        out_shape=jax.ShapeDtypeStruct((n_blocks * row_block, D_MODEL), F32),
        compiler_params=_cparams("parallel"),
        name="ffn_norm",
    )(x1, y4, y4, y4, y4, gate, ln_g, ln_b)


def _rope_tables(pos):
    half = HD_RET // 2
    inv = ROPE_BASE ** (-jnp.arange(half, dtype=F32) / half)
    ang = pos.astype(F32)[:, None] * inv[None, :]
    cos, sin = jnp.cos(ang), jnp.sin(ang)
    return jnp.concatenate([cos, cos], -1), jnp.concatenate([-sin, sin], -1)


def _decay_tables(c):
    log_g = jnp.log1p(-jnp.exp2(-5.0 - jnp.arange(H_RET, dtype=F32)))
    idx = jnp.arange(c, dtype=F32)
    diff = idx[:, None] - idx[None, :]
    intra = jnp.where(diff[None] >= 0, jnp.exp(jnp.maximum(diff, 0.0)[None] * log_g[:, None, None]), 0.0)
    q_dec = jnp.exp((idx + 1.0)[None] * log_g[:, None])
    k_dec = jnp.exp((c - 1.0 - idx)[None] * log_g[:, None])
    c_dec = jnp.exp(c * log_g)
    return intra, q_dec, k_dec, c_dec


def _t5_bucket(dist):
    n = jnp.maximum(dist, 0)
    max_exact = NUM_BUCKETS // 2
    nf = jnp.maximum(n, 1).astype(F32)
    large = max_exact + (jnp.log(nf / max_exact) / math.log(REL_MAX_DIST / max_exact)
                         * (NUM_BUCKETS - max_exact)).astype(jnp.int32)
    large = jnp.minimum(large, NUM_BUCKETS - 1)
    return jnp.where(n < max_exact, n, large)


def _bias_table(rel_bias, bucket):
    hit = bucket[..., None] == jnp.arange(NUM_BUCKETS)
    per_head = rel_bias.astype(F32).T.reshape((H_SWA,) + (1,) * bucket.ndim + (NUM_BUCKETS,))
    return jnp.sum(jnp.where(hit[None], per_head, 0.0), axis=-1)


def kernel(x_prompt, x_sample, state_ret, cache_swa_k, cache_swa_v, w_in, w_out, ret_gn_gain, swa_sinks,
           rel_bias, ln1_g, ln1_b, ln2_g, ln2_b, router_w, router_b, w_gate_up, b_gate_up, w_down, b_down):
    x = jnp.concatenate([x_prompt.reshape(T_PROMPT, D_MODEL), x_sample.reshape(DEC_BATCH, D_MODEL)], axis=0)
    w_in_bf = w_in.astype(BF16)
    w_out_bf = w_out.astype(BF16)
    router_w_bf = jnp.swapaxes(router_w, 1, 2).astype(BF16)
    router_b = router_b.reshape(DEPTH, N_EXPERTS, 1)
    ret_gn_gain, ln1_g, ln1_b, ln2_g, ln2_b = (
        a.reshape(DEPTH, 1, a.shape[-1]) for a in (ret_gn_gain, ln1_g, ln1_b, ln2_g, ln2_b))
    b_gu4 = b_gate_up.reshape(DEPTH, N_EXPERTS, 1, 2 * D_FF)
    b_d4 = b_down.reshape(DEPTH, N_EXPERTS, 1, D_MODEL)
    cache_k = jnp.transpose(cache_swa_k, (0, 1, 3, 4, 2)).reshape(DEPTH, DEC_BATCH, 2 * HD_SWA, WINDOW)
    cache_v = jnp.transpose(cache_swa_v, (0, 1, 3, 4, 2)).reshape(DEPTH, DEC_BATCH, 2 * HD_SWA, WINDOW)

    cos_p, sin_p = _rope_tables(jnp.arange(SEQ, dtype=jnp.int32))
    cos_s, sin_s = _rope_tables(PAST_LEN + jnp.arange(1, dtype=jnp.int32))
    intra, q_dec, k_dec, c_dec = _decay_tables(RET_CHUNK)
    ones = jnp.ones((1, 1, HD_RET), F32)
    tables_p = (cos_p, sin_p, intra, q_dec[:, :, None] * ones, k_dec[:, :, None] * ones, c_dec[:, None, None] * ones)
    intra1, q_dec1, k_dec1, c_dec1 = _decay_tables(1)
    dec_s = jnp.stack([intra1[:, 0, 0], q_dec1[:, 0], k_dec1[:, 0], c_dec1], axis=1)
    i = jnp.arange(WINDOW)[:, None]
    j = jnp.arange(2 * WINDOW)[None, :]
    in_window = (WINDOW + i - j >= 0) & (WINDOW + i - j < WINDOW)
    has_prev = jnp.array([False, True])[:, None, None, None]
    allowed = in_window[None, None] & (has_prev | (j >= WINDOW)[None, None])
    bias_p = jnp.where(allowed, _bias_table(rel_bias, _t5_bucket(WINDOW + i - j))[None], -jnp.inf)
    js = jnp.arange(WINDOW + 1)
    bias_s = _bias_table(rel_bias, _t5_bucket(WINDOW - js))
    bias_c, bias_n = bias_s[:, :WINDOW], bias_s[:, WINDOW:]
    kv_of_head = jnp.arange(H_SWA) // G_SWA
    head_mask = (kv_of_head[:, None] == jnp.arange(KV_SWA)[None, :]).astype(F32)

    zero_rows = jnp.zeros((SC_DISPATCH_ROWS, D_MODEL // 2), jnp.uint32)

    sp = ss = ko = vo = tail = None
    h, h_kv = _in_proj(x, w_in_bf, 0)
    for l in range(DEPTH):
        o, sp, tail = _mix_prompt(h, h_kv, tables_p, ret_gn_gain, swa_sinks[l], bias_p, sp, tail, l)
        hs = h[T_PROMPT:].astype(F32)
        kvs = h_kv[T_PROMPT:]
        o, ss = _ret_sample(hs, dec_s, cos_s, sin_s, ret_gn_gain, state_ret, o, ss, l)
        q8 = hs[:, 4 * RET_W:].reshape(DEC_BATCH, H_SWA, HD_SWA)
        qm = (q8[:, :, None, :] * head_mask[None, :, :, None]).reshape(DEC_BATCH, H_SWA, 2 * HD_SWA)
        kn = kvs[:, :SWA_KV_W].reshape(DEC_BATCH, 1, SWA_KV_W)
        vn = kvs[:, SWA_KV_W:].reshape(DEC_BATCH, 1, SWA_KV_W)
        osm, ko, vo = _swa_sample(qm, kn, vn, cache_k, cache_v, bias_c, bias_n, swa_sinks[l][:, None], ko, vo, l)
        os_s = jnp.sum(osm.reshape(DEC_BATCH, H_SWA, KV_SWA, HD_SWA) * head_mask[None, :, :, None], axis=2)
        o = lax.dynamic_update_slice(o, os_s.reshape(DEC_BATCH, SWA_Q_W), (T_PROMPT, RET_W))

        x1, xpk, top_idx, gate, rank, counts = _out_proj(o, x, w_out_bf, ln1_g, ln1_b, router_w_bf, router_b, l)
        pos_km, unused, group_start, group_blocks, group_next = _route(top_idx, rank, counts)
        xs = _sc_dispatch(xpk, pos_km, unused, zero_rows)
        ys = _moe(group_start, group_blocks, group_next, xs, w_gate_up, b_gu4, w_down, b_d4, l)
        y4 = _sc_collect(ys, pos_km.reshape(N_ASSIGN)).reshape(TOP_K, T_ALL, D_MODEL // 2)
        if l < DEPTH - 1:
            x, h, h_kv = _norm_proj(x1, y4, gate, ln2_g, ln2_b, w_in_bf, l)
        else:
            yp = _ffn_norm(x1, y4, gate, ln2_g, ln2_b, l, FINAL_ROW_BLOCK, 0, T_PROMPT // FINAL_ROW_BLOCK)
            ys_out = _ffn_norm(x1, y4, gate, ln2_g, ln2_b, l, DEC_BATCH, T_PROMPT // DEC_BATCH, 1)

    yp = yp.reshape(BATCH, SEQ, D_MODEL)
    ys_out = ys_out.reshape(DEC_BATCH, 1, D_MODEL)

    def window_major(t):
        return jnp.transpose(t.reshape(DEPTH, t.shape[1], KV_SWA, HD_SWA, WINDOW), (0, 1, 4, 2, 3))

    return (yp, ys_out, sp, window_major(tail[:, :, :SWA_KV_W]), window_major(tail[:, :, SWA_KV_W:]), ss,
            window_major(ko), window_major(vo))
```

```python
import functools
import math

import jax
import jax.numpy as jnp
from jax import lax
from jax.experimental import pallas as pl
from jax.experimental.pallas import tpu as pltpu
from jax.experimental.pallas import tpu_sc as plsc

F32 = jnp.float32
BF16 = jnp.bfloat16

D_MODEL = 1024
BATCH = 8
SEQ = 2048
DEPTH = 4
DEC_BATCH = 128
PAST_LEN = 8192
H_RET = 4
HD_RET = 128
RET_CHUNK = 128
ROPE_BASE = 10000.0
H_SWA = 8
KV_SWA = 2
G_SWA = H_SWA // KV_SWA
HD_SWA = 64
WINDOW = 128
NUM_BUCKETS = 32
REL_MAX_DIST = 128
RET_W = H_RET * HD_RET
SWA_Q_W = H_SWA * HD_SWA
SWA_KV_W = KV_SWA * HD_SWA
D_IN = 4 * RET_W + SWA_Q_W + 2 * SWA_KV_W
MIX_W = RET_W + SWA_Q_W
H_MAIN_W = 4 * RET_W + SWA_Q_W
N_EXPERTS = 32
TOP_K = 4
D_FF = D_MODEL
SWIGLU_LIMIT = 7.0
SWIGLU_ALPHA = 1.702
DEEPNORM_ALPHA = (2 * DEPTH) ** 0.25
LN_EPS = 1e-5
GN_EPS = 1e-5

T_PROMPT = BATCH * SEQ
T_ALL = T_PROMPT + DEC_BATCH
ROW_BLOCK = 384
SAMPLE_BLOCK = 8
MOE_ROWS = 128
MOE_CHUNK = 1024
MOE_PIECE = 1024
N_ASSIGN = T_ALL * TOP_K
MOE_BLOCKS = N_ASSIGN // MOE_ROWS + N_EXPERTS
MOE_SLOTS = MOE_BLOCKS * MOE_ROWS
N_UNUSED = MOE_SLOTS - N_ASSIGN
SC_CORES = 2
SC_SUBCORES = 16
SC_WORKERS = SC_CORES * SC_SUBCORES
SC_DISPATCH_ROWS = 64
SC_COLLECT_ROWS = 48
FINAL_ROW_BLOCK = 512
NORM_SUB_ROWS = 128
MIX_BLOCKS = 8
GATE_ROWS = 8
VMEM_LIMIT = 56 * 1024 * 1024

assert T_ALL % ROW_BLOCK == 0 and ROW_BLOCK % NORM_SUB_ROWS == 0 and T_PROMPT % FINAL_ROW_BLOCK == 0
assert DEC_BATCH % SAMPLE_BLOCK == 0 and (SEQ // WINDOW) % MIX_BLOCKS == 0 and RET_CHUNK == WINDOW
assert MOE_CHUNK % MOE_ROWS == 0 and MOE_CHUNK % MOE_PIECE == 0 and N_ASSIGN % MOE_ROWS == 0
assert T_ALL % SC_DISPATCH_ROWS == 0 and N_UNUSED % (SC_DISPATCH_ROWS * SC_WORKERS) == 0
assert N_ASSIGN % SC_WORKERS == 0 and (N_ASSIGN // SC_WORKERS) % SC_COLLECT_ROWS == 0


def _cparams(*sem):
    return pltpu.CompilerParams(dimension_semantics=sem, vmem_limit_bytes=VMEM_LIMIT)


def _project_in(x, w_ref, hm_ref, hkv_ref):
    h = jnp.dot(x.astype(BF16), w_ref[0], preferred_element_type=F32)
    hm_ref[...] = h[:, :H_MAIN_W].astype(BF16)
    hkv_ref[...] = h[:, H_MAIN_W:]


class _InputRows:
    N = T_ALL // ROW_BLOCK
    PROMPT_TAIL = ROW_BLOCK - DEC_BATCH

    @classmethod
    def specs(cls):
        assert (cls.N - 1) * ROW_BLOCK + cls.PROMPT_TAIL == T_PROMPT and T_PROMPT % cls.PROMPT_TAIL == 0
        return [pl.BlockSpec((ROW_BLOCK, D_MODEL), lambda i: (jnp.minimum(i, cls.N - 2), 0)),
                pl.BlockSpec((cls.PROMPT_TAIL, D_MODEL), lambda i: (T_PROMPT // cls.PROMPT_TAIL - 1, 0)),
                pl.BlockSpec((DEC_BATCH, D_MODEL), lambda i: (0, 0))]

    @staticmethod
    def args(x_prompt, x_sample):
        xp = x_prompt.reshape(T_PROMPT, D_MODEL)
        return [xp, xp, x_sample.reshape(DEC_BATCH, D_MODEL)]

    @classmethod
    def read(cls, main_ref, tail_ref, sample_ref):
        last = jnp.concatenate([tail_ref[...], sample_ref[...]], axis=0)
        return jnp.where(pl.program_id(0) == cls.N - 1, last, main_ref[...])


def _in_proj_kernel(xm_ref, xt_ref, xs_ref, w_ref, hm_ref, hkv_ref):
    _project_in(_InputRows.read(xm_ref, xt_ref, xs_ref), w_ref, hm_ref, hkv_ref)


_IN_PROJ_OUT_SPECS = [pl.BlockSpec((ROW_BLOCK, H_MAIN_W), lambda i: (i, 0)),
                      pl.BlockSpec((ROW_BLOCK, 2 * SWA_KV_W), lambda i: (i, 0))]
_IN_PROJ_OUT_SHAPES = [jax.ShapeDtypeStruct((T_ALL, H_MAIN_W), BF16),
                       jax.ShapeDtypeStruct((T_ALL, 2 * SWA_KV_W), F32)]


def _in_proj(x_prompt, x_sample, w_in_bf):
    return pl.pallas_call(
        _in_proj_kernel,
        grid=(T_ALL // ROW_BLOCK,),
        in_specs=_InputRows.specs() + [pl.BlockSpec((1, D_MODEL, D_IN), lambda i: (0, 0, 0))],
        out_specs=_IN_PROJ_OUT_SPECS,
        out_shape=_IN_PROJ_OUT_SHAPES,
        compiler_params=_cparams("parallel"),
        name="in_proj",
    )(*_InputRows.args(x_prompt, x_sample), w_in_bf)


def _rotate(x, cos, sin_signed):
    return x * cos + pltpu.roll(x, HD_RET // 2, 1) * sin_signed


def _group_norm_gate(o, gain, g):
    mu = jnp.mean(o, -1, keepdims=True)
    var = jnp.mean(jnp.square(o - mu), -1, keepdims=True)
    return (o - mu) * lax.rsqrt(var + GN_EPS) * gain * (g * jax.nn.sigmoid(g))


def _layer_norm(y, g, b):
    mu = jnp.mean(y, -1, keepdims=True)
    var = jnp.mean(jnp.square(y - mu), -1, keepdims=True)
    return (y - mu) * lax.rsqrt(var + LN_EPS) * g + b


def _ret_prompt_stages(q_ref, k_ref, v_ref, g_ref, cos_ref, sin_ref, intra_ref, qdec_ref, kdec_ref,
                       cdec_ref, gain_ref, o_ref, s_ref, rows):
    cos = cos_ref[rows, :]
    sin = sin_ref[rows, :]
    nt = (((1,), (1,)), ((), ()))
    tn = (((0,), (0,)), ((), ()))
    heads = [slice(h * HD_RET, (h + 1) * HD_RET) for h in range(H_RET)]
    vs, atts, crosses = [], [], []
    for h, sl in enumerate(heads):
        q = _rotate(q_ref[rows, sl].astype(F32), cos, sin)
        k = _rotate(k_ref[rows, sl].astype(F32), cos, sin) * (HD_RET ** -0.5)
        v = v_ref[rows, sl]
        s = s_ref[0, 0, h]
        atts.append(lax.dot_general(q.astype(BF16), k.astype(BF16), nt, preferred_element_type=F32))
        crosses.append(jnp.dot((q * qdec_ref[h]).astype(BF16), s.astype(BF16), preferred_element_type=F32))
        s_ref[0, 0, h] = s * cdec_ref[h] + lax.dot_general((k * kdec_ref[h]).astype(BF16), v, tn,
                                                          preferred_element_type=F32)
        vs.append(v)
    yield
    atts = [(att * intra_ref[h]).astype(BF16) for h, att in enumerate(atts)]
    yield
    outs = [jnp.dot(atts[h], vs[h], preferred_element_type=F32) + crosses[h] for h in range(H_RET)]
    yield
    for h, sl in enumerate(heads):
        o_ref[rows, sl] = _group_norm_gate(outs[h], gain_ref[0, :, sl], g_ref[rows, sl].astype(F32))


def _swa_prompt_stages(sink_ref, q_ref, kv_prev, kv_cur, bias, o_ref, rows, lg_scr, p_scr):
    kv = jnp.concatenate([kv_prev, kv_cur], axis=0)
    k_t = kv[:, :SWA_KV_W].T
    k_t_swapped = pltpu.roll(k_t, HD_SWA, 0)
    v = kv[:, SWA_KV_W:]
    v_swapped = pltpu.roll(v, HD_SWA, 1)
    k_row = lax.broadcasted_iota(jnp.int32, k_t.shape, 0)
    v_lane = lax.broadcasted_iota(jnp.int32, v.shape, 1)
    ones = jnp.ones(v.shape, BF16)
    k_pad, v_pad = {}, {}
    for kvh in range(KV_SWA):
        for par in range(2):
            k_src = k_t if kvh == par else k_t_swapped
            v_src = v if kvh == par else v_swapped
            k_pad[kvh, par] = jnp.where((k_row >= par * HD_SWA) & (k_row < (par + 1) * HD_SWA), k_src, 0.0).astype(BF16)
            v_pad[kvh, par] = jnp.where((v_lane >= par * HD_SWA) & (v_lane < (par + 1) * HD_SWA), v_src, 0.0).astype(BF16)
    for hd in range(H_SWA):
        pair, par, kvh = hd // 2, hd % 2, hd // G_SWA
        q2 = (q_ref[rows, pair * 2 * HD_SWA:(pair + 1) * 2 * HD_SWA] * (HD_SWA ** -0.5)).astype(BF16)
        lg_scr[hd] = jnp.dot(q2, k_pad[kvh, par], preferred_element_type=F32) + bias(hd)
    yield
    sink_terms = []
    for hd in range(H_SWA):
        logits = lg_scr[hd]
        sink = sink_ref[hd]
        m = jnp.maximum(jnp.max(logits, -1, keepdims=True), sink)
        p_scr[hd] = jnp.exp(logits - m).astype(BF16)
        sink_terms.append(jnp.exp(sink - m))
    yield
    for pair in range(H_SWA // 2):
        acc = None
        for par in range(2):
            hd = 2 * pair + par
            p = p_scr[hd]
            den = jnp.dot(p, ones, preferred_element_type=F32) + sink_terms[hd]
            part = jnp.dot(p, v_pad[hd // G_SWA, par], preferred_element_type=F32) / den
            acc = part if acc is None else acc + part
        o_ref[rows, pair * 2 * HD_SWA:(pair + 1) * 2 * HD_SWA] = acc


def _mix_prompt_kernel(*refs):
    n_ret_in = 11
    ret_in = refs[:n_ret_in]
    sink_ref, q_ref, kvc_ref, kvp_ref, bias_ref = refs[n_ret_in:n_ret_in + 5]
    o_ref, s_ref, tail_ref, lg_scr, p_scr = refs[-5:]
    o_r_ref = o_ref.at[:, pl.ds(0, RET_W)]
    o_s_ref = o_ref.at[:, pl.ds(RET_W, SWA_Q_W)]
    c = pl.program_id(1)

    @pl.when(c == 0)
    def _():
        s_ref[...] = jnp.zeros_like(s_ref)

    @pl.when(c == pl.num_programs(1) - 1)
    def _():
        tail_ref[0, 0] = kvc_ref[pl.ds((MIX_BLOCKS - 1) * WINDOW, WINDOW), :].T

    groups = []
    for j in range(MIX_BLOCKS):
        rows = pl.ds(j * WINDOW, WINDOW)
        kv_cur = kvc_ref[rows, :]
        if j == 0:
            kv_prev = kvp_ref[...]
            table = jnp.minimum(c, 1)
        else:
            kv_prev = kvc_ref[pl.ds((j - 1) * WINDOW, WINDOW), :]
            table = 1
        groups.append(_ret_prompt_stages(*ret_in, o_r_ref, s_ref, rows))

        def bias(hd, table=table):
            return bias_ref[table, hd]

        groups.append(_swa_prompt_stages(sink_ref, q_ref, kv_prev, kv_cur, bias, o_s_ref, rows,
                                         lg_scr.at[j], p_scr.at[j]))
    while groups:
        for g in list(groups):
            if next(g, groups) is groups:
                groups.remove(g)


def _mix_prompt(h, h_kv, tables, gain, sinks_l, bias, s_prev, tail_prev, layer):
    cos, sin, intra, qdec, kdec, cdec = tables
    nb = SEQ // WINDOW
    ns = nb // MIX_BLOCKS
    rows = MIX_BLOCKS * WINDOW
    qcol = 4 * RET_W // SWA_Q_W

    def col(j):
        return pl.BlockSpec((rows, RET_W), lambda b, c: (b * ns + c, j))

    def whole(a):
        return pl.BlockSpec(a.shape, lambda b, c: (0,) * a.ndim)

    in_specs = [col(0), col(1), col(2), col(3),
                pl.BlockSpec((rows, HD_RET), lambda b, c: (c, 0)),
                pl.BlockSpec((rows, HD_RET), lambda b, c: (c, 0)),
                whole(intra), whole(qdec), whole(kdec), whole(cdec),
                pl.BlockSpec((1, 1, RET_W), lambda b, c: (layer, 0, 0)),
                pl.BlockSpec(memory_space=pltpu.SMEM),
                pl.BlockSpec((rows, SWA_Q_W), lambda b, c: (b * ns + c, qcol)),
                pl.BlockSpec((rows, 2 * SWA_KV_W), lambda b, c: (b * ns + c, 0)),
                pl.BlockSpec((WINDOW, 2 * SWA_KV_W), lambda b, c: (b * nb + jnp.maximum(MIX_BLOCKS * c - 1, 0), 0)),
                whole(bias)]
    args = [h, h, h, h, cos, sin, intra, qdec, kdec, cdec, gain, sinks_l, h, h_kv, h_kv, bias]
    aliases = {}
    if s_prev is not None:
        in_specs += [pl.BlockSpec(memory_space=pl.ANY), pl.BlockSpec(memory_space=pl.ANY)]
        args += [s_prev, tail_prev]
        aliases = {16: 1, 17: 2}

    return pl.pallas_call(
        _mix_prompt_kernel,
        grid=(BATCH, ns),
        in_specs=in_specs,
        scratch_shapes=[pltpu.VMEM((MIX_BLOCKS, H_SWA, WINDOW, 2 * WINDOW), F32),
                        pltpu.VMEM((MIX_BLOCKS, H_SWA, WINDOW, 2 * WINDOW), BF16)],
        out_specs=[pl.BlockSpec((rows, MIX_W), lambda b, c: (b * ns + c, 0)),
                   pl.BlockSpec((1, 1, H_RET, HD_RET, HD_RET), lambda b, c: (layer, b, 0, 0, 0)),
                   pl.BlockSpec((1, 1, 2 * SWA_KV_W, WINDOW), lambda b, c: (layer, b, 0, 0))],
        out_shape=[jax.ShapeDtypeStruct((T_ALL, MIX_W), F32),
                   jax.ShapeDtypeStruct((DEPTH, BATCH, H_RET, HD_RET, HD_RET), F32),
                   jax.ShapeDtypeStruct((DEPTH, BATCH, 2 * SWA_KV_W, WINDOW), F32)],
        input_output_aliases=aliases,
        compiler_params=_cparams("parallel", "arbitrary"),
        name="mix_prompt",
    )(*args)


def _ret_sample_kernel(dec_ref, q_ref, k_ref, v_ref, g_ref, cos_ref, sin_ref, gain_ref, st_ref, o_ref, so_ref):
    cos = cos_ref[...]
    sin = sin_ref[...]
    for h in range(H_RET):
        sl = slice(h * HD_RET, (h + 1) * HD_RET)
        intra, qdec, kdec, cdec = dec_ref[h, 0], dec_ref[h, 1], dec_ref[h, 2], dec_ref[h, 3]
        q = _rotate(q_ref[:, sl], cos, sin)
        k = _rotate(k_ref[:, sl], cos, sin) * (HD_RET ** -0.5)
        v = v_ref[:, sl]
        att = jnp.sum(q * k, -1, keepdims=True) * intra
        q_col = (q * qdec).T
        k_col = (k * kdec).T
        rows = []
        for b in range(SAMPLE_BLOCK):
            s = st_ref[0, b, h]
            v_row = v[b:b + 1, :]
            rows.append(att[b:b + 1, :] * v_row + jnp.sum(q_col[:, b:b + 1] * s, axis=0, keepdims=True))
            so_ref[0, b, h] = s * cdec + k_col[:, b:b + 1] * v_row
        o = jnp.concatenate(rows, axis=0)
        o_ref[:, sl] = _group_norm_gate(o, gain_ref[0, :, sl], g_ref[:, sl])


def _ret_sample(h, dec_s, cos_s, sin_s, gain, state_ret, o_r, st_prev, layer):
    base = T_PROMPT // SAMPLE_BLOCK

    def col(j):
        return pl.BlockSpec((SAMPLE_BLOCK, RET_W), lambda i: (i, j))

    in_specs = [pl.BlockSpec(memory_space=pltpu.SMEM),
                col(0), col(1), col(2), col(3),
                pl.BlockSpec((1, HD_RET), lambda i: (0, 0)),
                pl.BlockSpec((1, HD_RET), lambda i: (0, 0)),
                pl.BlockSpec((1, 1, RET_W), lambda i: (layer, 0, 0)),
                pl.BlockSpec((1, SAMPLE_BLOCK, H_RET, HD_RET, HD_RET), lambda i: (layer, i, 0, 0, 0)),
                pl.BlockSpec(memory_space=pl.ANY)]
    args = [dec_s, h, h, h, h, cos_s, sin_s, gain, state_ret, o_r]
    aliases = {9: 0}
    if st_prev is not None:
        in_specs.append(pl.BlockSpec(memory_space=pl.ANY))
        args.append(st_prev)
        aliases[10] = 1

    def body(*refs):
        _ret_sample_kernel(*refs[:9], *refs[-2:])

    return pl.pallas_call(
        body,
        grid=(DEC_BATCH // SAMPLE_BLOCK,),
        in_specs=in_specs,
        out_specs=[pl.BlockSpec((SAMPLE_BLOCK, RET_W), lambda i: (base + i, 0)),
                   pl.BlockSpec((1, SAMPLE_BLOCK, H_RET, HD_RET, HD_RET), lambda i: (layer, i, 0, 0, 0))],
        out_shape=[jax.ShapeDtypeStruct((T_ALL, MIX_W), F32),
                   jax.ShapeDtypeStruct((DEPTH, DEC_BATCH, H_RET, HD_RET, HD_RET), F32)],
        input_output_aliases=aliases,
        compiler_params=_cparams("parallel"),
        name="ret_sample",
    )(*args)


def _swa_sample_kernel(qm_ref, kn_ref, vn_ref, knc_ref, vnc_ref, kc_ref, vc_ref, bias_ref, biasn_ref, sink_ref,
                       o_ref, ko_ref, vo_ref):
    qm = qm_ref[...]
    kn = kn_ref[...]
    vn = vn_ref[...]
    kc = kc_ref[0]
    vc = vc_ref[0]
    scale = HD_SWA ** -0.5
    logits = jnp.einsum('bhd,bdj->bhj', qm.astype(BF16), kc.astype(BF16),
                        preferred_element_type=F32) * scale + bias_ref[...][None]
    j = lax.broadcasted_iota(jnp.int32, logits.shape, 2)
    logits = jnp.where(j >= 1, logits, -jnp.inf)
    ln = jnp.sum(qm * kn, -1, keepdims=True) * scale + biasn_ref[...][None]
    sink = sink_ref[...][None]
    m = jnp.maximum(jnp.maximum(jnp.max(logits, -1, keepdims=True), ln), sink)
    p = jnp.exp(logits - m)
    pn = jnp.exp(ln - m)
    denom = jnp.sum(p, -1, keepdims=True) + pn + jnp.exp(sink - m)
    w = p / denom
    o_ref[...] = (jnp.einsum('bhj,bdj->bhd', w.astype(BF16), vc.astype(BF16), preferred_element_type=F32)
                  + (pn / denom) * vn)
    last = lax.broadcasted_iota(jnp.int32, kc.shape, 2) == WINDOW - 1
    ko_ref[0] = jnp.where(last, knc_ref[...], pltpu.roll(kc, WINDOW - 1, 2))
    vo_ref[0] = jnp.where(last, vnc_ref[...], pltpu.roll(vc, WINDOW - 1, 2))


def _swa_sample(qm, kn, vn, cache_k, cache_v, bias_c, bias_n, sink_col, ko_prev, vo_prev, layer):
    sb = SAMPLE_BLOCK
    cache_spec = pl.BlockSpec((1, sb, 2 * HD_SWA, WINDOW), lambda i: (layer, i, 0, 0))
    row_spec = pl.BlockSpec((sb, 1, 2 * HD_SWA), lambda i: (i, 0, 0))
    col_spec = pl.BlockSpec((sb, 2 * HD_SWA, 1), lambda i: (i, 0, 0))
    in_specs = [pl.BlockSpec((sb, H_SWA, 2 * HD_SWA), lambda i: (i, 0, 0)),
                row_spec, row_spec, col_spec, col_spec,
                cache_spec, cache_spec,
                pl.BlockSpec((H_SWA, WINDOW), lambda i: (0, 0)),
                pl.BlockSpec((H_SWA, 1), lambda i: (0, 0)),
                pl.BlockSpec((H_SWA, 1), lambda i: (0, 0))]
    args = [qm, kn, vn, kn.reshape(DEC_BATCH, 2 * HD_SWA, 1), vn.reshape(DEC_BATCH, 2 * HD_SWA, 1),
            cache_k, cache_v, bias_c, bias_n, sink_col]
    aliases = {}
    if ko_prev is not None:
        in_specs += [pl.BlockSpec(memory_space=pl.ANY), pl.BlockSpec(memory_space=pl.ANY)]
        args += [ko_prev, vo_prev]
        aliases = {10: 1, 11: 2}

    def body(*refs):
        _swa_sample_kernel(*refs[:10], *refs[-3:])

    cshape = jax.ShapeDtypeStruct((DEPTH, DEC_BATCH, 2 * HD_SWA, WINDOW), F32)
    return pl.pallas_call(
        body,
        grid=(DEC_BATCH // sb,),
        in_specs=in_specs,
        out_specs=[pl.BlockSpec((sb, H_SWA, 2 * HD_SWA), lambda i: (i, 0, 0)), cache_spec, cache_spec],
        out_shape=[jax.ShapeDtypeStruct((DEC_BATCH, H_SWA, 2 * HD_SWA), F32), cshape, cshape],
        input_output_aliases=aliases,
        compiler_params=_cparams("parallel"),
        name="swa_sample",
    )(*args)


def _pack_bf16_pairs(x):
    w = x.shape[1] // 2
    lo = lax.bitcast_convert_type(x[:, :w].astype(BF16).astype(F32), jnp.uint32) >> 16
    hi = lax.bitcast_convert_type(x[:, w:].astype(BF16).astype(F32), jnp.uint32)
    return lo | hi


def _unpack_bf16_pairs(p):
    lo = lax.bitcast_convert_type(p << 16, F32).astype(BF16)
    hi = lax.bitcast_convert_type(p & jnp.uint32(0xFFFF0000), F32).astype(BF16)
    return lo, hi


def _out_proj_kernel(n_x_refs, o_ref, *refs):
    x_refs, (w_ref, g_ref, b_ref, rw_ref, rb_ref,
             x1_ref, xpk_ref, idx_ref, gate_ref, rank_ref, cnt_ref) = refs[:n_x_refs], refs[n_x_refs:]
    x = x_refs[0][...] if n_x_refs == 1 else _InputRows.read(*x_refs)
    nt = (((1,), (1,)), ((), ()))
    mix = jnp.dot(o_ref[...].astype(BF16), w_ref[0], preferred_element_type=F32)
    x1 = _layer_norm(DEEPNORM_ALPHA * x + mix, g_ref[0], b_ref[0])
    x1_ref[...] = x1
    xpk_ref[...] = _pack_bf16_pairs(x1)
    logits = lax.dot_general(rw_ref[0], x1.astype(BF16), nt, preferred_element_type=F32) + rb_ref[0]
    rows = logits.shape[1]
    expert = lax.broadcasted_iota(jnp.int32, logits.shape, 0)
    vals, hits = [], []
    for kk in range(TOP_K):
        m = jnp.max(logits, 0, keepdims=True)
        idx = jnp.min(jnp.where(logits == m, expert, N_EXPERTS), 0, keepdims=True)
        vals.append(m)
        hits.append(expert == idx)
        idx_ref[kk:kk + 1, :] = idx
        logits = jnp.where(expert == idx, -jnp.inf, logits)
    es = [jnp.exp(v - vals[0]) for v in vals]
    tot = es[0] + es[1] + es[2] + es[3]
    gate_ref[...] = jnp.concatenate([e / tot for e in es] + [jnp.zeros((GATE_ROWS - TOP_K, rows), F32)], axis=0)
    chosen = sum(h.astype(F32) for h in hits)
    earlier = (lax.broadcasted_iota(jnp.int32, (rows, rows), 0)
               < lax.broadcasted_iota(jnp.int32, (rows, rows), 1)).astype(BF16)
    before = jnp.dot(chosen.astype(BF16), earlier, preferred_element_type=F32)
    for kk in range(TOP_K):
        rank_ref[kk:kk + 1, :] = jnp.sum(jnp.where(hits[kk], before, 0.0), 0, keepdims=True).astype(jnp.int32)
    cnt_ref[0] = jnp.sum(chosen, axis=1, keepdims=True).astype(jnp.int32)


def _out_proj(o, x, w_out_bf, ln_g, ln_b, router_w_bf, router_b, layer):
    def rows(w):
        return pl.BlockSpec((ROW_BLOCK, w), lambda i: (i, 0))

    def vec(w):
        return pl.BlockSpec((1, 1, w), lambda i: (layer, 0, 0))

    def cols(h):
        return pl.BlockSpec((h, ROW_BLOCK), lambda i: (0, i))

    x_args, x_specs = ([x], [rows(D_MODEL)]) if not isinstance(x, tuple) else (_InputRows.args(*x), _InputRows.specs())
    return pl.pallas_call(
        functools.partial(_out_proj_kernel, len(x_args)),
        grid=(T_ALL // ROW_BLOCK,),
        in_specs=[rows(MIX_W)] + x_specs + [
                  pl.BlockSpec((1, MIX_W, D_MODEL), lambda i: (layer, 0, 0)),
                  vec(D_MODEL), vec(D_MODEL),
                  pl.BlockSpec((1, N_EXPERTS, D_MODEL), lambda i: (layer, 0, 0)),
                  pl.BlockSpec((1, N_EXPERTS, 1), lambda i: (layer, 0, 0))],
        out_specs=[rows(D_MODEL), rows(D_MODEL // 2), cols(TOP_K), cols(GATE_ROWS), cols(TOP_K),
                   pl.BlockSpec((1, N_EXPERTS, 1), lambda i: (i, 0, 0))],
        out_shape=[jax.ShapeDtypeStruct((T_ALL, D_MODEL), F32),
                   jax.ShapeDtypeStruct((T_ALL, D_MODEL // 2), jnp.uint32),
                   jax.ShapeDtypeStruct((TOP_K, T_ALL), jnp.int32),
                   jax.ShapeDtypeStruct((GATE_ROWS, T_ALL), F32),
                   jax.ShapeDtypeStruct((TOP_K, T_ALL), jnp.int32),
                   jax.ShapeDtypeStruct((T_ALL // ROW_BLOCK, N_EXPERTS, 1), jnp.int32)],
        compiler_params=_cparams("parallel"),
        name="out_proj",
    )(o, *x_args, w_out_bf, ln_g, ln_b, router_w_bf, router_b)


def _moe_kernel(start_ref, nblk_ref, next_ref, xs_hbm, wgu_ref, bgu_ref, wd_ref, bd_ref, ys_hbm,
                wgu_bf, wd_bf, x_buf, y_buf, in_sems, out_sems, pending, ready):
    e = pl.program_id(0)
    per_chunk = MOE_CHUNK // MOE_ROWS
    tail_sizes = [t * MOE_ROWS for t in range(1, per_chunk)]
    nblk = nblk_ref[e]
    n_full = nblk // per_chunk
    tail_blocks = nblk % per_chunk
    base = start_ref[e]

    def in_copy(row0, rows, slot):
        src = xs_hbm.at[pl.ds(pl.multiple_of(row0, MOE_ROWS), rows)]
        return pltpu.make_async_copy(src, x_buf.at[slot, pl.ds(0, rows)], in_sems.at[slot])

    def out_copy(row0, rows, slot):
        dst = ys_hbm.at[pl.ds(pl.multiple_of(row0, MOE_ROWS), rows)]
        return pltpu.make_async_copy(y_buf.at[slot, pl.ds(0, rows)], dst, out_sems.at[slot])

    def start_first_chunk(expert, slot):
        @pl.when(nblk_ref[expert] >= per_chunk)
        def _():
            in_copy(start_ref[expert], MOE_CHUNK, slot).start()

        for rows in tail_sizes:
            @pl.when(nblk_ref[expert] == rows // MOE_ROWS)
            def _():
                in_copy(start_ref[expert], rows, slot).start()

    def prefetch_next_expert(slot):
        nxt = next_ref[e]

        @pl.when(nxt < N_EXPERTS)
        def _():
            start_first_chunk(nxt, slot)
            ready[0] = slot

    def drain(slot):
        for rows in [MOE_CHUNK] + tail_sizes:
            @pl.when(pending[slot] == rows)
            def _():
                out_copy(0, rows, slot).wait()
        pending[slot] = 0

    def compute(rows, slot):
        for r0 in range(0, rows, MOE_PIECE):
            piece = pl.ds(r0, min(MOE_PIECE, rows - r0))
            x = jnp.concatenate(_unpack_bf16_pairs(x_buf[slot, piece, :]), axis=1)
            gu = jnp.dot(x, wgu_bf[...], preferred_element_type=F32) + bgu_ref[0, 0]
            glu = jnp.minimum(gu[:, :D_FF], SWIGLU_LIMIT)
            lin = jnp.clip(gu[:, D_FF:], -SWIGLU_LIMIT, SWIGLU_LIMIT)
            act = glu * jax.nn.sigmoid(SWIGLU_ALPHA * glu) * (lin + 1.0)
            y = jnp.dot(act.astype(BF16), wd_bf[...], preferred_element_type=F32) + bd_ref[0, 0]
            y_buf[slot, piece, :] = _pack_bf16_pairs(y)

    def run_chunk(row0, rows, slot):
        in_copy(row0, rows, slot).wait()
        drain(slot)
        compute(rows, slot)
        out_copy(row0, rows, slot).start()
        pending[slot] = rows

    @pl.when(e == 0)
    def _():
        pending[0] = 0
        pending[1] = 0
        ready[0] = -1

    @pl.when(nblk > 0)
    def _():
        @pl.when(ready[0] < 0)
        def _():
            start_first_chunk(e, 0)
            ready[0] = 0

        first_slot = ready[0]
        wgu_bf[...] = wgu_ref[0, 0].astype(BF16)
        wd_bf[...] = wd_ref[0, 0].astype(BF16)

        def body(c, carry):
            slot = (first_slot + c) & 1
            row0 = pl.multiple_of(base + c * MOE_CHUNK, MOE_ROWS)

            @pl.when(c + 1 < n_full)
            def _():
                in_copy(row0 + MOE_CHUNK, MOE_CHUNK, 1 - slot).start()

            for rows in tail_sizes:
                @pl.when(jnp.logical_and(c + 1 == n_full, tail_blocks == rows // MOE_ROWS))
                def _():
                    in_copy(row0 + MOE_CHUNK, rows, 1 - slot).start()

            @pl.when(jnp.logical_and(c + 1 == n_full, tail_blocks == 0))
            def _():
                prefetch_next_expert(1 - slot)

            run_chunk(row0, MOE_CHUNK, slot)
            return carry

        lax.fori_loop(0, n_full, body, 0)

        for rows in tail_sizes:
            @pl.when(tail_blocks == rows // MOE_ROWS)
            def _():
                slot = (first_slot + n_full) & 1
                prefetch_next_expert(1 - slot)
                run_chunk(pl.multiple_of(base + n_full * MOE_CHUNK, MOE_ROWS), rows, slot)

    @pl.when(e == pl.num_programs(0) - 1)
    def _():
        drain(0)
        drain(1)


def _moe(group_start, group_blocks, group_next, xs, w_gate_up, b_gate_up, w_down, b_down, layer):
    def wsel(e, st, nb, nx):
        return (layer, e, 0, 0)

    return pl.pallas_call(
        _moe_kernel,
        grid_spec=pltpu.PrefetchScalarGridSpec(
            num_scalar_prefetch=3,
            grid=(N_EXPERTS,),
            in_specs=[pl.BlockSpec(memory_space=pl.ANY),
                      pl.BlockSpec((1, 1, D_MODEL, 2 * D_FF), wsel),
                      pl.BlockSpec((1, 1, 1, 2 * D_FF), wsel),
                      pl.BlockSpec((1, 1, D_FF, D_MODEL), wsel),
                      pl.BlockSpec((1, 1, 1, D_MODEL), wsel)],
            out_specs=pl.BlockSpec(memory_space=pl.ANY),
            scratch_shapes=[pltpu.VMEM((D_MODEL, 2 * D_FF), BF16), pltpu.VMEM((D_FF, D_MODEL), BF16),
                            pltpu.VMEM((2, MOE_CHUNK, D_MODEL // 2), jnp.uint32),
                            pltpu.VMEM((2, MOE_CHUNK, D_MODEL // 2), jnp.uint32),
                            pltpu.SemaphoreType.DMA((2,)), pltpu.SemaphoreType.DMA((2,)),
                            pltpu.SMEM((2,), jnp.int32), pltpu.SMEM((1,), jnp.int32)]),
        out_shape=jax.ShapeDtypeStruct((MOE_SLOTS, D_MODEL // 2), jnp.uint32),
        compiler_params=_cparams("arbitrary"),
        name="moe_experts",
    )(group_start, group_blocks, group_next, xs, w_gate_up, b_gate_up, w_down, b_down)


def _route(idx, rank, counts):
    experts = jnp.arange(N_EXPERTS, dtype=jnp.int32)
    counts = counts.reshape(T_ALL // ROW_BLOCK, N_EXPERTS)
    before_block = jnp.cumsum(counts, axis=0) - counts
    total = jnp.sum(counts, axis=0)
    padded = (total + MOE_ROWS - 1) // MOE_ROWS * MOE_ROWS
    pad_end = jnp.cumsum(padded)
    pad_start = pad_end - padded
    base = jnp.repeat((pad_start[None, :] + before_block).T, ROW_BLOCK, axis=1)
    pos = jnp.sum(jnp.where(idx[:, None, :] == experts[None, :, None], base[None], 0), axis=1) + rank
    gap = padded - total
    gap_end = jnp.concatenate([jnp.cumsum(gap), jnp.full((1,), N_UNUSED, jnp.int32)])
    gap_start = jnp.concatenate([jnp.zeros((1,), jnp.int32), gap_end[:-1]])
    first_free = jnp.concatenate([pad_start + total, pad_end[-1:]])
    j = jnp.arange(N_UNUSED, dtype=jnp.int32)
    region = (j[:, None] >= gap_start[None, :]) & (j[:, None] < gap_end[None, :])
    unused = jnp.sum(jnp.where(region, first_free[None, :] + j[:, None] - gap_start[None, :], 0), axis=-1)
    nonempty = jnp.where(total > 0, experts, N_EXPERTS)
    following = jnp.concatenate([lax.cummin(nonempty, reverse=True)[1:], jnp.full((1,), N_EXPERTS, jnp.int32)])
    return (pos.astype(jnp.int32), unused.astype(jnp.int32),
            pad_start.astype(jnp.int32), (padded // MOE_ROWS).astype(jnp.int32), following.astype(jnp.int32))


def _sc_worker():
    return lax.axis_index("s") * SC_CORES + lax.axis_index("c")


def _sc_dispatch(xpk, pos_km, unused, zero_rows):
    ch = SC_DISPATCH_ROWS
    n_chunks = T_ALL // ch
    n_zero_chunks = N_UNUSED // ch
    width = xpk.shape[1]
    mesh = plsc.VectorSubcoreMesh(core_axis_name="c", subcore_axis_name="s")

    @functools.partial(
        pl.kernel, mesh=mesh,
        out_type=jax.ShapeDtypeStruct((MOE_SLOTS, width), xpk.dtype),
        scratch_types=[pltpu.VMEM((2, ch, width), xpk.dtype), pltpu.VMEM((2, TOP_K, ch), jnp.int32),
                       pltpu.SemaphoreType.DMA((2,)), pltpu.SemaphoreType.DMA((TOP_K,))],
        name="sc_dispatch")
    def run(x_hbm, pos_hbm, unused_hbm, zero_hbm, xs_hbm, rows_v, idx_v, load_sems, store_sems):
        wid = _sc_worker()
        pltpu.sync_copy(zero_hbm, rows_v.at[0])
        for r in range(n_zero_chunks // SC_WORKERS):
            start = pl.multiple_of((wid + r * SC_WORKERS) * ch, ch)
            pltpu.sync_copy(unused_hbm.at[pl.ds(start, ch)], idx_v.at[0, 0])
            pltpu.sync_copy(rows_v.at[0], xs_hbm.at[idx_v.at[0, 0]])

        def first_row(r):
            return pl.multiple_of((wid + r * SC_WORKERS) * ch, ch)

        def rows_copy(r, slot):
            return pltpu.make_async_copy(x_hbm.at[pl.ds(first_row(r), ch)], rows_v.at[slot], load_sems.at[slot])

        def start_load(r, slot):
            for k in range(TOP_K):
                pltpu.sync_copy(pos_hbm.at[k, pl.ds(first_row(r), ch)], idx_v.at[slot, k])
            rows_copy(r, slot).start()

        def has_chunk(r):
            return wid + r * SC_WORKERS < n_chunks

        start_load(0, 0)

        def body(r, carry):
            slot = r & 1

            @pl.when(has_chunk(r + 1))
            def _():
                start_load(r + 1, 1 - slot)

            @pl.when(has_chunk(r))
            def _():
                rows_copy(r, slot).wait()
                copies = [pltpu.async_copy(rows_v.at[slot], xs_hbm.at[idx_v.at[slot, k]], store_sems.at[k])
                          for k in range(TOP_K)]
                for cp in copies:
                    cp.wait()
            return carry

        lax.fori_loop(0, pl.cdiv(n_chunks, SC_WORKERS), body, 0)

    return run(xpk, pos_km, unused, zero_rows)


def _sc_collect(ys, pos_flat):
    ch = SC_COLLECT_ROWS
    per_worker = N_ASSIGN // SC_WORKERS
    mesh = plsc.VectorSubcoreMesh(core_axis_name="c", subcore_axis_name="s")

    @functools.partial(
        pl.kernel, mesh=mesh,
        out_type=jax.ShapeDtypeStruct((N_ASSIGN, ys.shape[1]), ys.dtype),
        scratch_types=[pltpu.VMEM((2, ch, ys.shape[1]), ys.dtype), pltpu.VMEM((2, ch), jnp.int32),
                       pltpu.SemaphoreType.DMA((2,))],
        name="sc_collect")
    def run(ys_hbm, pos_hbm, y4_hbm, rows_v, idx_v, sems):
        wid = _sc_worker()
        n_chunks = per_worker // ch

        def first_row(r):
            return pl.multiple_of(wid * per_worker + r * ch, 8)

        def gather(slot):
            return pltpu.make_async_copy(ys_hbm.at[idx_v.at[slot]], rows_v.at[slot], sems.at[slot])

        def start_gather(r, slot):
            pltpu.sync_copy(pos_hbm.at[pl.ds(first_row(r), ch)], idx_v.at[slot])
            gather(slot).start()

        start_gather(0, 0)

        def body(r, carry):
            slot = r & 1

            @pl.when(r + 1 < n_chunks)
            def _():
                start_gather(r + 1, 1 - slot)

            gather(slot).wait()
            pltpu.sync_copy(rows_v.at[slot], y4_hbm.at[pl.ds(first_row(r), ch)])
            return carry

        lax.fori_loop(0, n_chunks, body, 0)

    return run(ys, pos_flat)


def _combine_norm(x_ref, y_refs, gate_ref, g_ref, b_ref, rows=slice(None)):
    gate = gate_ref[:, rows].T
    lo = hi = None
    for k, y_ref in enumerate(y_refs):
        p = y_ref[0, rows, :]
        g = gate[:, k:k + 1]
        lo_k = g * lax.bitcast_convert_type(p << 16, F32)
        hi_k = g * lax.bitcast_convert_type(p & jnp.uint32(0xFFFF0000), F32)
        lo = lo_k if lo is None else lo + lo_k
        hi = hi_k if hi is None else hi + hi_k
    ffn = jnp.concatenate([lo, hi], axis=1)
    return _layer_norm(DEEPNORM_ALPHA * x_ref[rows, :] + ffn, g_ref[0], b_ref[0])


def _ffn_norm_kernel(x_ref, y0_ref, y1_ref, y2_ref, y3_ref, gate_ref, g_ref, b_ref, o_ref):
    o_ref[...] = _combine_norm(x_ref, (y0_ref, y1_ref, y2_ref, y3_ref), gate_ref, g_ref, b_ref)


def _norm_proj_kernel(x_ref, y0_ref, y1_ref, y2_ref, y3_ref, gate_ref, g_ref, b_ref, w_ref,
                      x2_ref, hm_ref, hkv_ref):
    for r in range(ROW_BLOCK // NORM_SUB_ROWS):
        rows = pl.ds(r * NORM_SUB_ROWS, NORM_SUB_ROWS)
        x2 = _combine_norm(x_ref, (y0_ref, y1_ref, y2_ref, y3_ref), gate_ref, g_ref, b_ref, rows)
        x2_ref[rows, :] = x2
        _project_in(x2, w_ref, hm_ref.at[rows, :], hkv_ref.at[rows, :])


def _norm_proj(x1, y4, gate, ln_g, ln_b, w_in_bf, layer):
    rows = pl.BlockSpec((ROW_BLOCK, D_MODEL), lambda i: (i, 0))

    def plane(k):
        return pl.BlockSpec((1, ROW_BLOCK, D_MODEL // 2), lambda i: (k, i, 0))

    vec = pl.BlockSpec((1, 1, D_MODEL), lambda i: (layer, 0, 0))
    return pl.pallas_call(
        _norm_proj_kernel,
        grid=(T_ALL // ROW_BLOCK,),
        in_specs=[rows, plane(0), plane(1), plane(2), plane(3),
                  pl.BlockSpec((GATE_ROWS, ROW_BLOCK), lambda i: (0, i)), vec, vec,
                  pl.BlockSpec((1, D_MODEL, D_IN), lambda i: (layer + 1, 0, 0))],
        out_specs=[rows] + _IN_PROJ_OUT_SPECS,
        out_shape=[jax.ShapeDtypeStruct((T_ALL, D_MODEL), F32)] + _IN_PROJ_OUT_SHAPES,
        compiler_params=_cparams("parallel"),
        name="norm_proj",
    )(x1, y4, y4, y4, y4, gate, ln_g, ln_b, w_in_bf)


def _ffn_norm(x1, y4, gate, ln_g, ln_b, layer, row_block, first_block, n_blocks):
    def rows(w):
        return pl.BlockSpec((row_block, w), lambda i: (first_block + i, 0))

    def plane(k):
        return pl.BlockSpec((1, row_block, D_MODEL // 2), lambda i: (k, first_block + i, 0))

    vec = pl.BlockSpec((1, 1, D_MODEL), lambda i: (layer, 0, 0))
    return pl.pallas_call(
        _ffn_norm_kernel,
        grid=(n_blocks,),
        in_specs=[rows(D_MODEL), plane(0), plane(1), plane(2), plane(3),
                  pl.BlockSpec((GATE_ROWS, row_block), lambda i: (0, first_block + i)), vec, vec],
        out_specs=pl.BlockSpec((row_block, D_MODEL), lambda i: (i, 0)),
        out_shape=jax.ShapeDtypeStruct((n_blocks * row_block, D_MODEL), F32),
        compiler_params=_cparams("parallel"),
        name="ffn_norm",
    )(x1, y4, y4, y4, y4, gate, ln_g, ln_b)


def _rope_tables(pos):
    half = HD_RET // 2
    inv = ROPE_BASE ** (-jnp.arange(half, dtype=F32) / half)
    ang = pos.astype(F32)[:, None] * inv[None, :]
    cos, sin = jnp.cos(ang), jnp.sin(ang)
    return jnp.concatenate([cos, cos], -1), jnp.concatenate([-sin, sin], -1)


def _decay_tables(c):
    log_g = jnp.log1p(-jnp.exp2(-5.0 - jnp.arange(H_RET, dtype=F32)))
    idx = jnp.arange(c, dtype=F32)
    diff = idx[:, None] - idx[None, :]
    intra = jnp.where(diff[None] >= 0, jnp.exp(jnp.maximum(diff, 0.0)[None] * log_g[:, None, None]), 0.0)
    q_dec = jnp.exp((idx + 1.0)[None] * log_g[:, None])
    k_dec = jnp.exp((c - 1.0 - idx)[None] * log_g[:, None])
    c_dec = jnp.exp(c * log_g)
    return intra, q_dec, k_dec, c_dec


def _t5_bucket(dist):
    n = jnp.maximum(dist, 0)
    max_exact = NUM_BUCKETS // 2
    nf = jnp.maximum(n, 1).astype(F32)
    large = max_exact + (jnp.log(nf / max_exact) / math.log(REL_MAX_DIST / max_exact)
                         * (NUM_BUCKETS - max_exact)).astype(jnp.int32)
    large = jnp.minimum(large, NUM_BUCKETS - 1)
    return jnp.where(n < max_exact, n, large)


def _bias_table(rel_bias, bucket):
    hit = bucket[..., None] == jnp.arange(NUM_BUCKETS)
    per_head = rel_bias.astype(F32).T.reshape((H_SWA,) + (1,) * bucket.ndim + (NUM_BUCKETS,))
    return jnp.sum(jnp.where(hit[None], per_head, 0.0), axis=-1)


def kernel(x_prompt, x_sample, state_ret, cache_swa_k, cache_swa_v, w_in, w_out, ret_gn_gain, swa_sinks,
           rel_bias, ln1_g, ln1_b, ln2_g, ln2_b, router_w, router_b, w_gate_up, b_gate_up, w_down, b_down):
    assert x_prompt.shape == (BATCH, SEQ, D_MODEL) and x_sample.shape == (DEC_BATCH, 1, D_MODEL)
    x = (x_prompt, x_sample)
    w_in_bf = w_in.astype(BF16)
    w_out_bf = w_out.astype(BF16)
    router_w_bf = jnp.swapaxes(router_w, 1, 2).astype(BF16)
    router_b = router_b.reshape(DEPTH, N_EXPERTS, 1)
    ret_gn_gain, ln1_g, ln1_b, ln2_g, ln2_b = (
        a.reshape(DEPTH, 1, a.shape[-1]) for a in (ret_gn_gain, ln1_g, ln1_b, ln2_g, ln2_b))
    b_gu4 = b_gate_up.reshape(DEPTH, N_EXPERTS, 1, 2 * D_FF)
    b_d4 = b_down.reshape(DEPTH, N_EXPERTS, 1, D_MODEL)
    cache_k = jnp.transpose(cache_swa_k, (0, 1, 3, 4, 2)).reshape(DEPTH, DEC_BATCH, 2 * HD_SWA, WINDOW)
    cache_v = jnp.transpose(cache_swa_v, (0, 1, 3, 4, 2)).reshape(DEPTH, DEC_BATCH, 2 * HD_SWA, WINDOW)

    cos_p, sin_p = _rope_tables(jnp.arange(SEQ, dtype=jnp.int32))
    cos_s, sin_s = _rope_tables(PAST_LEN + jnp.arange(1, dtype=jnp.int32))
    intra, q_dec, k_dec, c_dec = _decay_tables(RET_CHUNK)
    ones = jnp.ones((1, 1, HD_RET), F32)
    tables_p = (cos_p, sin_p, intra, q_dec[:, :, None] * ones, k_dec[:, :, None] * ones, c_dec[:, None, None] * ones)
    intra1, q_dec1, k_dec1, c_dec1 = _decay_tables(1)
    dec_s = jnp.stack([intra1[:, 0, 0], q_dec1[:, 0], k_dec1[:, 0], c_dec1], axis=1)
    i = jnp.arange(WINDOW)[:, None]
    j = jnp.arange(2 * WINDOW)[None, :]
    in_window = (WINDOW + i - j >= 0) & (WINDOW + i - j < WINDOW)
    has_prev = jnp.array([False, True])[:, None, None, None]
    allowed = in_window[None, None] & (has_prev | (j >= WINDOW)[None, None])
    bias_p = jnp.where(allowed, _bias_table(rel_bias, _t5_bucket(WINDOW + i - j))[None], -jnp.inf)
    js = jnp.arange(WINDOW + 1)
    bias_s = _bias_table(rel_bias, _t5_bucket(WINDOW - js))
    bias_c, bias_n = bias_s[:, :WINDOW], bias_s[:, WINDOW:]
    kv_of_head = jnp.arange(H_SWA) // G_SWA
    head_mask = (kv_of_head[:, None] == jnp.arange(KV_SWA)[None, :]).astype(F32)

    zero_rows = jnp.zeros((SC_DISPATCH_ROWS, D_MODEL // 2), jnp.uint32)

    sp = ss = ko = vo = tail = None
    h, h_kv = _in_proj(x_prompt, x_sample, w_in_bf)
    for l in range(DEPTH):
        o, sp, tail = _mix_prompt(h, h_kv, tables_p, ret_gn_gain, swa_sinks[l], bias_p, sp, tail, l)
        hs = h[T_PROMPT:].astype(F32)
        kvs = h_kv[T_PROMPT:]
        o, ss = _ret_sample(hs, dec_s, cos_s, sin_s, ret_gn_gain, state_ret, o, ss, l)
        q8 = hs[:, 4 * RET_W:].reshape(DEC_BATCH, H_SWA, HD_SWA)
        qm = (q8[:, :, None, :] * head_mask[None, :, :, None]).reshape(DEC_BATCH, H_SWA, 2 * HD_SWA)
        kn = kvs[:, :SWA_KV_W].reshape(DEC_BATCH, 1, SWA_KV_W)
        vn = kvs[:, SWA_KV_W:].reshape(DEC_BATCH, 1, SWA_KV_W)
        osm, ko, vo = _swa_sample(qm, kn, vn, cache_k, cache_v, bias_c, bias_n, swa_sinks[l][:, None], ko, vo, l)
        os_s = jnp.sum(osm.reshape(DEC_BATCH, H_SWA, KV_SWA, HD_SWA) * head_mask[None, :, :, None], axis=2)
        o = lax.dynamic_update_slice(o, os_s.reshape(DEC_BATCH, SWA_Q_W), (T_PROMPT, RET_W))

        x1, xpk, top_idx, gate, rank, counts = _out_proj(o, x, w_out_bf, ln1_g, ln1_b, router_w_bf, router_b, l)
        pos_km, unused, group_start, group_blocks, group_next = _route(top_idx, rank, counts)
        xs = _sc_dispatch(xpk, pos_km, unused, zero_rows)
        ys = _moe(group_start, group_blocks, group_next, xs, w_gate_up, b_gu4, w_down, b_d4, l)
        y4 = _sc_collect(ys, pos_km.reshape(N_ASSIGN)).reshape(TOP_K, T_ALL, D_MODEL // 2)
        if l < DEPTH - 1:
            x, h, h_kv = _norm_proj(x1, y4, gate, ln2_g, ln2_b, w_in_bf, l)
        else:
            yp = _ffn_norm(x1, y4, gate, ln2_g, ln2_b, l, FINAL_ROW_BLOCK, 0, T_PROMPT // FINAL_ROW_BLOCK)
            ys_out = _ffn_norm(x1, y4, gate, ln2_g, ln2_b, l, DEC_BATCH, T_PROMPT // DEC_BATCH, 1)

    yp = yp.reshape(BATCH, SEQ, D_MODEL)
    ys_out = ys_out.reshape(DEC_BATCH, 1, D_MODEL)

    def window_major(t):
        return jnp.transpose(t.reshape(DEPTH, t.shape[1], KV_SWA, HD_SWA, WINDOW), (0, 1, 4, 2, 3))

    return (yp, ys_out, sp, window_major(tail[:, :, :SWA_KV_W]), window_major(tail[:, :, SWA_KV_W:]), ss,
            window_major(ko), window_major(vo))
```

```python
import functools
import math

import jax
import jax.numpy as jnp
from jax import lax
from jax.experimental import pallas as pl
from jax.experimental.pallas import tpu as pltpu
from jax.experimental.pallas import tpu_sc as plsc

F32 = jnp.float32
BF16 = jnp.bfloat16

D_MODEL = 1024
BATCH = 8
SEQ = 2048
DEPTH = 4
DEC_BATCH = 128
PAST_LEN = 8192
H_RET = 4
HD_RET = 128
RET_CHUNK = 128
ROPE_BASE = 10000.0
H_SWA = 8
KV_SWA = 2
G_SWA = H_SWA // KV_SWA
HD_SWA = 64
WINDOW = 128
NUM_BUCKETS = 32
REL_MAX_DIST = 128
RET_W = H_RET * HD_RET
SWA_Q_W = H_SWA * HD_SWA
SWA_KV_W = KV_SWA * HD_SWA
D_IN = 4 * RET_W + SWA_Q_W + 2 * SWA_KV_W
MIX_W = RET_W + SWA_Q_W
H_MAIN_W = 4 * RET_W + SWA_Q_W
N_EXPERTS = 32
TOP_K = 4
D_FF = D_MODEL
SWIGLU_LIMIT = 7.0
SWIGLU_ALPHA = 1.702
DEEPNORM_ALPHA = (2 * DEPTH) ** 0.25
LN_EPS = 1e-5
GN_EPS = 1e-5

T_PROMPT = BATCH * SEQ
T_ALL = T_PROMPT + DEC_BATCH
ROW_BLOCK = 384
SAMPLE_BLOCK = 32
MOE_ROWS = 128
MOE_CHUNK = 1024
MOE_PIECE = 1024
N_ASSIGN = T_ALL * TOP_K
MOE_BLOCKS = N_ASSIGN // MOE_ROWS + N_EXPERTS
MOE_SLOTS = MOE_BLOCKS * MOE_ROWS
N_UNUSED = MOE_SLOTS - N_ASSIGN
SC_CORES = 2
SC_SUBCORES = 16
SC_WORKERS = SC_CORES * SC_SUBCORES
SC_DISPATCH_ROWS = 64
SC_COLLECT_ROWS = 48
FINAL_ROW_BLOCK = 512
NORM_SUB_ROWS = 128
MIX_BLOCKS = 8
GATE_ROWS = 8
VMEM_LIMIT = 56 * 1024 * 1024

assert T_ALL % ROW_BLOCK == 0 and ROW_BLOCK % NORM_SUB_ROWS == 0 and T_PROMPT % FINAL_ROW_BLOCK == 0
assert DEC_BATCH % SAMPLE_BLOCK == 0 and (SEQ // WINDOW) % MIX_BLOCKS == 0 and RET_CHUNK == WINDOW
assert MOE_CHUNK % MOE_ROWS == 0 and MOE_CHUNK % MOE_PIECE == 0 and N_ASSIGN % MOE_ROWS == 0
assert T_ALL % SC_DISPATCH_ROWS == 0 and N_UNUSED % (SC_DISPATCH_ROWS * SC_WORKERS) == 0
assert N_ASSIGN % SC_WORKERS == 0 and (N_ASSIGN // SC_WORKERS) % SC_COLLECT_ROWS == 0


def _cparams(*sem):
    return pltpu.CompilerParams(dimension_semantics=sem, vmem_limit_bytes=VMEM_LIMIT)


def _project_in(x, w_ref, hm_ref, hkv_ref):
    h = jnp.dot(x.astype(BF16), w_ref[0], preferred_element_type=F32)
    hm_ref[...] = h[:, :H_MAIN_W].astype(BF16)
    hkv_ref[...] = h[:, H_MAIN_W:]


class _InputRows:
    N = T_ALL // ROW_BLOCK
    PROMPT_TAIL = ROW_BLOCK - DEC_BATCH

    @classmethod
    def specs(cls):
        assert (cls.N - 1) * ROW_BLOCK + cls.PROMPT_TAIL == T_PROMPT and T_PROMPT % cls.PROMPT_TAIL == 0
        return [pl.BlockSpec((ROW_BLOCK, D_MODEL), lambda i: (jnp.minimum(i, cls.N - 2), 0)),
                pl.BlockSpec((cls.PROMPT_TAIL, D_MODEL), lambda i: (T_PROMPT // cls.PROMPT_TAIL - 1, 0)),
                pl.BlockSpec((DEC_BATCH, D_MODEL), lambda i: (0, 0))]

    @staticmethod
    def args(x_prompt, x_sample):
        xp = x_prompt.reshape(T_PROMPT, D_MODEL)
        return [xp, xp, x_sample.reshape(DEC_BATCH, D_MODEL)]

    @classmethod
    def read(cls, main_ref, tail_ref, sample_ref):
        last = jnp.concatenate([tail_ref[...], sample_ref[...]], axis=0)
        return jnp.where(pl.program_id(0) == cls.N - 1, last, main_ref[...])


def _in_proj_kernel(xm_ref, xt_ref, xs_ref, w_ref, hm_ref, hkv_ref):
    _project_in(_InputRows.read(xm_ref, xt_ref, xs_ref), w_ref, hm_ref, hkv_ref)


_IN_PROJ_OUT_SPECS = [pl.BlockSpec((ROW_BLOCK, H_MAIN_W), lambda i: (i, 0)),
                      pl.BlockSpec((ROW_BLOCK, 2 * SWA_KV_W), lambda i: (i, 0))]
_IN_PROJ_OUT_SHAPES = [jax.ShapeDtypeStruct((T_ALL, H_MAIN_W), BF16),
                       jax.ShapeDtypeStruct((T_ALL, 2 * SWA_KV_W), F32)]


def _in_proj(x_prompt, x_sample, w_in_bf):
    return pl.pallas_call(
        _in_proj_kernel,
        grid=(T_ALL // ROW_BLOCK,),
        in_specs=_InputRows.specs() + [pl.BlockSpec((1, D_MODEL, D_IN), lambda i: (0, 0, 0))],
        out_specs=_IN_PROJ_OUT_SPECS,
        out_shape=_IN_PROJ_OUT_SHAPES,
        compiler_params=_cparams("parallel"),
        name="in_proj",
    )(*_InputRows.args(x_prompt, x_sample), w_in_bf)


def _rotate(x, cos, sin_signed):
    return x * cos + pltpu.roll(x, HD_RET // 2, 1) * sin_signed


def _group_norm_gate(o, gain, g):
    mu = jnp.mean(o, -1, keepdims=True)
    var = jnp.mean(jnp.square(o - mu), -1, keepdims=True)
    return (o - mu) * lax.rsqrt(var + GN_EPS) * gain * (g * jax.nn.sigmoid(g))


def _layer_norm(y, g, b):
    mu = jnp.mean(y, -1, keepdims=True)
    var = jnp.mean(jnp.square(y - mu), -1, keepdims=True)
    return (y - mu) * lax.rsqrt(var + LN_EPS) * g + b


def _ret_prompt_stages(q_ref, k_ref, v_ref, g_ref, cos_ref, sin_ref, intra_ref, qdec_ref, kdec_ref,
                       cdec_ref, gain_ref, o_ref, s_ref, rows):
    cos = cos_ref[rows, :]
    sin = sin_ref[rows, :]
    nt = (((1,), (1,)), ((), ()))
    tn = (((0,), (0,)), ((), ()))
    heads = [slice(h * HD_RET, (h + 1) * HD_RET) for h in range(H_RET)]
    vs, atts, crosses = [], [], []
    for h, sl in enumerate(heads):
        q = _rotate(q_ref[rows, sl].astype(F32), cos, sin)
        k = _rotate(k_ref[rows, sl].astype(F32), cos, sin) * (HD_RET ** -0.5)
        v = v_ref[rows, sl]
        s = s_ref[0, 0, h]
        atts.append(lax.dot_general(q.astype(BF16), k.astype(BF16), nt, preferred_element_type=F32))
        crosses.append(jnp.dot((q * qdec_ref[h]).astype(BF16), s.astype(BF16), preferred_element_type=F32))
        s_ref[0, 0, h] = s * cdec_ref[h] + lax.dot_general((k * kdec_ref[h]).astype(BF16), v, tn,
                                                          preferred_element_type=F32)
        vs.append(v)
    yield
    atts = [(att * intra_ref[h]).astype(BF16) for h, att in enumerate(atts)]
    yield
    outs = [jnp.dot(atts[h], vs[h], preferred_element_type=F32) + crosses[h] for h in range(H_RET)]
    yield
    for h, sl in enumerate(heads):
        o_ref[rows, sl] = _group_norm_gate(outs[h], gain_ref[0, :, sl], g_ref[rows, sl].astype(F32))


def _swa_prompt_stages(sink_ref, q_ref, kv_prev, kv_cur, bias, o_ref, rows, lg_scr, p_scr):
    kv = jnp.concatenate([kv_prev, kv_cur], axis=0)
    k_t = kv[:, :SWA_KV_W].T
    k_t_swapped = pltpu.roll(k_t, HD_SWA, 0)
    v = kv[:, SWA_KV_W:]
    v_swapped = pltpu.roll(v, HD_SWA, 1)
    k_row = lax.broadcasted_iota(jnp.int32, k_t.shape, 0)
    v_lane = lax.broadcasted_iota(jnp.int32, v.shape, 1)
    ones = jnp.ones(v.shape, BF16)
    k_pad, v_pad = {}, {}
    for kvh in range(KV_SWA):
        for par in range(2):
            k_src = k_t if kvh == par else k_t_swapped
            v_src = v if kvh == par else v_swapped
            k_pad[kvh, par] = jnp.where((k_row >= par * HD_SWA) & (k_row < (par + 1) * HD_SWA), k_src, 0.0).astype(BF16)
            v_pad[kvh, par] = jnp.where((v_lane >= par * HD_SWA) & (v_lane < (par + 1) * HD_SWA), v_src, 0.0).astype(BF16)
    for hd in range(H_SWA):
        pair, par, kvh = hd // 2, hd % 2, hd // G_SWA
        q2 = (q_ref[rows, pair * 2 * HD_SWA:(pair + 1) * 2 * HD_SWA] * (HD_SWA ** -0.5)).astype(BF16)
        lg_scr[hd] = jnp.dot(q2, k_pad[kvh, par], preferred_element_type=F32) + bias(hd)
    yield
    sink_terms = []
    for hd in range(H_SWA):
        logits = lg_scr[hd]
        sink = sink_ref[hd]
        m = jnp.maximum(jnp.max(logits, -1, keepdims=True), sink)
        p_scr[hd] = jnp.exp(logits - m).astype(BF16)
        sink_terms.append(jnp.exp(sink - m))
    yield
    for pair in range(H_SWA // 2):
        acc = None
        for par in range(2):
            hd = 2 * pair + par
            p = p_scr[hd]
            den = jnp.dot(p, ones, preferred_element_type=F32) + sink_terms[hd]
            part = jnp.dot(p, v_pad[hd // G_SWA, par], preferred_element_type=F32) / den
            acc = part if acc is None else acc + part
        o_ref[rows, pair * 2 * HD_SWA:(pair + 1) * 2 * HD_SWA] = acc


def _mix_prompt_kernel(*refs):
    n_ret_in = 11
    ret_in = refs[:n_ret_in]
    sink_ref, q_ref, kvc_ref, kvp_ref, bias_ref = refs[n_ret_in:n_ret_in + 5]
    o_ref, s_ref, tail_ref, lg_scr, p_scr = refs[-5:]
    o_r_ref = o_ref.at[:, pl.ds(0, RET_W)]
    o_s_ref = o_ref.at[:, pl.ds(RET_W, SWA_Q_W)]
    c = pl.program_id(1)

    @pl.when(c == 0)
    def _():
        s_ref[...] = jnp.zeros_like(s_ref)

    @pl.when(c == pl.num_programs(1) - 1)
    def _():
        tail_ref[0, 0] = kvc_ref[pl.ds((MIX_BLOCKS - 1) * WINDOW, WINDOW), :].T

    groups = []
    for j in range(MIX_BLOCKS):
        rows = pl.ds(j * WINDOW, WINDOW)
        kv_cur = kvc_ref[rows, :]
        if j == 0:
            kv_prev = kvp_ref[...]
            table = jnp.minimum(c, 1)
        else:
            kv_prev = kvc_ref[pl.ds((j - 1) * WINDOW, WINDOW), :]
            table = 1
        groups.append(_ret_prompt_stages(*ret_in, o_r_ref, s_ref, rows))

        def bias(hd, table=table):
            return bias_ref[table, hd]

        groups.append(_swa_prompt_stages(sink_ref, q_ref, kv_prev, kv_cur, bias, o_s_ref, rows,
                                         lg_scr.at[j], p_scr.at[j]))
    while groups:
        for g in list(groups):
            if next(g, groups) is groups:
                groups.remove(g)


def _mix_prompt(h, h_kv, tables, gain, sinks_l, bias, s_prev, tail_prev, layer):
    cos, sin, intra, qdec, kdec, cdec = tables
    nb = SEQ // WINDOW
    ns = nb // MIX_BLOCKS
    rows = MIX_BLOCKS * WINDOW
    qcol = 4 * RET_W // SWA_Q_W

    def col(j):
        return pl.BlockSpec((rows, RET_W), lambda b, c: (b * ns + c, j))

    def whole(a):
        return pl.BlockSpec(a.shape, lambda b, c: (0,) * a.ndim)

    in_specs = [col(0), col(1), col(2), col(3),
                pl.BlockSpec((rows, HD_RET), lambda b, c: (c, 0)),
                pl.BlockSpec((rows, HD_RET), lambda b, c: (c, 0)),
                whole(intra), whole(qdec), whole(kdec), whole(cdec),
                pl.BlockSpec((1, 1, RET_W), lambda b, c: (layer, 0, 0)),
                pl.BlockSpec(memory_space=pltpu.SMEM),
                pl.BlockSpec((rows, SWA_Q_W), lambda b, c: (b * ns + c, qcol)),
                pl.BlockSpec((rows, 2 * SWA_KV_W), lambda b, c: (b * ns + c, 0)),
                pl.BlockSpec((WINDOW, 2 * SWA_KV_W), lambda b, c: (b * nb + jnp.maximum(MIX_BLOCKS * c - 1, 0), 0)),
                whole(bias)]
    args = [h, h, h, h, cos, sin, intra, qdec, kdec, cdec, gain, sinks_l, h, h_kv, h_kv, bias]
    aliases = {}
    if s_prev is not None:
        in_specs += [pl.BlockSpec(memory_space=pl.ANY), pl.BlockSpec(memory_space=pl.ANY)]
        args += [s_prev, tail_prev]
        aliases = {16: 1, 17: 2}

    return pl.pallas_call(
        _mix_prompt_kernel,
        grid=(BATCH, ns),
        in_specs=in_specs,
        scratch_shapes=[pltpu.VMEM((MIX_BLOCKS, H_SWA, WINDOW, 2 * WINDOW), F32),
                        pltpu.VMEM((MIX_BLOCKS, H_SWA, WINDOW, 2 * WINDOW), BF16)],
        out_specs=[pl.BlockSpec((rows, MIX_W), lambda b, c: (b * ns + c, 0)),
                   pl.BlockSpec((1, 1, H_RET, HD_RET, HD_RET), lambda b, c: (layer, b, 0, 0, 0)),
                   pl.BlockSpec((1, 1, 2 * SWA_KV_W, WINDOW), lambda b, c: (layer, b, 0, 0))],
        out_shape=[jax.ShapeDtypeStruct((T_ALL, MIX_W), F32),
                   jax.ShapeDtypeStruct((DEPTH, BATCH, H_RET, HD_RET, HD_RET), F32),
                   jax.ShapeDtypeStruct((DEPTH, BATCH, 2 * SWA_KV_W, WINDOW), F32)],
        input_output_aliases=aliases,
        compiler_params=_cparams("parallel", "arbitrary"),
        name="mix_prompt",
    )(*args)


def _ret_sample_kernel(dec_ref, q_ref, k_ref, v_ref, g_ref, cos_ref, sin_ref, gain_ref, st_ref, o_ref, so_ref):
    cos = cos_ref[...]
    sin = sin_ref[...]
    for h in range(H_RET):
        sl = slice(h * HD_RET, (h + 1) * HD_RET)
        intra, qdec, kdec, cdec = dec_ref[h, 0], dec_ref[h, 1], dec_ref[h, 2], dec_ref[h, 3]
        q = _rotate(q_ref[:, sl], cos, sin)
        k = _rotate(k_ref[:, sl], cos, sin) * (HD_RET ** -0.5)
        v = v_ref[:, sl]
        att = jnp.sum(q * k, -1, keepdims=True) * intra
        q_col = (q * qdec).T
        k_col = (k * kdec).T
        rows = []
        for b in range(SAMPLE_BLOCK):
            s = st_ref[0, b, h]
            v_row = v[b:b + 1, :]
            rows.append(att[b:b + 1, :] * v_row + jnp.sum(q_col[:, b:b + 1] * s, axis=0, keepdims=True))
            so_ref[0, b, h] = s * cdec + k_col[:, b:b + 1] * v_row
        o = jnp.concatenate(rows, axis=0)
        o_ref[:, sl] = _group_norm_gate(o, gain_ref[0, :, sl], g_ref[:, sl])


def _ret_sample(h, dec_s, cos_s, sin_s, gain, state_ret, o_r, st_prev, layer):
    base = T_PROMPT // SAMPLE_BLOCK

    def col(j):
        return pl.BlockSpec((SAMPLE_BLOCK, RET_W), lambda i: (i, j))

    in_specs = [pl.BlockSpec(memory_space=pltpu.SMEM),
                col(0), col(1), col(2), col(3),
                pl.BlockSpec((1, HD_RET), lambda i: (0, 0)),
                pl.BlockSpec((1, HD_RET), lambda i: (0, 0)),
                pl.BlockSpec((1, 1, RET_W), lambda i: (layer, 0, 0)),
                pl.BlockSpec((1, SAMPLE_BLOCK, H_RET, HD_RET, HD_RET), lambda i: (layer, i, 0, 0, 0)),
                pl.BlockSpec(memory_space=pl.ANY)]
    args = [dec_s, h, h, h, h, cos_s, sin_s, gain, state_ret, o_r]
    aliases = {9: 0}
    if st_prev is not None:
        in_specs.append(pl.BlockSpec(memory_space=pl.ANY))
        args.append(st_prev)
        aliases[10] = 1

    def body(*refs):
        _ret_sample_kernel(*refs[:9], *refs[-2:])

    return pl.pallas_call(
        body,
        grid=(DEC_BATCH // SAMPLE_BLOCK,),
        in_specs=in_specs,
        out_specs=[pl.BlockSpec((SAMPLE_BLOCK, RET_W), lambda i: (base + i, 0)),
                   pl.BlockSpec((1, SAMPLE_BLOCK, H_RET, HD_RET, HD_RET), lambda i: (layer, i, 0, 0, 0))],
        out_shape=[jax.ShapeDtypeStruct((T_ALL, MIX_W), F32),
                   jax.ShapeDtypeStruct((DEPTH, DEC_BATCH, H_RET, HD_RET, HD_RET), F32)],
        input_output_aliases=aliases,
        compiler_params=_cparams("parallel"),
        name="ret_sample",
    )(*args)


def _swa_sample_kernel(qm_ref, kn_ref, vn_ref, knc_ref, vnc_ref, kc_ref, vc_ref, bias_ref, biasn_ref, sink_ref,
                       o_ref, ko_ref, vo_ref):
    qm = qm_ref[...]
    kn = kn_ref[...]
    vn = vn_ref[...]
    kc = kc_ref[0]
    vc = vc_ref[0]
    scale = HD_SWA ** -0.5
    logits = jnp.einsum('bhd,bdj->bhj', qm.astype(BF16), kc.astype(BF16),
                        preferred_element_type=F32) * scale + bias_ref[...][None]
    j = lax.broadcasted_iota(jnp.int32, logits.shape, 2)
    logits = jnp.where(j >= 1, logits, -jnp.inf)
    ln = jnp.sum(qm * kn, -1, keepdims=True) * scale + biasn_ref[...][None]
    sink = sink_ref[...][None]
    m = jnp.maximum(jnp.maximum(jnp.max(logits, -1, keepdims=True), ln), sink)
    p = jnp.exp(logits - m)
    pn = jnp.exp(ln - m)
    denom = jnp.sum(p, -1, keepdims=True) + pn + jnp.exp(sink - m)
    w = p / denom
    o_ref[...] = (jnp.einsum('bhj,bdj->bhd', w.astype(BF16), vc.astype(BF16), preferred_element_type=F32)
                  + (pn / denom) * vn)
    last = lax.broadcasted_iota(jnp.int32, kc.shape, 2) == WINDOW - 1
    ko_ref[0] = jnp.where(last, knc_ref[...], pltpu.roll(kc, WINDOW - 1, 2))
    vo_ref[0] = jnp.where(last, vnc_ref[...], pltpu.roll(vc, WINDOW - 1, 2))


def _swa_sample(qm, kn, vn, cache_k, cache_v, bias_c, bias_n, sink_col, ko_prev, vo_prev, layer):
    sb = SAMPLE_BLOCK
    cache_spec = pl.BlockSpec((1, sb, 2 * HD_SWA, WINDOW), lambda i: (layer, i, 0, 0))
    row_spec = pl.BlockSpec((sb, 1, 2 * HD_SWA), lambda i: (i, 0, 0))
    col_spec = pl.BlockSpec((sb, 2 * HD_SWA, 1), lambda i: (i, 0, 0))
    in_specs = [pl.BlockSpec((sb, H_SWA, 2 * HD_SWA), lambda i: (i, 0, 0)),
                row_spec, row_spec, col_spec, col_spec,
                cache_spec, cache_spec,
                pl.BlockSpec((H_SWA, WINDOW), lambda i: (0, 0)),
                pl.BlockSpec((H_SWA, 1), lambda i: (0, 0)),
                pl.BlockSpec((H_SWA, 1), lambda i: (0, 0))]
    args = [qm, kn, vn, kn.reshape(DEC_BATCH, 2 * HD_SWA, 1), vn.reshape(DEC_BATCH, 2 * HD_SWA, 1),
            cache_k, cache_v, bias_c, bias_n, sink_col]
    aliases = {}
    if ko_prev is not None:
        in_specs += [pl.BlockSpec(memory_space=pl.ANY), pl.BlockSpec(memory_space=pl.ANY)]
        args += [ko_prev, vo_prev]
        aliases = {10: 1, 11: 2}

    def body(*refs):
        _swa_sample_kernel(*refs[:10], *refs[-3:])

    cshape = jax.ShapeDtypeStruct((DEPTH, DEC_BATCH, 2 * HD_SWA, WINDOW), F32)
    return pl.pallas_call(
        body,
        grid=(DEC_BATCH // sb,),
        in_specs=in_specs,
        out_specs=[pl.BlockSpec((sb, H_SWA, 2 * HD_SWA), lambda i: (i, 0, 0)), cache_spec, cache_spec],
        out_shape=[jax.ShapeDtypeStruct((DEC_BATCH, H_SWA, 2 * HD_SWA), F32), cshape, cshape],
        input_output_aliases=aliases,
        compiler_params=_cparams("parallel"),
        name="swa_sample",
    )(*args)


def _pack_bf16_pairs(x):
    w = x.shape[1] // 2
    lo = lax.bitcast_convert_type(x[:, :w].astype(BF16).astype(F32), jnp.uint32) >> 16
    hi = lax.bitcast_convert_type(x[:, w:].astype(BF16).astype(F32), jnp.uint32)
    return lo | hi


def _unpack_bf16_pairs(p):
    lo = lax.bitcast_convert_type(p << 16, F32).astype(BF16)
    hi = lax.bitcast_convert_type(p & jnp.uint32(0xFFFF0000), F32).astype(BF16)
    return lo, hi


def _out_proj_kernel(n_x_refs, o_ref, *refs):
    x_refs, (w_ref, g_ref, b_ref, rw_ref, rb_ref,
             x1_ref, xpk_ref, idx_ref, gate_ref, rank_ref, cnt_ref) = refs[:n_x_refs], refs[n_x_refs:]
    x = x_refs[0][...] if n_x_refs == 1 else _InputRows.read(*x_refs)
    nt = (((1,), (1,)), ((), ()))
    mix = jnp.dot(o_ref[...].astype(BF16), w_ref[0], preferred_element_type=F32)
    x1 = _layer_norm(DEEPNORM_ALPHA * x + mix, g_ref[0], b_ref[0])
    x1_ref[...] = x1
    xpk_ref[...] = _pack_bf16_pairs(x1)
    logits = lax.dot_general(rw_ref[0], x1.astype(BF16), nt, preferred_element_type=F32) + rb_ref[0]
    rows = logits.shape[1]
    expert = lax.broadcasted_iota(jnp.int32, logits.shape, 0)
    vals, hits = [], []
    for kk in range(TOP_K):
        m = jnp.max(logits, 0, keepdims=True)
        idx = jnp.min(jnp.where(logits == m, expert, N_EXPERTS), 0, keepdims=True)
        vals.append(m)
        hits.append(expert == idx)
        idx_ref[kk:kk + 1, :] = idx
        logits = jnp.where(expert == idx, -jnp.inf, logits)
    es = [jnp.exp(v - vals[0]) for v in vals]
    tot = es[0] + es[1] + es[2] + es[3]
    gate_ref[...] = jnp.concatenate([e / tot for e in es] + [jnp.zeros((GATE_ROWS - TOP_K, rows), F32)], axis=0)
    chosen = sum(h.astype(F32) for h in hits)
    earlier = (lax.broadcasted_iota(jnp.int32, (rows, rows), 0)
               < lax.broadcasted_iota(jnp.int32, (rows, rows), 1)).astype(BF16)
    before = jnp.dot(chosen.astype(BF16), earlier, preferred_element_type=F32)
    for kk in range(TOP_K):
        rank_ref[kk:kk + 1, :] = jnp.sum(jnp.where(hits[kk], before, 0.0), 0, keepdims=True).astype(jnp.int32)
    cnt_ref[0] = jnp.sum(chosen, axis=1, keepdims=True).astype(jnp.int32)


def _out_proj(o, x, w_out_bf, ln_g, ln_b, router_w_bf, router_b, layer):
    def rows(w):
        return pl.BlockSpec((ROW_BLOCK, w), lambda i: (i, 0))

    def vec(w):
        return pl.BlockSpec((1, 1, w), lambda i: (layer, 0, 0))

    def cols(h):
        return pl.BlockSpec((h, ROW_BLOCK), lambda i: (0, i))

    x_args, x_specs = ([x], [rows(D_MODEL)]) if not isinstance(x, tuple) else (_InputRows.args(*x), _InputRows.specs())
    return pl.pallas_call(
        functools.partial(_out_proj_kernel, len(x_args)),
        grid=(T_ALL // ROW_BLOCK,),
        in_specs=[rows(MIX_W)] + x_specs + [
                  pl.BlockSpec((1, MIX_W, D_MODEL), lambda i: (layer, 0, 0)),
                  vec(D_MODEL), vec(D_MODEL),
                  pl.BlockSpec((1, N_EXPERTS, D_MODEL), lambda i: (layer, 0, 0)),
                  pl.BlockSpec((1, N_EXPERTS, 1), lambda i: (layer, 0, 0))],
        out_specs=[rows(D_MODEL), rows(D_MODEL // 2), cols(TOP_K), cols(GATE_ROWS), cols(TOP_K),
                   pl.BlockSpec((1, N_EXPERTS, 1), lambda i: (i, 0, 0))],
        out_shape=[jax.ShapeDtypeStruct((T_ALL, D_MODEL), F32),
                   jax.ShapeDtypeStruct((T_ALL, D_MODEL // 2), jnp.uint32),
                   jax.ShapeDtypeStruct((TOP_K, T_ALL), jnp.int32),
                   jax.ShapeDtypeStruct((GATE_ROWS, T_ALL), F32),
                   jax.ShapeDtypeStruct((TOP_K, T_ALL), jnp.int32),
                   jax.ShapeDtypeStruct((T_ALL // ROW_BLOCK, N_EXPERTS, 1), jnp.int32)],
        compiler_params=_cparams("parallel"),
        name="out_proj",
    )(o, *x_args, w_out_bf, ln_g, ln_b, router_w_bf, router_b)


def _moe_kernel(start_ref, nblk_ref, next_ref, xs_hbm, wgu_ref, bgu_ref, wd_ref, bd_ref, ys_hbm,
                wgu_bf, wd_bf, x_buf, y_buf, in_sems, out_sems, pending, ready):
    e = pl.program_id(0)
    per_chunk = MOE_CHUNK // MOE_ROWS
    tail_sizes = [t * MOE_ROWS for t in range(1, per_chunk)]
    nblk = nblk_ref[e]
    n_full = nblk // per_chunk
    tail_blocks = nblk % per_chunk
    base = start_ref[e]

    def in_copy(row0, rows, slot):
        src = xs_hbm.at[pl.ds(pl.multiple_of(row0, MOE_ROWS), rows)]
        return pltpu.make_async_copy(src, x_buf.at[slot, pl.ds(0, rows)], in_sems.at[slot])

    def out_copy(row0, rows, slot):
        dst = ys_hbm.at[pl.ds(pl.multiple_of(row0, MOE_ROWS), rows)]
        return pltpu.make_async_copy(y_buf.at[slot, pl.ds(0, rows)], dst, out_sems.at[slot])

    def start_first_chunk(expert, slot):
        @pl.when(nblk_ref[expert] >= per_chunk)
        def _():
            in_copy(start_ref[expert], MOE_CHUNK, slot).start()

        for rows in tail_sizes:
            @pl.when(nblk_ref[expert] == rows // MOE_ROWS)
            def _():
                in_copy(start_ref[expert], rows, slot).start()

    def prefetch_next_expert(slot):
        nxt = next_ref[e]

        @pl.when(nxt < N_EXPERTS)
        def _():
            start_first_chunk(nxt, slot)
            ready[0] = slot

    def drain(slot):
        for rows in [MOE_CHUNK] + tail_sizes:
            @pl.when(pending[slot] == rows)
            def _():
                out_copy(0, rows, slot).wait()
        pending[slot] = 0

    def compute(rows, slot):
        for r0 in range(0, rows, MOE_PIECE):
            piece = pl.ds(r0, min(MOE_PIECE, rows - r0))
            x = jnp.concatenate(_unpack_bf16_pairs(x_buf[slot, piece, :]), axis=1)
            gu = jnp.dot(x, wgu_bf[...], preferred_element_type=F32) + bgu_ref[0, 0]
            glu = jnp.minimum(gu[:, :D_FF], SWIGLU_LIMIT)
            lin = jnp.clip(gu[:, D_FF:], -SWIGLU_LIMIT, SWIGLU_LIMIT)
            act = glu * jax.nn.sigmoid(SWIGLU_ALPHA * glu) * (lin + 1.0)
            y = jnp.dot(act.astype(BF16), wd_bf[...], preferred_element_type=F32) + bd_ref[0, 0]
            y_buf[slot, piece, :] = _pack_bf16_pairs(y)

    def run_chunk(row0, rows, slot):
        in_copy(row0, rows, slot).wait()
        drain(slot)
        compute(rows, slot)
        out_copy(row0, rows, slot).start()
        pending[slot] = rows

    @pl.when(e == 0)
    def _():
        pending[0] = 0
        pending[1] = 0
        ready[0] = -1

    @pl.when(nblk > 0)
    def _():
        @pl.when(ready[0] < 0)
        def _():
            start_first_chunk(e, 0)
            ready[0] = 0

        first_slot = ready[0]
        wgu_bf[...] = wgu_ref[0, 0].astype(BF16)
        wd_bf[...] = wd_ref[0, 0].astype(BF16)

        def body(c, carry):
            slot = (first_slot + c) & 1
            row0 = pl.multiple_of(base + c * MOE_CHUNK, MOE_ROWS)

            @pl.when(c + 1 < n_full)
            def _():
                in_copy(row0 + MOE_CHUNK, MOE_CHUNK, 1 - slot).start()

            for rows in tail_sizes:
                @pl.when(jnp.logical_and(c + 1 == n_full, tail_blocks == rows // MOE_ROWS))
                def _():
                    in_copy(row0 + MOE_CHUNK, rows, 1 - slot).start()

            @pl.when(jnp.logical_and(c + 1 == n_full, tail_blocks == 0))
            def _():
                prefetch_next_expert(1 - slot)

            run_chunk(row0, MOE_CHUNK, slot)
            return carry

        lax.fori_loop(0, n_full, body, 0)

        for rows in tail_sizes:
            @pl.when(tail_blocks == rows // MOE_ROWS)
            def _():
                slot = (first_slot + n_full) & 1
                prefetch_next_expert(1 - slot)
                run_chunk(pl.multiple_of(base + n_full * MOE_CHUNK, MOE_ROWS), rows, slot)

    @pl.when(e == pl.num_programs(0) - 1)
    def _():
        drain(0)
        drain(1)


def _moe(group_start, group_blocks, group_next, xs, w_gate_up, b_gate_up, w_down, b_down, layer):
    def wsel(e, st, nb, nx):
        return (layer, e, 0, 0)

    return pl.pallas_call(
        _moe_kernel,
        grid_spec=pltpu.PrefetchScalarGridSpec(
            num_scalar_prefetch=3,
            grid=(N_EXPERTS,),
            in_specs=[pl.BlockSpec(memory_space=pl.ANY),
                      pl.BlockSpec((1, 1, D_MODEL, 2 * D_FF), wsel),
                      pl.BlockSpec((1, 1, 1, 2 * D_FF), wsel),
                      pl.BlockSpec((1, 1, D_FF, D_MODEL), wsel),
                      pl.BlockSpec((1, 1, 1, D_MODEL), wsel)],
            out_specs=pl.BlockSpec(memory_space=pl.ANY),
            scratch_shapes=[pltpu.VMEM((D_MODEL, 2 * D_FF), BF16), pltpu.VMEM((D_FF, D_MODEL), BF16),
                            pltpu.VMEM((2, MOE_CHUNK, D_MODEL // 2), jnp.uint32),
                            pltpu.VMEM((2, MOE_CHUNK, D_MODEL // 2), jnp.uint32),
                            pltpu.SemaphoreType.DMA((2,)), pltpu.SemaphoreType.DMA((2,)),
                            pltpu.SMEM((2,), jnp.int32), pltpu.SMEM((1,), jnp.int32)]),
        out_shape=jax.ShapeDtypeStruct((MOE_SLOTS, D_MODEL // 2), jnp.uint32),
        compiler_params=_cparams("arbitrary"),
        name="moe_experts",
    )(group_start, group_blocks, group_next, xs, w_gate_up, b_gate_up, w_down, b_down)


def _route(idx, rank, counts):
    experts = jnp.arange(N_EXPERTS, dtype=jnp.int32)
    counts = counts.reshape(T_ALL // ROW_BLOCK, N_EXPERTS)
    before_block = jnp.cumsum(counts, axis=0) - counts
    total = jnp.sum(counts, axis=0)
    padded = (total + MOE_ROWS - 1) // MOE_ROWS * MOE_ROWS
    pad_end = jnp.cumsum(padded)
    pad_start = pad_end - padded
    base = jnp.repeat((pad_start[None, :] + before_block).T, ROW_BLOCK, axis=1)
    pos = jnp.sum(jnp.where(idx[:, None, :] == experts[None, :, None], base[None], 0), axis=1) + rank
    gap = padded - total
    gap_end = jnp.concatenate([jnp.cumsum(gap), jnp.full((1,), N_UNUSED, jnp.int32)])
    gap_start = jnp.concatenate([jnp.zeros((1,), jnp.int32), gap_end[:-1]])
    first_free = jnp.concatenate([pad_start + total, pad_end[-1:]])
    j = jnp.arange(N_UNUSED, dtype=jnp.int32)
    region = (j[:, None] >= gap_start[None, :]) & (j[:, None] < gap_end[None, :])
    unused = jnp.sum(jnp.where(region, first_free[None, :] + j[:, None] - gap_start[None, :], 0), axis=-1)
    nonempty = jnp.where(total > 0, experts, N_EXPERTS)
    following = jnp.concatenate([lax.cummin(nonempty, reverse=True)[1:], jnp.full((1,), N_EXPERTS, jnp.int32)])
    return (pos.astype(jnp.int32), unused.astype(jnp.int32),
            pad_start.astype(jnp.int32), (padded // MOE_ROWS).astype(jnp.int32), following.astype(jnp.int32))


def _sc_worker():
    return lax.axis_index("s") * SC_CORES + lax.axis_index("c")


def _sc_dispatch(xpk, pos_km, unused, zero_rows):
    ch = SC_DISPATCH_ROWS
    n_chunks = T_ALL // ch
    n_zero_chunks = N_UNUSED // ch
    width = xpk.shape[1]
    mesh = plsc.VectorSubcoreMesh(core_axis_name="c", subcore_axis_name="s")

    @functools.partial(
        pl.kernel, mesh=mesh,
        out_type=jax.ShapeDtypeStruct((MOE_SLOTS, width), xpk.dtype),
        scratch_types=[pltpu.VMEM((2, ch, width), xpk.dtype), pltpu.VMEM((2, TOP_K, ch), jnp.int32),
                       pltpu.SemaphoreType.DMA((2,)), pltpu.SemaphoreType.DMA((TOP_K,))],
        name="sc_dispatch")
    def run(x_hbm, pos_hbm, unused_hbm, zero_hbm, xs_hbm, rows_v, idx_v, load_sems, store_sems):
        wid = _sc_worker()
        pltpu.sync_copy(zero_hbm, rows_v.at[0])
        for r in range(n_zero_chunks // SC_WORKERS):
            start = pl.multiple_of((wid + r * SC_WORKERS) * ch, ch)
            pltpu.sync_copy(unused_hbm.at[pl.ds(start, ch)], idx_v.at[0, 0])
            pltpu.sync_copy(rows_v.at[0], xs_hbm.at[idx_v.at[0, 0]])

        def first_row(r):
            return pl.multiple_of((wid + r * SC_WORKERS) * ch, ch)

        def rows_copy(r, slot):
            return pltpu.make_async_copy(x_hbm.at[pl.ds(first_row(r), ch)], rows_v.at[slot], load_sems.at[slot])

        def start_load(r, slot):
            for k in range(TOP_K):
                pltpu.sync_copy(pos_hbm.at[k, pl.ds(first_row(r), ch)], idx_v.at[slot, k])
            rows_copy(r, slot).start()

        def has_chunk(r):
            return wid + r * SC_WORKERS < n_chunks

        start_load(0, 0)

        def body(r, carry):
            slot = r & 1

            @pl.when(has_chunk(r + 1))
            def _():
                start_load(r + 1, 1 - slot)

            @pl.when(has_chunk(r))
            def _():
                rows_copy(r, slot).wait()
                copies = [pltpu.async_copy(rows_v.at[slot], xs_hbm.at[idx_v.at[slot, k]], store_sems.at[k])
                          for k in range(TOP_K)]
                for cp in copies:
                    cp.wait()
            return carry

        lax.fori_loop(0, pl.cdiv(n_chunks, SC_WORKERS), body, 0)

    return run(xpk, pos_km, unused, zero_rows)


def _sc_collect(ys, pos_flat):
    ch = SC_COLLECT_ROWS
    per_worker = N_ASSIGN // SC_WORKERS
    mesh = plsc.VectorSubcoreMesh(core_axis_name="c", subcore_axis_name="s")

    @functools.partial(
        pl.kernel, mesh=mesh,
        out_type=jax.ShapeDtypeStruct((N_ASSIGN, ys.shape[1]), ys.dtype),
        scratch_types=[pltpu.VMEM((2, ch, ys.shape[1]), ys.dtype), pltpu.VMEM((2, ch), jnp.int32),
                       pltpu.SemaphoreType.DMA((2,))],
        name="sc_collect")
    def run(ys_hbm, pos_hbm, y4_hbm, rows_v, idx_v, sems):
        wid = _sc_worker()
        n_chunks = per_worker // ch

        def first_row(r):
            return pl.multiple_of(wid * per_worker + r * ch, 8)

        def gather(slot):
            return pltpu.make_async_copy(ys_hbm.at[idx_v.at[slot]], rows_v.at[slot], sems.at[slot])

        def start_gather(r, slot):
            pltpu.sync_copy(pos_hbm.at[pl.ds(first_row(r), ch)], idx_v.at[slot])
            gather(slot).start()

        start_gather(0, 0)

        def body(r, carry):
            slot = r & 1

            @pl.when(r + 1 < n_chunks)
            def _():
                start_gather(r + 1, 1 - slot)

            gather(slot).wait()
            pltpu.sync_copy(rows_v.at[slot], y4_hbm.at[pl.ds(first_row(r), ch)])
            return carry

        lax.fori_loop(0, n_chunks, body, 0)

    return run(ys, pos_flat)


def _combine_norm(x_ref, y_refs, gate_ref, g_ref, b_ref, rows=slice(None)):
    gate = gate_ref[:, rows].T
    lo = hi = None
    for k, y_ref in enumerate(y_refs):
        p = y_ref[0, rows, :]
        g = gate[:, k:k + 1]
        lo_k = g * lax.bitcast_convert_type(p << 16, F32)
        hi_k = g * lax.bitcast_convert_type(p & jnp.uint32(0xFFFF0000), F32)
        lo = lo_k if lo is None else lo + lo_k
        hi = hi_k if hi is None else hi + hi_k
    ffn = jnp.concatenate([lo, hi], axis=1)
    return _layer_norm(DEEPNORM_ALPHA * x_ref[rows, :] + ffn, g_ref[0], b_ref[0])


def _ffn_norm_kernel(x_ref, y0_ref, y1_ref, y2_ref, y3_ref, gate_ref, g_ref, b_ref, o_ref):
    o_ref[...] = _combine_norm(x_ref, (y0_ref, y1_ref, y2_ref, y3_ref), gate_ref, g_ref, b_ref)


def _norm_proj_kernel(x_ref, y0_ref, y1_ref, y2_ref, y3_ref, gate_ref, g_ref, b_ref, w_ref,
                      x2_ref, hm_ref, hkv_ref):
    for r in range(ROW_BLOCK // NORM_SUB_ROWS):
        rows = pl.ds(r * NORM_SUB_ROWS, NORM_SUB_ROWS)
        x2 = _combine_norm(x_ref, (y0_ref, y1_ref, y2_ref, y3_ref), gate_ref, g_ref, b_ref, rows)
        x2_ref[rows, :] = x2
        _project_in(x2, w_ref, hm_ref.at[rows, :], hkv_ref.at[rows, :])


def _norm_proj(x1, y4, gate, ln_g, ln_b, w_in_bf, layer):
    rows = pl.BlockSpec((ROW_BLOCK, D_MODEL), lambda i: (i, 0))

    def plane(k):
        return pl.BlockSpec((1, ROW_BLOCK, D_MODEL // 2), lambda i: (k, i, 0))

    vec = pl.BlockSpec((1, 1, D_MODEL), lambda i: (layer, 0, 0))
    return pl.pallas_call(
        _norm_proj_kernel,
        grid=(T_ALL // ROW_BLOCK,),
        in_specs=[rows, plane(0), plane(1), plane(2), plane(3),
                  pl.BlockSpec((GATE_ROWS, ROW_BLOCK), lambda i: (0, i)), vec, vec,
                  pl.BlockSpec((1, D_MODEL, D_IN), lambda i: (layer + 1, 0, 0))],
        out_specs=[rows] + _IN_PROJ_OUT_SPECS,
        out_shape=[jax.ShapeDtypeStruct((T_ALL, D_MODEL), F32)] + _IN_PROJ_OUT_SHAPES,
        compiler_params=_cparams("parallel"),
        name="norm_proj",
    )(x1, y4, y4, y4, y4, gate, ln_g, ln_b, w_in_bf)


def _ffn_norm(x1, y4, gate, ln_g, ln_b, layer, row_block, first_block, n_blocks):
    def rows(w):
        return pl.BlockSpec((row_block, w), lambda i: (first_block + i, 0))

    def plane(k):
        return pl.BlockSpec((1, row_block, D_MODEL // 2), lambda i: (k, first_block + i, 0))

    vec = pl.BlockSpec((1, 1, D_MODEL), lambda i: (layer, 0, 0))
    return pl.pallas_call(
        _ffn_norm_kernel,
        grid=(n_blocks,),
        in_specs=[rows(D_MODEL), plane(0), plane(1), plane(2), plane(3),
                  pl.BlockSpec((GATE_ROWS, row_block), lambda i: (0, first_block + i)), vec, vec],
        out_specs=pl.BlockSpec((row_block, D_MODEL), lambda i: (i, 0)),
        out_shape=jax.ShapeDtypeStruct((n_blocks * row_block, D_MODEL), F32),
        compiler_params=_cparams("parallel"),
        name="ffn_norm",
    )(x1, y4, y4, y4, y4, gate, ln_g, ln_b)


def _rope_tables(pos):
    half = HD_RET // 2
    inv = ROPE_BASE ** (-jnp.arange(half, dtype=F32) / half)
    ang = pos.astype(F32)[:, None] * inv[None, :]
    cos, sin = jnp.cos(ang), jnp.sin(ang)
    return jnp.concatenate([cos, cos], -1), jnp.concatenate([-sin, sin], -1)


def _decay_tables(c):
    log_g = jnp.log1p(-jnp.exp2(-5.0 - jnp.arange(H_RET, dtype=F32)))
    idx = jnp.arange(c, dtype=F32)
    diff = idx[:, None] - idx[None, :]
    intra = jnp.where(diff[None] >= 0, jnp.exp(jnp.maximum(diff, 0.0)[None] * log_g[:, None, None]), 0.0)
    q_dec = jnp.exp((idx + 1.0)[None] * log_g[:, None])
    k_dec = jnp.exp((c - 1.0 - idx)[None] * log_g[:, None])
    c_dec = jnp.exp(c * log_g)
    return intra, q_dec, k_dec, c_dec


def _t5_bucket(dist):
    n = jnp.maximum(dist, 0)
    max_exact = NUM_BUCKETS // 2
    nf = jnp.maximum(n, 1).astype(F32)
    large = max_exact + (jnp.log(nf / max_exact) / math.log(REL_MAX_DIST / max_exact)
                         * (NUM_BUCKETS - max_exact)).astype(jnp.int32)
    large = jnp.minimum(large, NUM_BUCKETS - 1)
    return jnp.where(n < max_exact, n, large)


def _bias_table(rel_bias, bucket):
    hit = bucket[..., None] == jnp.arange(NUM_BUCKETS)
    per_head = rel_bias.astype(F32).T.reshape((H_SWA,) + (1,) * bucket.ndim + (NUM_BUCKETS,))
    return jnp.sum(jnp.where(hit[None], per_head, 0.0), axis=-1)


def kernel(x_prompt, x_sample, state_ret, cache_swa_k, cache_swa_v, w_in, w_out, ret_gn_gain, swa_sinks,
           rel_bias, ln1_g, ln1_b, ln2_g, ln2_b, router_w, router_b, w_gate_up, b_gate_up, w_down, b_down):
    assert x_prompt.shape == (BATCH, SEQ, D_MODEL) and x_sample.shape == (DEC_BATCH, 1, D_MODEL)
    x = (x_prompt, x_sample)
    w_in_bf = w_in.astype(BF16)
    w_out_bf = w_out.astype(BF16)
    router_w_bf = jnp.swapaxes(router_w, 1, 2).astype(BF16)
    router_b = router_b.reshape(DEPTH, N_EXPERTS, 1)
    ret_gn_gain, ln1_g, ln1_b, ln2_g, ln2_b = (
        a.reshape(DEPTH, 1, a.shape[-1]) for a in (ret_gn_gain, ln1_g, ln1_b, ln2_g, ln2_b))
    b_gu4 = b_gate_up.reshape(DEPTH, N_EXPERTS, 1, 2 * D_FF)
    b_d4 = b_down.reshape(DEPTH, N_EXPERTS, 1, D_MODEL)
    cache_k = jnp.transpose(cache_swa_k, (0, 1, 3, 4, 2)).reshape(DEPTH, DEC_BATCH, 2 * HD_SWA, WINDOW)
    cache_v = jnp.transpose(cache_swa_v, (0, 1, 3, 4, 2)).reshape(DEPTH, DEC_BATCH, 2 * HD_SWA, WINDOW)

    cos_p, sin_p = _rope_tables(jnp.arange(SEQ, dtype=jnp.int32))
    cos_s, sin_s = _rope_tables(PAST_LEN + jnp.arange(1, dtype=jnp.int32))
    intra, q_dec, k_dec, c_dec = _decay_tables(RET_CHUNK)
    ones = jnp.ones((1, 1, HD_RET), F32)
    tables_p = (cos_p, sin_p, intra, q_dec[:, :, None] * ones, k_dec[:, :, None] * ones, c_dec[:, None, None] * ones)
    intra1, q_dec1, k_dec1, c_dec1 = _decay_tables(1)
    dec_s = jnp.stack([intra1[:, 0, 0], q_dec1[:, 0], k_dec1[:, 0], c_dec1], axis=1)
    i = jnp.arange(WINDOW)[:, None]
    j = jnp.arange(2 * WINDOW)[None, :]
    in_window = (WINDOW + i - j >= 0) & (WINDOW + i - j < WINDOW)
    has_prev = jnp.array([False, True])[:, None, None, None]
    allowed = in_window[None, None] & (has_prev | (j >= WINDOW)[None, None])
    bias_p = jnp.where(allowed, _bias_table(rel_bias, _t5_bucket(WINDOW + i - j))[None], -jnp.inf)
    js = jnp.arange(WINDOW + 1)
    bias_s = _bias_table(rel_bias, _t5_bucket(WINDOW - js))
    bias_c, bias_n = bias_s[:, :WINDOW], bias_s[:, WINDOW:]
    kv_of_head = jnp.arange(H_SWA) // G_SWA
    head_mask = (kv_of_head[:, None] == jnp.arange(KV_SWA)[None, :]).astype(F32)

    zero_rows = jnp.zeros((SC_DISPATCH_ROWS, D_MODEL // 2), jnp.uint32)

    sp = ss = ko = vo = tail = None
    h, h_kv = _in_proj(x_prompt, x_sample, w_in_bf)
    for l in range(DEPTH):
        o, sp, tail = _mix_prompt(h, h_kv, tables_p, ret_gn_gain, swa_sinks[l], bias_p, sp, tail, l)
        hs = h[T_PROMPT:].astype(F32)
        kvs = h_kv[T_PROMPT:]
        o, ss = _ret_sample(hs, dec_s, cos_s, sin_s, ret_gn_gain, state_ret, o, ss, l)
        q8 = hs[:, 4 * RET_W:].reshape(DEC_BATCH, H_SWA, HD_SWA)
        qm = (q8[:, :, None, :] * head_mask[None, :, :, None]).reshape(DEC_BATCH, H_SWA, 2 * HD_SWA)
        kn = kvs[:, :SWA_KV_W].reshape(DEC_BATCH, 1, SWA_KV_W)
        vn = kvs[:, SWA_KV_W:].reshape(DEC_BATCH, 1, SWA_KV_W)
        osm, ko, vo = _swa_sample(qm, kn, vn, cache_k, cache_v, bias_c, bias_n, swa_sinks[l][:, None], ko, vo, l)
        os_s = jnp.sum(osm.reshape(DEC_BATCH, H_SWA, KV_SWA, HD_SWA) * head_mask[None, :, :, None], axis=2)
        o = lax.dynamic_update_slice(o, os_s.reshape(DEC_BATCH, SWA_Q_W), (T_PROMPT, RET_W))

        x1, xpk, top_idx, gate, rank, counts = _out_proj(o, x, w_out_bf, ln1_g, ln1_b, router_w_bf, router_b, l)
        pos_km, unused, group_start, group_blocks, group_next = _route(top_idx, rank, counts)
        xs = _sc_dispatch(xpk, pos_km, unused, zero_rows)
        ys = _moe(group_start, group_blocks, group_next, xs, w_gate_up, b_gu4, w_down, b_d4, l)
        y4 = _sc_collect(ys, pos_km.reshape(N_ASSIGN)).reshape(TOP_K, T_ALL, D_MODEL // 2)
        if l < DEPTH - 1:
            x, h, h_kv = _norm_proj(x1, y4, gate, ln2_g, ln2_b, w_in_bf, l)
        else:
            yp = _ffn_norm(x1, y4, gate, ln2_g, ln2_b, l, FINAL_ROW_BLOCK, 0, T_PROMPT // FINAL_ROW_BLOCK)
            ys_out = _ffn_norm(x1, y4, gate, ln2_g, ln2_b, l, DEC_BATCH, T_PROMPT // DEC_BATCH, 1)

    yp = yp.reshape(BATCH, SEQ, D_MODEL)
    ys_out = ys_out.reshape(DEC_BATCH, 1, D_MODEL)

    def window_major(t):
        return jnp.transpose(t.reshape(DEPTH, t.shape[1], KV_SWA, HD_SWA, WINDOW), (0, 1, 4, 2, 3))

    return (yp, ys_out, sp, window_major(tail[:, :, :SWA_KV_W]), window_major(tail[:, :, SWA_KV_W:]), ss,
            window_major(ko), window_major(vo))
```

```python
import functools
import math

import jax
import jax.numpy as jnp
from jax import lax
from jax.experimental import pallas as pl
from jax.experimental.pallas import tpu as pltpu
from jax.experimental.pallas import tpu_sc as plsc

F32 = jnp.float32
BF16 = jnp.bfloat16

D_MODEL = 1024
BATCH = 8
SEQ = 2048
DEPTH = 4
DEC_BATCH = 128
PAST_LEN = 8192
H_RET = 4
HD_RET = 128
RET_CHUNK = 128
ROPE_BASE = 10000.0
H_SWA = 8
KV_SWA = 2
G_SWA = H_SWA // KV_SWA
HD_SWA = 64
WINDOW = 128
NUM_BUCKETS = 32
REL_MAX_DIST = 128
RET_W = H_RET * HD_RET
SWA_Q_W = H_SWA * HD_SWA
SWA_KV_W = KV_SWA * HD_SWA
D_IN = 4 * RET_W + SWA_Q_W + 2 * SWA_KV_W
MIX_W = RET_W + SWA_Q_W
H_MAIN_W = 4 * RET_W + SWA_Q_W
N_EXPERTS = 32
TOP_K = 4
D_FF = D_MODEL
SWIGLU_LIMIT = 7.0
SWIGLU_ALPHA = 1.702
DEEPNORM_ALPHA = (2 * DEPTH) ** 0.25
LN_EPS = 1e-5
GN_EPS = 1e-5

T_PROMPT = BATCH * SEQ
T_ALL = T_PROMPT + DEC_BATCH
ROW_BLOCK = 384
SAMPLE_BLOCK = 32
MOE_ROWS = 128
MOE_CHUNK = 1024
MOE_PIECE = 1024
N_ASSIGN = T_ALL * TOP_K
MOE_BLOCKS = N_ASSIGN // MOE_ROWS + N_EXPERTS
MOE_SLOTS = MOE_BLOCKS * MOE_ROWS
N_UNUSED = MOE_SLOTS - N_ASSIGN
SC_CORES = 2
SC_SUBCORES = 16
SC_WORKERS = SC_CORES * SC_SUBCORES
SC_DISPATCH_ROWS = 64
SC_COLLECT_ROWS = 48
FINAL_ROW_BLOCK = 512
NORM_SUB_ROWS = 128
MIX_BLOCKS = 8
GATE_ROWS = 8
VMEM_LIMIT = 56 * 1024 * 1024

assert T_ALL % ROW_BLOCK == 0 and ROW_BLOCK % NORM_SUB_ROWS == 0 and T_PROMPT % FINAL_ROW_BLOCK == 0
assert DEC_BATCH % SAMPLE_BLOCK == 0 and (SEQ // WINDOW) % MIX_BLOCKS == 0 and RET_CHUNK == WINDOW
assert MOE_CHUNK % MOE_ROWS == 0 and MOE_CHUNK % MOE_PIECE == 0 and N_ASSIGN % MOE_ROWS == 0
assert T_ALL % SC_DISPATCH_ROWS == 0 and N_UNUSED % (SC_DISPATCH_ROWS * SC_WORKERS) == 0
assert N_ASSIGN % SC_WORKERS == 0 and (N_ASSIGN // SC_WORKERS) % SC_COLLECT_ROWS == 0


def _cparams(*sem):
    return pltpu.CompilerParams(dimension_semantics=sem, vmem_limit_bytes=VMEM_LIMIT)


def _project_in(x, w_ref, hm_ref, hkv_ref):
    h = jnp.dot(x.astype(BF16), w_ref[0], preferred_element_type=F32)
    hm_ref[...] = h[:, :H_MAIN_W].astype(BF16)
    hkv_ref[...] = h[:, H_MAIN_W:]


class _InputRows:
    N = T_ALL // ROW_BLOCK
    PROMPT_TAIL = ROW_BLOCK - DEC_BATCH

    @classmethod
    def specs(cls):
        assert (cls.N - 1) * ROW_BLOCK + cls.PROMPT_TAIL == T_PROMPT and T_PROMPT % cls.PROMPT_TAIL == 0
        return [pl.BlockSpec((ROW_BLOCK, D_MODEL), lambda i: (jnp.minimum(i, cls.N - 2), 0)),
                pl.BlockSpec((cls.PROMPT_TAIL, D_MODEL), lambda i: (T_PROMPT // cls.PROMPT_TAIL - 1, 0)),
                pl.BlockSpec((DEC_BATCH, D_MODEL), lambda i: (0, 0))]

    @staticmethod
    def args(x_prompt, x_sample):
        xp = x_prompt.reshape(T_PROMPT, D_MODEL)
        return [xp, xp, x_sample.reshape(DEC_BATCH, D_MODEL)]

    @classmethod
    def read(cls, main_ref, tail_ref, sample_ref):
        last = jnp.concatenate([tail_ref[...], sample_ref[...]], axis=0)
        return jnp.where(pl.program_id(0) == cls.N - 1, last, main_ref[...])


def _in_proj_kernel(xm_ref, xt_ref, xs_ref, w_ref, hm_ref, hkv_ref):
    _project_in(_InputRows.read(xm_ref, xt_ref, xs_ref), w_ref, hm_ref, hkv_ref)


_IN_PROJ_OUT_SPECS = [pl.BlockSpec((ROW_BLOCK, H_MAIN_W), lambda i: (i, 0)),
                      pl.BlockSpec((ROW_BLOCK, 2 * SWA_KV_W), lambda i: (i, 0))]
_IN_PROJ_OUT_SHAPES = [jax.ShapeDtypeStruct((T_ALL, H_MAIN_W), BF16),
                       jax.ShapeDtypeStruct((T_ALL, 2 * SWA_KV_W), F32)]


def _in_proj(x_prompt, x_sample, w_in_bf):
    return pl.pallas_call(
        _in_proj_kernel,
        grid=(T_ALL // ROW_BLOCK,),
        in_specs=_InputRows.specs() + [pl.BlockSpec((1, D_MODEL, D_IN), lambda i: (0, 0, 0))],
        out_specs=_IN_PROJ_OUT_SPECS,
        out_shape=_IN_PROJ_OUT_SHAPES,
        compiler_params=_cparams("parallel"),
        name="in_proj",
    )(*_InputRows.args(x_prompt, x_sample), w_in_bf)


def _rotate(x, cos, sin_signed):
    return x * cos + pltpu.roll(x, HD_RET // 2, 1) * sin_signed


def _group_norm_gate(o, gain, g):
    mu = jnp.mean(o, -1, keepdims=True)
    var = jnp.mean(jnp.square(o - mu), -1, keepdims=True)
    return (o - mu) * lax.rsqrt(var + GN_EPS) * gain * (g * jax.nn.sigmoid(g))


def _layer_norm(y, g, b):
    mu = jnp.mean(y, -1, keepdims=True)
    var = jnp.mean(jnp.square(y - mu), -1, keepdims=True)
    return (y - mu) * lax.rsqrt(var + LN_EPS) * g + b


def _ret_prompt_stages(q_ref, k_ref, v_ref, g_ref, cos_ref, sin_ref, intra_ref, qdec_ref, kdec_ref,
                       cdec_ref, gain_ref, o_ref, s_ref, rows):
    cos = cos_ref[rows, :]
    sin = sin_ref[rows, :]
    nt = (((1,), (1,)), ((), ()))
    tn = (((0,), (0,)), ((), ()))
    heads = [slice(h * HD_RET, (h + 1) * HD_RET) for h in range(H_RET)]
    vs, atts, crosses = [], [], []
    for h, sl in enumerate(heads):
        q = _rotate(q_ref[rows, sl].astype(F32), cos, sin)
        k = _rotate(k_ref[rows, sl].astype(F32), cos, sin) * (HD_RET ** -0.5)
        v = v_ref[rows, sl]
        s = s_ref[0, 0, h]
        atts.append(lax.dot_general(q.astype(BF16), k.astype(BF16), nt, preferred_element_type=F32))
        crosses.append(jnp.dot((q * qdec_ref[h]).astype(BF16), s.astype(BF16), preferred_element_type=F32))
        s_ref[0, 0, h] = s * cdec_ref[h] + lax.dot_general((k * kdec_ref[h]).astype(BF16), v, tn,
                                                          preferred_element_type=F32)
        vs.append(v)
    yield
    atts = [(att * intra_ref[h]).astype(BF16) for h, att in enumerate(atts)]
    yield
    outs = [jnp.dot(atts[h], vs[h], preferred_element_type=F32) + crosses[h] for h in range(H_RET)]
    yield
    for h, sl in enumerate(heads):
        o_ref[rows, sl] = _group_norm_gate(outs[h], gain_ref[0, :, sl], g_ref[rows, sl].astype(F32))


def _swa_prompt_stages(sink_ref, q_ref, kv_prev, kv_cur, bias, o_ref, rows, lg_scr, p_scr):
    kv = jnp.concatenate([kv_prev, kv_cur], axis=0)
    k_t = kv[:, :SWA_KV_W].T
    k_t_swapped = pltpu.roll(k_t, HD_SWA, 0)
    v = kv[:, SWA_KV_W:]
    v_swapped = pltpu.roll(v, HD_SWA, 1)
    k_row = lax.broadcasted_iota(jnp.int32, k_t.shape, 0)
    v_lane = lax.broadcasted_iota(jnp.int32, v.shape, 1)
    ones = jnp.ones(v.shape, BF16)
    k_pad, v_pad = {}, {}
    for kvh in range(KV_SWA):
        for par in range(2):
            k_src = k_t if kvh == par else k_t_swapped
            v_src = v if kvh == par else v_swapped
            k_pad[kvh, par] = jnp.where((k_row >= par * HD_SWA) & (k_row < (par + 1) * HD_SWA), k_src, 0.0).astype(BF16)
            v_pad[kvh, par] = jnp.where((v_lane >= par * HD_SWA) & (v_lane < (par + 1) * HD_SWA), v_src, 0.0).astype(BF16)
    for hd in range(H_SWA):
        pair, par, kvh = hd // 2, hd % 2, hd // G_SWA
        q2 = (q_ref[rows, pair * 2 * HD_SWA:(pair + 1) * 2 * HD_SWA] * (HD_SWA ** -0.5)).astype(BF16)
        lg_scr[hd] = jnp.dot(q2, k_pad[kvh, par], preferred_element_type=F32) + bias(hd)
    yield
    sink_terms = []
    for hd in range(H_SWA):
        logits = lg_scr[hd]
        sink = sink_ref[hd]
        m = jnp.maximum(jnp.max(logits, -1, keepdims=True), sink)
        p_scr[hd] = jnp.exp(logits - m).astype(BF16)
        sink_terms.append(jnp.exp(sink - m))
    yield
    for pair in range(H_SWA // 2):
        acc = None
        for par in range(2):
            hd = 2 * pair + par
            p = p_scr[hd]
            den = jnp.dot(p, ones, preferred_element_type=F32) + sink_terms[hd]
            part = jnp.dot(p, v_pad[hd // G_SWA, par], preferred_element_type=F32) / den
            acc = part if acc is None else acc + part
        o_ref[rows, pair * 2 * HD_SWA:(pair + 1) * 2 * HD_SWA] = acc


def _mix_prompt_kernel(*refs):
    n_ret_in = 11
    ret_in = refs[:n_ret_in]
    sink_ref, q_ref, kvc_ref, kvp_ref, bias_ref = refs[n_ret_in:n_ret_in + 5]
    o_ref, s_ref, tail_ref, lg_scr, p_scr = refs[-5:]
    o_r_ref = o_ref.at[:, pl.ds(0, RET_W)]
    o_s_ref = o_ref.at[:, pl.ds(RET_W, SWA_Q_W)]
    c = pl.program_id(1)

    @pl.when(c == 0)
    def _():
        s_ref[...] = jnp.zeros_like(s_ref)

    @pl.when(c == pl.num_programs(1) - 1)
    def _():
        tail_ref[0, 0] = kvc_ref[pl.ds((MIX_BLOCKS - 1) * WINDOW, WINDOW), :].T

    groups = []
    for j in range(MIX_BLOCKS):
        rows = pl.ds(j * WINDOW, WINDOW)
        kv_cur = kvc_ref[rows, :]
        if j == 0:
            kv_prev = kvp_ref[...]
            table = jnp.minimum(c, 1)
        else:
            kv_prev = kvc_ref[pl.ds((j - 1) * WINDOW, WINDOW), :]
            table = 1
        groups.append(_ret_prompt_stages(*ret_in, o_r_ref, s_ref, rows))

        def bias(hd, table=table):
            return bias_ref[table, hd]

        groups.append(_swa_prompt_stages(sink_ref, q_ref, kv_prev, kv_cur, bias, o_s_ref, rows,
                                         lg_scr.at[j], p_scr.at[j]))
    while groups:
        for g in list(groups):
            if next(g, groups) is groups:
                groups.remove(g)


def _mix_prompt(h, h_kv, tables, gain, sinks_l, bias, s_prev, tail_prev, layer):
    cos, sin, intra, qdec, kdec, cdec = tables
    nb = SEQ // WINDOW
    ns = nb // MIX_BLOCKS
    rows = MIX_BLOCKS * WINDOW
    qcol = 4 * RET_W // SWA_Q_W

    def col(j):
        return pl.BlockSpec((rows, RET_W), lambda b, c: (b * ns + c, j))

    def whole(a):
        return pl.BlockSpec(a.shape, lambda b, c: (0,) * a.ndim)

    in_specs = [col(0), col(1), col(2), col(3),
                pl.BlockSpec((rows, HD_RET), lambda b, c: (c, 0)),
                pl.BlockSpec((rows, HD_RET), lambda b, c: (c, 0)),
                whole(intra), whole(qdec), whole(kdec), whole(cdec),
                pl.BlockSpec((1, 1, RET_W), lambda b, c: (layer, 0, 0)),
                pl.BlockSpec(memory_space=pltpu.SMEM),
                pl.BlockSpec((rows, SWA_Q_W), lambda b, c: (b * ns + c, qcol)),
                pl.BlockSpec((rows, 2 * SWA_KV_W), lambda b, c: (b * ns + c, 0)),
                pl.BlockSpec((WINDOW, 2 * SWA_KV_W), lambda b, c: (b * nb + jnp.maximum(MIX_BLOCKS * c - 1, 0), 0)),
                whole(bias)]
    args = [h, h, h, h, cos, sin, intra, qdec, kdec, cdec, gain, sinks_l, h, h_kv, h_kv, bias]
    aliases = {}
    if s_prev is not None:
        in_specs += [pl.BlockSpec(memory_space=pl.ANY), pl.BlockSpec(memory_space=pl.ANY)]
        args += [s_prev, tail_prev]
        aliases = {16: 1, 17: 2}

    return pl.pallas_call(
        _mix_prompt_kernel,
        grid=(BATCH, ns),
        in_specs=in_specs,
        scratch_shapes=[pltpu.VMEM((MIX_BLOCKS, H_SWA, WINDOW, 2 * WINDOW), F32),
                        pltpu.VMEM((MIX_BLOCKS, H_SWA, WINDOW, 2 * WINDOW), BF16)],
        out_specs=[pl.BlockSpec((rows, MIX_W), lambda b, c: (b * ns + c, 0)),
                   pl.BlockSpec((1, 1, H_RET, HD_RET, HD_RET), lambda b, c: (layer, b, 0, 0, 0)),
                   pl.BlockSpec((1, 1, 2 * SWA_KV_W, WINDOW), lambda b, c: (layer, b, 0, 0))],
        out_shape=[jax.ShapeDtypeStruct((T_ALL, MIX_W), F32),
                   jax.ShapeDtypeStruct((DEPTH, BATCH, H_RET, HD_RET, HD_RET), F32),
                   jax.ShapeDtypeStruct((DEPTH, BATCH, 2 * SWA_KV_W, WINDOW), F32)],
        input_output_aliases=aliases,
        compiler_params=_cparams("parallel", "arbitrary"),
        name="mix_prompt",
    )(*args)


def _ret_sample_kernel(dec_ref, q_ref, k_ref, v_ref, g_ref, cos_ref, sin_ref, gain_ref, st_ref, o_ref, so_ref):
    cos = cos_ref[...]
    sin = sin_ref[...]
    for h in range(H_RET):
        sl = slice(h * HD_RET, (h + 1) * HD_RET)
        intra, qdec, kdec, cdec = dec_ref[h, 0], dec_ref[h, 1], dec_ref[h, 2], dec_ref[h, 3]
        q = _rotate(q_ref[:, sl], cos, sin)
        k = _rotate(k_ref[:, sl], cos, sin) * (HD_RET ** -0.5)
        v = v_ref[:, sl]
        att = jnp.sum(q * k, -1, keepdims=True) * intra
        q_col = (q * qdec).T
        k_col = (k * kdec).T
        rows = []
        for b in range(SAMPLE_BLOCK):
            s = st_ref[0, b, h]
            v_row = v[b:b + 1, :]
            rows.append(att[b:b + 1, :] * v_row + jnp.sum(q_col[:, b:b + 1] * s, axis=0, keepdims=True))
            so_ref[0, b, h] = s * cdec + k_col[:, b:b + 1] * v_row
        o = jnp.concatenate(rows, axis=0)
        o_ref[:, sl] = _group_norm_gate(o, gain_ref[0, :, sl], g_ref[:, sl])


def _ret_sample(h, dec_s, cos_s, sin_s, gain, state_ret, o_r, st_prev, layer):
    base = T_PROMPT // SAMPLE_BLOCK

    def col(j):
        return pl.BlockSpec((SAMPLE_BLOCK, RET_W), lambda i: (i, j))

    in_specs = [pl.BlockSpec(memory_space=pltpu.SMEM),
                col(0), col(1), col(2), col(3),
                pl.BlockSpec((1, HD_RET), lambda i: (0, 0)),
                pl.BlockSpec((1, HD_RET), lambda i: (0, 0)),
                pl.BlockSpec((1, 1, RET_W), lambda i: (layer, 0, 0)),
                pl.BlockSpec((1, SAMPLE_BLOCK, H_RET, HD_RET, HD_RET), lambda i: (layer, i, 0, 0, 0)),
                pl.BlockSpec(memory_space=pl.ANY)]
    args = [dec_s, h, h, h, h, cos_s, sin_s, gain, state_ret, o_r]
    aliases = {9: 0}
    if st_prev is not None:
        in_specs.append(pl.BlockSpec(memory_space=pl.ANY))
        args.append(st_prev)
        aliases[10] = 1

    def body(*refs):
        _ret_sample_kernel(*refs[:9], *refs[-2:])

    return pl.pallas_call(
        body,
        grid=(DEC_BATCH // SAMPLE_BLOCK,),
        in_specs=in_specs,
        out_specs=[pl.BlockSpec((SAMPLE_BLOCK, RET_W), lambda i: (base + i, 0)),
                   pl.BlockSpec((1, SAMPLE_BLOCK, H_RET, HD_RET, HD_RET), lambda i: (layer, i, 0, 0, 0))],
        out_shape=[jax.ShapeDtypeStruct((T_ALL, MIX_W), F32),
                   jax.ShapeDtypeStruct((DEPTH, DEC_BATCH, H_RET, HD_RET, HD_RET), F32)],
        input_output_aliases=aliases,
        compiler_params=_cparams("parallel"),
        name="ret_sample",
    )(*args)


def _swa_sample_kernel(qm_ref, kn_ref, vn_ref, knc_ref, vnc_ref, kc_ref, vc_ref, bias_ref, biasn_ref, sink_ref,
                       o_ref, ko_ref, vo_ref):
    qm = qm_ref[...]
    kn = kn_ref[...]
    vn = vn_ref[...]
    kc = kc_ref[0]
    vc = vc_ref[0]
    scale = HD_SWA ** -0.5
    logits = jnp.einsum('bhd,bdj->bhj', qm.astype(BF16), kc.astype(BF16),
                        preferred_element_type=F32) * scale + bias_ref[...][None]
    j = lax.broadcasted_iota(jnp.int32, logits.shape, 2)
    logits = jnp.where(j >= 1, logits, -jnp.inf)
    ln = jnp.sum(qm * kn, -1, keepdims=True) * scale + biasn_ref[...][None]
    sink = sink_ref[...][None]
    m = jnp.maximum(jnp.maximum(jnp.max(logits, -1, keepdims=True), ln), sink)
    p = jnp.exp(logits - m)
    pn = jnp.exp(ln - m)
    denom = jnp.sum(p, -1, keepdims=True) + pn + jnp.exp(sink - m)
    w = p / denom
    o_ref[...] = (jnp.einsum('bhj,bdj->bhd', w.astype(BF16), vc.astype(BF16), preferred_element_type=F32)
                  + (pn / denom) * vn)
    last = lax.broadcasted_iota(jnp.int32, kc.shape, 2) == WINDOW - 1
    ko_ref[0] = jnp.where(last, knc_ref[...], pltpu.roll(kc, WINDOW - 1, 2))
    vo_ref[0] = jnp.where(last, vnc_ref[...], pltpu.roll(vc, WINDOW - 1, 2))


def _swa_sample(qm, kn, vn, cache_k, cache_v, bias_c, bias_n, sink_col, ko_prev, vo_prev, layer):
    sb = SAMPLE_BLOCK
    cache_spec = pl.BlockSpec((1, sb, 2 * HD_SWA, WINDOW), lambda i: (layer, i, 0, 0))
    row_spec = pl.BlockSpec((sb, 1, 2 * HD_SWA), lambda i: (i, 0, 0))
    col_spec = pl.BlockSpec((sb, 2 * HD_SWA, 1), lambda i: (i, 0, 0))
    in_specs = [pl.BlockSpec((sb, H_SWA, 2 * HD_SWA), lambda i: (i, 0, 0)),
                row_spec, row_spec, col_spec, col_spec,
                cache_spec, cache_spec,
                pl.BlockSpec((H_SWA, WINDOW), lambda i: (0, 0)),
                pl.BlockSpec((H_SWA, 1), lambda i: (0, 0)),
                pl.BlockSpec((H_SWA, 1), lambda i: (0, 0))]
    args = [qm, kn, vn, kn.reshape(DEC_BATCH, 2 * HD_SWA, 1), vn.reshape(DEC_BATCH, 2 * HD_SWA, 1),
            cache_k, cache_v, bias_c, bias_n, sink_col]
    aliases = {}
    if ko_prev is not None:
        in_specs += [pl.BlockSpec(memory_space=pl.ANY), pl.BlockSpec(memory_space=pl.ANY)]
        args += [ko_prev, vo_prev]
        aliases = {10: 1, 11: 2}

    def body(*refs):
        _swa_sample_kernel(*refs[:10], *refs[-3:])

    cshape = jax.ShapeDtypeStruct((DEPTH, DEC_BATCH, 2 * HD_SWA, WINDOW), F32)
    return pl.pallas_call(
        body,
        grid=(DEC_BATCH // sb,),
        in_specs=in_specs,
        out_specs=[pl.BlockSpec((sb, H_SWA, 2 * HD_SWA), lambda i: (i, 0, 0)), cache_spec, cache_spec],
        out_shape=[jax.ShapeDtypeStruct((DEC_BATCH, H_SWA, 2 * HD_SWA), F32), cshape, cshape],
        input_output_aliases=aliases,
        compiler_params=_cparams("parallel"),
        name="swa_sample",
    )(*args)


def _pack_bf16_pairs(x):
    w = x.shape[1] // 2
    lo = lax.bitcast_convert_type(x[:, :w].astype(BF16).astype(F32), jnp.uint32) >> 16
    hi = lax.bitcast_convert_type(x[:, w:].astype(BF16).astype(F32), jnp.uint32)
    return lo | hi


def _unpack_bf16_pairs(p):
    lo = lax.bitcast_convert_type(p << 16, F32).astype(BF16)
    hi = lax.bitcast_convert_type(p & jnp.uint32(0xFFFF0000), F32).astype(BF16)
    return lo, hi


def _out_proj_kernel(n_x_refs, o_ref, *refs):
    x_refs, (w_ref, g_ref, b_ref, rw_ref, rb_ref,
             x1_ref, xpk_ref, idx_ref, gate_ref, rank_ref, cnt_ref) = refs[:n_x_refs], refs[n_x_refs:]
    x = x_refs[0][...] if n_x_refs == 1 else _InputRows.read(*x_refs)
    nt = (((1,), (1,)), ((), ()))
    mix = jnp.dot(o_ref[...].astype(BF16), w_ref[0], preferred_element_type=F32)
    x1 = _layer_norm(DEEPNORM_ALPHA * x + mix, g_ref[0], b_ref[0])
    x1_ref[...] = x1
    xpk_ref[...] = _pack_bf16_pairs(x1)
    logits = lax.dot_general(rw_ref[0], x1.astype(BF16), nt, preferred_element_type=F32) + rb_ref[0]
    rows = logits.shape[1]
    expert = lax.broadcasted_iota(jnp.int32, logits.shape, 0)
    vals, hits = [], []
    for kk in range(TOP_K):
        m = jnp.max(logits, 0, keepdims=True)
        idx = jnp.min(jnp.where(logits == m, expert, N_EXPERTS), 0, keepdims=True)
        vals.append(m)
        hits.append(expert == idx)
        idx_ref[kk:kk + 1, :] = idx
        logits = jnp.where(expert == idx, -jnp.inf, logits)
    es = [jnp.exp(v - vals[0]) for v in vals]
    tot = es[0] + es[1] + es[2] + es[3]
    gate_ref[...] = jnp.concatenate([e / tot for e in es] + [jnp.zeros((GATE_ROWS - TOP_K, rows), F32)], axis=0)
    chosen = sum(h.astype(F32) for h in hits)
    earlier = (lax.broadcasted_iota(jnp.int32, (rows, rows), 0)
               < lax.broadcasted_iota(jnp.int32, (rows, rows), 1)).astype(BF16)
    before = jnp.dot(chosen.astype(BF16), earlier, preferred_element_type=F32)
    for kk in range(TOP_K):
        rank_ref[kk:kk + 1, :] = jnp.sum(jnp.where(hits[kk], before, 0.0), 0, keepdims=True).astype(jnp.int32)
    cnt_ref[0] = jnp.sum(chosen, axis=1, keepdims=True).astype(jnp.int32)


def _out_proj(o, x, w_out_bf, ln_g, ln_b, router_w_bf, router_b, layer):
    def rows(w):
        return pl.BlockSpec((ROW_BLOCK, w), lambda i: (i, 0))

    def vec(w):
        return pl.BlockSpec((1, 1, w), lambda i: (layer, 0, 0))

    def cols(h):
        return pl.BlockSpec((h, ROW_BLOCK), lambda i: (0, i))

    x_args, x_specs = ([x], [rows(D_MODEL)]) if not isinstance(x, tuple) else (_InputRows.args(*x), _InputRows.specs())
    return pl.pallas_call(
        functools.partial(_out_proj_kernel, len(x_args)),
        grid=(T_ALL // ROW_BLOCK,),
        in_specs=[rows(MIX_W)] + x_specs + [
                  pl.BlockSpec((1, MIX_W, D_MODEL), lambda i: (layer, 0, 0)),
                  vec(D_MODEL), vec(D_MODEL),
                  pl.BlockSpec((1, N_EXPERTS, D_MODEL), lambda i: (layer, 0, 0)),
                  pl.BlockSpec((1, N_EXPERTS, 1), lambda i: (layer, 0, 0))],
        out_specs=[rows(D_MODEL), rows(D_MODEL // 2), cols(TOP_K), cols(GATE_ROWS), cols(TOP_K),
                   pl.BlockSpec((1, N_EXPERTS, 1), lambda i: (i, 0, 0))],
        out_shape=[jax.ShapeDtypeStruct((T_ALL, D_MODEL), F32),
                   jax.ShapeDtypeStruct((T_ALL, D_MODEL // 2), jnp.uint32),
                   jax.ShapeDtypeStruct((TOP_K, T_ALL), jnp.int32),
                   jax.ShapeDtypeStruct((GATE_ROWS, T_ALL), F32),
                   jax.ShapeDtypeStruct((TOP_K, T_ALL), jnp.int32),
                   jax.ShapeDtypeStruct((T_ALL // ROW_BLOCK, N_EXPERTS, 1), jnp.int32)],
        compiler_params=_cparams("parallel"),
        name="out_proj",
    )(o, *x_args, w_out_bf, ln_g, ln_b, router_w_bf, router_b)


def _moe_kernel(start_ref, nblk_ref, next_ref, xs_hbm, wgu_ref, bgu_ref, wd_ref, bd_ref, ys_hbm,
                wgu_bf, wd_bf, x_buf, y_buf, in_sems, out_sems, pending, ready):
    e = pl.program_id(0)
    per_chunk = MOE_CHUNK // MOE_ROWS
    tail_sizes = [t * MOE_ROWS for t in range(1, per_chunk)]
    nblk = nblk_ref[e]
    n_full = nblk // per_chunk
    tail_blocks = nblk % per_chunk
    base = start_ref[e]

    def in_copy(row0, rows, slot):
        src = xs_hbm.at[pl.ds(pl.multiple_of(row0, MOE_ROWS), rows)]
        return pltpu.make_async_copy(src, x_buf.at[slot, pl.ds(0, rows)], in_sems.at[slot])

    def out_copy(row0, rows, slot):
        dst = ys_hbm.at[pl.ds(pl.multiple_of(row0, MOE_ROWS), rows)]
        return pltpu.make_async_copy(y_buf.at[slot, pl.ds(0, rows)], dst, out_sems.at[slot])

    def start_first_chunk(expert, slot):
        @pl.when(nblk_ref[expert] >= per_chunk)
        def _():
            in_copy(start_ref[expert], MOE_CHUNK, slot).start()

        for rows in tail_sizes:
            @pl.when(nblk_ref[expert] == rows // MOE_ROWS)
            def _():
                in_copy(start_ref[expert], rows, slot).start()

    def prefetch_next_expert(slot):
        nxt = next_ref[e]

        @pl.when(nxt < N_EXPERTS)
        def _():
            start_first_chunk(nxt, slot)
            ready[0] = slot

    def drain(slot):
        for rows in [MOE_CHUNK] + tail_sizes:
            @pl.when(pending[slot] == rows)
            def _():
                out_copy(0, rows, slot).wait()
        pending[slot] = 0

    def compute(rows, slot):
        for r0 in range(0, rows, MOE_PIECE):
            piece = pl.ds(r0, min(MOE_PIECE, rows - r0))
            x = jnp.concatenate(_unpack_bf16_pairs(x_buf[slot, piece, :]), axis=1)
            gu = jnp.dot(x, wgu_bf[...], preferred_element_type=F32) + bgu_ref[0, 0]
            glu = jnp.minimum(gu[:, :D_FF], SWIGLU_LIMIT)
            lin = jnp.clip(gu[:, D_FF:], -SWIGLU_LIMIT, SWIGLU_LIMIT)
            act = glu * jax.nn.sigmoid(SWIGLU_ALPHA * glu) * (lin + 1.0)
            y = jnp.dot(act.astype(BF16), wd_bf[...], preferred_element_type=F32) + bd_ref[0, 0]
            y_buf[slot, piece, :] = _pack_bf16_pairs(y)

    def run_chunk(row0, rows, slot):
        in_copy(row0, rows, slot).wait()
        drain(slot)
        compute(rows, slot)
        out_copy(row0, rows, slot).start()
        pending[slot] = rows

    @pl.when(e == 0)
    def _():
        pending[0] = 0
        pending[1] = 0
        ready[0] = -1

    @pl.when(nblk > 0)
    def _():
        @pl.when(ready[0] < 0)
        def _():
            start_first_chunk(e, 0)
            ready[0] = 0

        first_slot = ready[0]
        wgu_bf[...] = wgu_ref[0, 0].astype(BF16)
        wd_bf[...] = wd_ref[0, 0].astype(BF16)

        def body(c, carry):
            slot = (first_slot + c) & 1
            row0 = pl.multiple_of(base + c * MOE_CHUNK, MOE_ROWS)

            @pl.when(c + 1 < n_full)
            def _():
                in_copy(row0 + MOE_CHUNK, MOE_CHUNK, 1 - slot).start()

            for rows in tail_sizes:
                @pl.when(jnp.logical_and(c + 1 == n_full, tail_blocks == rows // MOE_ROWS))
                def _():
                    in_copy(row0 + MOE_CHUNK, rows, 1 - slot).start()

            @pl.when(jnp.logical_and(c + 1 == n_full, tail_blocks == 0))
            def _():
                prefetch_next_expert(1 - slot)

            run_chunk(row0, MOE_CHUNK, slot)
            return carry

        lax.fori_loop(0, n_full, body, 0)

        for rows in tail_sizes:
            @pl.when(tail_blocks == rows // MOE_ROWS)
            def _():
                slot = (first_slot + n_full) & 1
                prefetch_next_expert(1 - slot)
                run_chunk(pl.multiple_of(base + n_full * MOE_CHUNK, MOE_ROWS), rows, slot)

    @pl.when(e == pl.num_programs(0) - 1)
    def _():
        drain(0)
        drain(1)


def _moe(group_start, group_blocks, group_next, xs, w_gate_up, b_gate_up, w_down, b_down, layer):
    def wsel(e, st, nb, nx):
        return (layer, e, 0, 0)

    return pl.pallas_call(
        _moe_kernel,
        grid_spec=pltpu.PrefetchScalarGridSpec(
            num_scalar_prefetch=3,
            grid=(N_EXPERTS,),
            in_specs=[pl.BlockSpec(memory_space=pl.ANY),
                      pl.BlockSpec((1, 1, D_MODEL, 2 * D_FF), wsel),
                      pl.BlockSpec((1, 1, 1, 2 * D_FF), wsel),
                      pl.BlockSpec((1, 1, D_FF, D_MODEL), wsel),
                      pl.BlockSpec((1, 1, 1, D_MODEL), wsel)],
            out_specs=pl.BlockSpec(memory_space=pl.ANY),
            scratch_shapes=[pltpu.VMEM((D_MODEL, 2 * D_FF), BF16), pltpu.VMEM((D_FF, D_MODEL), BF16),
                            pltpu.VMEM((2, MOE_CHUNK, D_MODEL // 2), jnp.uint32),
                            pltpu.VMEM((2, MOE_CHUNK, D_MODEL // 2), jnp.uint32),
                            pltpu.SemaphoreType.DMA((2,)), pltpu.SemaphoreType.DMA((2,)),
                            pltpu.SMEM((2,), jnp.int32), pltpu.SMEM((1,), jnp.int32)]),
        out_shape=jax.ShapeDtypeStruct((MOE_SLOTS, D_MODEL // 2), jnp.uint32),
        compiler_params=_cparams("arbitrary"),
        name="moe_experts",
    )(group_start, group_blocks, group_next, xs, w_gate_up, b_gate_up, w_down, b_down)


def _route(idx, rank, counts):
    experts = jnp.arange(N_EXPERTS, dtype=jnp.int32)
    counts = counts.reshape(T_ALL // ROW_BLOCK, N_EXPERTS)
    before_block = jnp.cumsum(counts, axis=0) - counts
    total = jnp.sum(counts, axis=0)
    padded = (total + MOE_ROWS - 1) // MOE_ROWS * MOE_ROWS
    pad_end = jnp.cumsum(padded)
    pad_start = pad_end - padded
    base = jnp.repeat((pad_start[None, :] + before_block).T, ROW_BLOCK, axis=1)
    pos = jnp.sum(jnp.where(idx[:, None, :] == experts[None, :, None], base[None], 0), axis=1) + rank
    gap = padded - total
    gap_end = jnp.concatenate([jnp.cumsum(gap), jnp.full((1,), N_UNUSED, jnp.int32)])
    gap_start = jnp.concatenate([jnp.zeros((1,), jnp.int32), gap_end[:-1]])
    first_free = jnp.concatenate([pad_start + total, pad_end[-1:]])
    j = jnp.arange(N_UNUSED, dtype=jnp.int32)
    region = (j[:, None] >= gap_start[None, :]) & (j[:, None] < gap_end[None, :])
    unused = jnp.sum(jnp.where(region, first_free[None, :] + j[:, None] - gap_start[None, :], 0), axis=-1)
    nonempty = jnp.where(total > 0, experts, N_EXPERTS)
    following = jnp.concatenate([lax.cummin(nonempty, reverse=True)[1:], jnp.full((1,), N_EXPERTS, jnp.int32)])
    return (pos.astype(jnp.int32), unused.astype(jnp.int32),
            pad_start.astype(jnp.int32), (padded // MOE_ROWS).astype(jnp.int32), following.astype(jnp.int32))


def _sc_worker():
    return lax.axis_index("s") * SC_CORES + lax.axis_index("c")


def _sc_dispatch(xpk, pos_km, unused, zero_rows):
    ch = SC_DISPATCH_ROWS
    n_chunks = T_ALL // ch
    n_zero_chunks = N_UNUSED // ch
    width = xpk.shape[1]
    mesh = plsc.VectorSubcoreMesh(core_axis_name="c", subcore_axis_name="s")

    @functools.partial(
        pl.kernel, mesh=mesh,
        out_type=jax.ShapeDtypeStruct((MOE_SLOTS, width), xpk.dtype),
        scratch_types=[pltpu.VMEM((2, ch, width), xpk.dtype), pltpu.VMEM((2, TOP_K, ch), jnp.int32),
                       pltpu.SemaphoreType.DMA((2,)), pltpu.SemaphoreType.DMA((2, TOP_K)),
                       pltpu.SemaphoreType.DMA((TOP_K,))],
        name="sc_dispatch")
    def run(x_hbm, pos_hbm, unused_hbm, zero_hbm, xs_hbm, rows_v, idx_v, load_sems, idx_sems, store_sems):
        wid = _sc_worker()
        pltpu.sync_copy(zero_hbm, rows_v.at[0])
        for r in range(n_zero_chunks // SC_WORKERS):
            start = pl.multiple_of((wid + r * SC_WORKERS) * ch, ch)
            pltpu.sync_copy(unused_hbm.at[pl.ds(start, ch)], idx_v.at[0, 0])
            pltpu.sync_copy(rows_v.at[0], xs_hbm.at[idx_v.at[0, 0]])

        def first_row(r):
            return pl.multiple_of((wid + r * SC_WORKERS) * ch, ch)

        def rows_copy(r, slot):
            return pltpu.make_async_copy(x_hbm.at[pl.ds(first_row(r), ch)], rows_v.at[slot], load_sems.at[slot])

        def idx_copy(r, slot, k):
            return pltpu.make_async_copy(pos_hbm.at[k, pl.ds(first_row(r), ch)], idx_v.at[slot, k],
                                         idx_sems.at[slot, k])

        def start_load(r, slot):
            for k in range(TOP_K):
                idx_copy(r, slot, k).start()
            rows_copy(r, slot).start()

        def has_chunk(r):
            return wid + r * SC_WORKERS < n_chunks

        start_load(0, 0)

        def body(r, carry):
            slot = r & 1

            @pl.when(has_chunk(r + 1))
            def _():
                start_load(r + 1, 1 - slot)

            @pl.when(has_chunk(r))
            def _():
                rows_copy(r, slot).wait()
                for k in range(TOP_K):
                    idx_copy(r, slot, k).wait()
                copies = [pltpu.async_copy(rows_v.at[slot], xs_hbm.at[idx_v.at[slot, k]], store_sems.at[k])
                          for k in range(TOP_K)]
                for cp in copies:
                    cp.wait()
            return carry

        lax.fori_loop(0, pl.cdiv(n_chunks, SC_WORKERS), body, 0)

    return run(xpk, pos_km, unused, zero_rows)


def _sc_collect(ys, pos_flat):
    ch = SC_COLLECT_ROWS
    per_worker = N_ASSIGN // SC_WORKERS
    mesh = plsc.VectorSubcoreMesh(core_axis_name="c", subcore_axis_name="s")

    @functools.partial(
        pl.kernel, mesh=mesh,
        out_type=jax.ShapeDtypeStruct((N_ASSIGN, ys.shape[1]), ys.dtype),
        scratch_types=[pltpu.VMEM((2, ch, ys.shape[1]), ys.dtype), pltpu.VMEM((per_worker,), jnp.int32),
                       pltpu.SemaphoreType.DMA((2,))],
        name="sc_collect")
    def run(ys_hbm, pos_hbm, y4_hbm, rows_v, idx_v, sems):
        wid = _sc_worker()
        n_chunks = per_worker // ch
        first = pl.multiple_of(wid * per_worker, 8)
        pltpu.sync_copy(pos_hbm.at[pl.ds(first, per_worker)], idx_v)

        def gather(r, slot):
            idx = idx_v.at[pl.ds(pl.multiple_of(r * ch, 8), ch)]
            return pltpu.make_async_copy(ys_hbm.at[idx], rows_v.at[slot], sems.at[slot])

        gather(0, 0).start()

        def body(r, carry):
            slot = r & 1

            @pl.when(r + 1 < n_chunks)
            def _():
                gather(r + 1, 1 - slot).start()

            gather(r, slot).wait()
            pltpu.sync_copy(rows_v.at[slot], y4_hbm.at[pl.ds(pl.multiple_of(first + r * ch, 8), ch)])
            return carry

        lax.fori_loop(0, n_chunks, body, 0)

    return run(ys, pos_flat)


def _combine_norm(x_ref, y_refs, gate_ref, g_ref, b_ref, rows=slice(None)):
    gate = gate_ref[:, rows].T
    lo = hi = None
    for k, y_ref in enumerate(y_refs):
        p = y_ref[0, rows, :]
        g = gate[:, k:k + 1]
        lo_k = g * lax.bitcast_convert_type(p << 16, F32)
        hi_k = g * lax.bitcast_convert_type(p & jnp.uint32(0xFFFF0000), F32)
        lo = lo_k if lo is None else lo + lo_k
        hi = hi_k if hi is None else hi + hi_k
    ffn = jnp.concatenate([lo, hi], axis=1)
    return _layer_norm(DEEPNORM_ALPHA * x_ref[rows, :] + ffn, g_ref[0], b_ref[0])


def _ffn_norm_kernel(x_ref, y0_ref, y1_ref, y2_ref, y3_ref, gate_ref, g_ref, b_ref, o_ref):
    o_ref[...] = _combine_norm(x_ref, (y0_ref, y1_ref, y2_ref, y3_ref), gate_ref, g_ref, b_ref)


def _norm_proj_kernel(x_ref, y0_ref, y1_ref, y2_ref, y3_ref, gate_ref, g_ref, b_ref, w_ref,
                      x2_ref, hm_ref, hkv_ref):
    for r in range(ROW_BLOCK // NORM_SUB_ROWS):
        rows = pl.ds(r * NORM_SUB_ROWS, NORM_SUB_ROWS)
        x2 = _combine_norm(x_ref, (y0_ref, y1_ref, y2_ref, y3_ref), gate_ref, g_ref, b_ref, rows)
        x2_ref[rows, :] = x2
        _project_in(x2, w_ref, hm_ref.at[rows, :], hkv_ref.at[rows, :])


def _norm_proj(x1, y4, gate, ln_g, ln_b, w_in_bf, layer):
    rows = pl.BlockSpec((ROW_BLOCK, D_MODEL), lambda i: (i, 0))

    def plane(k):
        return pl.BlockSpec((1, ROW_BLOCK, D_MODEL // 2), lambda i: (k, i, 0))

    vec = pl.BlockSpec((1, 1, D_MODEL), lambda i: (layer, 0, 0))
    return pl.pallas_call(
        _norm_proj_kernel,
        grid=(T_ALL // ROW_BLOCK,),
        in_specs=[rows, plane(0), plane(1), plane(2), plane(3),
                  pl.BlockSpec((GATE_ROWS, ROW_BLOCK), lambda i: (0, i)), vec, vec,
                  pl.BlockSpec((1, D_MODEL, D_IN), lambda i: (layer + 1, 0, 0))],
        out_specs=[rows] + _IN_PROJ_OUT_SPECS,
        out_shape=[jax.ShapeDtypeStruct((T_ALL, D_MODEL), F32)] + _IN_PROJ_OUT_SHAPES,
        compiler_params=_cparams("parallel"),
        name="norm_proj",
    )(x1, y4, y4, y4, y4, gate, ln_g, ln_b, w_in_bf)


def _ffn_norm(x1, y4, gate, ln_g, ln_b, layer, row_block, first_block, n_blocks):
    def rows(w):
        return pl.BlockSpec((row_block, w), lambda i: (first_block + i, 0))

    def plane(k):
        return pl.BlockSpec((1, row_block, D_MODEL // 2), lambda i: (k, first_block + i, 0))

    vec = pl.BlockSpec((1, 1, D_MODEL), lambda i: (layer, 0, 0))
    return pl.pallas_call(
        _ffn_norm_kernel,
        grid=(n_blocks,),
        in_specs=[rows(D_MODEL), plane(0), plane(1), plane(2), plane(3),
                  pl.BlockSpec((GATE_ROWS, row_block), lambda i: (0, first_block + i)), vec, vec],
        out_specs=pl.BlockSpec((row_block, D_MODEL), lambda i: (i, 0)),
        out_shape=jax.ShapeDtypeStruct((n_blocks * row_block, D_MODEL), F32),
        compiler_params=_cparams("parallel"),
        name="ffn_norm",
    )(x1, y4, y4, y4, y4, gate, ln_g, ln_b)


def _rope_tables(pos):
    half = HD_RET // 2
    inv = ROPE_BASE ** (-jnp.arange(half, dtype=F32) / half)
    ang = pos.astype(F32)[:, None] * inv[None, :]
    cos, sin = jnp.cos(ang), jnp.sin(ang)
    return jnp.concatenate([cos, cos], -1), jnp.concatenate([-sin, sin], -1)


def _decay_tables(c):
    log_g = jnp.log1p(-jnp.exp2(-5.0 - jnp.arange(H_RET, dtype=F32)))
    idx = jnp.arange(c, dtype=F32)
    diff = idx[:, None] - idx[None, :]
    intra = jnp.where(diff[None] >= 0, jnp.exp(jnp.maximum(diff, 0.0)[None] * log_g[:, None, None]), 0.0)
    q_dec = jnp.exp((idx + 1.0)[None] * log_g[:, None])
    k_dec = jnp.exp((c - 1.0 - idx)[None] * log_g[:, None])
    c_dec = jnp.exp(c * log_g)
    return intra, q_dec, k_dec, c_dec


def _t5_bucket(dist):
    n = jnp.maximum(dist, 0)
    max_exact = NUM_BUCKETS // 2
    nf = jnp.maximum(n, 1).astype(F32)
    large = max_exact + (jnp.log(nf / max_exact) / math.log(REL_MAX_DIST / max_exact)
                         * (NUM_BUCKETS - max_exact)).astype(jnp.int32)
    large = jnp.minimum(large, NUM_BUCKETS - 1)
    return jnp.where(n < max_exact, n, large)


def _bias_table(rel_bias, bucket):
    hit = bucket[..., None] == jnp.arange(NUM_BUCKETS)
    per_head = rel_bias.astype(F32).T.reshape((H_SWA,) + (1,) * bucket.ndim + (NUM_BUCKETS,))
    return jnp.sum(jnp.where(hit[None], per_head, 0.0), axis=-1)


def kernel(x_prompt, x_sample, state_ret, cache_swa_k, cache_swa_v, w_in, w_out, ret_gn_gain, swa_sinks,
           rel_bias, ln1_g, ln1_b, ln2_g, ln2_b, router_w, router_b, w_gate_up, b_gate_up, w_down, b_down):
    assert x_prompt.shape == (BATCH, SEQ, D_MODEL) and x_sample.shape == (DEC_BATCH, 1, D_MODEL)
    x = (x_prompt, x_sample)
    w_in_bf = w_in.astype(BF16)
    w_out_bf = w_out.astype(BF16)
    router_w_bf = jnp.swapaxes(router_w, 1, 2).astype(BF16)
    router_b = router_b.reshape(DEPTH, N_EXPERTS, 1)
    ret_gn_gain, ln1_g, ln1_b, ln2_g, ln2_b = (
        a.reshape(DEPTH, 1, a.shape[-1]) for a in (ret_gn_gain, ln1_g, ln1_b, ln2_g, ln2_b))
    b_gu4 = b_gate_up.reshape(DEPTH, N_EXPERTS, 1, 2 * D_FF)
    b_d4 = b_down.reshape(DEPTH, N_EXPERTS, 1, D_MODEL)
    cache_k = jnp.transpose(cache_swa_k, (0, 1, 3, 4, 2)).reshape(DEPTH, DEC_BATCH, 2 * HD_SWA, WINDOW)
    cache_v = jnp.transpose(cache_swa_v, (0, 1, 3, 4, 2)).reshape(DEPTH, DEC_BATCH, 2 * HD_SWA, WINDOW)

    cos_p, sin_p = _rope_tables(jnp.arange(SEQ, dtype=jnp.int32))
    cos_s, sin_s = _rope_tables(PAST_LEN + jnp.arange(1, dtype=jnp.int32))
    intra, q_dec, k_dec, c_dec = _decay_tables(RET_CHUNK)
    ones = jnp.ones((1, 1, HD_RET), F32)
    tables_p = (cos_p, sin_p, intra, q_dec[:, :, None] * ones, k_dec[:, :, None] * ones, c_dec[:, None, None] * ones)
    intra1, q_dec1, k_dec1, c_dec1 = _decay_tables(1)
    dec_s = jnp.stack([intra1[:, 0, 0], q_dec1[:, 0], k_dec1[:, 0], c_dec1], axis=1)
    i = jnp.arange(WINDOW)[:, None]
    j = jnp.arange(2 * WINDOW)[None, :]
    in_window = (WINDOW + i - j >= 0) & (WINDOW + i - j < WINDOW)
    has_prev = jnp.array([False, True])[:, None, None, None]
    allowed = in_window[None, None] & (has_prev | (j >= WINDOW)[None, None])
    bias_p = jnp.where(allowed, _bias_table(rel_bias, _t5_bucket(WINDOW + i - j))[None], -jnp.inf)
    js = jnp.arange(WINDOW + 1)
    bias_s = _bias_table(rel_bias, _t5_bucket(WINDOW - js))
    bias_c, bias_n = bias_s[:, :WINDOW], bias_s[:, WINDOW:]
    kv_of_head = jnp.arange(H_SWA) // G_SWA
    head_mask = (kv_of_head[:, None] == jnp.arange(KV_SWA)[None, :]).astype(F32)

    zero_rows = jnp.zeros((SC_DISPATCH_ROWS, D_MODEL // 2), jnp.uint32)

    sp = ss = ko = vo = tail = None
    h, h_kv = _in_proj(x_prompt, x_sample, w_in_bf)
    for l in range(DEPTH):
        o, sp, tail = _mix_prompt(h, h_kv, tables_p, ret_gn_gain, swa_sinks[l], bias_p, sp, tail, l)
        hs = h[T_PROMPT:].astype(F32)
        kvs = h_kv[T_PROMPT:]
        o, ss = _ret_sample(hs, dec_s, cos_s, sin_s, ret_gn_gain, state_ret, o, ss, l)
        q8 = hs[:, 4 * RET_W:].reshape(DEC_BATCH, H_SWA, HD_SWA)
        qm = (q8[:, :, None, :] * head_mask[None, :, :, None]).reshape(DEC_BATCH, H_SWA, 2 * HD_SWA)
        kn = kvs[:, :SWA_KV_W].reshape(DEC_BATCH, 1, SWA_KV_W)
        vn = kvs[:, SWA_KV_W:].reshape(DEC_BATCH, 1, SWA_KV_W)
        osm, ko, vo = _swa_sample(qm, kn, vn, cache_k, cache_v, bias_c, bias_n, swa_sinks[l][:, None], ko, vo, l)
        os_s = jnp.sum(osm.reshape(DEC_BATCH, H_SWA, KV_SWA, HD_SWA) * head_mask[None, :, :, None], axis=2)
        o = lax.dynamic_update_slice(o, os_s.reshape(DEC_BATCH, SWA_Q_W), (T_PROMPT, RET_W))

        x1, xpk, top_idx, gate, rank, counts = _out_proj(o, x, w_out_bf, ln1_g, ln1_b, router_w_bf, router_b, l)
        pos_km, unused, group_start, group_blocks, group_next = _route(top_idx, rank, counts)
        xs = _sc_dispatch(xpk, pos_km, unused, zero_rows)
        ys = _moe(group_start, group_blocks, group_next, xs, w_gate_up, b_gu4, w_down, b_d4, l)
        y4 = _sc_collect(ys, pos_km.reshape(N_ASSIGN)).reshape(TOP_K, T_ALL, D_MODEL // 2)
        if l < DEPTH - 1:
            x, h, h_kv = _norm_proj(x1, y4, gate, ln2_g, ln2_b, w_in_bf, l)
        else:
            yp = _ffn_norm(x1, y4, gate, ln2_g, ln2_b, l, FINAL_ROW_BLOCK, 0, T_PROMPT // FINAL_ROW_BLOCK)
            ys_out = _ffn_norm(x1, y4, gate, ln2_g, ln2_b, l, DEC_BATCH, T_PROMPT // DEC_BATCH, 1)

    yp = yp.reshape(BATCH, SEQ, D_MODEL)
    ys_out = ys_out.reshape(DEC_BATCH, 1, D_MODEL)

    def window_major(t):
        return jnp.transpose(t.reshape(DEPTH, t.shape[1], KV_SWA, HD_SWA, WINDOW), (0, 1, 4, 2, 3))

    return (yp, ys_out, sp, window_major(tail[:, :, :SWA_KV_W]), window_major(tail[:, :, SWA_KV_W:]), ss,
            window_major(ko), window_major(vo))
```
